```python
import jax, jax.numpy as jnp
from jax import lax
import numpy as np

D_MODEL = 1024
BATCH = 8
SEQ = 2048
DEPTH = 1
DEC_BATCH = 16
DEC_SEQ = 16
PAST_LEN = 2048

CHUNK = 64
WINDOW = 128
WIN_CHUNKS = WINDOW // CHUNK
HEAD_DIM = 64
N_HEADS = 8
N_KV_HEADS = 2
Q_PER_KV = N_HEADS // N_KV_HEADS
ATTN_WIDTH = N_HEADS * HEAD_DIM
KV_WIDTH = N_KV_HEADS * HEAD_DIM
ATTN_SCALE = HEAD_DIM ** -0.5
MLP_CHUNK = 128
MLP_GROUPS = 8
MLP_WIDTH = D_MODEL
MLP_GROUP_DIM = MLP_WIDTH // MLP_GROUPS
N_EXPERTS = 32
TOP_K = 4
D_EXPERT = D_MODEL
SWIGLU_ALPHA = 1.702
SWIGLU_LIMIT = 7.0
EPS = 1e-6
NEG_INF = -1e30
Q_END = ATTN_WIDTH
K_END = Q_END + KV_WIDTH
V_END = K_END + KV_WIDTH
U_END = V_END + MLP_WIDTH
VM_END = U_END + MLP_WIDTH
GA_END = VM_END + D_MODEL
IN_COLS = GA_END + D_MODEL

kernel_name = 'hybrid_swa_sgu_moe_stream_step'


def rms_norm(x, g):
    xf = x.astype(jnp.float32)
    y = xf * lax.rsqrt(jnp.mean(xf * xf, axis=-1, keepdims=True) + EPS)
    return y.astype(x.dtype) * g


def alibi_slopes():
    s = 2.0 ** (-8.0 * np.arange(1, N_HEADS + 1) / N_HEADS)
    return jnp.asarray(s, jnp.float32).reshape(N_KV_HEADS, Q_PER_KV)


def sink_softmax(logits, sinks):
    sk = jnp.broadcast_to(sinks.astype(jnp.float32)[:, :, None, None], logits.shape[:-1] + (1,))
    p = jax.nn.softmax(jnp.concatenate([logits, sk], axis=-1), axis=-1)
    return p[..., :-1]


def mixer_inputs(h, w_in, q_norm_g, k_norm_g, sgu_norm_g):
    lead = h.shape[:-1]
    p = h @ w_in
    q = rms_norm(p[..., :Q_END].reshape(*lead, N_HEADS, HEAD_DIM), q_norm_g)
    k = rms_norm(p[..., Q_END:K_END].reshape(*lead, N_KV_HEADS, HEAD_DIM), k_norm_g)
    v = p[..., K_END:V_END].reshape(*lead, N_KV_HEADS, HEAD_DIM)
    u = jax.nn.gelu(p[..., V_END:U_END]).reshape(*lead, MLP_GROUPS, MLP_GROUP_DIM)
    vm = rms_norm(jax.nn.gelu(p[..., U_END:VM_END]).reshape(*lead, MLP_GROUPS, MLP_GROUP_DIM), sgu_norm_g)
    gate_a = jax.nn.sigmoid(p[..., VM_END:GA_END])
    gate_b = jax.nn.sigmoid(p[..., GA_END:])
    return q, k, v, u, vm, gate_a, gate_b


def window_attention_prompt(q, k, v, sinks):
    B, S = q.shape[0], q.shape[1]
    nc = S // CHUNK
    band = (WIN_CHUNKS + 1) * CHUNK
    qc = q.reshape(B, nc, CHUNK, N_KV_HEADS, Q_PER_KV, HEAD_DIM)
    pad = ((0, 0), (WIN_CHUNKS, 0), (0, 0), (0, 0), (0, 0))
    kp = jnp.pad(k.reshape(B, nc, CHUNK, N_KV_HEADS, HEAD_DIM), pad)
    vp = jnp.pad(v.reshape(B, nc, CHUNK, N_KV_HEADS, HEAD_DIM), pad)
    kb = jnp.concatenate([kp[:, o:o + nc] for o in range(WIN_CHUNKS + 1)], axis=2)
    vb = jnp.concatenate([vp[:, o:o + nc] for o in range(WIN_CHUNKS + 1)], axis=2)
    s = jnp.einsum('bnqkgd,bnskd->bnkgqs', qc, kb).astype(jnp.float32) * ATTN_SCALE
    qi = np.arange(CHUNK)[:, None]
    kj = np.arange(band)[None, :]
    dist = jnp.asarray(np.abs(qi + WIN_CHUNKS * CHUNK - kj), jnp.float32)
    key_chunk = np.arange(nc)[:, None] - WIN_CHUNKS + (np.arange(band) // CHUNK)[None, :]
    valid = jnp.asarray(key_chunk >= 0)
    logits = s - alibi_slopes()[:, :, None, None] * dist
    logits = jnp.where(valid[None, :, None, None, None, :], logits, NEG_INF)
    p = sink_softmax(logits, sinks.reshape(N_KV_HEADS, Q_PER_KV)).astype(v.dtype)
    o = jnp.einsum('bnkgqs,bnskd->bnqkgd', p, vb)
    return o.reshape(B, S, ATTN_WIDTH)


def window_attention_sample(q, k_all, v_all, sinks):
    DB, DS = q.shape[0], q.shape[1]
    R = k_all.shape[1] - DS
    qg = q.reshape(DB, DS, N_KV_HEADS, Q_PER_KV, HEAD_DIM)
    s = jnp.einsum('bqkgd,bskd->bkgqs', qg, k_all).astype(jnp.float32) * ATTN_SCALE
    qi = np.arange(DS)[:, None]
    kr = np.arange(R + DS)[None, :]
    dist = jnp.asarray(np.abs(qi + R - kr), jnp.float32)
    logits = s - alibi_slopes()[:, :, None, None] * dist
    p = sink_softmax(logits, sinks.reshape(N_KV_HEADS, Q_PER_KV)).astype(v_all.dtype)
    o = jnp.einsum('bkgqs,bskd->bqkgd', p, v_all)
    return o.reshape(DB, DS, ATTN_WIDTH)


def spatial_gating_prompt(u, vm, w_s, b_s):
    B, S = u.shape[0], u.shape[1]
    nm = S // MLP_CHUNK
    wm = w_s * jnp.tril(jnp.ones((MLP_CHUNK, MLP_CHUNK), w_s.dtype))
    vc = vm.reshape(B, nm, MLP_CHUNK, MLP_GROUPS, MLP_GROUP_DIM)
    mixed = jnp.einsum('gts,bnsgc->bntgc', wm, vc) + b_s.T[None, None, :, :, None]
    return (u * mixed.reshape(B, S, MLP_GROUPS, MLP_GROUP_DIM)).reshape(B, S, MLP_WIDTH)


def spatial_gating_sample(u, vm, w_s, b_s):
    DB, DS = u.shape[0], u.shape[1]
    wm = (w_s * jnp.tril(jnp.ones((MLP_CHUNK, MLP_CHUNK), w_s.dtype)))[:, :DS, :DS]
    mixed = jnp.einsum('gts,bsgc->btgc', wm, vm) + b_s[:, :DS].T[None, :, :, None]
    return (u * mixed).reshape(DB, DS, MLP_WIDTH)


def merge_branches(att, sgu, gate_a, gate_b, w_branch_attn, w_branch_sgu, w_out):
    m = gate_a * (att @ w_branch_attn) + gate_b * (sgu @ w_branch_sgu)
    return m @ w_out


def moe(x, w_router, b_router, w_gate_up, b_gate_up, w_down, b_down):
    logits = (x @ w_router).astype(jnp.float32) + b_router.astype(jnp.float32)
    top_v, top_i = lax.top_k(logits, TOP_K)
    top_w = jax.nn.softmax(top_v, axis=-1)
    gates = jnp.sum(jax.nn.one_hot(top_i, N_EXPERTS, dtype=jnp.float32) * top_w[..., None], axis=1).astype(x.dtype)
    out = jnp.zeros_like(x)
    for e in range(N_EXPERTS):
        hgu = x @ w_gate_up[e] + b_gate_up[e]
        glu = jnp.minimum(hgu[:, :D_EXPERT], SWIGLU_LIMIT)
        lin = jnp.clip(hgu[:, D_EXPERT:], -SWIGLU_LIMIT, SWIGLU_LIMIT)
        act = glu * jax.nn.sigmoid(glu * SWIGLU_ALPHA) * (lin + 1.0)
        out = out + gates[:, e:e + 1] * (act @ w_down[e] + b_down[e])
    return out


def setup_inputs(seed: int = 0) -> dict:
    key = jax.random.key(seed)
    ks = jax.random.split(key, 24)
    f32 = jnp.float32

    def nrm(k, shape, scale):
        return jax.random.normal(k, shape, f32) * scale

    win_rows = min(WINDOW, PAST_LEN)
    return {
        'x_prompt': nrm(ks[0], (BATCH, SEQ, D_MODEL), 1.0),
        'x_sample': nrm(ks[1], (DEC_BATCH, DEC_SEQ, D_MODEL), 1.0),
        'cache_k_win': nrm(ks[2], (DEPTH, DEC_BATCH, win_rows, N_KV_HEADS, HEAD_DIM), 1.0),
        'cache_v_win': nrm(ks[3], (DEPTH, DEC_BATCH, win_rows, N_KV_HEADS, HEAD_DIM), 1.0),
        'g_mix': 1.0 + nrm(ks[4], (DEPTH, D_MODEL), 0.02),
        'w_in': nrm(ks[5], (DEPTH, D_MODEL, IN_COLS), D_MODEL ** -0.5),
        'q_norm_g': 1.0 + nrm(ks[6], (DEPTH, HEAD_DIM), 0.02),
        'k_norm_g': 1.0 + nrm(ks[7], (DEPTH, HEAD_DIM), 0.02),
        'attn_sinks': nrm(ks[8], (DEPTH, N_HEADS), 0.5),
        'sgu_norm_g': 1.0 + nrm(ks[9], (DEPTH, MLP_GROUPS, MLP_GROUP_DIM), 0.02),
        'w_spatial': nrm(ks[10], (DEPTH, MLP_GROUPS, MLP_CHUNK, MLP_CHUNK), MLP_CHUNK ** -0.5),
        'b_spatial': 1.0 + nrm(ks[11], (DEPTH, MLP_GROUPS, MLP_CHUNK), 0.1),
        'w_branch_attn': nrm(ks[12], (DEPTH, ATTN_WIDTH, D_MODEL), ATTN_WIDTH ** -0.5),
        'w_branch_sgu': nrm(ks[13], (DEPTH, MLP_WIDTH, D_MODEL), MLP_WIDTH ** -0.5),
        'w_out': nrm(ks[14], (DEPTH, D_MODEL, D_MODEL), D_MODEL ** -0.5),
        'g_ffn': 1.0 + nrm(ks[15], (DEPTH, D_MODEL), 0.02),
        'w_router': nrm(ks[16], (DEPTH, D_MODEL, N_EXPERTS), D_MODEL ** -0.5),
        'b_router': nrm(ks[17], (DEPTH, N_EXPERTS), 0.01),
        'w_gate_up': nrm(ks[18], (DEPTH, N_EXPERTS, D_MODEL, 2 * D_EXPERT), D_MODEL ** -0.5),
        'b_gate_up': nrm(ks[19], (DEPTH, N_EXPERTS, 2 * D_EXPERT), 0.01),
        'w_down': nrm(ks[20], (DEPTH, N_EXPERTS, D_EXPERT, D_MODEL), D_EXPERT ** -0.5),
        'b_down': nrm(ks[21], (DEPTH, N_EXPERTS, D_MODEL), 0.01),
    }


def reference(x_prompt, x_sample, cache_k_win, cache_v_win, g_mix, w_in, q_norm_g, k_norm_g,
              attn_sinks, sgu_norm_g, w_spatial, b_spatial, w_branch_attn, w_branch_sgu, w_out,
              g_ffn, w_router, b_router, w_gate_up, b_gate_up, w_down, b_down):
    xp, xs = x_prompt, x_sample
    n_prompt_tok = xp.shape[0] * xp.shape[1]
    ds = xs.shape[1]
    kp_l, vp_l, ks_l, vs_l, sp_l, ss_l = [], [], [], [], [], []
    for l in range(DEPTH):
        hp = rms_norm(xp, g_mix[l])
        hs = rms_norm(xs, g_mix[l])
        qp, kp, vp, up, vmp, gap, gbp = mixer_inputs(hp, w_in[l], q_norm_g[l], k_norm_g[l], sgu_norm_g[l])
        qs, kn, vn, us, vms, gas, gbs = mixer_inputs(hs, w_in[l], q_norm_g[l], k_norm_g[l], sgu_norm_g[l])

        att_p = window_attention_prompt(qp, kp, vp, attn_sinks[l])
        k_all = jnp.concatenate([cache_k_win[l], kn], axis=1)
        v_all = jnp.concatenate([cache_v_win[l], vn], axis=1)
        att_s = window_attention_sample(qs, k_all, v_all, attn_sinks[l])

        sgu_p = spatial_gating_prompt(up, vmp, w_spatial[l], b_spatial[l])
        sgu_s = spatial_gating_sample(us, vms, w_spatial[l], b_spatial[l])

        xp = xp + merge_branches(att_p, sgu_p, gap, gbp, w_branch_attn[l], w_branch_sgu[l], w_out[l])
        xs = xs + merge_branches(att_s, sgu_s, gas, gbs, w_branch_attn[l], w_branch_sgu[l], w_out[l])

        h2 = jnp.concatenate([rms_norm(xp, g_ffn[l]).reshape(-1, D_MODEL),
                              rms_norm(xs, g_ffn[l]).reshape(-1, D_MODEL)], axis=0)
        f = moe(h2, w_router[l], b_router[l], w_gate_up[l], b_gate_up[l], w_down[l], b_down[l])
        xp = xp + f[:n_prompt_tok].reshape(xp.shape)
        xs = xs + f[n_prompt_tok:].reshape(xs.shape)

        kp_l.append(kp[:, -WINDOW:])
        vp_l.append(vp[:, -WINDOW:])
        ks_l.append(k_all[:, ds:])
        vs_l.append(v_all[:, ds:])
        sp_l.append(vmp[:, -MLP_CHUNK:])
        ss_l.append(vms)
    k_win_prompt = jnp.stack(kp_l)
    v_win_prompt = jnp.stack(vp_l)
    k_win_sample = jnp.stack(ks_l)
    v_win_sample = jnp.stack(vs_l)
    sgu_v_prompt = jnp.stack(sp_l)
    sgu_v_sample = jnp.stack(ss_l)
    return (xp, xs, k_win_prompt, v_win_prompt, k_win_sample, v_win_sample, sgu_v_prompt, sgu_v_sample)
```

```python
import functools

import jax
import jax.numpy as jnp
import numpy as np
from jax import lax
from jax.experimental import pallas as pl
from jax.experimental.pallas import tpu as pltpu

D_MODEL = 1024
CHUNK = 64
WINDOW = 128
HEAD_DIM = 64
N_HEADS = 8
N_KV_HEADS = 2
Q_PER_KV = N_HEADS // N_KV_HEADS
ATTN_WIDTH = N_HEADS * HEAD_DIM
KV_WIDTH = N_KV_HEADS * HEAD_DIM
ATTN_SCALE = HEAD_DIM ** -0.5
MLP_CHUNK = 128
MLP_GROUPS = 8
MLP_WIDTH = D_MODEL
MLP_GROUP_DIM = MLP_WIDTH // MLP_GROUPS
N_EXPERTS = 32
TOP_K = 4
D_EXPERT = D_MODEL
SWIGLU_ALPHA = 1.702
SWIGLU_LIMIT = 7.0
EPS = 1e-6
NEG_INF = -1e30
Q_END = ATTN_WIDTH
K_END = Q_END + KV_WIDTH
V_END = K_END + KV_WIDTH
U_END = V_END + MLP_WIDTH
VM_END = U_END + MLP_WIDTH
GA_END = VM_END + D_MODEL
IN_COLS = GA_END + D_MODEL

LANES = 128
TOKEN_BLOCK = 256
PAIR = 2 * CHUNK
BAND = PAIR + WINDOW
EXPERT_TILE = 256
VMEM_LIMIT = 56 * 1024 * 1024

F32 = jnp.float32
BF16 = jnp.bfloat16


def _dot(a, b):
    return jnp.dot(a, b, preferred_element_type=F32)


def _dot_nt(a, b):
    return lax.dot_general(a, b, (((1,), (1,)), ((), ())), preferred_element_type=F32)


def _segment_mean(sq, seg):
    hi = sq.astype(BF16)
    lo = (sq - hi.astype(F32)).astype(BF16)
    return _dot(hi, seg) + _dot(lo, seg)


def _rms_rows(x, gain):
    ms = jnp.mean(x * x, axis=-1, keepdims=True)
    return (x * lax.rsqrt(ms + EPS)) * gain


def _project(h, w_in_ref, qg, kg, sgug, seg64, seg128):
    qkv = _dot(h, w_in_ref[:, 0:V_END])
    qk_cols = []
    for c in range(K_END // LANES):
        blk = qkv[:, c * LANES:(c + 1) * LANES]
        ms = _segment_mean(blk * blk, seg64)
        g = qg if c < Q_END // LANES else kg
        qk_cols.append((blk * lax.rsqrt(ms + EPS)) * g)
    qn = jnp.concatenate(qk_cols[:Q_END // LANES], axis=-1)
    kn = qk_cols[Q_END // LANES]
    v = qkv[:, K_END:V_END]
    u = jax.nn.gelu(_dot(h, w_in_ref[:, V_END:U_END]))
    vg = jax.nn.gelu(_dot(h, w_in_ref[:, U_END:VM_END]))
    vm_cols = []
    for g in range(MLP_GROUPS):
        blk = vg[:, g * LANES:(g + 1) * LANES]
        ms = _segment_mean(blk * blk, seg128)
        vm_cols.append((blk * lax.rsqrt(ms + EPS)) * sgug[:, g * LANES:(g + 1) * LANES])
    vm = jnp.concatenate(vm_cols, axis=-1)
    ga = jax.nn.sigmoid(_dot(h, w_in_ref[:, VM_END:GA_END]))
    gb = jax.nn.sigmoid(_dot(h, w_in_ref[:, GA_END:IN_COLS]))
    return qn, kn, v, u, vm, ga, gb


def _stack_heads(q_rows):
    lane = lax.broadcasted_iota(jnp.int32, (q_rows.shape[0], LANES), 1)
    blocks = []
    for head in range(N_HEADS):
        j, half = head % Q_PER_KV, head // Q_PER_KV
        col = q_rows[:, j * LANES:(j + 1) * LANES]
        keep = (lane < HEAD_DIM) if half == 0 else (lane >= HEAD_DIM)
        blocks.append(jnp.where(keep, col, 0.0))
    return jnp.concatenate(blocks, axis=0).astype(BF16)


def _unstack_heads(o, rows):
    lane = lax.broadcasted_iota(jnp.int32, (rows, LANES), 1)
    cols = []
    for j in range(Q_PER_KV):
        lo = o[j * rows:(j + 1) * rows]
        hi = o[(j + Q_PER_KV) * rows:(j + Q_PER_KV + 1) * rows]
        cols.append(jnp.where(lane < HEAD_DIM, lo, hi))
    return jnp.concatenate(cols, axis=-1)


def _merge(att, sgu, ga, gb, wba_ref, wbs_ref, wout_ref):
    m = ga * _dot(att.astype(BF16), wba_ref[...]) + gb * _dot(sgu.astype(BF16), wbs_ref[...])
    return _dot(m.astype(BF16), wout_ref[...])


def _route(h2, wr_ref, br_ref, tri_ref, run_ref):
    rows = h2.shape[0]
    logits = _dot(h2, wr_ref[...]) + br_ref[...]
    eidx = lax.broadcasted_iota(jnp.int32, (rows, N_EXPERTS), 1).astype(F32)
    work = logits
    vals, picks, onehots = [], [], []
    for _ in range(TOP_K):
        m = jnp.max(work, axis=-1, keepdims=True)
        sel = jnp.min(jnp.where(work == m, eidx, float(N_EXPERTS)), axis=-1, keepdims=True)
        oh = eidx == sel
        vals.append(m)
        picks.append(sel)
        onehots.append(oh)
        work = jnp.where(oh, -jnp.inf, work)
    exps = [jnp.exp(v - vals[0]) for v in vals]
    den = exps[0] + exps[1] + exps[2] + exps[3]
    mask = jnp.zeros((rows, N_EXPERTS), F32)
    for oh in onehots:
        mask = mask + jnp.where(oh, 1.0, 0.0)
    before = _dot(tri_ref[...], mask.astype(BF16)) + run_ref[...]
    ranks = [jnp.sum(jnp.where(oh, before, 0.0), axis=-1, keepdims=True) for oh in onehots]
    run_ref[...] = run_ref[...] + jnp.sum(mask, axis=0, keepdims=True)
    k4 = lax.broadcasted_iota(jnp.int32, (rows, TOP_K), 1)
    idx4 = jnp.zeros((rows, TOP_K), jnp.int32)
    w4 = jnp.zeros((rows, TOP_K), F32)
    r4 = jnp.zeros((rows, TOP_K), jnp.int32)
    for k in range(TOP_K):
        idx4 = jnp.where(k4 == k, picks[k].astype(jnp.int32), idx4)
        w4 = jnp.where(k4 == k, exps[k] / den, w4)
        r4 = jnp.where(k4 == k, ranks[k].astype(jnp.int32), r4)
    return idx4, w4, r4


def _softmax_rows(parts, sink):
    m = sink
    for l in parts:
        m = jnp.maximum(m, jnp.max(l, axis=-1, keepdims=True))
    es = [jnp.exp(l - m) for l in parts]
    den = jnp.exp(sink - m)
    for e in es:
        den = den + jnp.sum(e, axis=-1, keepdims=True)
    return [e / den for e in es]


def _mixer_prompt_body(sinks_ref, x_ref, w_in_ref, gmix_ref, qg_ref, kg_ref, sgug_ref, seg64_ref, seg128_ref,
                       bias_ref, wsp_ref, bsp_ref, wba_ref, wbs_ref, wout_ref, gffn_ref, wr_ref, br_ref, tri_ref,
                       x1_ref, h2_ref, idx_ref, gw_ref, rank_ref, cnt_ref, kwin_ref, vwin_ref, sguv_ref,
                       kcarry, vcarry, run_ref, *, blocks_per_seq, n_blocks):
    i = pl.program_id(0)

    @pl.when(i == 0)
    def _():
        run_ref[...] = jnp.zeros_like(run_ref)

    @pl.when(i == n_blocks)
    def _():
        h2_ref[...] = jnp.zeros_like(h2_ref)
        idx_ref[...] = jnp.zeros_like(idx_ref)
        gw_ref[...] = jnp.zeros_like(gw_ref)
        rank_ref[...] = jnp.zeros_like(rank_ref)

    @pl.when(i < n_blocks)
    def _():
        _mixer_prompt_block(i % blocks_per_seq, sinks_ref, x_ref, w_in_ref, gmix_ref, qg_ref, kg_ref, sgug_ref,
                            seg64_ref, seg128_ref, bias_ref, wsp_ref, bsp_ref, wba_ref, wbs_ref, wout_ref, gffn_ref,
                            wr_ref, br_ref, tri_ref, x1_ref, h2_ref, idx_ref, gw_ref, rank_ref, cnt_ref, kwin_ref,
                            vwin_ref, sguv_ref, kcarry, vcarry, run_ref)


def _mixer_prompt_block(j, sinks_ref, x_ref, w_in_ref, gmix_ref, qg_ref, kg_ref, sgug_ref, seg64_ref, seg128_ref,
                        bias_ref, wsp_ref, bsp_ref, wba_ref, wbs_ref, wout_ref, gffn_ref, wr_ref, br_ref, tri_ref,
                        x1_ref, h2_ref, idx_ref, gw_ref, rank_ref, cnt_ref, kwin_ref, vwin_ref, sguv_ref,
                        kcarry, vcarry, run_ref):
    @pl.when(j == 0)
    def _():
        kcarry[...] = jnp.zeros_like(kcarry)
        vcarry[...] = jnp.zeros_like(vcarry)

    x = x_ref[0]
    h = _rms_rows(x, gmix_ref[...]).astype(BF16)
    qn, kn, v, u, vm, ga, gb = _project(h, w_in_ref, qg_ref[...], kg_ref[...], sgug_ref[...],
                                        seg64_ref[...], seg128_ref[...])
    k_ext = jnp.concatenate([kcarry[...], kn.astype(BF16)], axis=0)
    v_ext = jnp.concatenate([vcarry[...], v.astype(BF16)], axis=0)
    kcarry[...] = k_ext[TOKEN_BLOCK:]
    vcarry[...] = v_ext[TOKEN_BLOCK:]

    tri_mask = (lax.broadcasted_iota(jnp.int32, (MLP_CHUNK, MLP_CHUNK), 0)
                >= lax.broadcasted_iota(jnp.int32, (MLP_CHUNK, MLP_CHUNK), 1))
    att_rows, sgu_rows = [], []
    for pm in range(TOKEN_BLOCK // PAIR):
        r0 = pm * PAIR
        q_stack = _stack_heads(qn[r0:r0 + PAIR])
        k_band = k_ext[r0:r0 + BAND]
        v_band = v_ext[r0:r0 + BAND]
        s = _dot_nt(q_stack, k_band)
        first = jnp.where(j == 0, 0, 1) if pm == 0 else 1
        probs = []
        for head in range(N_HEADS):
            logit = s[head * PAIR:(head + 1) * PAIR] * ATTN_SCALE + bias_ref[first, head]
            probs.append(_softmax_rows([logit], sinks_ref[head])[0].astype(BF16))
        o = _dot(jnp.concatenate(probs, axis=0), v_band)
        att_rows.append(_unstack_heads(o, PAIR))
        cols = []
        for g in range(MLP_GROUPS):
            wm = jnp.where(tri_mask, wsp_ref[g], 0.0).astype(BF16)
            cols.append(_dot(wm, vm[r0:r0 + PAIR, g * LANES:(g + 1) * LANES].astype(BF16)))
        mixed = jnp.concatenate(cols, axis=-1) + bsp_ref[...]
        sgu_rows.append(u[r0:r0 + PAIR] * mixed)
    att = jnp.concatenate(att_rows, axis=0)
    sgu = jnp.concatenate(sgu_rows, axis=0)

    x1 = x + _merge(att, sgu, ga, gb, wba_ref, wbs_ref, wout_ref)
    h2 = _rms_rows(x1, gffn_ref[...])
    idx4, w4, r4 = _route(h2.astype(BF16), wr_ref, br_ref, tri_ref, run_ref)

    x1_ref[0] = x1
    h2_ref[...] = h2
    idx_ref[...] = idx4
    gw_ref[...] = w4
    rank_ref[...] = r4
    cnt_ref[...] = run_ref[...]
    kwin_ref[0] = kn[TOKEN_BLOCK - WINDOW:]
    vwin_ref[0] = v[TOKEN_BLOCK - WINDOW:]
    sguv_ref[0] = vm[TOKEN_BLOCK - MLP_CHUNK:]


def _mixer_sample_body(sinks_ref, x_ref, ck_ref, cv_ref, w_in_ref, gmix_ref, qg_ref, kg_ref, sgug_ref, seg64_ref,
                       seg128_ref, biasc_ref, biasn_ref, wsp_ref, bsp_ref, wba_ref, wbs_ref, wout_ref, gffn_ref,
                       wr_ref, br_ref, tri_ref, h2_in_ref, idx_in_ref, gw_in_ref, rank_in_ref,
                       x1_ref, h2_ref, idx_ref, gw_ref, rank_ref, cnt_ref, kwin_ref, vwin_ref, sguv_ref,
                       run_ref, *, n_streams, n_new):
    del h2_in_ref, idx_in_ref, gw_in_ref, rank_in_ref
    run_ref[...] = jnp.zeros_like(run_ref)
    x = x_ref[...]
    h = _rms_rows(x, gmix_ref[...]).astype(BF16)
    qn, kn, v, u, vm, ga, gb = _project(h, w_in_ref, qg_ref[...], kg_ref[...], sgug_ref[...],
                                        seg64_ref[...], seg128_ref[...])
    n_cached = ck_ref.shape[1]
    att_rows = []
    for s_i in range(n_streams):
        r0 = s_i * n_new
        q_stack = _stack_heads(qn[r0:r0 + n_new])
        k_new = kn[r0:r0 + n_new]
        v_new = v[r0:r0 + n_new]
        s_c = _dot_nt(q_stack, ck_ref[s_i].astype(BF16))
        s_n = _dot_nt(q_stack, k_new.astype(BF16))
        pc, pn = [], []
        for head in range(N_HEADS):
            rows = slice(head * n_new, (head + 1) * n_new)
            lc = s_c[rows] * ATTN_SCALE + biasc_ref[rows]
            ln = s_n[rows] * ATTN_SCALE + biasn_ref[rows]
            p_c, p_n = _softmax_rows([lc, ln], sinks_ref[head])
            pc.append(p_c.astype(BF16))
            pn.append(p_n.astype(BF16))
        o = (_dot(jnp.concatenate(pc, axis=0), cv_ref[s_i].astype(BF16))
             + _dot(jnp.concatenate(pn, axis=0), v_new.astype(BF16)))
        att_rows.append(_unstack_heads(o, n_new))
        kwin_ref[s_i, 0:n_cached - n_new] = ck_ref[s_i, n_new:n_cached]
        kwin_ref[s_i, n_cached - n_new:n_cached] = k_new
        vwin_ref[s_i, 0:n_cached - n_new] = cv_ref[s_i, n_new:n_cached]
        vwin_ref[s_i, n_cached - n_new:n_cached] = v_new
    att = jnp.concatenate(att_rows, axis=0)

    rows = n_streams * n_new
    ri = lax.broadcasted_iota(jnp.int32, (rows, rows), 0)
    ci = lax.broadcasted_iota(jnp.int32, (rows, rows), 1)
    keep = jnp.logical_and(ri // n_new == ci // n_new, ri % n_new >= ci % n_new)
    cols = []
    for g in range(MLP_GROUPS):
        wm = jnp.where(keep, wsp_ref[g], 0.0).astype(BF16)
        cols.append(_dot(wm, vm[:, g * LANES:(g + 1) * LANES].astype(BF16)))
    sgu = u * (jnp.concatenate(cols, axis=-1) + bsp_ref[...])

    x1 = x + _merge(att, sgu, ga, gb, wba_ref, wbs_ref, wout_ref)
    h2 = _rms_rows(x1, gffn_ref[...])
    idx4, w4, r4 = _route(h2.astype(BF16), wr_ref, br_ref, tri_ref, run_ref)
    x1_ref[...] = x1
    h2_ref[...] = h2
    idx_ref[...] = idx4
    gw_ref[...] = w4
    rank_ref[...] = r4
    cnt_ref[...] = run_ref[...]
    sguv_ref[...] = vm


def _wait_rows(src, dst, sem, n_rows):
    n_bulk = pl.multiple_of((n_rows >> 3) << 3, 8)

    @pl.when(n_bulk > 0)
    def _():
        pltpu.make_async_copy(src.at[pl.ds(0, n_bulk), :], dst.at[pl.ds(0, n_bulk), :], sem).wait()

    def wait_row(r, carry):
        pltpu.make_async_copy(src.at[pl.ds(0, 1), :], dst.at[pl.ds(0, 1), :], sem).wait()
        return carry
    lax.fori_loop(n_bulk, n_rows, wait_row, 0)


def _expert_body(te_ref, nvalid_ref, tok_ref, dst_ref, h2_hbm, wgu_ref, bgu_ref, wd_ref, bd_ref, yg_hbm,
                 xbuf, ybuf, wgu_bf, wd_bf, sem_in, sem_out):
    i = pl.program_id(0)
    n_valid = nvalid_ref[i]

    @pl.when(i == 0)
    def _():
        xbuf[...] = jnp.zeros_like(xbuf)

    @pl.when(n_valid > 0)
    def _():
        def gather_row(r, carry):
            pltpu.make_async_copy(h2_hbm.at[pl.ds(tok_ref[0, 0, r], 1), :], xbuf.at[pl.ds(r, 1), :], sem_in).start()
            return carry
        lax.fori_loop(0, n_valid, gather_row, 0)

        prev = te_ref[jnp.maximum(i - 1, 0)]

        @pl.when(jnp.logical_or(i == 0, te_ref[i] != prev))
        def _():
            wgu_bf[...] = wgu_ref[0].astype(BF16)
            wd_bf[...] = wd_ref[0].astype(BF16)

        _wait_rows(h2_hbm, xbuf, sem_in, n_valid)
        xb = xbuf[...].astype(BF16)
        hgu = _dot(xb, wgu_bf[...]) + bgu_ref[0]
        glu = jnp.minimum(hgu[:, :D_EXPERT], SWIGLU_LIMIT)
        lin = jnp.clip(hgu[:, D_EXPERT:], -SWIGLU_LIMIT, SWIGLU_LIMIT)
        act = glu * jax.nn.sigmoid(glu * SWIGLU_ALPHA) * (lin + 1.0)
        ybuf[...] = _dot(act.astype(BF16), wd_bf[...]) + bd_ref[0]

        def scatter_row(r, carry):
            pltpu.make_async_copy(ybuf.at[pl.ds(r, 1), :], yg_hbm.at[pl.ds(dst_ref[0, 0, r], 1), :], sem_out).start()
            return carry
        lax.fori_loop(0, n_valid, scatter_row, 0)
        _wait_rows(ybuf, yg_hbm, sem_out, n_valid)


def _combine_body(x1_ref, gw_ref, y0_ref, y1_ref, y2_ref, y3_ref, out_ref):
    gw = gw_ref[...]
    acc = x1_ref[...]
    for k, y_ref in enumerate((y0_ref, y1_ref, y2_ref, y3_ref)):
        acc = acc + gw[:, k:k + 1] * y_ref[...]
    out_ref[...] = acc


def _const_spec(shape):
    nd = len(shape)
    return pl.BlockSpec(shape, lambda *_: (0,) * nd, pipeline_mode=pl.Buffered(1))


def _q_perm():
    cols = np.arange(ATTN_WIDTH)
    j, half, d = cols // LANES, (cols % LANES) // HEAD_DIM, cols % HEAD_DIM
    return (j + Q_PER_KV * half) * HEAD_DIM + d


def _alibi_slopes():
    return 2.0 ** (-8.0 * np.arange(1, N_HEADS + 1) / N_HEADS)


def _prompt_bias():
    qi = np.arange(PAIR)[:, None]
    kj = np.arange(BAND)[None, :]
    dist = np.abs(qi + WINDOW - kj).astype(np.float64)
    cq, ck = qi // CHUNK, kj // CHUNK
    in_band = (ck >= cq) & (ck <= cq + WINDOW // CHUNK)
    base = -_alibi_slopes()[:, None, None] * dist[None]
    later = np.where(in_band[None], base, NEG_INF)
    first = np.where((kj >= WINDOW)[None], later, NEG_INF)
    return np.stack([first, later]).astype(np.float32)


def _sample_bias(n_new, n_cached):
    qi = np.arange(n_new)[:, None]
    dc = np.abs(qi + n_cached - np.arange(n_cached)[None, :]).astype(np.float64)
    dn = np.abs(qi - np.arange(n_new)[None, :]).astype(np.float64)
    sl = _alibi_slopes()[:, None, None]
    bc = (-sl * dc[None]).reshape(N_HEADS * n_new, n_cached)
    bn = (-sl * dn[None]).reshape(N_HEADS * n_new, n_new)
    return bc.astype(np.float32), bn.astype(np.float32)


def kernel(x_prompt, x_sample, cache_k_win, cache_v_win, g_mix, w_in, q_norm_g, k_norm_g, attn_sinks, sgu_norm_g,
           w_spatial, b_spatial, w_branch_attn, w_branch_sgu, w_out, g_ffn, w_router, b_router, w_gate_up,
           b_gate_up, w_down, b_down):
    n_b, seq, _ = x_prompt.shape
    n_streams, n_new, _ = x_sample.shape
    n_cached = cache_k_win.shape[2]
    n_prompt = n_b * seq
    n_sample = n_streams * n_new
    n_tok = n_prompt + n_sample
    assert seq % TOKEN_BLOCK == 0 and n_sample == TOKEN_BLOCK and n_cached == WINDOW
    blocks_per_seq = seq // TOKEN_BLOCK
    n_prompt_blocks = n_prompt // TOKEN_BLOCK

    perm = _q_perm()
    w_in_l = w_in[0]
    w_in_b = jnp.concatenate([w_in_l[:, perm], w_in_l[:, Q_END:]], axis=1).astype(BF16)
    wba_b = w_branch_attn[0][perm, :].astype(BF16)
    wbs_b = w_branch_sgu[0].astype(BF16)
    wout_b = w_out[0].astype(BF16)
    wr_b = w_router[0].astype(BF16)
    br = b_router[0].reshape(1, N_EXPERTS).astype(F32)
    gmix = g_mix[0].reshape(1, D_MODEL)
    gffn = g_ffn[0].reshape(1, D_MODEL)
    qg = jnp.tile(q_norm_g[0], LANES // HEAD_DIM).reshape(1, LANES)
    kg = jnp.tile(k_norm_g[0], LANES // HEAD_DIM).reshape(1, LANES)
    sgug = sgu_norm_g[0].reshape(1, MLP_WIDTH)
    sinks = attn_sinks[0].astype(F32)
    lane_seg = np.arange(LANES) // HEAD_DIM
    seg64 = jnp.asarray((lane_seg[:, None] == lane_seg[None, :]) / HEAD_DIM, BF16)
    seg128 = jnp.full((LANES, LANES), 1.0 / MLP_GROUP_DIM, BF16)
    tri = jnp.asarray(np.tril(np.ones((TOKEN_BLOCK, TOKEN_BLOCK)), -1), BF16)
    wsp = w_spatial[0]
    bsp_p = jnp.repeat(b_spatial[0].T, MLP_GROUP_DIM, axis=1)
    wsp_s = jnp.tile(wsp[:, :n_new, :n_new], (1, n_streams, n_streams))
    bsp_s = jnp.tile(jnp.repeat(b_spatial[0][:, :n_new].T, MLP_GROUP_DIM, axis=1), (n_streams, 1))
    bias_p = jnp.asarray(_prompt_bias())
    bias_c, bias_n = (jnp.asarray(a) for a in _sample_bias(n_new, n_cached))

    smem = pl.BlockSpec(memory_space=pltpu.SMEM)
    tb = TOKEN_BLOCK
    last_blk = n_prompt_blocks - 1
    seq_of = lambda i: jnp.minimum(i, last_blk) // blocks_per_seq
    x_spec = pl.BlockSpec((1, tb, D_MODEL), lambda i: (seq_of(i), jnp.minimum(i, last_blk) % blocks_per_seq, 0))
    tok_spec = lambda width: pl.BlockSpec((tb, width), lambda i: (i, 0))
    win_spec = lambda width: pl.BlockSpec((1, WINDOW, width), lambda i: (seq_of(i), 0, 0))
    route_shapes = [jax.ShapeDtypeStruct((n_tok, TOP_K), jnp.int32), jax.ShapeDtypeStruct((n_tok, TOP_K), F32),
                    jax.ShapeDtypeStruct((n_tok, TOP_K), jnp.int32)]

    x1p, h2, idx_p, gw_p, rank_p, cnt_p, kwin_p, vwin_p, sguv_p = pl.pallas_call(
        functools.partial(_mixer_prompt_body, blocks_per_seq=blocks_per_seq, n_blocks=n_prompt_blocks),
        grid=(n_prompt_blocks + 1,),
        in_specs=[smem, x_spec,
                  _const_spec((D_MODEL, IN_COLS)), _const_spec((1, D_MODEL)), _const_spec((1, LANES)),
                  _const_spec((1, LANES)), _const_spec((1, MLP_WIDTH)), _const_spec((LANES, LANES)),
                  _const_spec((LANES, LANES)), _const_spec((2, N_HEADS, PAIR, BAND)),
                  _const_spec((MLP_GROUPS, MLP_CHUNK, MLP_CHUNK)), _const_spec((MLP_CHUNK, MLP_WIDTH)),
                  _const_spec((ATTN_WIDTH, D_MODEL)), _const_spec((MLP_WIDTH, D_MODEL)),
                  _const_spec((D_MODEL, D_MODEL)), _const_spec((1, D_MODEL)), _const_spec((D_MODEL, N_EXPERTS)),
                  _const_spec((1, N_EXPERTS)), _const_spec((tb, tb))],
        out_specs=[x_spec,
                   tok_spec(D_MODEL), tok_spec(TOP_K), tok_spec(TOP_K), tok_spec(TOP_K),
                   pl.BlockSpec((1, N_EXPERTS), lambda i: (0, 0)),
                   win_spec(KV_WIDTH), win_spec(KV_WIDTH), win_spec(MLP_WIDTH)],
        out_shape=[jax.ShapeDtypeStruct((n_b, seq, D_MODEL), F32), jax.ShapeDtypeStruct((n_tok, D_MODEL), F32),
                   *route_shapes, jax.ShapeDtypeStruct((1, N_EXPERTS), F32),
                   jax.ShapeDtypeStruct((n_b, WINDOW, KV_WIDTH), F32),
                   jax.ShapeDtypeStruct((n_b, WINDOW, KV_WIDTH), F32),
                   jax.ShapeDtypeStruct((n_b, MLP_CHUNK, MLP_WIDTH), F32)],
        scratch_shapes=[pltpu.VMEM((WINDOW, KV_WIDTH), BF16), pltpu.VMEM((WINDOW, KV_WIDTH), BF16),
                        pltpu.VMEM((1, N_EXPERTS), F32)],
        compiler_params=pltpu.CompilerParams(dimension_semantics=("arbitrary",), vmem_limit_bytes=VMEM_LIMIT),
        name="mixer_prompt",
    )(sinks, x_prompt, w_in_b, gmix, qg, kg, sgug, seg64, seg128, bias_p, wsp, bsp_p, wba_b, wbs_b, wout_b,
      gffn, wr_b, br, tri)

    full = lambda shape: pl.BlockSpec(shape, lambda i: (0,) * len(shape))
    last_tok = lambda width: pl.BlockSpec((tb, width), lambda i: (n_prompt_blocks, 0))
    any_spec = pl.BlockSpec(memory_space=pl.ANY)
    ck = cache_k_win[0].reshape(n_streams, n_cached, KV_WIDTH)
    cv = cache_v_win[0].reshape(n_streams, n_cached, KV_WIDTH)
    x1s, h2, idx_a, gw_a, rank_a, cnt_s, kwin_s, vwin_s, sguv_s = pl.pallas_call(
        functools.partial(_mixer_sample_body, n_streams=n_streams, n_new=n_new),
        grid=(1,),
        in_specs=[smem, full((n_sample, D_MODEL)), full((n_streams, n_cached, KV_WIDTH)),
                  full((n_streams, n_cached, KV_WIDTH)),
                  full((D_MODEL, IN_COLS)), full((1, D_MODEL)), full((1, LANES)), full((1, LANES)),
                  full((1, MLP_WIDTH)), full((LANES, LANES)), full((LANES, LANES)),
                  full((N_HEADS * n_new, n_cached)), full((N_HEADS * n_new, n_new)),
                  full((MLP_GROUPS, n_sample, n_sample)), full((n_sample, MLP_WIDTH)),
                  full((ATTN_WIDTH, D_MODEL)), full((MLP_WIDTH, D_MODEL)), full((D_MODEL, D_MODEL)),
                  full((1, D_MODEL)), full((D_MODEL, N_EXPERTS)), full((1, N_EXPERTS)), full((tb, tb)),
                  any_spec, any_spec, any_spec, any_spec],
        out_specs=[full((n_sample, D_MODEL)), last_tok(D_MODEL), last_tok(TOP_K), last_tok(TOP_K), last_tok(TOP_K),
                   full((1, N_EXPERTS)), full((n_streams, n_cached, KV_WIDTH)),
                   full((n_streams, n_cached, KV_WIDTH)), full((n_sample, MLP_WIDTH))],
        out_shape=[jax.ShapeDtypeStruct((n_sample, D_MODEL), F32), jax.ShapeDtypeStruct((n_tok, D_MODEL), F32),
                   *route_shapes, jax.ShapeDtypeStruct((1, N_EXPERTS), F32),
                   jax.ShapeDtypeStruct((n_streams, n_cached, KV_WIDTH), F32),
                   jax.ShapeDtypeStruct((n_streams, n_cached, KV_WIDTH), F32),
                   jax.ShapeDtypeStruct((n_sample, MLP_WIDTH), F32)],
        input_output_aliases={22: 1, 23: 2, 24: 3, 25: 4},
        scratch_shapes=[pltpu.VMEM((1, N_EXPERTS), F32)],
        compiler_params=pltpu.CompilerParams(dimension_semantics=("arbitrary",), vmem_limit_bytes=VMEM_LIMIT),
        name="mixer_sample",
    )(sinks, x_sample.reshape(n_sample, D_MODEL), ck, cv, w_in_b, gmix, qg, kg, sgug, seg64, seg128, bias_c,
      bias_n, wsp_s, bsp_s, wba_b, wbs_b, wout_b, gffn, wr_b, br, tri, h2, idx_p, gw_p, rank_p)

    tm = EXPERT_TILE
    cnt_prompt = cnt_p[0].astype(jnp.int32)
    cnt_all = cnt_prompt + cnt_s[0].astype(jnp.int32)
    is_sample = (jnp.arange(n_tok) >= n_prompt)[:, None]
    rank_glob = rank_a + jnp.where(is_sample, cnt_prompt[idx_a], 0)
    padded = ((cnt_all + tm - 1) // tm) * tm
    pad_end = jnp.cumsum(padded)
    pad_off = pad_end - padded
    dest = (pad_off[idx_a] + rank_glob).reshape(-1)
    n_tiles = (n_tok * TOP_K) // tm + N_EXPERTS
    n_rows = n_tiles * tm
    tok_of_pair = jnp.repeat(jnp.arange(n_tok, dtype=jnp.int32), TOP_K)
    slot_of_pair = jnp.tile(jnp.arange(TOP_K, dtype=jnp.int32), n_tok) * n_tok + tok_of_pair
    tok_sorted = jnp.zeros((n_rows,), jnp.int32).at[dest].set(tok_of_pair, unique_indices=True)
    dst_sorted = jnp.zeros((n_rows,), jnp.int32).at[dest].set(slot_of_pair, unique_indices=True)
    n_used = (pad_end[-1] // tm).astype(jnp.int32)
    tile_ids = jnp.arange(n_tiles, dtype=jnp.int32)
    tile_start = jnp.minimum(tile_ids, n_used - 1) * tm
    tile_expert = jnp.sum((pad_end[None, :] <= tile_start[:, None]).astype(jnp.int32), axis=1)
    tile_expert = jnp.minimum(tile_expert, N_EXPERTS - 1)
    rows_end = (pad_off + cnt_all)[tile_expert]
    tile_valid = jnp.where(tile_ids < n_used, jnp.clip(rows_end - tile_start, 0, tm), 0).astype(jnp.int32)

    yg = pl.pallas_call(
        _expert_body,
        grid_spec=pltpu.PrefetchScalarGridSpec(
            num_scalar_prefetch=2,
            grid=(n_tiles,),
            in_specs=[pl.BlockSpec((1, 1, tm), lambda i, te, nu: (i, 0, 0), memory_space=pltpu.SMEM),
                      pl.BlockSpec((1, 1, tm), lambda i, te, nu: (i, 0, 0), memory_space=pltpu.SMEM),
                      pl.BlockSpec(memory_space=pl.ANY),
                      pl.BlockSpec((1, D_MODEL, 2 * D_EXPERT), lambda i, te, nu: (te[i], 0, 0)),
                      pl.BlockSpec((1, 1, 2 * D_EXPERT), lambda i, te, nu: (te[i], 0, 0)),
                      pl.BlockSpec((1, D_EXPERT, D_MODEL), lambda i, te, nu: (te[i], 0, 0)),
                      pl.BlockSpec((1, 1, D_MODEL), lambda i, te, nu: (te[i], 0, 0))],
            out_specs=pl.BlockSpec(memory_space=pl.ANY),
            scratch_shapes=[pltpu.VMEM((tm, D_MODEL), F32), pltpu.VMEM((tm, D_MODEL), F32),
                            pltpu.VMEM((D_MODEL, 2 * D_EXPERT), BF16), pltpu.VMEM((D_EXPERT, D_MODEL), BF16),
                            pltpu.SemaphoreType.DMA(()), pltpu.SemaphoreType.DMA(())]),
        out_shape=jax.ShapeDtypeStruct((TOP_K * n_tok, D_MODEL), F32),
        compiler_params=pltpu.CompilerParams(dimension_semantics=("arbitrary",), vmem_limit_bytes=VMEM_LIMIT),
        name="moe_experts",
    )(tile_expert, tile_valid, tok_sorted.reshape(n_tiles, 1, tm), dst_sorted.reshape(n_tiles, 1, tm),
      h2, w_gate_up[0], b_gate_up[0].reshape(N_EXPERTS, 1, 2 * D_EXPERT), w_down[0],
      b_down[0].reshape(N_EXPERTS, 1, D_MODEL))

    n_tok_blocks = n_tok // tb

    def combine(x1, first_block, n_blocks):
        y_spec = lambda k: pl.BlockSpec((tb, D_MODEL), lambda i: (k * n_tok_blocks + first_block + i, 0))
        return pl.pallas_call(
            _combine_body,
            grid=(n_blocks,),
            in_specs=[pl.BlockSpec((tb, D_MODEL), lambda i: (i, 0)),
                      pl.BlockSpec((tb, TOP_K), lambda i: (first_block + i, 0)),
                      y_spec(0), y_spec(1), y_spec(2), y_spec(3)],
            out_specs=pl.BlockSpec((tb, D_MODEL), lambda i: (i, 0)),
            out_shape=jax.ShapeDtypeStruct(x1.shape, F32),
            compiler_params=pltpu.CompilerParams(dimension_semantics=("arbitrary",)),
            name="moe_combine",
        )(x1, gw_a, yg, yg, yg, yg)

    y_prompt = combine(x1p.reshape(n_prompt, D_MODEL), 0, n_prompt_blocks).reshape(n_b, seq, D_MODEL)
    y_sample = combine(x1s, n_prompt_blocks, 1).reshape(n_streams, n_new, D_MODEL)

    kv_shape = (N_KV_HEADS, HEAD_DIM)
    sg_shape = (MLP_GROUPS, MLP_GROUP_DIM)
    return (y_prompt, y_sample,
            kwin_p.reshape(1, n_b, WINDOW, *kv_shape), vwin_p.reshape(1, n_b, WINDOW, *kv_shape),
            kwin_s.reshape(1, n_streams, n_cached, *kv_shape), vwin_s.reshape(1, n_streams, n_cached, *kv_shape),
            sguv_p.reshape(1, n_b, MLP_CHUNK, *sg_shape), sguv_s.reshape(1, n_streams, n_new, *sg_shape))
```

```python
import functools

import jax
import jax.numpy as jnp
import numpy as np
from jax import lax
from jax.experimental import pallas as pl
from jax.experimental.pallas import tpu as pltpu

D_MODEL = 1024
CHUNK = 64
WINDOW = 128
HEAD_DIM = 64
N_HEADS = 8
N_KV_HEADS = 2
Q_PER_KV = N_HEADS // N_KV_HEADS
ATTN_WIDTH = N_HEADS * HEAD_DIM
KV_WIDTH = N_KV_HEADS * HEAD_DIM
ATTN_SCALE = HEAD_DIM ** -0.5
MLP_CHUNK = 128
MLP_GROUPS = 8
MLP_WIDTH = D_MODEL
MLP_GROUP_DIM = MLP_WIDTH // MLP_GROUPS
N_EXPERTS = 32
TOP_K = 4
D_EXPERT = D_MODEL
SWIGLU_ALPHA = 1.702
SWIGLU_LIMIT = 7.0
EPS = 1e-6
NEG_INF = -1e30
Q_END = ATTN_WIDTH
K_END = Q_END + KV_WIDTH
V_END = K_END + KV_WIDTH
U_END = V_END + MLP_WIDTH
VM_END = U_END + MLP_WIDTH
GA_END = VM_END + D_MODEL
IN_COLS = GA_END + D_MODEL

LANES = 128
TOKEN_BLOCK = 256
PAIR = 2 * CHUNK
BAND = PAIR + WINDOW
EXPERT_TILE = 256
VMEM_LIMIT = 56 * 1024 * 1024

F32 = jnp.float32
BF16 = jnp.bfloat16


def _dot(a, b):
    return jnp.dot(a, b, preferred_element_type=F32)


def _dot_nt(a, b):
    return lax.dot_general(a, b, (((1,), (1,)), ((), ())), preferred_element_type=F32)


def _segment_mean(sq, seg):
    hi = sq.astype(BF16)
    lo = (sq - hi.astype(F32)).astype(BF16)
    return _dot(hi, seg) + _dot(lo, seg)


def _rms_rows(x, gain):
    ms = jnp.mean(x * x, axis=-1, keepdims=True)
    return (x * lax.rsqrt(ms + EPS)) * gain


def _project(h, w_in_ref, qg, kg, sgug, seg64, seg128):
    qkv = _dot(h, w_in_ref[:, 0:V_END])
    qk_cols = []
    for c in range(K_END // LANES):
        blk = qkv[:, c * LANES:(c + 1) * LANES]
        ms = _segment_mean(blk * blk, seg64)
        g = qg if c < Q_END // LANES else kg
        qk_cols.append((blk * lax.rsqrt(ms + EPS)) * g)
    qn = jnp.concatenate(qk_cols[:Q_END // LANES], axis=-1)
    kn = qk_cols[Q_END // LANES]
    v = qkv[:, K_END:V_END]
    u = jax.nn.gelu(_dot(h, w_in_ref[:, V_END:U_END]))
    vg = jax.nn.gelu(_dot(h, w_in_ref[:, U_END:VM_END]))
    vm_cols = []
    for g in range(MLP_GROUPS):
        blk = vg[:, g * LANES:(g + 1) * LANES]
        ms = _segment_mean(blk * blk, seg128)
        vm_cols.append((blk * lax.rsqrt(ms + EPS)) * sgug[:, g * LANES:(g + 1) * LANES])
    vm = jnp.concatenate(vm_cols, axis=-1)
    ga = jax.nn.sigmoid(_dot(h, w_in_ref[:, VM_END:GA_END]))
    gb = jax.nn.sigmoid(_dot(h, w_in_ref[:, GA_END:IN_COLS]))
    return qn, kn, v, u, vm, ga, gb


def _stack_heads(q_rows):
    lane = lax.broadcasted_iota(jnp.int32, (q_rows.shape[0], LANES), 1)
    blocks = []
    for head in range(N_HEADS):
        j, half = head % Q_PER_KV, head // Q_PER_KV
        col = q_rows[:, j * LANES:(j + 1) * LANES]
        keep = (lane < HEAD_DIM) if half == 0 else (lane >= HEAD_DIM)
        blocks.append(jnp.where(keep, col, 0.0))
    return jnp.concatenate(blocks, axis=0).astype(BF16)


def _unstack_heads(o, rows):
    lane = lax.broadcasted_iota(jnp.int32, (rows, LANES), 1)
    cols = []
    for j in range(Q_PER_KV):
        lo = o[j * rows:(j + 1) * rows]
        hi = o[(j + Q_PER_KV) * rows:(j + Q_PER_KV + 1) * rows]
        cols.append(jnp.where(lane < HEAD_DIM, lo, hi))
    return jnp.concatenate(cols, axis=-1)


def _merge(att, sgu, ga, gb, wba_ref, wbs_ref, wout_ref):
    m = ga * _dot(att.astype(BF16), wba_ref[...]) + gb * _dot(sgu.astype(BF16), wbs_ref[...])
    return _dot(m.astype(BF16), wout_ref[...])


def _route(h2, wr_ref, br_ref, tri_ref, run_ref):
    rows = h2.shape[0]
    logits = _dot(h2, wr_ref[...]) + br_ref[...]
    eidx = lax.broadcasted_iota(jnp.int32, (rows, N_EXPERTS), 1).astype(F32)
    work = logits
    vals, picks, onehots = [], [], []
    for _ in range(TOP_K):
        m = jnp.max(work, axis=-1, keepdims=True)
        sel = jnp.min(jnp.where(work == m, eidx, float(N_EXPERTS)), axis=-1, keepdims=True)
        oh = eidx == sel
        vals.append(m)
        picks.append(sel)
        onehots.append(oh)
        work = jnp.where(oh, -jnp.inf, work)
    exps = [jnp.exp(v - vals[0]) for v in vals]
    den = exps[0] + exps[1] + exps[2] + exps[3]
    mask = jnp.zeros((rows, N_EXPERTS), F32)
    for oh in onehots:
        mask = mask + jnp.where(oh, 1.0, 0.0)
    before = _dot(tri_ref[...], mask.astype(BF16)) + run_ref[...]
    ranks = [jnp.sum(jnp.where(oh, before, 0.0), axis=-1, keepdims=True) for oh in onehots]
    run_ref[...] = run_ref[...] + jnp.sum(mask, axis=0, keepdims=True)
    k4 = lax.broadcasted_iota(jnp.int32, (rows, TOP_K), 1)
    idx4 = jnp.zeros((rows, TOP_K), jnp.int32)
    w4 = jnp.zeros((rows, TOP_K), F32)
    r4 = jnp.zeros((rows, TOP_K), jnp.int32)
    for k in range(TOP_K):
        idx4 = jnp.where(k4 == k, picks[k].astype(jnp.int32), idx4)
        w4 = jnp.where(k4 == k, exps[k] / den, w4)
        r4 = jnp.where(k4 == k, ranks[k].astype(jnp.int32), r4)
    return idx4, w4, r4


def _softmax_rows(parts, sink):
    m = sink
    for l in parts:
        m = jnp.maximum(m, jnp.max(l, axis=-1, keepdims=True))
    es = [jnp.exp(l - m) for l in parts]
    den = jnp.exp(sink - m)
    for e in es:
        den = den + jnp.sum(e, axis=-1, keepdims=True)
    return [e / den for e in es]


def _mixer_prompt_body(sinks_ref, x_ref, w_in_ref, gmix_ref, qg_ref, kg_ref, sgug_ref, seg64_ref, seg128_ref,
                       bias_ref, wsp_ref, bsp_ref, wba_ref, wbs_ref, wout_ref, gffn_ref, wr_ref, br_ref, tri_ref,
                       x1_ref, h2_ref, idx_ref, gw_ref, rank_ref, cnt_ref, kwin_ref, vwin_ref, sguv_ref,
                       kcarry, vcarry, run_ref, *, blocks_per_seq, n_blocks):
    i = pl.program_id(0)

    @pl.when(i == 0)
    def _():
        run_ref[...] = jnp.zeros_like(run_ref)

    @pl.when(i == n_blocks)
    def _():
        h2_ref[...] = jnp.zeros_like(h2_ref)
        idx_ref[...] = jnp.zeros_like(idx_ref)
        gw_ref[...] = jnp.zeros_like(gw_ref)
        rank_ref[...] = jnp.zeros_like(rank_ref)

    @pl.when(i < n_blocks)
    def _():
        _mixer_prompt_block(i % blocks_per_seq, sinks_ref, x_ref, w_in_ref, gmix_ref, qg_ref, kg_ref, sgug_ref,
                            seg64_ref, seg128_ref, bias_ref, wsp_ref, bsp_ref, wba_ref, wbs_ref, wout_ref, gffn_ref,
                            wr_ref, br_ref, tri_ref, x1_ref, h2_ref, idx_ref, gw_ref, rank_ref, cnt_ref, kwin_ref,
                            vwin_ref, sguv_ref, kcarry, vcarry, run_ref)


def _mixer_prompt_block(j, sinks_ref, x_ref, w_in_ref, gmix_ref, qg_ref, kg_ref, sgug_ref, seg64_ref, seg128_ref,
                        bias_ref, wsp_ref, bsp_ref, wba_ref, wbs_ref, wout_ref, gffn_ref, wr_ref, br_ref, tri_ref,
                        x1_ref, h2_ref, idx_ref, gw_ref, rank_ref, cnt_ref, kwin_ref, vwin_ref, sguv_ref,
                        kcarry, vcarry, run_ref):
    @pl.when(j == 0)
    def _():
        kcarry[...] = jnp.zeros_like(kcarry)
        vcarry[...] = jnp.zeros_like(vcarry)

    x = x_ref[0]
    h = _rms_rows(x, gmix_ref[...]).astype(BF16)
    qn, kn, v, u, vm, ga, gb = _project(h, w_in_ref, qg_ref[...], kg_ref[...], sgug_ref[...],
                                        seg64_ref[...], seg128_ref[...])
    k_ext = jnp.concatenate([kcarry[...], kn.astype(BF16)], axis=0)
    v_ext = jnp.concatenate([vcarry[...], v.astype(BF16)], axis=0)
    kcarry[...] = k_ext[TOKEN_BLOCK:]
    vcarry[...] = v_ext[TOKEN_BLOCK:]

    tri_mask = (lax.broadcasted_iota(jnp.int32, (MLP_CHUNK, MLP_CHUNK), 0)
                >= lax.broadcasted_iota(jnp.int32, (MLP_CHUNK, MLP_CHUNK), 1))
    att_rows, sgu_rows = [], []
    for pm in range(TOKEN_BLOCK // PAIR):
        r0 = pm * PAIR
        q_stack = _stack_heads(qn[r0:r0 + PAIR])
        k_band = k_ext[r0:r0 + BAND]
        v_band = v_ext[r0:r0 + BAND]
        s = _dot_nt(q_stack, k_band)
        first = jnp.where(j == 0, 0, 1) if pm == 0 else 1
        probs = []
        for head in range(N_HEADS):
            logit = s[head * PAIR:(head + 1) * PAIR] * ATTN_SCALE + bias_ref[first, head]
            probs.append(_softmax_rows([logit], sinks_ref[head])[0].astype(BF16))
        o = _dot(jnp.concatenate(probs, axis=0), v_band)
        att_rows.append(_unstack_heads(o, PAIR))
        cols = []
        for g in range(MLP_GROUPS):
            wm = jnp.where(tri_mask, wsp_ref[g], 0.0).astype(BF16)
            cols.append(_dot(wm, vm[r0:r0 + PAIR, g * LANES:(g + 1) * LANES].astype(BF16)))
        mixed = jnp.concatenate(cols, axis=-1) + bsp_ref[...]
        sgu_rows.append(u[r0:r0 + PAIR] * mixed)
    att = jnp.concatenate(att_rows, axis=0)
    sgu = jnp.concatenate(sgu_rows, axis=0)

    x1 = x + _merge(att, sgu, ga, gb, wba_ref, wbs_ref, wout_ref)
    h2 = _rms_rows(x1, gffn_ref[...])
    idx4, w4, r4 = _route(h2.astype(BF16), wr_ref, br_ref, tri_ref, run_ref)

    x1_ref[0] = x1
    h2_ref[...] = h2
    idx_ref[...] = idx4
    gw_ref[...] = w4
    rank_ref[...] = r4
    cnt_ref[...] = run_ref[...]
    kwin_ref[0] = kn[TOKEN_BLOCK - WINDOW:]
    vwin_ref[0] = v[TOKEN_BLOCK - WINDOW:]
    sguv_ref[0] = vm[TOKEN_BLOCK - MLP_CHUNK:]


def _mixer_sample_body(sinks_ref, x_ref, ck_ref, cv_ref, w_in_ref, gmix_ref, qg_ref, kg_ref, sgug_ref, seg64_ref,
                       seg128_ref, biasc_ref, biasn_ref, wsp_ref, bsp_ref, wba_ref, wbs_ref, wout_ref, gffn_ref,
                       wr_ref, br_ref, tri_ref, h2_in_ref, idx_in_ref, gw_in_ref, rank_in_ref,
                       x1_ref, h2_ref, idx_ref, gw_ref, rank_ref, cnt_ref, kwin_ref, vwin_ref, sguv_ref,
                       run_ref, *, n_streams, n_new):
    del h2_in_ref, idx_in_ref, gw_in_ref, rank_in_ref
    run_ref[...] = jnp.zeros_like(run_ref)
    x = x_ref[...]
    h = _rms_rows(x, gmix_ref[...]).astype(BF16)
    qn, kn, v, u, vm, ga, gb = _project(h, w_in_ref, qg_ref[...], kg_ref[...], sgug_ref[...],
                                        seg64_ref[...], seg128_ref[...])
    n_cached = ck_ref.shape[1]
    att_rows = []
    for s_i in range(n_streams):
        r0 = s_i * n_new
        q_stack = _stack_heads(qn[r0:r0 + n_new])
        k_new = kn[r0:r0 + n_new]
        v_new = v[r0:r0 + n_new]
        s_c = _dot_nt(q_stack, ck_ref[s_i].astype(BF16))
        s_n = _dot_nt(q_stack, k_new.astype(BF16))
        pc, pn = [], []
        for head in range(N_HEADS):
            rows = slice(head * n_new, (head + 1) * n_new)
            lc = s_c[rows] * ATTN_SCALE + biasc_ref[rows]
            ln = s_n[rows] * ATTN_SCALE + biasn_ref[rows]
            p_c, p_n = _softmax_rows([lc, ln], sinks_ref[head])
            pc.append(p_c.astype(BF16))
            pn.append(p_n.astype(BF16))
        o = (_dot(jnp.concatenate(pc, axis=0), cv_ref[s_i].astype(BF16))
             + _dot(jnp.concatenate(pn, axis=0), v_new.astype(BF16)))
        att_rows.append(_unstack_heads(o, n_new))
        kwin_ref[s_i, 0:n_cached - n_new] = ck_ref[s_i, n_new:n_cached]
        kwin_ref[s_i, n_cached - n_new:n_cached] = k_new
        vwin_ref[s_i, 0:n_cached - n_new] = cv_ref[s_i, n_new:n_cached]
        vwin_ref[s_i, n_cached - n_new:n_cached] = v_new
    att = jnp.concatenate(att_rows, axis=0)

    rows = n_streams * n_new
    ri = lax.broadcasted_iota(jnp.int32, (rows, rows), 0)
    ci = lax.broadcasted_iota(jnp.int32, (rows, rows), 1)
    keep = jnp.logical_and(ri // n_new == ci // n_new, ri % n_new >= ci % n_new)
    cols = []
    for g in range(MLP_GROUPS):
        wm = jnp.where(keep, wsp_ref[g], 0.0).astype(BF16)
        cols.append(_dot(wm, vm[:, g * LANES:(g + 1) * LANES].astype(BF16)))
    sgu = u * (jnp.concatenate(cols, axis=-1) + bsp_ref[...])

    x1 = x + _merge(att, sgu, ga, gb, wba_ref, wbs_ref, wout_ref)
    h2 = _rms_rows(x1, gffn_ref[...])
    idx4, w4, r4 = _route(h2.astype(BF16), wr_ref, br_ref, tri_ref, run_ref)
    x1_ref[...] = x1
    h2_ref[...] = h2
    idx_ref[...] = idx4
    gw_ref[...] = w4
    rank_ref[...] = r4
    cnt_ref[...] = run_ref[...]
    sguv_ref[...] = vm


def _row_copy_in(h2_hbm, tok_ref, xbuf, sem, r):
    return pltpu.make_async_copy(h2_hbm.at[pl.ds(tok_ref[0, 0, r], 1), :], xbuf.at[pl.ds(r, 1), :], sem)


def _row_copy_out(ybuf, dst_ref, yg_hbm, sem, r):
    return pltpu.make_async_copy(ybuf.at[pl.ds(r, 1), :], yg_hbm.at[pl.ds(dst_ref[0, 0, r], 1), :], sem)


def _expert_tile(tok_next_ref, dst_prev_ref, h2_hbm, bgu_ref, bd_ref, yg_hbm, wgu_bf, wd_bf,
                 x_cur, x_next, y_cur, y_prev, sem_in_cur, sem_in_next, sem_out):
    pltpu.make_async_copy(h2_hbm.at[pl.ds(0, EXPERT_TILE), :], x_cur, sem_in_cur).wait()
    for r in range(EXPERT_TILE):
        _row_copy_in(h2_hbm, tok_next_ref, x_next, sem_in_next, r).start()
    for r in range(EXPERT_TILE):
        _row_copy_out(y_prev, dst_prev_ref, yg_hbm, sem_out, r).start()
    xb = x_cur[...].astype(BF16)
    hgu = _dot(xb, wgu_bf[...]) + bgu_ref[0]
    glu = jnp.minimum(hgu[:, :D_EXPERT], SWIGLU_LIMIT)
    lin = jnp.clip(hgu[:, D_EXPERT:], -SWIGLU_LIMIT, SWIGLU_LIMIT)
    act = glu * jax.nn.sigmoid(glu * SWIGLU_ALPHA) * (lin + 1.0)
    y_cur[...] = _dot(act.astype(BF16), wd_bf[...]) + bd_ref[0]
    pltpu.make_async_copy(y_prev, yg_hbm.at[pl.ds(0, EXPERT_TILE), :], sem_out).wait()


def _expert_body(te_ref, nused_ref, tok_first_ref, tok_next_ref, dst_prev_ref, h2_hbm, wgu_ref, bgu_ref, wd_ref,
                 bd_ref, yg_hbm, xbuf0, xbuf1, ybuf0, ybuf1, wgu_bf, wd_bf, sem_in, sem_out):
    i = pl.program_id(0)
    n_used = nused_ref[0]
    xbufs, ybufs = (xbuf0, xbuf1), (ybuf0, ybuf1)

    @pl.when(i == 0)
    def _():
        ybuf1[...] = jnp.zeros_like(ybuf1)

        def first_rows(r, carry):
            _row_copy_in(h2_hbm, tok_first_ref, xbuf0, sem_in.at[0], r).start()
            return carry
        lax.fori_loop(0, EXPERT_TILE, first_rows, 0)

    @pl.when(jnp.logical_and(i < n_used, jnp.logical_or(i == 0, te_ref[i] != te_ref[jnp.maximum(i - 1, 0)])))
    def _():
        wgu_bf[...] = wgu_ref[0].astype(BF16)
        wd_bf[...] = wd_ref[0].astype(BF16)

    for par in range(2):
        @pl.when(jnp.logical_and(i < n_used, i % 2 == par))
        def _(par=par):
            _expert_tile(tok_next_ref, dst_prev_ref, h2_hbm, bgu_ref, bd_ref, yg_hbm, wgu_bf, wd_bf,
                         xbufs[par], xbufs[1 - par], ybufs[par], ybufs[1 - par],
                         sem_in.at[par], sem_in.at[1 - par], sem_out)

        @pl.when(jnp.logical_and(i == n_used, i % 2 == par))
        def _(par=par):
            pltpu.make_async_copy(h2_hbm.at[pl.ds(0, EXPERT_TILE), :], xbufs[par], sem_in.at[par]).wait()

            def last_rows(r, carry):
                _row_copy_out(ybufs[1 - par], dst_prev_ref, yg_hbm, sem_out, r).start()
                return carry
            lax.fori_loop(0, EXPERT_TILE, last_rows, 0)
            pltpu.make_async_copy(ybufs[1 - par], yg_hbm.at[pl.ds(0, EXPERT_TILE), :], sem_out).wait()


def _combine_body(x1_ref, gw_ref, y0_ref, y1_ref, y2_ref, y3_ref, out_ref):
    gw = gw_ref[...]
    acc = x1_ref[...]
    for k, y_ref in enumerate((y0_ref, y1_ref, y2_ref, y3_ref)):
        acc = acc + gw[:, k:k + 1] * y_ref[...]
    out_ref[...] = acc


def _const_spec(shape):
    nd = len(shape)
    return pl.BlockSpec(shape, lambda *_: (0,) * nd, pipeline_mode=pl.Buffered(1))


def _q_perm():
    cols = np.arange(ATTN_WIDTH)
    j, half, d = cols // LANES, (cols % LANES) // HEAD_DIM, cols % HEAD_DIM
    return (j + Q_PER_KV * half) * HEAD_DIM + d


def _alibi_slopes():
    return 2.0 ** (-8.0 * np.arange(1, N_HEADS + 1) / N_HEADS)


def _prompt_bias():
    qi = np.arange(PAIR)[:, None]
    kj = np.arange(BAND)[None, :]
    dist = np.abs(qi + WINDOW - kj).astype(np.float64)
    cq, ck = qi // CHUNK, kj // CHUNK
    in_band = (ck >= cq) & (ck <= cq + WINDOW // CHUNK)
    base = -_alibi_slopes()[:, None, None] * dist[None]
    later = np.where(in_band[None], base, NEG_INF)
    first = np.where((kj >= WINDOW)[None], later, NEG_INF)
    return np.stack([first, later]).astype(np.float32)


def _sample_bias(n_new, n_cached):
    qi = np.arange(n_new)[:, None]
    dc = np.abs(qi + n_cached - np.arange(n_cached)[None, :]).astype(np.float64)
    dn = np.abs(qi - np.arange(n_new)[None, :]).astype(np.float64)
    sl = _alibi_slopes()[:, None, None]
    bc = (-sl * dc[None]).reshape(N_HEADS * n_new, n_cached)
    bn = (-sl * dn[None]).reshape(N_HEADS * n_new, n_new)
    return bc.astype(np.float32), bn.astype(np.float32)


def kernel(x_prompt, x_sample, cache_k_win, cache_v_win, g_mix, w_in, q_norm_g, k_norm_g, attn_sinks, sgu_norm_g,
           w_spatial, b_spatial, w_branch_attn, w_branch_sgu, w_out, g_ffn, w_router, b_router, w_gate_up,
           b_gate_up, w_down, b_down):
    n_b, seq, _ = x_prompt.shape
    n_streams, n_new, _ = x_sample.shape
    n_cached = cache_k_win.shape[2]
    n_prompt = n_b * seq
    n_sample = n_streams * n_new
    n_tok = n_prompt + n_sample
    assert seq % TOKEN_BLOCK == 0 and n_sample == TOKEN_BLOCK and n_cached == WINDOW
    blocks_per_seq = seq // TOKEN_BLOCK
    n_prompt_blocks = n_prompt // TOKEN_BLOCK

    perm = _q_perm()
    w_in_l = w_in[0]
    w_in_b = jnp.concatenate([w_in_l[:, perm], w_in_l[:, Q_END:]], axis=1).astype(BF16)
    wba_b = w_branch_attn[0][perm, :].astype(BF16)
    wbs_b = w_branch_sgu[0].astype(BF16)
    wout_b = w_out[0].astype(BF16)
    wr_b = w_router[0].astype(BF16)
    br = b_router[0].reshape(1, N_EXPERTS).astype(F32)
    gmix = g_mix[0].reshape(1, D_MODEL)
    gffn = g_ffn[0].reshape(1, D_MODEL)
    qg = jnp.tile(q_norm_g[0], LANES // HEAD_DIM).reshape(1, LANES)
    kg = jnp.tile(k_norm_g[0], LANES // HEAD_DIM).reshape(1, LANES)
    sgug = sgu_norm_g[0].reshape(1, MLP_WIDTH)
    sinks = attn_sinks[0].astype(F32)
    lane_seg = np.arange(LANES) // HEAD_DIM
    seg64 = jnp.asarray((lane_seg[:, None] == lane_seg[None, :]) / HEAD_DIM, BF16)
    seg128 = jnp.full((LANES, LANES), 1.0 / MLP_GROUP_DIM, BF16)
    tri = jnp.asarray(np.tril(np.ones((TOKEN_BLOCK, TOKEN_BLOCK)), -1), BF16)
    wsp = w_spatial[0]
    bsp_p = jnp.repeat(b_spatial[0].T, MLP_GROUP_DIM, axis=1)
    wsp_s = jnp.tile(wsp[:, :n_new, :n_new], (1, n_streams, n_streams))
    bsp_s = jnp.tile(jnp.repeat(b_spatial[0][:, :n_new].T, MLP_GROUP_DIM, axis=1), (n_streams, 1))
    bias_p = jnp.asarray(_prompt_bias())
    bias_c, bias_n = (jnp.asarray(a) for a in _sample_bias(n_new, n_cached))

    smem = pl.BlockSpec(memory_space=pltpu.SMEM)
    tb = TOKEN_BLOCK
    last_blk = n_prompt_blocks - 1
    seq_of = lambda i: jnp.minimum(i, last_blk) // blocks_per_seq
    x_spec = pl.BlockSpec((1, tb, D_MODEL), lambda i: (seq_of(i), jnp.minimum(i, last_blk) % blocks_per_seq, 0))
    tok_spec = lambda width: pl.BlockSpec((tb, width), lambda i: (i, 0))
    win_spec = lambda width: pl.BlockSpec((1, WINDOW, width), lambda i: (seq_of(i), 0, 0))
    route_shapes = [jax.ShapeDtypeStruct((n_tok, TOP_K), jnp.int32), jax.ShapeDtypeStruct((n_tok, TOP_K), F32),
                    jax.ShapeDtypeStruct((n_tok, TOP_K), jnp.int32)]

    x1p, h2, idx_p, gw_p, rank_p, cnt_p, kwin_p, vwin_p, sguv_p = pl.pallas_call(
        functools.partial(_mixer_prompt_body, blocks_per_seq=blocks_per_seq, n_blocks=n_prompt_blocks),
        grid=(n_prompt_blocks + 1,),
        in_specs=[smem, x_spec,
                  _const_spec((D_MODEL, IN_COLS)), _const_spec((1, D_MODEL)), _const_spec((1, LANES)),
                  _const_spec((1, LANES)), _const_spec((1, MLP_WIDTH)), _const_spec((LANES, LANES)),
                  _const_spec((LANES, LANES)), _const_spec((2, N_HEADS, PAIR, BAND)),
                  _const_spec((MLP_GROUPS, MLP_CHUNK, MLP_CHUNK)), _const_spec((MLP_CHUNK, MLP_WIDTH)),
                  _const_spec((ATTN_WIDTH, D_MODEL)), _const_spec((MLP_WIDTH, D_MODEL)),
                  _const_spec((D_MODEL, D_MODEL)), _const_spec((1, D_MODEL)), _const_spec((D_MODEL, N_EXPERTS)),
                  _const_spec((1, N_EXPERTS)), _const_spec((tb, tb))],
        out_specs=[x_spec,
                   tok_spec(D_MODEL), tok_spec(TOP_K), tok_spec(TOP_K), tok_spec(TOP_K),
                   pl.BlockSpec((1, N_EXPERTS), lambda i: (0, 0)),
                   win_spec(KV_WIDTH), win_spec(KV_WIDTH), win_spec(MLP_WIDTH)],
        out_shape=[jax.ShapeDtypeStruct((n_b, seq, D_MODEL), F32), jax.ShapeDtypeStruct((n_tok, D_MODEL), F32),
                   *route_shapes, jax.ShapeDtypeStruct((1, N_EXPERTS), F32),
                   jax.ShapeDtypeStruct((n_b, WINDOW, KV_WIDTH), F32),
                   jax.ShapeDtypeStruct((n_b, WINDOW, KV_WIDTH), F32),
                   jax.ShapeDtypeStruct((n_b, MLP_CHUNK, MLP_WIDTH), F32)],
        scratch_shapes=[pltpu.VMEM((WINDOW, KV_WIDTH), BF16), pltpu.VMEM((WINDOW, KV_WIDTH), BF16),
                        pltpu.VMEM((1, N_EXPERTS), F32)],
        compiler_params=pltpu.CompilerParams(dimension_semantics=("arbitrary",), vmem_limit_bytes=VMEM_LIMIT),
        name="mixer_prompt",
    )(sinks, x_prompt, w_in_b, gmix, qg, kg, sgug, seg64, seg128, bias_p, wsp, bsp_p, wba_b, wbs_b, wout_b,
      gffn, wr_b, br, tri)

    full = lambda shape: pl.BlockSpec(shape, lambda i: (0,) * len(shape))
    last_tok = lambda width: pl.BlockSpec((tb, width), lambda i: (n_prompt_blocks, 0))
    any_spec = pl.BlockSpec(memory_space=pl.ANY)
    ck = cache_k_win[0].reshape(n_streams, n_cached, KV_WIDTH)
    cv = cache_v_win[0].reshape(n_streams, n_cached, KV_WIDTH)
    x1s, h2, idx_a, gw_a, rank_a, cnt_s, kwin_s, vwin_s, sguv_s = pl.pallas_call(
        functools.partial(_mixer_sample_body, n_streams=n_streams, n_new=n_new),
        grid=(1,),
        in_specs=[smem, full((n_sample, D_MODEL)), full((n_streams, n_cached, KV_WIDTH)),
                  full((n_streams, n_cached, KV_WIDTH)),
                  full((D_MODEL, IN_COLS)), full((1, D_MODEL)), full((1, LANES)), full((1, LANES)),
                  full((1, MLP_WIDTH)), full((LANES, LANES)), full((LANES, LANES)),
                  full((N_HEADS * n_new, n_cached)), full((N_HEADS * n_new, n_new)),
                  full((MLP_GROUPS, n_sample, n_sample)), full((n_sample, MLP_WIDTH)),
                  full((ATTN_WIDTH, D_MODEL)), full((MLP_WIDTH, D_MODEL)), full((D_MODEL, D_MODEL)),
                  full((1, D_MODEL)), full((D_MODEL, N_EXPERTS)), full((1, N_EXPERTS)), full((tb, tb)),
                  any_spec, any_spec, any_spec, any_spec],
        out_specs=[full((n_sample, D_MODEL)), last_tok(D_MODEL), last_tok(TOP_K), last_tok(TOP_K), last_tok(TOP_K),
                   full((1, N_EXPERTS)), full((n_streams, n_cached, KV_WIDTH)),
                   full((n_streams, n_cached, KV_WIDTH)), full((n_sample, MLP_WIDTH))],
        out_shape=[jax.ShapeDtypeStruct((n_sample, D_MODEL), F32), jax.ShapeDtypeStruct((n_tok, D_MODEL), F32),
                   *route_shapes, jax.ShapeDtypeStruct((1, N_EXPERTS), F32),
                   jax.ShapeDtypeStruct((n_streams, n_cached, KV_WIDTH), F32),
                   jax.ShapeDtypeStruct((n_streams, n_cached, KV_WIDTH), F32),
                   jax.ShapeDtypeStruct((n_sample, MLP_WIDTH), F32)],
        input_output_aliases={22: 1, 23: 2, 24: 3, 25: 4},
        scratch_shapes=[pltpu.VMEM((1, N_EXPERTS), F32)],
        compiler_params=pltpu.CompilerParams(dimension_semantics=("arbitrary",), vmem_limit_bytes=VMEM_LIMIT),
        name="mixer_sample",
    )(sinks, x_sample.reshape(n_sample, D_MODEL), ck, cv, w_in_b, gmix, qg, kg, sgug, seg64, seg128, bias_c,
      bias_n, wsp_s, bsp_s, wba_b, wbs_b, wout_b, gffn, wr_b, br, tri, h2, idx_p, gw_p, rank_p)

    tm = EXPERT_TILE
    cnt_prompt = cnt_p[0].astype(jnp.int32)
    cnt_all = cnt_prompt + cnt_s[0].astype(jnp.int32)
    is_sample = (jnp.arange(n_tok) >= n_prompt)[:, None]
    rank_glob = rank_a + jnp.where(is_sample, cnt_prompt[idx_a], 0)
    padded = ((cnt_all + tm - 1) // tm) * tm
    pad_end = jnp.cumsum(padded)
    pad_off = pad_end - padded
    dest = (pad_off[idx_a] + rank_glob).reshape(-1)
    n_tiles = (n_tok * TOP_K) // tm + N_EXPERTS
    n_rows = n_tiles * tm
    tok_of_pair = jnp.repeat(jnp.arange(n_tok, dtype=jnp.int32), TOP_K)
    slot_of_pair = jnp.tile(jnp.arange(TOP_K, dtype=jnp.int32), n_tok) * n_tok + tok_of_pair
    n_pairs = TOP_K * n_tok
    spare = n_pairs + jnp.arange(tm, dtype=jnp.int32)
    dst_sorted = jnp.tile(spare, n_tiles).at[dest].set(slot_of_pair, unique_indices=True)
    tok_sorted = jnp.where(dst_sorted < n_pairs, dst_sorted % n_tok, 0)
    tok_tiles = tok_sorted.reshape(n_tiles, 1, tm)
    n_steps = n_tiles + 1
    tok_next = jnp.concatenate([tok_tiles[1:], jnp.zeros((2, 1, tm), jnp.int32)], axis=0)
    dst_prev = jnp.concatenate([spare.reshape(1, 1, tm), dst_sorted.reshape(n_tiles, 1, tm)], axis=0)
    n_used = (pad_end[-1] // tm).astype(jnp.int32)
    tile_start = jnp.minimum(jnp.arange(n_steps, dtype=jnp.int32), n_used - 1) * tm
    tile_expert = jnp.sum((pad_end[None, :] <= tile_start[:, None]).astype(jnp.int32), axis=1)
    tile_expert = jnp.minimum(tile_expert, N_EXPERTS - 1)

    yg = pl.pallas_call(
        _expert_body,
        grid_spec=pltpu.PrefetchScalarGridSpec(
            num_scalar_prefetch=2,
            grid=(n_steps,),
            in_specs=[pl.BlockSpec((1, 1, tm), lambda i, te, nu: (0, 0, 0), memory_space=pltpu.SMEM),
                      pl.BlockSpec((1, 1, tm), lambda i, te, nu: (i, 0, 0), memory_space=pltpu.SMEM),
                      pl.BlockSpec((1, 1, tm), lambda i, te, nu: (i, 0, 0), memory_space=pltpu.SMEM),
                      pl.BlockSpec(memory_space=pl.ANY),
                      pl.BlockSpec((1, D_MODEL, 2 * D_EXPERT), lambda i, te, nu: (te[i], 0, 0)),
                      pl.BlockSpec((1, 1, 2 * D_EXPERT), lambda i, te, nu: (te[i], 0, 0)),
                      pl.BlockSpec((1, D_EXPERT, D_MODEL), lambda i, te, nu: (te[i], 0, 0)),
                      pl.BlockSpec((1, 1, D_MODEL), lambda i, te, nu: (te[i], 0, 0))],
            out_specs=pl.BlockSpec(memory_space=pl.ANY),
            scratch_shapes=[pltpu.VMEM((tm, D_MODEL), F32), pltpu.VMEM((tm, D_MODEL), F32),
                            pltpu.VMEM((tm, D_MODEL), F32), pltpu.VMEM((tm, D_MODEL), F32),
                            pltpu.VMEM((D_MODEL, 2 * D_EXPERT), BF16), pltpu.VMEM((D_EXPERT, D_MODEL), BF16),
                            pltpu.SemaphoreType.DMA((2,)), pltpu.SemaphoreType.DMA(())]),
        out_shape=jax.ShapeDtypeStruct((n_pairs + tm, D_MODEL), F32),
        compiler_params=pltpu.CompilerParams(dimension_semantics=("arbitrary",), vmem_limit_bytes=VMEM_LIMIT),
        name="moe_experts",
    )(tile_expert, n_used.reshape(1), tok_tiles[0:1], tok_next, dst_prev,
      h2, w_gate_up[0], b_gate_up[0].reshape(N_EXPERTS, 1, 2 * D_EXPERT), w_down[0],
      b_down[0].reshape(N_EXPERTS, 1, D_MODEL))

    n_tok_blocks = n_tok // tb

    def combine(x1, first_block, n_blocks):
        y_spec = lambda k: pl.BlockSpec((tb, D_MODEL), lambda i: (k * n_tok_blocks + first_block + i, 0))
        return pl.pallas_call(
            _combine_body,
            grid=(n_blocks,),
            in_specs=[pl.BlockSpec((tb, D_MODEL), lambda i: (i, 0)),
                      pl.BlockSpec((tb, TOP_K), lambda i: (first_block + i, 0)),
                      y_spec(0), y_spec(1), y_spec(2), y_spec(3)],
            out_specs=pl.BlockSpec((tb, D_MODEL), lambda i: (i, 0)),
            out_shape=jax.ShapeDtypeStruct(x1.shape, F32),
            compiler_params=pltpu.CompilerParams(dimension_semantics=("arbitrary",)),
            name="moe_combine",
        )(x1, gw_a, yg, yg, yg, yg)

    y_prompt = combine(x1p.reshape(n_prompt, D_MODEL), 0, n_prompt_blocks).reshape(n_b, seq, D_MODEL)
    y_sample = combine(x1s, n_prompt_blocks, 1).reshape(n_streams, n_new, D_MODEL)

    kv_shape = (N_KV_HEADS, HEAD_DIM)
    sg_shape = (MLP_GROUPS, MLP_GROUP_DIM)
    return (y_prompt, y_sample,
            kwin_p.reshape(1, n_b, WINDOW, *kv_shape), vwin_p.reshape(1, n_b, WINDOW, *kv_shape),
            kwin_s.reshape(1, n_streams, n_cached, *kv_shape), vwin_s.reshape(1, n_streams, n_cached, *kv_shape),
            sguv_p.reshape(1, n_b, MLP_CHUNK, *sg_shape), sguv_s.reshape(1, n_streams, n_new, *sg_shape))
```

```python
import functools

import jax
import jax.numpy as jnp
import numpy as np
from jax import lax
from jax.experimental import pallas as pl
from jax.experimental.pallas import tpu as pltpu

D_MODEL = 1024
CHUNK = 64
WINDOW = 128
HEAD_DIM = 64
N_HEADS = 8
N_KV_HEADS = 2
Q_PER_KV = N_HEADS // N_KV_HEADS
ATTN_WIDTH = N_HEADS * HEAD_DIM
KV_WIDTH = N_KV_HEADS * HEAD_DIM
ATTN_SCALE = HEAD_DIM ** -0.5
MLP_CHUNK = 128
MLP_GROUPS = 8
MLP_WIDTH = D_MODEL
MLP_GROUP_DIM = MLP_WIDTH // MLP_GROUPS
N_EXPERTS = 32
TOP_K = 4
D_EXPERT = D_MODEL
SWIGLU_ALPHA = 1.702
SWIGLU_LIMIT = 7.0
EPS = 1e-6
NEG_INF = -1e30
Q_END = ATTN_WIDTH
K_END = Q_END + KV_WIDTH
V_END = K_END + KV_WIDTH
U_END = V_END + MLP_WIDTH
VM_END = U_END + MLP_WIDTH
GA_END = VM_END + D_MODEL
IN_COLS = GA_END + D_MODEL

LANES = 128
ROW_TILE = D_MODEL // LANES
TOKEN_BLOCK = 256
PAIR = 2 * CHUNK
BAND = PAIR + WINDOW
EXPERT_TILE = 256
VMEM_LIMIT = 56 * 1024 * 1024

F32 = jnp.float32
BF16 = jnp.bfloat16


def _dot(a, b):
    return jnp.dot(a, b, preferred_element_type=F32)


def _dot_nt(a, b):
    return lax.dot_general(a, b, (((1,), (1,)), ((), ())), preferred_element_type=F32)


def _store_token_major(ref, val):
    rows = val.shape[0]
    for s in range(ROW_TILE):
        ref[pl.ds(s, rows, stride=ROW_TILE), :] = val[:, s * LANES:(s + 1) * LANES]


def _load_token_major(ref, rows):
    return jnp.concatenate([ref[pl.ds(s, rows, stride=ROW_TILE), :] for s in range(ROW_TILE)], axis=-1)


def _segment_mean(sq, seg):
    hi = sq.astype(BF16)
    lo = (sq - hi.astype(F32)).astype(BF16)
    return _dot(hi, seg) + _dot(lo, seg)


def _rms_rows(x, gain):
    ms = jnp.mean(x * x, axis=-1, keepdims=True)
    return (x * lax.rsqrt(ms + EPS)) * gain


def _project(h, w_in_ref, qg, kg, sgug, seg64, seg128):
    qkv = _dot(h, w_in_ref[:, 0:V_END])
    qk_cols = []
    for c in range(K_END // LANES):
        blk = qkv[:, c * LANES:(c + 1) * LANES]
        ms = _segment_mean(blk * blk, seg64)
        g = qg if c < Q_END // LANES else kg
        qk_cols.append((blk * lax.rsqrt(ms + EPS)) * g)
    qn = jnp.concatenate(qk_cols[:Q_END // LANES], axis=-1)
    kn = qk_cols[Q_END // LANES]
    v = qkv[:, K_END:V_END]
    u = jax.nn.gelu(_dot(h, w_in_ref[:, V_END:U_END]))
    vg = jax.nn.gelu(_dot(h, w_in_ref[:, U_END:VM_END]))
    vm_cols = []
    for g in range(MLP_GROUPS):
        blk = vg[:, g * LANES:(g + 1) * LANES]
        ms = _segment_mean(blk * blk, seg128)
        vm_cols.append((blk * lax.rsqrt(ms + EPS)) * sgug[:, g * LANES:(g + 1) * LANES])
    vm = jnp.concatenate(vm_cols, axis=-1)
    ga = jax.nn.sigmoid(_dot(h, w_in_ref[:, VM_END:GA_END]))
    gb = jax.nn.sigmoid(_dot(h, w_in_ref[:, GA_END:IN_COLS]))
    return qn, kn, v, u, vm, ga, gb


def _stack_heads(q_rows):
    lane = lax.broadcasted_iota(jnp.int32, (q_rows.shape[0], LANES), 1)
    blocks = []
    for head in range(N_HEADS):
        j, half = head % Q_PER_KV, head // Q_PER_KV
        col = q_rows[:, j * LANES:(j + 1) * LANES]
        keep = (lane < HEAD_DIM) if half == 0 else (lane >= HEAD_DIM)
        blocks.append(jnp.where(keep, col, 0.0))
    return jnp.concatenate(blocks, axis=0).astype(BF16)


def _unstack_heads(o, rows):
    lane = lax.broadcasted_iota(jnp.int32, (rows, LANES), 1)
    cols = []
    for j in range(Q_PER_KV):
        lo = o[j * rows:(j + 1) * rows]
        hi = o[(j + Q_PER_KV) * rows:(j + Q_PER_KV + 1) * rows]
        cols.append(jnp.where(lane < HEAD_DIM, lo, hi))
    return jnp.concatenate(cols, axis=-1)


def _merge(att, sgu, ga, gb, wba_ref, wbs_ref, wout_ref):
    m = ga * _dot(att.astype(BF16), wba_ref[...]) + gb * _dot(sgu.astype(BF16), wbs_ref[...])
    return _dot(m.astype(BF16), wout_ref[...])


def _route(h2, wr_ref, br_ref, tri_ref, run_ref):
    rows = h2.shape[0]
    logits = _dot(h2, wr_ref[...]) + br_ref[...]
    eidx = lax.broadcasted_iota(jnp.int32, (rows, N_EXPERTS), 1).astype(F32)
    work = logits
    vals, picks, onehots = [], [], []
    for _ in range(TOP_K):
        m = jnp.max(work, axis=-1, keepdims=True)
        sel = jnp.min(jnp.where(work == m, eidx, float(N_EXPERTS)), axis=-1, keepdims=True)
        oh = eidx == sel
        vals.append(m)
        picks.append(sel)
        onehots.append(oh)
        work = jnp.where(oh, -jnp.inf, work)
    exps = [jnp.exp(v - vals[0]) for v in vals]
    den = exps[0] + exps[1] + exps[2] + exps[3]
    mask = jnp.zeros((rows, N_EXPERTS), F32)
    for oh in onehots:
        mask = mask + jnp.where(oh, 1.0, 0.0)
    before = _dot(tri_ref[...], mask.astype(BF16)) + run_ref[...]
    ranks = [jnp.sum(jnp.where(oh, before, 0.0), axis=-1, keepdims=True) for oh in onehots]
    run_ref[...] = run_ref[...] + jnp.sum(mask, axis=0, keepdims=True)
    k4 = lax.broadcasted_iota(jnp.int32, (rows, TOP_K), 1)
    idx4 = jnp.zeros((rows, TOP_K), jnp.int32)
    w4 = jnp.zeros((rows, TOP_K), F32)
    r4 = jnp.zeros((rows, TOP_K), jnp.int32)
    for k in range(TOP_K):
        idx4 = jnp.where(k4 == k, picks[k].astype(jnp.int32), idx4)
        w4 = jnp.where(k4 == k, exps[k] / den, w4)
        r4 = jnp.where(k4 == k, ranks[k].astype(jnp.int32), r4)
    return idx4, w4, r4


def _softmax_rows(parts, sink):
    m = sink
    for l in parts:
        m = jnp.maximum(m, jnp.max(l, axis=-1, keepdims=True))
    es = [jnp.exp(l - m) for l in parts]
    den = jnp.exp(sink - m)
    for e in es:
        den = den + jnp.sum(e, axis=-1, keepdims=True)
    return [e / den for e in es]


def _mixer_prompt_body(sinks_ref, x_ref, w_in_ref, gmix_ref, qg_ref, kg_ref, sgug_ref, seg64_ref, seg128_ref,
                       bias_ref, wsp_ref, bsp_ref, wba_ref, wbs_ref, wout_ref, gffn_ref, wr_ref, br_ref, tri_ref,
                       x1_ref, h2_ref, idx_ref, gw_ref, rank_ref, cnt_ref, kwin_ref, vwin_ref, sguv_ref,
                       kcarry, vcarry, run_ref, *, blocks_per_seq, n_blocks):
    i = pl.program_id(0)

    @pl.when(i == 0)
    def _():
        run_ref[...] = jnp.zeros_like(run_ref)

    @pl.when(i == n_blocks)
    def _():
        h2_ref[...] = jnp.zeros_like(h2_ref)
        idx_ref[...] = jnp.zeros_like(idx_ref)
        gw_ref[...] = jnp.zeros_like(gw_ref)
        rank_ref[...] = jnp.zeros_like(rank_ref)

    @pl.when(i < n_blocks)
    def _():
        _mixer_prompt_block(i % blocks_per_seq, sinks_ref, x_ref, w_in_ref, gmix_ref, qg_ref, kg_ref, sgug_ref,
                            seg64_ref, seg128_ref, bias_ref, wsp_ref, bsp_ref, wba_ref, wbs_ref, wout_ref, gffn_ref,
                            wr_ref, br_ref, tri_ref, x1_ref, h2_ref, idx_ref, gw_ref, rank_ref, cnt_ref, kwin_ref,
                            vwin_ref, sguv_ref, kcarry, vcarry, run_ref)


def _mixer_prompt_block(j, sinks_ref, x_ref, w_in_ref, gmix_ref, qg_ref, kg_ref, sgug_ref, seg64_ref, seg128_ref,
                        bias_ref, wsp_ref, bsp_ref, wba_ref, wbs_ref, wout_ref, gffn_ref, wr_ref, br_ref, tri_ref,
                        x1_ref, h2_ref, idx_ref, gw_ref, rank_ref, cnt_ref, kwin_ref, vwin_ref, sguv_ref,
                        kcarry, vcarry, run_ref):
    @pl.when(j == 0)
    def _():
        kcarry[...] = jnp.zeros_like(kcarry)
        vcarry[...] = jnp.zeros_like(vcarry)

    x = x_ref[0]
    h = _rms_rows(x, gmix_ref[...]).astype(BF16)
    qn, kn, v, u, vm, ga, gb = _project(h, w_in_ref, qg_ref[...], kg_ref[...], sgug_ref[...],
                                        seg64_ref[...], seg128_ref[...])
    k_ext = jnp.concatenate([kcarry[...], kn.astype(BF16)], axis=0)
    v_ext = jnp.concatenate([vcarry[...], v.astype(BF16)], axis=0)
    kcarry[...] = k_ext[TOKEN_BLOCK:]
    vcarry[...] = v_ext[TOKEN_BLOCK:]

    tri_mask = (lax.broadcasted_iota(jnp.int32, (MLP_CHUNK, MLP_CHUNK), 0)
                >= lax.broadcasted_iota(jnp.int32, (MLP_CHUNK, MLP_CHUNK), 1))
    att_rows, sgu_rows = [], []
    for pm in range(TOKEN_BLOCK // PAIR):
        r0 = pm * PAIR
        q_stack = _stack_heads(qn[r0:r0 + PAIR])
        k_band = k_ext[r0:r0 + BAND]
        v_band = v_ext[r0:r0 + BAND]
        s = _dot_nt(q_stack, k_band)
        first = jnp.where(j == 0, 0, 1) if pm == 0 else 1
        probs = []
        for head in range(N_HEADS):
            logit = s[head * PAIR:(head + 1) * PAIR] * ATTN_SCALE + bias_ref[first, head]
            probs.append(_softmax_rows([logit], sinks_ref[head])[0].astype(BF16))
        o = _dot(jnp.concatenate(probs, axis=0), v_band)
        att_rows.append(_unstack_heads(o, PAIR))
        cols = []
        for g in range(MLP_GROUPS):
            wm = jnp.where(tri_mask, wsp_ref[g], 0.0).astype(BF16)
            cols.append(_dot(wm, vm[r0:r0 + PAIR, g * LANES:(g + 1) * LANES].astype(BF16)))
        mixed = jnp.concatenate(cols, axis=-1) + bsp_ref[...]
        sgu_rows.append(u[r0:r0 + PAIR] * mixed)
    att = jnp.concatenate(att_rows, axis=0)
    sgu = jnp.concatenate(sgu_rows, axis=0)

    x1 = x + _merge(att, sgu, ga, gb, wba_ref, wbs_ref, wout_ref)
    h2 = _rms_rows(x1, gffn_ref[...])
    idx4, w4, r4 = _route(h2.astype(BF16), wr_ref, br_ref, tri_ref, run_ref)

    x1_ref[0] = x1
    _store_token_major(h2_ref, h2)
    idx_ref[...] = idx4
    gw_ref[...] = w4
    rank_ref[...] = r4
    cnt_ref[...] = run_ref[...]
    kwin_ref[0] = kn[TOKEN_BLOCK - WINDOW:]
    vwin_ref[0] = v[TOKEN_BLOCK - WINDOW:]
    sguv_ref[0] = vm[TOKEN_BLOCK - MLP_CHUNK:]


def _mixer_sample_body(sinks_ref, x_ref, ck_ref, cv_ref, w_in_ref, gmix_ref, qg_ref, kg_ref, sgug_ref, seg64_ref,
                       seg128_ref, biasc_ref, biasn_ref, wsp_ref, bsp_ref, wba_ref, wbs_ref, wout_ref, gffn_ref,
                       wr_ref, br_ref, tri_ref, h2_in_ref, idx_in_ref, gw_in_ref, rank_in_ref,
                       x1_ref, h2_ref, idx_ref, gw_ref, rank_ref, cnt_ref, kwin_ref, vwin_ref, sguv_ref,
                       run_ref, *, n_streams, n_new):
    del h2_in_ref, idx_in_ref, gw_in_ref, rank_in_ref
    run_ref[...] = jnp.zeros_like(run_ref)
    x = x_ref[...]
    h = _rms_rows(x, gmix_ref[...]).astype(BF16)
    qn, kn, v, u, vm, ga, gb = _project(h, w_in_ref, qg_ref[...], kg_ref[...], sgug_ref[...],
                                        seg64_ref[...], seg128_ref[...])
    n_cached = ck_ref.shape[1]
    att_rows = []
    for s_i in range(n_streams):
        r0 = s_i * n_new
        q_stack = _stack_heads(qn[r0:r0 + n_new])
        k_new = kn[r0:r0 + n_new]
        v_new = v[r0:r0 + n_new]
        s_c = _dot_nt(q_stack, ck_ref[s_i].astype(BF16))
        s_n = _dot_nt(q_stack, k_new.astype(BF16))
        pc, pn = [], []
        for head in range(N_HEADS):
            rows = slice(head * n_new, (head + 1) * n_new)
            lc = s_c[rows] * ATTN_SCALE + biasc_ref[rows]
            ln = s_n[rows] * ATTN_SCALE + biasn_ref[rows]
            p_c, p_n = _softmax_rows([lc, ln], sinks_ref[head])
            pc.append(p_c.astype(BF16))
            pn.append(p_n.astype(BF16))
        o = (_dot(jnp.concatenate(pc, axis=0), cv_ref[s_i].astype(BF16))
             + _dot(jnp.concatenate(pn, axis=0), v_new.astype(BF16)))
        att_rows.append(_unstack_heads(o, n_new))
        kwin_ref[s_i, 0:n_cached - n_new] = ck_ref[s_i, n_new:n_cached]
        kwin_ref[s_i, n_cached - n_new:n_cached] = k_new
        vwin_ref[s_i, 0:n_cached - n_new] = cv_ref[s_i, n_new:n_cached]
        vwin_ref[s_i, n_cached - n_new:n_cached] = v_new
    att = jnp.concatenate(att_rows, axis=0)

    rows = n_streams * n_new
    ri = lax.broadcasted_iota(jnp.int32, (rows, rows), 0)
    ci = lax.broadcasted_iota(jnp.int32, (rows, rows), 1)
    keep = jnp.logical_and(ri // n_new == ci // n_new, ri % n_new >= ci % n_new)
    cols = []
    for g in range(MLP_GROUPS):
        wm = jnp.where(keep, wsp_ref[g], 0.0).astype(BF16)
        cols.append(_dot(wm, vm[:, g * LANES:(g + 1) * LANES].astype(BF16)))
    sgu = u * (jnp.concatenate(cols, axis=-1) + bsp_ref[...])

    x1 = x + _merge(att, sgu, ga, gb, wba_ref, wbs_ref, wout_ref)
    h2 = _rms_rows(x1, gffn_ref[...])
    idx4, w4, r4 = _route(h2.astype(BF16), wr_ref, br_ref, tri_ref, run_ref)
    x1_ref[...] = x1
    _store_token_major(h2_ref, h2)
    idx_ref[...] = idx4
    gw_ref[...] = w4
    rank_ref[...] = r4
    cnt_ref[...] = run_ref[...]
    sguv_ref[...] = vm


def _row_copy_in(h2_hbm, tok_ref, xbuf, sem, r):
    src = pl.multiple_of(tok_ref[0, 0, r], ROW_TILE)
    return pltpu.make_async_copy(h2_hbm.at[pl.ds(src, ROW_TILE), :], xbuf.at[_tile_rows(r), :], sem)


def _row_copy_out(ybuf, dst_ref, yg_hbm, sem, r):
    dst = pl.multiple_of(dst_ref[0, 0, r], ROW_TILE)
    return pltpu.make_async_copy(ybuf.at[_tile_rows(r), :], yg_hbm.at[pl.ds(dst, ROW_TILE), :], sem)


def _tile_rows(r):
    start = r * ROW_TILE
    return pl.ds(start if isinstance(r, int) else pl.multiple_of(start, ROW_TILE), ROW_TILE)


def _expert_tile(tok_next_ref, dst_prev_ref, h2_hbm, bgu_ref, bd_ref, yg_hbm, wgu_bf, wd_bf,
                 x_cur, x_next, y_cur, y_prev, sem_in_cur, sem_in_next, sem_out):
    pltpu.make_async_copy(h2_hbm.at[pl.ds(0, EXPERT_TILE * ROW_TILE), :], x_cur, sem_in_cur).wait()
    for r in range(EXPERT_TILE):
        _row_copy_in(h2_hbm, tok_next_ref, x_next, sem_in_next, r).start()
    for r in range(EXPERT_TILE):
        _row_copy_out(y_prev, dst_prev_ref, yg_hbm, sem_out, r).start()
    xb = _load_token_major(x_cur, EXPERT_TILE).astype(BF16)
    hgu = _dot(xb, wgu_bf[...]) + bgu_ref[0]
    glu = jnp.minimum(hgu[:, :D_EXPERT], SWIGLU_LIMIT)
    lin = jnp.clip(hgu[:, D_EXPERT:], -SWIGLU_LIMIT, SWIGLU_LIMIT)
    act = glu * jax.nn.sigmoid(glu * SWIGLU_ALPHA) * (lin + 1.0)
    _store_token_major(y_cur, _dot(act.astype(BF16), wd_bf[...]) + bd_ref[0])
    pltpu.make_async_copy(y_prev, yg_hbm.at[pl.ds(0, EXPERT_TILE * ROW_TILE), :], sem_out).wait()


def _expert_body(te_ref, nused_ref, tok_first_ref, tok_next_ref, dst_prev_ref, h2_hbm, wgu_ref, bgu_ref, wd_ref,
                 bd_ref, yg_hbm, xbuf0, xbuf1, ybuf0, ybuf1, wgu_bf, wd_bf, sem_in, sem_out):
    i = pl.program_id(0)
    n_used = nused_ref[0]
    xbufs, ybufs = (xbuf0, xbuf1), (ybuf0, ybuf1)

    @pl.when(i == 0)
    def _():
        ybuf1[...] = jnp.zeros_like(ybuf1)

        def first_rows(r, carry):
            _row_copy_in(h2_hbm, tok_first_ref, xbuf0, sem_in.at[0], r).start()
            return carry
        lax.fori_loop(0, EXPERT_TILE, first_rows, 0)

    @pl.when(jnp.logical_and(i < n_used, jnp.logical_or(i == 0, te_ref[i] != te_ref[jnp.maximum(i - 1, 0)])))
    def _():
        wgu_bf[...] = wgu_ref[0].astype(BF16)
        wd_bf[...] = wd_ref[0].astype(BF16)

    for par in range(2):
        @pl.when(jnp.logical_and(i < n_used, i % 2 == par))
        def _(par=par):
            _expert_tile(tok_next_ref, dst_prev_ref, h2_hbm, bgu_ref, bd_ref, yg_hbm, wgu_bf, wd_bf,
                         xbufs[par], xbufs[1 - par], ybufs[par], ybufs[1 - par],
                         sem_in.at[par], sem_in.at[1 - par], sem_out)

        @pl.when(jnp.logical_and(i == n_used, i % 2 == par))
        def _(par=par):
            pltpu.make_async_copy(h2_hbm.at[pl.ds(0, EXPERT_TILE * ROW_TILE), :], xbufs[par], sem_in.at[par]).wait()

            def last_rows(r, carry):
                _row_copy_out(ybufs[1 - par], dst_prev_ref, yg_hbm, sem_out, r).start()
                return carry
            lax.fori_loop(0, EXPERT_TILE, last_rows, 0)
            pltpu.make_async_copy(ybufs[1 - par], yg_hbm.at[pl.ds(0, EXPERT_TILE * ROW_TILE), :], sem_out).wait()


def _combine_body(x1_ref, gw_ref, y0_ref, y1_ref, y2_ref, y3_ref, out_ref):
    gw = gw_ref[...]
    rows = x1_ref.shape[0]
    for s in range(ROW_TILE):
        cols = slice(s * LANES, (s + 1) * LANES)
        acc = x1_ref[:, cols]
        for k, y_ref in enumerate((y0_ref, y1_ref, y2_ref, y3_ref)):
            acc = acc + gw[:, k:k + 1] * y_ref[pl.ds(s, rows, stride=ROW_TILE), :]
        out_ref[:, cols] = acc


def _const_spec(shape):
    nd = len(shape)
    return pl.BlockSpec(shape, lambda *_: (0,) * nd, pipeline_mode=pl.Buffered(1))


def _q_perm():
    cols = np.arange(ATTN_WIDTH)
    j, half, d = cols // LANES, (cols % LANES) // HEAD_DIM, cols % HEAD_DIM
    return (j + Q_PER_KV * half) * HEAD_DIM + d


def _alibi_slopes():
    return 2.0 ** (-8.0 * np.arange(1, N_HEADS + 1) / N_HEADS)


def _prompt_bias():
    qi = np.arange(PAIR)[:, None]
    kj = np.arange(BAND)[None, :]
    dist = np.abs(qi + WINDOW - kj).astype(np.float64)
    cq, ck = qi // CHUNK, kj // CHUNK
    in_band = (ck >= cq) & (ck <= cq + WINDOW // CHUNK)
    base = -_alibi_slopes()[:, None, None] * dist[None]
    later = np.where(in_band[None], base, NEG_INF)
    first = np.where((kj >= WINDOW)[None], later, NEG_INF)
    return np.stack([first, later]).astype(np.float32)


def _sample_bias(n_new, n_cached):
    qi = np.arange(n_new)[:, None]
    dc = np.abs(qi + n_cached - np.arange(n_cached)[None, :]).astype(np.float64)
    dn = np.abs(qi - np.arange(n_new)[None, :]).astype(np.float64)
    sl = _alibi_slopes()[:, None, None]
    bc = (-sl * dc[None]).reshape(N_HEADS * n_new, n_cached)
    bn = (-sl * dn[None]).reshape(N_HEADS * n_new, n_new)
    return bc.astype(np.float32), bn.astype(np.float32)


def kernel(x_prompt, x_sample, cache_k_win, cache_v_win, g_mix, w_in, q_norm_g, k_norm_g, attn_sinks, sgu_norm_g,
           w_spatial, b_spatial, w_branch_attn, w_branch_sgu, w_out, g_ffn, w_router, b_router, w_gate_up,
           b_gate_up, w_down, b_down):
    n_b, seq, _ = x_prompt.shape
    n_streams, n_new, _ = x_sample.shape
    n_cached = cache_k_win.shape[2]
    n_prompt = n_b * seq
    n_sample = n_streams * n_new
    n_tok = n_prompt + n_sample
    assert seq % TOKEN_BLOCK == 0 and n_sample == TOKEN_BLOCK and n_cached == WINDOW
    blocks_per_seq = seq // TOKEN_BLOCK
    n_prompt_blocks = n_prompt // TOKEN_BLOCK

    perm = _q_perm()
    w_in_l = w_in[0]
    w_in_b = jnp.concatenate([w_in_l[:, perm], w_in_l[:, Q_END:]], axis=1).astype(BF16)
    wba_b = w_branch_attn[0][perm, :].astype(BF16)
    wbs_b = w_branch_sgu[0].astype(BF16)
    wout_b = w_out[0].astype(BF16)
    wr_b = w_router[0].astype(BF16)
    br = b_router[0].reshape(1, N_EXPERTS).astype(F32)
    gmix = g_mix[0].reshape(1, D_MODEL)
    gffn = g_ffn[0].reshape(1, D_MODEL)
    qg = jnp.tile(q_norm_g[0], LANES // HEAD_DIM).reshape(1, LANES)
    kg = jnp.tile(k_norm_g[0], LANES // HEAD_DIM).reshape(1, LANES)
    sgug = sgu_norm_g[0].reshape(1, MLP_WIDTH)
    sinks = attn_sinks[0].astype(F32)
    lane_seg = np.arange(LANES) // HEAD_DIM
    seg64 = jnp.asarray((lane_seg[:, None] == lane_seg[None, :]) / HEAD_DIM, BF16)
    seg128 = jnp.full((LANES, LANES), 1.0 / MLP_GROUP_DIM, BF16)
    tri = jnp.asarray(np.tril(np.ones((TOKEN_BLOCK, TOKEN_BLOCK)), -1), BF16)
    wsp = w_spatial[0]
    bsp_p = jnp.repeat(b_spatial[0].T, MLP_GROUP_DIM, axis=1)
    wsp_s = jnp.tile(wsp[:, :n_new, :n_new], (1, n_streams, n_streams))
    bsp_s = jnp.tile(jnp.repeat(b_spatial[0][:, :n_new].T, MLP_GROUP_DIM, axis=1), (n_streams, 1))
    bias_p = jnp.asarray(_prompt_bias())
    bias_c, bias_n = (jnp.asarray(a) for a in _sample_bias(n_new, n_cached))

    smem = pl.BlockSpec(memory_space=pltpu.SMEM)
    tb = TOKEN_BLOCK
    last_blk = n_prompt_blocks - 1
    seq_of = lambda i: jnp.minimum(i, last_blk) // blocks_per_seq
    x_spec = pl.BlockSpec((1, tb, D_MODEL), lambda i: (seq_of(i), jnp.minimum(i, last_blk) % blocks_per_seq, 0))
    tok_spec = lambda width: pl.BlockSpec((tb, width), lambda i: (i, 0))
    win_spec = lambda width: pl.BlockSpec((1, WINDOW, width), lambda i: (seq_of(i), 0, 0))
    route_shapes = [jax.ShapeDtypeStruct((n_tok, TOP_K), jnp.int32), jax.ShapeDtypeStruct((n_tok, TOP_K), F32),
                    jax.ShapeDtypeStruct((n_tok, TOP_K), jnp.int32)]

    x1p, h2, idx_p, gw_p, rank_p, cnt_p, kwin_p, vwin_p, sguv_p = pl.pallas_call(
        functools.partial(_mixer_prompt_body, blocks_per_seq=blocks_per_seq, n_blocks=n_prompt_blocks),
        grid=(n_prompt_blocks + 1,),
        in_specs=[smem, x_spec,
                  _const_spec((D_MODEL, IN_COLS)), _const_spec((1, D_MODEL)), _const_spec((1, LANES)),
                  _const_spec((1, LANES)), _const_spec((1, MLP_WIDTH)), _const_spec((LANES, LANES)),
                  _const_spec((LANES, LANES)), _const_spec((2, N_HEADS, PAIR, BAND)),
                  _const_spec((MLP_GROUPS, MLP_CHUNK, MLP_CHUNK)), _const_spec((MLP_CHUNK, MLP_WIDTH)),
                  _const_spec((ATTN_WIDTH, D_MODEL)), _const_spec((MLP_WIDTH, D_MODEL)),
                  _const_spec((D_MODEL, D_MODEL)), _const_spec((1, D_MODEL)), _const_spec((D_MODEL, N_EXPERTS)),
                  _const_spec((1, N_EXPERTS)), _const_spec((tb, tb))],
        out_specs=[x_spec,
                   pl.BlockSpec((tb * ROW_TILE, LANES), lambda i: (i, 0)), tok_spec(TOP_K), tok_spec(TOP_K), tok_spec(TOP_K),
                   pl.BlockSpec((1, N_EXPERTS), lambda i: (0, 0)),
                   win_spec(KV_WIDTH), win_spec(KV_WIDTH), win_spec(MLP_WIDTH)],
        out_shape=[jax.ShapeDtypeStruct((n_b, seq, D_MODEL), F32), jax.ShapeDtypeStruct((n_tok * ROW_TILE, LANES), F32),
                   *route_shapes, jax.ShapeDtypeStruct((1, N_EXPERTS), F32),
                   jax.ShapeDtypeStruct((n_b, WINDOW, KV_WIDTH), F32),
                   jax.ShapeDtypeStruct((n_b, WINDOW, KV_WIDTH), F32),
                   jax.ShapeDtypeStruct((n_b, MLP_CHUNK, MLP_WIDTH), F32)],
        scratch_shapes=[pltpu.VMEM((WINDOW, KV_WIDTH), BF16), pltpu.VMEM((WINDOW, KV_WIDTH), BF16),
                        pltpu.VMEM((1, N_EXPERTS), F32)],
        compiler_params=pltpu.CompilerParams(dimension_semantics=("arbitrary",), vmem_limit_bytes=VMEM_LIMIT),
        name="mixer_prompt",
    )(sinks, x_prompt, w_in_b, gmix, qg, kg, sgug, seg64, seg128, bias_p, wsp, bsp_p, wba_b, wbs_b, wout_b,
      gffn, wr_b, br, tri)

    full = lambda shape: pl.BlockSpec(shape, lambda i: (0,) * len(shape))
    last_tok = lambda width: pl.BlockSpec((tb, width), lambda i: (n_prompt_blocks, 0))
    any_spec = pl.BlockSpec(memory_space=pl.ANY)
    ck = cache_k_win[0].reshape(n_streams, n_cached, KV_WIDTH)
    cv = cache_v_win[0].reshape(n_streams, n_cached, KV_WIDTH)
    x1s, h2, idx_a, gw_a, rank_a, cnt_s, kwin_s, vwin_s, sguv_s = pl.pallas_call(
        functools.partial(_mixer_sample_body, n_streams=n_streams, n_new=n_new),
        grid=(1,),
        in_specs=[smem, full((n_sample, D_MODEL)), full((n_streams, n_cached, KV_WIDTH)),
                  full((n_streams, n_cached, KV_WIDTH)),
                  full((D_MODEL, IN_COLS)), full((1, D_MODEL)), full((1, LANES)), full((1, LANES)),
                  full((1, MLP_WIDTH)), full((LANES, LANES)), full((LANES, LANES)),
                  full((N_HEADS * n_new, n_cached)), full((N_HEADS * n_new, n_new)),
                  full((MLP_GROUPS, n_sample, n_sample)), full((n_sample, MLP_WIDTH)),
                  full((ATTN_WIDTH, D_MODEL)), full((MLP_WIDTH, D_MODEL)), full((D_MODEL, D_MODEL)),
                  full((1, D_MODEL)), full((D_MODEL, N_EXPERTS)), full((1, N_EXPERTS)), full((tb, tb)),
                  any_spec, any_spec, any_spec, any_spec],
        out_specs=[full((n_sample, D_MODEL)), pl.BlockSpec((tb * ROW_TILE, LANES), lambda i: (n_prompt_blocks, 0)), last_tok(TOP_K), last_tok(TOP_K), last_tok(TOP_K),
                   full((1, N_EXPERTS)), full((n_streams, n_cached, KV_WIDTH)),
                   full((n_streams, n_cached, KV_WIDTH)), full((n_sample, MLP_WIDTH))],
        out_shape=[jax.ShapeDtypeStruct((n_sample, D_MODEL), F32), jax.ShapeDtypeStruct((n_tok * ROW_TILE, LANES), F32),
                   *route_shapes, jax.ShapeDtypeStruct((1, N_EXPERTS), F32),
                   jax.ShapeDtypeStruct((n_streams, n_cached, KV_WIDTH), F32),
                   jax.ShapeDtypeStruct((n_streams, n_cached, KV_WIDTH), F32),
                   jax.ShapeDtypeStruct((n_sample, MLP_WIDTH), F32)],
        input_output_aliases={22: 1, 23: 2, 24: 3, 25: 4},
        scratch_shapes=[pltpu.VMEM((1, N_EXPERTS), F32)],
        compiler_params=pltpu.CompilerParams(dimension_semantics=("arbitrary",), vmem_limit_bytes=VMEM_LIMIT),
        name="mixer_sample",
    )(sinks, x_sample.reshape(n_sample, D_MODEL), ck, cv, w_in_b, gmix, qg, kg, sgug, seg64, seg128, bias_c,
      bias_n, wsp_s, bsp_s, wba_b, wbs_b, wout_b, gffn, wr_b, br, tri, h2, idx_p, gw_p, rank_p)

    tm = EXPERT_TILE
    cnt_prompt = cnt_p[0].astype(jnp.int32)
    cnt_all = cnt_prompt + cnt_s[0].astype(jnp.int32)
    padded = ((cnt_all + tm - 1) // tm) * tm
    pad_end = jnp.cumsum(padded)
    pad_off = pad_end - padded
    is_sample = (jnp.arange(n_tok) >= n_prompt)[:, None, None]
    first_row = jnp.where(is_sample, (pad_off + cnt_prompt)[None, None, :], pad_off[None, None, :])
    picked = idx_a[:, :, None] == jnp.arange(N_EXPERTS, dtype=jnp.int32)[None, None, :]
    dest = (rank_a + jnp.sum(jnp.where(picked, first_row, 0), axis=-1)).reshape(-1)
    n_tiles = (n_tok * TOP_K) // tm + N_EXPERTS
    n_rows = n_tiles * tm
    tok_of_pair = jnp.repeat(jnp.arange(n_tok, dtype=jnp.int32), TOP_K)
    slot_of_pair = jnp.tile(jnp.arange(TOP_K, dtype=jnp.int32), n_tok) * n_tok + tok_of_pair
    n_pairs = TOP_K * n_tok
    spare = n_pairs + jnp.arange(tm, dtype=jnp.int32)
    dst_sorted = jnp.tile(spare, n_tiles).at[dest].set(slot_of_pair, unique_indices=True)
    tok_sorted = jnp.where(dst_sorted < n_pairs, dst_sorted % n_tok, 0)
    tok_tiles = tok_sorted.reshape(n_tiles, 1, tm)
    n_steps = n_tiles + 1
    tok_next = jnp.concatenate([tok_tiles[1:], jnp.zeros((2, 1, tm), jnp.int32)], axis=0)
    dst_prev = jnp.concatenate([spare.reshape(1, 1, tm), dst_sorted.reshape(n_tiles, 1, tm)], axis=0)
    n_used = (pad_end[-1] // tm).astype(jnp.int32)
    tile_start = jnp.minimum(jnp.arange(n_steps, dtype=jnp.int32), n_used - 1) * tm
    tile_expert = jnp.sum((pad_end[None, :] <= tile_start[:, None]).astype(jnp.int32), axis=1)
    tile_expert = jnp.minimum(tile_expert, N_EXPERTS - 1)

    yg = pl.pallas_call(
        _expert_body,
        grid_spec=pltpu.PrefetchScalarGridSpec(
            num_scalar_prefetch=2,
            grid=(n_steps,),
            in_specs=[pl.BlockSpec((1, 1, tm), lambda i, te, nu: (0, 0, 0), memory_space=pltpu.SMEM),
                      pl.BlockSpec((1, 1, tm), lambda i, te, nu: (i, 0, 0), memory_space=pltpu.SMEM),
                      pl.BlockSpec((1, 1, tm), lambda i, te, nu: (i, 0, 0), memory_space=pltpu.SMEM),
                      pl.BlockSpec(memory_space=pl.ANY),
                      pl.BlockSpec((1, D_MODEL, 2 * D_EXPERT), lambda i, te, nu: (te[i], 0, 0)),
                      pl.BlockSpec((1, 1, 2 * D_EXPERT), lambda i, te, nu: (te[i], 0, 0)),
                      pl.BlockSpec((1, D_EXPERT, D_MODEL), lambda i, te, nu: (te[i], 0, 0)),
                      pl.BlockSpec((1, 1, D_MODEL), lambda i, te, nu: (te[i], 0, 0))],
            out_specs=pl.BlockSpec(memory_space=pl.ANY),
            scratch_shapes=[pltpu.VMEM((tm * ROW_TILE, LANES), F32), pltpu.VMEM((tm * ROW_TILE, LANES), F32),
                            pltpu.VMEM((tm * ROW_TILE, LANES), F32), pltpu.VMEM((tm * ROW_TILE, LANES), F32),
                            pltpu.VMEM((D_MODEL, 2 * D_EXPERT), BF16), pltpu.VMEM((D_EXPERT, D_MODEL), BF16),
                            pltpu.SemaphoreType.DMA((2,)), pltpu.SemaphoreType.DMA(())]),
        out_shape=jax.ShapeDtypeStruct(((n_pairs + tm) * ROW_TILE, LANES), F32),
        compiler_params=pltpu.CompilerParams(dimension_semantics=("arbitrary",), vmem_limit_bytes=VMEM_LIMIT),
        name="moe_experts",
    )(tile_expert, n_used.reshape(1), tok_tiles[0:1] * ROW_TILE, tok_next * ROW_TILE, dst_prev * ROW_TILE,
      h2, w_gate_up[0], b_gate_up[0].reshape(N_EXPERTS, 1, 2 * D_EXPERT), w_down[0],
      b_down[0].reshape(N_EXPERTS, 1, D_MODEL))

    n_tok_blocks = n_tok // tb

    def combine(x1, first_block, n_blocks):
        y_spec = lambda k: pl.BlockSpec((tb * ROW_TILE, LANES), lambda i: (k * n_tok_blocks + first_block + i, 0))
        return pl.pallas_call(
            _combine_body,
            grid=(n_blocks,),
            in_specs=[pl.BlockSpec((tb, D_MODEL), lambda i: (i, 0)),
                      pl.BlockSpec((tb, TOP_K), lambda i: (first_block + i, 0)),
                      y_spec(0), y_spec(1), y_spec(2), y_spec(3)],
            out_specs=pl.BlockSpec((tb, D_MODEL), lambda i: (i, 0)),
            out_shape=jax.ShapeDtypeStruct(x1.shape, F32),
            compiler_params=pltpu.CompilerParams(dimension_semantics=("arbitrary",)),
            name="moe_combine",
        )(x1, gw_a, yg, yg, yg, yg)

    y_prompt = combine(x1p.reshape(n_prompt, D_MODEL), 0, n_prompt_blocks).reshape(n_b, seq, D_MODEL)
    y_sample = combine(x1s, n_prompt_blocks, 1).reshape(n_streams, n_new, D_MODEL)

    kv_shape = (N_KV_HEADS, HEAD_DIM)
    sg_shape = (MLP_GROUPS, MLP_GROUP_DIM)
    return (y_prompt, y_sample,
            kwin_p.reshape(1, n_b, WINDOW, *kv_shape), vwin_p.reshape(1, n_b, WINDOW, *kv_shape),
            kwin_s.reshape(1, n_streams, n_cached, *kv_shape), vwin_s.reshape(1, n_streams, n_cached, *kv_shape),
            sguv_p.reshape(1, n_b, MLP_CHUNK, *sg_shape), sguv_s.reshape(1, n_streams, n_new, *sg_shape))
```

```python
import functools

import jax
import jax.numpy as jnp
import numpy as np
from jax import lax
from jax.experimental import pallas as pl
from jax.experimental.pallas import tpu as pltpu

D_MODEL = 1024
CHUNK = 64
WINDOW = 128
HEAD_DIM = 64
N_HEADS = 8
N_KV_HEADS = 2
Q_PER_KV = N_HEADS // N_KV_HEADS
ATTN_WIDTH = N_HEADS * HEAD_DIM
KV_WIDTH = N_KV_HEADS * HEAD_DIM
ATTN_SCALE = HEAD_DIM ** -0.5
MLP_CHUNK = 128
MLP_GROUPS = 8
MLP_WIDTH = D_MODEL
MLP_GROUP_DIM = MLP_WIDTH // MLP_GROUPS
N_EXPERTS = 32
TOP_K = 4
D_EXPERT = D_MODEL
SWIGLU_ALPHA = 1.702
SWIGLU_LIMIT = 7.0
EPS = 1e-6
NEG_INF = -1e30
Q_END = ATTN_WIDTH
K_END = Q_END + KV_WIDTH
V_END = K_END + KV_WIDTH
U_END = V_END + MLP_WIDTH
VM_END = U_END + MLP_WIDTH
GA_END = VM_END + D_MODEL
IN_COLS = GA_END + D_MODEL

LANES = 128
ROW_TILE = D_MODEL // LANES
TOKEN_BLOCK = 256
PAIR = 2 * CHUNK
BAND = PAIR + WINDOW
EXPERT_TILE = 256
VMEM_LIMIT = 56 * 1024 * 1024

F32 = jnp.float32
BF16 = jnp.bfloat16


def _dot(a, b):
    return jnp.dot(a, b, preferred_element_type=F32)


def _dot_nt(a, b):
    return lax.dot_general(a, b, (((1,), (1,)), ((), ())), preferred_element_type=F32)


def _store_token_major(ref, val):
    rows = val.shape[0]
    for s in range(ROW_TILE):
        ref[pl.ds(s, rows, stride=ROW_TILE), :] = val[:, s * LANES:(s + 1) * LANES]


def _load_token_major(ref, rows):
    return jnp.concatenate([ref[pl.ds(s, rows, stride=ROW_TILE), :] for s in range(ROW_TILE)], axis=-1)


def _segment_mean(sq, seg):
    hi = sq.astype(BF16)
    lo = (sq - hi.astype(F32)).astype(BF16)
    return _dot(hi, seg) + _dot(lo, seg)


def _rms_rows(x, gain):
    ms = jnp.mean(x * x, axis=-1, keepdims=True)
    return (x * lax.rsqrt(ms + EPS)) * gain


def _project(h, w_in_ref, qg, kg, sgug, seg64, seg128):
    qkv = _dot(h, w_in_ref[:, 0:V_END])
    qk_cols = []
    for c in range(K_END // LANES):
        blk = qkv[:, c * LANES:(c + 1) * LANES]
        ms = _segment_mean(blk * blk, seg64)
        g = qg if c < Q_END // LANES else kg
        qk_cols.append((blk * lax.rsqrt(ms + EPS)) * g)
    qn = jnp.concatenate(qk_cols[:Q_END // LANES], axis=-1)
    kn = qk_cols[Q_END // LANES]
    v = qkv[:, K_END:V_END]
    u = jax.nn.gelu(_dot(h, w_in_ref[:, V_END:U_END]))
    vg = jax.nn.gelu(_dot(h, w_in_ref[:, U_END:VM_END]))
    vm_cols = []
    for g in range(MLP_GROUPS):
        blk = vg[:, g * LANES:(g + 1) * LANES]
        ms = _segment_mean(blk * blk, seg128)
        vm_cols.append((blk * lax.rsqrt(ms + EPS)) * sgug[:, g * LANES:(g + 1) * LANES])
    vm = jnp.concatenate(vm_cols, axis=-1)
    ga = jax.nn.sigmoid(_dot(h, w_in_ref[:, VM_END:GA_END]))
    gb = jax.nn.sigmoid(_dot(h, w_in_ref[:, GA_END:IN_COLS]))
    return qn, kn, v, u, vm, ga, gb


def _stack_heads(q_rows):
    lane = lax.broadcasted_iota(jnp.int32, (q_rows.shape[0], LANES), 1)
    blocks = []
    for head in range(N_HEADS):
        j, half = head % Q_PER_KV, head // Q_PER_KV
        col = q_rows[:, j * LANES:(j + 1) * LANES]
        keep = (lane < HEAD_DIM) if half == 0 else (lane >= HEAD_DIM)
        blocks.append(jnp.where(keep, col, 0.0))
    return jnp.concatenate(blocks, axis=0).astype(BF16)


def _unstack_heads(o, rows):
    lane = lax.broadcasted_iota(jnp.int32, (rows, LANES), 1)
    cols = []
    for j in range(Q_PER_KV):
        lo = o[j * rows:(j + 1) * rows]
        hi = o[(j + Q_PER_KV) * rows:(j + Q_PER_KV + 1) * rows]
        cols.append(jnp.where(lane < HEAD_DIM, lo, hi))
    return jnp.concatenate(cols, axis=-1)


def _merge(att, sgu, ga, gb, wba_ref, wbs_ref, wout_ref):
    m = ga * _dot(att.astype(BF16), wba_ref[...]) + gb * _dot(sgu.astype(BF16), wbs_ref[...])
    return _dot(m.astype(BF16), wout_ref[...])


def _route(h2, wr_ref, br_ref, tri_ref, run_ref):
    rows = h2.shape[0]
    logits = _dot(h2, wr_ref[...]) + br_ref[...]
    eidx = lax.broadcasted_iota(jnp.int32, (rows, N_EXPERTS), 1).astype(F32)
    work = logits
    vals, picks, onehots = [], [], []
    for _ in range(TOP_K):
        m = jnp.max(work, axis=-1, keepdims=True)
        sel = jnp.min(jnp.where(work == m, eidx, float(N_EXPERTS)), axis=-1, keepdims=True)
        oh = eidx == sel
        vals.append(m)
        picks.append(sel)
        onehots.append(oh)
        work = jnp.where(oh, -jnp.inf, work)
    exps = [jnp.exp(v - vals[0]) for v in vals]
    den = exps[0] + exps[1] + exps[2] + exps[3]
    mask = jnp.zeros((rows, N_EXPERTS), F32)
    for oh in onehots:
        mask = mask + jnp.where(oh, 1.0, 0.0)
    before = _dot(tri_ref[...], mask.astype(BF16)) + run_ref[...]
    ranks = [jnp.sum(jnp.where(oh, before, 0.0), axis=-1, keepdims=True) for oh in onehots]
    run_ref[...] = run_ref[...] + jnp.sum(mask, axis=0, keepdims=True)
    k4 = lax.broadcasted_iota(jnp.int32, (rows, TOP_K), 1)
    idx4 = jnp.zeros((rows, TOP_K), jnp.int32)
    w4 = jnp.zeros((rows, TOP_K), F32)
    r4 = jnp.zeros((rows, TOP_K), jnp.int32)
    for k in range(TOP_K):
        idx4 = jnp.where(k4 == k, picks[k].astype(jnp.int32), idx4)
        w4 = jnp.where(k4 == k, exps[k] / den, w4)
        r4 = jnp.where(k4 == k, ranks[k].astype(jnp.int32), r4)
    return idx4, w4, r4


def _softmax_rows(parts, sink):
    m = sink
    for l in parts:
        m = jnp.maximum(m, jnp.max(l, axis=-1, keepdims=True))
    es = [jnp.exp(l - m) for l in parts]
    den = jnp.exp(sink - m)
    for e in es:
        den = den + jnp.sum(e, axis=-1, keepdims=True)
    return [e / den for e in es]


def _mixer_prompt_body(sinks_ref, x_ref, w_in_ref, gmix_ref, qg_ref, kg_ref, sgug_ref, seg64_ref, seg128_ref,
                       bias_ref, wsp_ref, bsp_ref, wba_ref, wbs_ref, wout_ref, gffn_ref, wr_ref, br_ref, tri_ref,
                       x1_ref, h2_ref, idx_ref, gw_ref, rank_ref, cnt_ref, kwin_ref, vwin_ref, sguv_ref,
                       kcarry, vcarry, run_ref, *, blocks_per_seq, n_blocks):
    i = pl.program_id(0)

    @pl.when(i == 0)
    def _():
        run_ref[...] = jnp.zeros_like(run_ref)

    @pl.when(i == n_blocks)
    def _():
        h2_ref[...] = jnp.zeros_like(h2_ref)
        idx_ref[...] = jnp.zeros_like(idx_ref)
        gw_ref[...] = jnp.zeros_like(gw_ref)
        rank_ref[...] = jnp.zeros_like(rank_ref)

    @pl.when(i < n_blocks)
    def _():
        _mixer_prompt_block(i % blocks_per_seq, sinks_ref, x_ref, w_in_ref, gmix_ref, qg_ref, kg_ref, sgug_ref,
                            seg64_ref, seg128_ref, bias_ref, wsp_ref, bsp_ref, wba_ref, wbs_ref, wout_ref, gffn_ref,
                            wr_ref, br_ref, tri_ref, x1_ref, h2_ref, idx_ref, gw_ref, rank_ref, cnt_ref, kwin_ref,
                            vwin_ref, sguv_ref, kcarry, vcarry, run_ref)


def _mixer_prompt_block(j, sinks_ref, x_ref, w_in_ref, gmix_ref, qg_ref, kg_ref, sgug_ref, seg64_ref, seg128_ref,
                        bias_ref, wsp_ref, bsp_ref, wba_ref, wbs_ref, wout_ref, gffn_ref, wr_ref, br_ref, tri_ref,
                        x1_ref, h2_ref, idx_ref, gw_ref, rank_ref, cnt_ref, kwin_ref, vwin_ref, sguv_ref,
                        kcarry, vcarry, run_ref):
    @pl.when(j == 0)
    def _():
        kcarry[...] = jnp.zeros_like(kcarry)
        vcarry[...] = jnp.zeros_like(vcarry)

    x = x_ref[0]
    h = _rms_rows(x, gmix_ref[...]).astype(BF16)
    qn, kn, v, u, vm, ga, gb = _project(h, w_in_ref, qg_ref[...], kg_ref[...], sgug_ref[...],
                                        seg64_ref[...], seg128_ref[...])
    k_ext = jnp.concatenate([kcarry[...], kn.astype(BF16)], axis=0)
    v_ext = jnp.concatenate([vcarry[...], v.astype(BF16)], axis=0)
    kcarry[...] = k_ext[TOKEN_BLOCK:]
    vcarry[...] = v_ext[TOKEN_BLOCK:]

    tri_mask = (lax.broadcasted_iota(jnp.int32, (MLP_CHUNK, MLP_CHUNK), 0)
                >= lax.broadcasted_iota(jnp.int32, (MLP_CHUNK, MLP_CHUNK), 1))
    att_rows, sgu_rows = [], []
    for pm in range(TOKEN_BLOCK // PAIR):
        r0 = pm * PAIR
        q_stack = _stack_heads(qn[r0:r0 + PAIR])
        k_band = k_ext[r0:r0 + BAND]
        v_band = v_ext[r0:r0 + BAND]
        s = _dot_nt(q_stack, k_band)
        first = jnp.where(j == 0, 0, 1) if pm == 0 else 1
        probs = []
        for head in range(N_HEADS):
            logit = s[head * PAIR:(head + 1) * PAIR] * ATTN_SCALE + bias_ref[first, head]
            probs.append(_softmax_rows([logit], sinks_ref[head])[0].astype(BF16))
        o = _dot(jnp.concatenate(probs, axis=0), v_band)
        att_rows.append(_unstack_heads(o, PAIR))
        cols = []
        for g in range(MLP_GROUPS):
            wm = jnp.where(tri_mask, wsp_ref[g], 0.0).astype(BF16)
            cols.append(_dot(wm, vm[r0:r0 + PAIR, g * LANES:(g + 1) * LANES].astype(BF16)))
        mixed = jnp.concatenate(cols, axis=-1) + bsp_ref[...]
        sgu_rows.append(u[r0:r0 + PAIR] * mixed)
    att = jnp.concatenate(att_rows, axis=0)
    sgu = jnp.concatenate(sgu_rows, axis=0)

    x1 = x + _merge(att, sgu, ga, gb, wba_ref, wbs_ref, wout_ref)
    h2 = _rms_rows(x1, gffn_ref[...])
    idx4, w4, r4 = _route(h2.astype(BF16), wr_ref, br_ref, tri_ref, run_ref)

    x1_ref[0] = x1
    _store_token_major(h2_ref, h2)
    idx_ref[...] = idx4
    gw_ref[...] = w4
    rank_ref[...] = r4
    cnt_ref[...] = run_ref[...]
    kwin_ref[0] = kn[TOKEN_BLOCK - WINDOW:]
    vwin_ref[0] = v[TOKEN_BLOCK - WINDOW:]
    sguv_ref[0] = vm[TOKEN_BLOCK - MLP_CHUNK:]


def _mixer_sample_body(sinks_ref, x_ref, ck_ref, cv_ref, w_in_ref, gmix_ref, qg_ref, kg_ref, sgug_ref, seg64_ref,
                       seg128_ref, biasc_ref, biasn_ref, wsp_ref, bsp_ref, wba_ref, wbs_ref, wout_ref, gffn_ref,
                       wr_ref, br_ref, tri_ref, h2_in_ref, idx_in_ref, gw_in_ref, rank_in_ref,
                       x1_ref, h2_ref, idx_ref, gw_ref, rank_ref, cnt_ref, kwin_ref, vwin_ref, sguv_ref,
                       run_ref, *, n_streams, n_new):
    del h2_in_ref, idx_in_ref, gw_in_ref, rank_in_ref
    run_ref[...] = jnp.zeros_like(run_ref)
    x = x_ref[...]
    h = _rms_rows(x, gmix_ref[...]).astype(BF16)
    qn, kn, v, u, vm, ga, gb = _project(h, w_in_ref, qg_ref[...], kg_ref[...], sgug_ref[...],
                                        seg64_ref[...], seg128_ref[...])
    n_cached = ck_ref.shape[1]
    att_rows = []
    for s_i in range(n_streams):
        r0 = s_i * n_new
        q_stack = _stack_heads(qn[r0:r0 + n_new])
        k_new = kn[r0:r0 + n_new]
        v_new = v[r0:r0 + n_new]
        s_c = _dot_nt(q_stack, ck_ref[s_i].astype(BF16))
        s_n = _dot_nt(q_stack, k_new.astype(BF16))
        pc, pn = [], []
        for head in range(N_HEADS):
            rows = slice(head * n_new, (head + 1) * n_new)
            lc = s_c[rows] * ATTN_SCALE + biasc_ref[rows]
            ln = s_n[rows] * ATTN_SCALE + biasn_ref[rows]
            p_c, p_n = _softmax_rows([lc, ln], sinks_ref[head])
            pc.append(p_c.astype(BF16))
            pn.append(p_n.astype(BF16))
        o = (_dot(jnp.concatenate(pc, axis=0), cv_ref[s_i].astype(BF16))
             + _dot(jnp.concatenate(pn, axis=0), v_new.astype(BF16)))
        att_rows.append(_unstack_heads(o, n_new))
        kwin_ref[s_i, 0:n_cached - n_new] = ck_ref[s_i, n_new:n_cached]
        kwin_ref[s_i, n_cached - n_new:n_cached] = k_new
        vwin_ref[s_i, 0:n_cached - n_new] = cv_ref[s_i, n_new:n_cached]
        vwin_ref[s_i, n_cached - n_new:n_cached] = v_new
    att = jnp.concatenate(att_rows, axis=0)

    rows = n_streams * n_new
    ri = lax.broadcasted_iota(jnp.int32, (rows, rows), 0)
    ci = lax.broadcasted_iota(jnp.int32, (rows, rows), 1)
    keep = jnp.logical_and(ri // n_new == ci // n_new, ri % n_new >= ci % n_new)
    cols = []
    for g in range(MLP_GROUPS):
        wm = jnp.where(keep, wsp_ref[g], 0.0).astype(BF16)
        cols.append(_dot(wm, vm[:, g * LANES:(g + 1) * LANES].astype(BF16)))
    sgu = u * (jnp.concatenate(cols, axis=-1) + bsp_ref[...])

    x1 = x + _merge(att, sgu, ga, gb, wba_ref, wbs_ref, wout_ref)
    h2 = _rms_rows(x1, gffn_ref[...])
    idx4, w4, r4 = _route(h2.astype(BF16), wr_ref, br_ref, tri_ref, run_ref)
    x1_ref[...] = x1
    _store_token_major(h2_ref, h2)
    idx_ref[...] = idx4
    gw_ref[...] = w4
    rank_ref[...] = r4
    cnt_ref[...] = run_ref[...]
    sguv_ref[...] = vm


def _row_copy_in(h2_hbm, tok_ref, xbuf, sem, r):
    src = pl.multiple_of(tok_ref[0, 0, r], ROW_TILE)
    return pltpu.make_async_copy(h2_hbm.at[pl.ds(src, ROW_TILE), :], xbuf.at[_tile_rows(r), :], sem)


def _row_copy_out(ybuf, dst_ref, yg_hbm, sem, r):
    dst = pl.multiple_of(dst_ref[0, 0, r], ROW_TILE)
    return pltpu.make_async_copy(ybuf.at[_tile_rows(r), :], yg_hbm.at[pl.ds(dst, ROW_TILE), :], sem)


def _tile_rows(r):
    start = r * ROW_TILE
    return pl.ds(start if isinstance(r, int) else pl.multiple_of(start, ROW_TILE), ROW_TILE)


def _expert_tile(tok_next_ref, dst_prev_ref, h2_hbm, bgu_ref, bd_ref, yg_hbm, wgu_bf, wd_bf,
                 x_cur, x_next, y_cur, y_prev, sem_in_cur, sem_in_next, sem_out):
    pltpu.make_async_copy(h2_hbm.at[pl.ds(0, EXPERT_TILE * ROW_TILE), :], x_cur, sem_in_cur).wait()
    for r in range(EXPERT_TILE):
        _row_copy_in(h2_hbm, tok_next_ref, x_next, sem_in_next, r).start(priority=r % 2)
    for r in range(EXPERT_TILE):
        _row_copy_out(y_prev, dst_prev_ref, yg_hbm, sem_out, r).start(priority=r % 2)
    xb = _load_token_major(x_cur, EXPERT_TILE).astype(BF16)
    hgu = _dot(xb, wgu_bf[...]) + bgu_ref[0]
    glu = jnp.minimum(hgu[:, :D_EXPERT], SWIGLU_LIMIT)
    lin = jnp.clip(hgu[:, D_EXPERT:], -SWIGLU_LIMIT, SWIGLU_LIMIT)
    act = glu * jax.nn.sigmoid(glu * SWIGLU_ALPHA) * (lin + 1.0)
    _store_token_major(y_cur, _dot(act.astype(BF16), wd_bf[...]) + bd_ref[0])
    pltpu.make_async_copy(y_prev, yg_hbm.at[pl.ds(0, EXPERT_TILE * ROW_TILE), :], sem_out).wait()


def _expert_body(te_ref, nused_ref, tok_first_ref, tok_next_ref, dst_prev_ref, h2_hbm, wgu_ref, bgu_ref, wd_ref,
                 bd_ref, yg_hbm, xbuf0, xbuf1, ybuf0, ybuf1, wgu_bf, wd_bf, sem_in, sem_out):
    i = pl.program_id(0)
    n_used = nused_ref[0]
    xbufs, ybufs = (xbuf0, xbuf1), (ybuf0, ybuf1)

    @pl.when(i == 0)
    def _():
        ybuf1[...] = jnp.zeros_like(ybuf1)

        def first_rows(r, carry):
            _row_copy_in(h2_hbm, tok_first_ref, xbuf0, sem_in.at[0], r).start()
            return carry
        lax.fori_loop(0, EXPERT_TILE, first_rows, 0)

    @pl.when(jnp.logical_and(i < n_used, jnp.logical_or(i == 0, te_ref[i] != te_ref[jnp.maximum(i - 1, 0)])))
    def _():
        wgu_bf[...] = wgu_ref[0].astype(BF16)
        wd_bf[...] = wd_ref[0].astype(BF16)

    for par in range(2):
        @pl.when(jnp.logical_and(i < n_used, i % 2 == par))
        def _(par=par):
            _expert_tile(tok_next_ref, dst_prev_ref, h2_hbm, bgu_ref, bd_ref, yg_hbm, wgu_bf, wd_bf,
                         xbufs[par], xbufs[1 - par], ybufs[par], ybufs[1 - par],
                         sem_in.at[par], sem_in.at[1 - par], sem_out)

        @pl.when(jnp.logical_and(i == n_used, i % 2 == par))
        def _(par=par):
            pltpu.make_async_copy(h2_hbm.at[pl.ds(0, EXPERT_TILE * ROW_TILE), :], xbufs[par], sem_in.at[par]).wait()

            def last_rows(r, carry):
                _row_copy_out(ybufs[1 - par], dst_prev_ref, yg_hbm, sem_out, r).start()
                return carry
            lax.fori_loop(0, EXPERT_TILE, last_rows, 0)
            pltpu.make_async_copy(ybufs[1 - par], yg_hbm.at[pl.ds(0, EXPERT_TILE * ROW_TILE), :], sem_out).wait()


def _combine_body(x1_ref, gw_ref, y0_ref, y1_ref, y2_ref, y3_ref, out_ref):
    gw = gw_ref[...]
    rows = x1_ref.shape[0]
    for s in range(ROW_TILE):
        cols = slice(s * LANES, (s + 1) * LANES)
        acc = x1_ref[:, cols]
        for k, y_ref in enumerate((y0_ref, y1_ref, y2_ref, y3_ref)):
            acc = acc + gw[:, k:k + 1] * y_ref[pl.ds(s, rows, stride=ROW_TILE), :]
        out_ref[:, cols] = acc


def _const_spec(shape):
    nd = len(shape)
    return pl.BlockSpec(shape, lambda *_: (0,) * nd, pipeline_mode=pl.Buffered(1))


def _q_perm():
    cols = np.arange(ATTN_WIDTH)
    j, half, d = cols // LANES, (cols % LANES) // HEAD_DIM, cols % HEAD_DIM
    return (j + Q_PER_KV * half) * HEAD_DIM + d


def _alibi_slopes():
    return 2.0 ** (-8.0 * np.arange(1, N_HEADS + 1) / N_HEADS)


def _prompt_bias():
    qi = np.arange(PAIR)[:, None]
    kj = np.arange(BAND)[None, :]
    dist = np.abs(qi + WINDOW - kj).astype(np.float64)
    cq, ck = qi // CHUNK, kj // CHUNK
    in_band = (ck >= cq) & (ck <= cq + WINDOW // CHUNK)
    base = -_alibi_slopes()[:, None, None] * dist[None]
    later = np.where(in_band[None], base, NEG_INF)
    first = np.where((kj >= WINDOW)[None], later, NEG_INF)
    return np.stack([first, later]).astype(np.float32)


def _sample_bias(n_new, n_cached):
    qi = np.arange(n_new)[:, None]
    dc = np.abs(qi + n_cached - np.arange(n_cached)[None, :]).astype(np.float64)
    dn = np.abs(qi - np.arange(n_new)[None, :]).astype(np.float64)
    sl = _alibi_slopes()[:, None, None]
    bc = (-sl * dc[None]).reshape(N_HEADS * n_new, n_cached)
    bn = (-sl * dn[None]).reshape(N_HEADS * n_new, n_new)
    return bc.astype(np.float32), bn.astype(np.float32)


def kernel(x_prompt, x_sample, cache_k_win, cache_v_win, g_mix, w_in, q_norm_g, k_norm_g, attn_sinks, sgu_norm_g,
           w_spatial, b_spatial, w_branch_attn, w_branch_sgu, w_out, g_ffn, w_router, b_router, w_gate_up,
           b_gate_up, w_down, b_down):
    n_b, seq, _ = x_prompt.shape
    n_streams, n_new, _ = x_sample.shape
    n_cached = cache_k_win.shape[2]
    n_prompt = n_b * seq
    n_sample = n_streams * n_new
    n_tok = n_prompt + n_sample
    assert seq % TOKEN_BLOCK == 0 and n_sample == TOKEN_BLOCK and n_cached == WINDOW
    blocks_per_seq = seq // TOKEN_BLOCK
    n_prompt_blocks = n_prompt // TOKEN_BLOCK

    perm = _q_perm()
    w_in_l = w_in[0]
    w_in_b = jnp.concatenate([w_in_l[:, perm], w_in_l[:, Q_END:]], axis=1).astype(BF16)
    wba_b = w_branch_attn[0][perm, :].astype(BF16)
    wbs_b = w_branch_sgu[0].astype(BF16)
    wout_b = w_out[0].astype(BF16)
    wr_b = w_router[0].astype(BF16)
    br = b_router[0].reshape(1, N_EXPERTS).astype(F32)
    gmix = g_mix[0].reshape(1, D_MODEL)
    gffn = g_ffn[0].reshape(1, D_MODEL)
    qg = jnp.tile(q_norm_g[0], LANES // HEAD_DIM).reshape(1, LANES)
    kg = jnp.tile(k_norm_g[0], LANES // HEAD_DIM).reshape(1, LANES)
    sgug = sgu_norm_g[0].reshape(1, MLP_WIDTH)
    sinks = attn_sinks[0].astype(F32)
    lane_seg = np.arange(LANES) // HEAD_DIM
    seg64 = jnp.asarray((lane_seg[:, None] == lane_seg[None, :]) / HEAD_DIM, BF16)
    seg128 = jnp.full((LANES, LANES), 1.0 / MLP_GROUP_DIM, BF16)
    tri = jnp.asarray(np.tril(np.ones((TOKEN_BLOCK, TOKEN_BLOCK)), -1), BF16)
    wsp = w_spatial[0]
    bsp_p = jnp.repeat(b_spatial[0].T, MLP_GROUP_DIM, axis=1)
    wsp_s = jnp.tile(wsp[:, :n_new, :n_new], (1, n_streams, n_streams))
    bsp_s = jnp.tile(jnp.repeat(b_spatial[0][:, :n_new].T, MLP_GROUP_DIM, axis=1), (n_streams, 1))
    bias_p = jnp.asarray(_prompt_bias())
    bias_c, bias_n = (jnp.asarray(a) for a in _sample_bias(n_new, n_cached))

    smem = pl.BlockSpec(memory_space=pltpu.SMEM)
    tb = TOKEN_BLOCK
    last_blk = n_prompt_blocks - 1
    seq_of = lambda i: jnp.minimum(i, last_blk) // blocks_per_seq
    x_spec = pl.BlockSpec((1, tb, D_MODEL), lambda i: (seq_of(i), jnp.minimum(i, last_blk) % blocks_per_seq, 0))
    tok_spec = lambda width: pl.BlockSpec((tb, width), lambda i: (i, 0))
    win_spec = lambda width: pl.BlockSpec((1, WINDOW, width), lambda i: (seq_of(i), 0, 0))
    route_shapes = [jax.ShapeDtypeStruct((n_tok, TOP_K), jnp.int32), jax.ShapeDtypeStruct((n_tok, TOP_K), F32),
                    jax.ShapeDtypeStruct((n_tok, TOP_K), jnp.int32)]

    x1p, h2, idx_p, gw_p, rank_p, cnt_p, kwin_p, vwin_p, sguv_p = pl.pallas_call(
        functools.partial(_mixer_prompt_body, blocks_per_seq=blocks_per_seq, n_blocks=n_prompt_blocks),
        grid=(n_prompt_blocks + 1,),
        in_specs=[smem, x_spec,
                  _const_spec((D_MODEL, IN_COLS)), _const_spec((1, D_MODEL)), _const_spec((1, LANES)),
                  _const_spec((1, LANES)), _const_spec((1, MLP_WIDTH)), _const_spec((LANES, LANES)),
                  _const_spec((LANES, LANES)), _const_spec((2, N_HEADS, PAIR, BAND)),
                  _const_spec((MLP_GROUPS, MLP_CHUNK, MLP_CHUNK)), _const_spec((MLP_CHUNK, MLP_WIDTH)),
                  _const_spec((ATTN_WIDTH, D_MODEL)), _const_spec((MLP_WIDTH, D_MODEL)),
                  _const_spec((D_MODEL, D_MODEL)), _const_spec((1, D_MODEL)), _const_spec((D_MODEL, N_EXPERTS)),
                  _const_spec((1, N_EXPERTS)), _const_spec((tb, tb))],
        out_specs=[x_spec,
                   pl.BlockSpec((tb * ROW_TILE, LANES), lambda i: (i, 0)), tok_spec(TOP_K), tok_spec(TOP_K), tok_spec(TOP_K),
                   pl.BlockSpec((1, N_EXPERTS), lambda i: (0, 0)),
                   win_spec(KV_WIDTH), win_spec(KV_WIDTH), win_spec(MLP_WIDTH)],
        out_shape=[jax.ShapeDtypeStruct((n_b, seq, D_MODEL), F32), jax.ShapeDtypeStruct((n_tok * ROW_TILE, LANES), F32),
                   *route_shapes, jax.ShapeDtypeStruct((1, N_EXPERTS), F32),
                   jax.ShapeDtypeStruct((n_b, WINDOW, KV_WIDTH), F32),
                   jax.ShapeDtypeStruct((n_b, WINDOW, KV_WIDTH), F32),
                   jax.ShapeDtypeStruct((n_b, MLP_CHUNK, MLP_WIDTH), F32)],
        scratch_shapes=[pltpu.VMEM((WINDOW, KV_WIDTH), BF16), pltpu.VMEM((WINDOW, KV_WIDTH), BF16),
                        pltpu.VMEM((1, N_EXPERTS), F32)],
        compiler_params=pltpu.CompilerParams(dimension_semantics=("arbitrary",), vmem_limit_bytes=VMEM_LIMIT),
        name="mixer_prompt",
    )(sinks, x_prompt, w_in_b, gmix, qg, kg, sgug, seg64, seg128, bias_p, wsp, bsp_p, wba_b, wbs_b, wout_b,
      gffn, wr_b, br, tri)

    full = lambda shape: pl.BlockSpec(shape, lambda i: (0,) * len(shape))
    last_tok = lambda width: pl.BlockSpec((tb, width), lambda i: (n_prompt_blocks, 0))
    any_spec = pl.BlockSpec(memory_space=pl.ANY)
    ck = cache_k_win[0].reshape(n_streams, n_cached, KV_WIDTH)
    cv = cache_v_win[0].reshape(n_streams, n_cached, KV_WIDTH)
    x1s, h2, idx_a, gw_a, rank_a, cnt_s, kwin_s, vwin_s, sguv_s = pl.pallas_call(
        functools.partial(_mixer_sample_body, n_streams=n_streams, n_new=n_new),
        grid=(1,),
        in_specs=[smem, full((n_sample, D_MODEL)), full((n_streams, n_cached, KV_WIDTH)),
                  full((n_streams, n_cached, KV_WIDTH)),
                  full((D_MODEL, IN_COLS)), full((1, D_MODEL)), full((1, LANES)), full((1, LANES)),
                  full((1, MLP_WIDTH)), full((LANES, LANES)), full((LANES, LANES)),
                  full((N_HEADS * n_new, n_cached)), full((N_HEADS * n_new, n_new)),
                  full((MLP_GROUPS, n_sample, n_sample)), full((n_sample, MLP_WIDTH)),
                  full((ATTN_WIDTH, D_MODEL)), full((MLP_WIDTH, D_MODEL)), full((D_MODEL, D_MODEL)),
                  full((1, D_MODEL)), full((D_MODEL, N_EXPERTS)), full((1, N_EXPERTS)), full((tb, tb)),
                  any_spec, any_spec, any_spec, any_spec],
        out_specs=[full((n_sample, D_MODEL)), pl.BlockSpec((tb * ROW_TILE, LANES), lambda i: (n_prompt_blocks, 0)), last_tok(TOP_K), last_tok(TOP_K), last_tok(TOP_K),
                   full((1, N_EXPERTS)), full((n_streams, n_cached, KV_WIDTH)),
                   full((n_streams, n_cached, KV_WIDTH)), full((n_sample, MLP_WIDTH))],
        out_shape=[jax.ShapeDtypeStruct((n_sample, D_MODEL), F32), jax.ShapeDtypeStruct((n_tok * ROW_TILE, LANES), F32),
                   *route_shapes, jax.ShapeDtypeStruct((1, N_EXPERTS), F32),
                   jax.ShapeDtypeStruct((n_streams, n_cached, KV_WIDTH), F32),
                   jax.ShapeDtypeStruct((n_streams, n_cached, KV_WIDTH), F32),
                   jax.ShapeDtypeStruct((n_sample, MLP_WIDTH), F32)],
        input_output_aliases={22: 1, 23: 2, 24: 3, 25: 4},
        scratch_shapes=[pltpu.VMEM((1, N_EXPERTS), F32)],
        compiler_params=pltpu.CompilerParams(dimension_semantics=("arbitrary",), vmem_limit_bytes=VMEM_LIMIT),
        name="mixer_sample",
    )(sinks, x_sample.reshape(n_sample, D_MODEL), ck, cv, w_in_b, gmix, qg, kg, sgug, seg64, seg128, bias_c,
      bias_n, wsp_s, bsp_s, wba_b, wbs_b, wout_b, gffn, wr_b, br, tri, h2, idx_p, gw_p, rank_p)

    tm = EXPERT_TILE
    cnt_prompt = cnt_p[0].astype(jnp.int32)
    cnt_all = cnt_prompt + cnt_s[0].astype(jnp.int32)
    padded = ((cnt_all + tm - 1) // tm) * tm
    pad_end = jnp.cumsum(padded)
    pad_off = pad_end - padded
    is_sample = (jnp.arange(n_tok) >= n_prompt)[:, None, None]
    first_row = jnp.where(is_sample, (pad_off + cnt_prompt)[None, None, :], pad_off[None, None, :])
    picked = idx_a[:, :, None] == jnp.arange(N_EXPERTS, dtype=jnp.int32)[None, None, :]
    dest = (rank_a + jnp.sum(jnp.where(picked, first_row, 0), axis=-1)).reshape(-1)
    n_tiles = (n_tok * TOP_K) // tm + N_EXPERTS
    n_rows = n_tiles * tm
    tok_of_pair = jnp.repeat(jnp.arange(n_tok, dtype=jnp.int32), TOP_K)
    slot_of_pair = jnp.tile(jnp.arange(TOP_K, dtype=jnp.int32), n_tok) * n_tok + tok_of_pair
    n_pairs = TOP_K * n_tok
    spare = n_pairs + jnp.arange(tm, dtype=jnp.int32)
    dst_sorted = jnp.tile(spare, n_tiles).at[dest].set(slot_of_pair, unique_indices=True)
    tok_sorted = jnp.where(dst_sorted < n_pairs, dst_sorted % n_tok, 0)
    tok_tiles = tok_sorted.reshape(n_tiles, 1, tm)
    n_steps = n_tiles + 1
    tok_next = jnp.concatenate([tok_tiles[1:], jnp.zeros((2, 1, tm), jnp.int32)], axis=0)
    dst_prev = jnp.concatenate([spare.reshape(1, 1, tm), dst_sorted.reshape(n_tiles, 1, tm)], axis=0)
    n_used = (pad_end[-1] // tm).astype(jnp.int32)
    tile_start = jnp.minimum(jnp.arange(n_steps, dtype=jnp.int32), n_used - 1) * tm
    tile_expert = jnp.sum((pad_end[None, :] <= tile_start[:, None]).astype(jnp.int32), axis=1)
    tile_expert = jnp.minimum(tile_expert, N_EXPERTS - 1)

    yg = pl.pallas_call(
        _expert_body,
        grid_spec=pltpu.PrefetchScalarGridSpec(
            num_scalar_prefetch=2,
            grid=(n_steps,),
            in_specs=[pl.BlockSpec((1, 1, tm), lambda i, te, nu: (0, 0, 0), memory_space=pltpu.SMEM),
                      pl.BlockSpec((1, 1, tm), lambda i, te, nu: (i, 0, 0), memory_space=pltpu.SMEM),
                      pl.BlockSpec((1, 1, tm), lambda i, te, nu: (i, 0, 0), memory_space=pltpu.SMEM),
                      pl.BlockSpec(memory_space=pl.ANY),
                      pl.BlockSpec((1, D_MODEL, 2 * D_EXPERT), lambda i, te, nu: (te[i], 0, 0)),
                      pl.BlockSpec((1, 1, 2 * D_EXPERT), lambda i, te, nu: (te[i], 0, 0)),
                      pl.BlockSpec((1, D_EXPERT, D_MODEL), lambda i, te, nu: (te[i], 0, 0)),
                      pl.BlockSpec((1, 1, D_MODEL), lambda i, te, nu: (te[i], 0, 0))],
            out_specs=pl.BlockSpec(memory_space=pl.ANY),
            scratch_shapes=[pltpu.VMEM((tm * ROW_TILE, LANES), F32), pltpu.VMEM((tm * ROW_TILE, LANES), F32),
                            pltpu.VMEM((tm * ROW_TILE, LANES), F32), pltpu.VMEM((tm * ROW_TILE, LANES), F32),
                            pltpu.VMEM((D_MODEL, 2 * D_EXPERT), BF16), pltpu.VMEM((D_EXPERT, D_MODEL), BF16),
                            pltpu.SemaphoreType.DMA((2,)), pltpu.SemaphoreType.DMA(())]),
        out_shape=jax.ShapeDtypeStruct(((n_pairs + tm) * ROW_TILE, LANES), F32),
        compiler_params=pltpu.CompilerParams(dimension_semantics=("arbitrary",), vmem_limit_bytes=VMEM_LIMIT),
        name="moe_experts",
    )(tile_expert, n_used.reshape(1), tok_tiles[0:1] * ROW_TILE, tok_next * ROW_TILE, dst_prev * ROW_TILE,
      h2, w_gate_up[0], b_gate_up[0].reshape(N_EXPERTS, 1, 2 * D_EXPERT), w_down[0],
      b_down[0].reshape(N_EXPERTS, 1, D_MODEL))

    n_tok_blocks = n_tok // tb

    def combine(x1, first_block, n_blocks):
        y_spec = lambda k: pl.BlockSpec((tb * ROW_TILE, LANES), lambda i: (k * n_tok_blocks + first_block + i, 0))
        return pl.pallas_call(
            _combine_body,
            grid=(n_blocks,),
            in_specs=[pl.BlockSpec((tb, D_MODEL), lambda i: (i, 0)),
                      pl.BlockSpec((tb, TOP_K), lambda i: (first_block + i, 0)),
                      y_spec(0), y_spec(1), y_spec(2), y_spec(3)],
            out_specs=pl.BlockSpec((tb, D_MODEL), lambda i: (i, 0)),
            out_shape=jax.ShapeDtypeStruct(x1.shape, F32),
            compiler_params=pltpu.CompilerParams(dimension_semantics=("arbitrary",)),
            name="moe_combine",
        )(x1, gw_a, yg, yg, yg, yg)

    y_prompt = combine(x1p.reshape(n_prompt, D_MODEL), 0, n_prompt_blocks).reshape(n_b, seq, D_MODEL)
    y_sample = combine(x1s, n_prompt_blocks, 1).reshape(n_streams, n_new, D_MODEL)

    kv_shape = (N_KV_HEADS, HEAD_DIM)
    sg_shape = (MLP_GROUPS, MLP_GROUP_DIM)
    return (y_prompt, y_sample,
            kwin_p.reshape(1, n_b, WINDOW, *kv_shape), vwin_p.reshape(1, n_b, WINDOW, *kv_shape),
            kwin_s.reshape(1, n_streams, n_cached, *kv_shape), vwin_s.reshape(1, n_streams, n_cached, *kv_shape),
            sguv_p.reshape(1, n_b, MLP_CHUNK, *sg_shape), sguv_s.reshape(1, n_streams, n_new, *sg_shape))
```

```python
import functools

import jax
import jax.numpy as jnp
import numpy as np
from jax import lax
from jax.experimental import pallas as pl
from jax.experimental.pallas import tpu as pltpu

D_MODEL = 1024
CHUNK = 64
WINDOW = 128
HEAD_DIM = 64
N_HEADS = 8
N_KV_HEADS = 2
Q_PER_KV = N_HEADS // N_KV_HEADS
ATTN_WIDTH = N_HEADS * HEAD_DIM
KV_WIDTH = N_KV_HEADS * HEAD_DIM
ATTN_SCALE = HEAD_DIM ** -0.5
MLP_CHUNK = 128
MLP_GROUPS = 8
MLP_WIDTH = D_MODEL
MLP_GROUP_DIM = MLP_WIDTH // MLP_GROUPS
N_EXPERTS = 32
TOP_K = 4
D_EXPERT = D_MODEL
SWIGLU_ALPHA = 1.702
SWIGLU_LIMIT = 7.0
EPS = 1e-6
NEG_INF = -1e30
Q_END = ATTN_WIDTH
K_END = Q_END + KV_WIDTH
V_END = K_END + KV_WIDTH
U_END = V_END + MLP_WIDTH
VM_END = U_END + MLP_WIDTH
GA_END = VM_END + D_MODEL
IN_COLS = GA_END + D_MODEL

LANES = 128
ROW_TILE = D_MODEL // LANES
TOKEN_BLOCK = 256
PAIR = 2 * CHUNK
BAND = PAIR + WINDOW
EXPERT_TILE = 256
VMEM_LIMIT = 56 * 1024 * 1024

F32 = jnp.float32
BF16 = jnp.bfloat16


def _dot(a, b):
    return jnp.dot(a, b, preferred_element_type=F32)


def _dot_nt(a, b):
    return lax.dot_general(a, b, (((1,), (1,)), ((), ())), preferred_element_type=F32)


def _store_token_major(ref, val):
    rows = val.shape[0]
    for s in range(ROW_TILE):
        ref[pl.ds(s, rows, stride=ROW_TILE), :] = val[:, s * LANES:(s + 1) * LANES]


def _load_token_major(ref, rows):
    return jnp.concatenate([ref[pl.ds(s, rows, stride=ROW_TILE), :] for s in range(ROW_TILE)], axis=-1)


def _segment_mean(sq, seg):
    hi = sq.astype(BF16)
    lo = (sq - hi.astype(F32)).astype(BF16)
    return _dot(hi, seg) + _dot(lo, seg)


def _rms_rows(x, gain):
    ms = jnp.mean(x * x, axis=-1, keepdims=True)
    return (x * lax.rsqrt(ms + EPS)) * gain


def _project(h, w_in_ref, qg, kg, sgug, seg64, seg128):
    qkv = _dot(h, w_in_ref[:, 0:V_END])
    qk_cols = []
    for c in range(K_END // LANES):
        blk = qkv[:, c * LANES:(c + 1) * LANES]
        ms = _segment_mean(blk * blk, seg64)
        g = qg if c < Q_END // LANES else kg
        qk_cols.append((blk * lax.rsqrt(ms + EPS)) * g)
    qn = jnp.concatenate(qk_cols[:Q_END // LANES], axis=-1)
    kn = qk_cols[Q_END // LANES]
    v = qkv[:, K_END:V_END]
    u = jax.nn.gelu(_dot(h, w_in_ref[:, V_END:U_END]))
    vg = jax.nn.gelu(_dot(h, w_in_ref[:, U_END:VM_END]))
    vm_cols = []
    for g in range(MLP_GROUPS):
        blk = vg[:, g * LANES:(g + 1) * LANES]
        ms = _segment_mean(blk * blk, seg128)
        vm_cols.append((blk * lax.rsqrt(ms + EPS)) * sgug[:, g * LANES:(g + 1) * LANES])
    vm = jnp.concatenate(vm_cols, axis=-1)
    ga = jax.nn.sigmoid(_dot(h, w_in_ref[:, VM_END:GA_END]))
    gb = jax.nn.sigmoid(_dot(h, w_in_ref[:, GA_END:IN_COLS]))
    return qn, kn, v, u, vm, ga, gb


def _stack_heads(q_rows):
    lane = lax.broadcasted_iota(jnp.int32, (q_rows.shape[0], LANES), 1)
    blocks = []
    for head in range(N_HEADS):
        j, half = head % Q_PER_KV, head // Q_PER_KV
        col = q_rows[:, j * LANES:(j + 1) * LANES]
        keep = (lane < HEAD_DIM) if half == 0 else (lane >= HEAD_DIM)
        blocks.append(jnp.where(keep, col, 0.0))
    return jnp.concatenate(blocks, axis=0).astype(BF16)


def _unstack_heads(o, rows):
    lane = lax.broadcasted_iota(jnp.int32, (rows, LANES), 1)
    cols = []
    for j in range(Q_PER_KV):
        lo = o[j * rows:(j + 1) * rows]
        hi = o[(j + Q_PER_KV) * rows:(j + Q_PER_KV + 1) * rows]
        cols.append(jnp.where(lane < HEAD_DIM, lo, hi))
    return jnp.concatenate(cols, axis=-1)


def _merge(att, sgu, ga, gb, wba_ref, wbs_ref, wout_ref):
    m = ga * _dot(att.astype(BF16), wba_ref[...]) + gb * _dot(sgu.astype(BF16), wbs_ref[...])
    return _dot(m.astype(BF16), wout_ref[...])


def _route(h2, wr_ref, br_ref, tri_ref, run_ref):
    rows = h2.shape[0]
    logits = _dot(h2, wr_ref[...]) + br_ref[...]
    eidx = lax.broadcasted_iota(jnp.int32, (rows, N_EXPERTS), 1).astype(F32)
    work = logits
    vals, picks, onehots = [], [], []
    for _ in range(TOP_K):
        m = jnp.max(work, axis=-1, keepdims=True)
        sel = jnp.min(jnp.where(work == m, eidx, float(N_EXPERTS)), axis=-1, keepdims=True)
        oh = eidx == sel
        vals.append(m)
        picks.append(sel)
        onehots.append(oh)
        work = jnp.where(oh, -jnp.inf, work)
    exps = [jnp.exp(v - vals[0]) for v in vals]
    den = exps[0] + exps[1] + exps[2] + exps[3]
    mask = jnp.zeros((rows, N_EXPERTS), F32)
    for oh in onehots:
        mask = mask + jnp.where(oh, 1.0, 0.0)
    before = _dot(tri_ref[...], mask.astype(BF16)) + run_ref[...]
    ranks = [jnp.sum(jnp.where(oh, before, 0.0), axis=-1, keepdims=True) for oh in onehots]
    run_ref[...] = run_ref[...] + jnp.sum(mask, axis=0, keepdims=True)
    k4 = lax.broadcasted_iota(jnp.int32, (rows, TOP_K), 1)
    idx4 = jnp.zeros((rows, TOP_K), jnp.int32)
    w4 = jnp.zeros((rows, TOP_K), F32)
    r4 = jnp.zeros((rows, TOP_K), jnp.int32)
    for k in range(TOP_K):
        idx4 = jnp.where(k4 == k, picks[k].astype(jnp.int32), idx4)
        w4 = jnp.where(k4 == k, exps[k] / den, w4)
        r4 = jnp.where(k4 == k, ranks[k].astype(jnp.int32), r4)
    return idx4, w4, r4


def _softmax_rows(parts, sink):
    m = sink
    for l in parts:
        m = jnp.maximum(m, jnp.max(l, axis=-1, keepdims=True))
    es = [jnp.exp(l - m) for l in parts]
    den = jnp.exp(sink - m)
    for e in es:
        den = den + jnp.sum(e, axis=-1, keepdims=True)
    return [e / den for e in es]


def _mixer_prompt_body(sinks_ref, x_ref, w_in_ref, gmix_ref, qg_ref, kg_ref, sgug_ref, seg64_ref, seg128_ref,
                       bias_ref, wsp_ref, bsp_ref, wba_ref, wbs_ref, wout_ref, gffn_ref, wr_ref, br_ref, tri_ref,
                       x1_ref, h2_ref, idx_ref, gw_ref, rank_ref, cnt_ref, kwin_ref, vwin_ref, sguv_ref,
                       kcarry, vcarry, run_ref, *, blocks_per_seq, n_blocks):
    i = pl.program_id(0)

    @pl.when(i == 0)
    def _():
        run_ref[...] = jnp.zeros_like(run_ref)

    @pl.when(i == n_blocks)
    def _():
        h2_ref[...] = jnp.zeros_like(h2_ref)
        idx_ref[...] = jnp.zeros_like(idx_ref)
        gw_ref[...] = jnp.zeros_like(gw_ref)
        rank_ref[...] = jnp.zeros_like(rank_ref)

    @pl.when(i < n_blocks)
    def _():
        _mixer_prompt_block(i % blocks_per_seq, sinks_ref, x_ref, w_in_ref, gmix_ref, qg_ref, kg_ref, sgug_ref,
                            seg64_ref, seg128_ref, bias_ref, wsp_ref, bsp_ref, wba_ref, wbs_ref, wout_ref, gffn_ref,
                            wr_ref, br_ref, tri_ref, x1_ref, h2_ref, idx_ref, gw_ref, rank_ref, cnt_ref, kwin_ref,
                            vwin_ref, sguv_ref, kcarry, vcarry, run_ref)


def _mixer_prompt_block(j, sinks_ref, x_ref, w_in_ref, gmix_ref, qg_ref, kg_ref, sgug_ref, seg64_ref, seg128_ref,
                        bias_ref, wsp_ref, bsp_ref, wba_ref, wbs_ref, wout_ref, gffn_ref, wr_ref, br_ref, tri_ref,
                        x1_ref, h2_ref, idx_ref, gw_ref, rank_ref, cnt_ref, kwin_ref, vwin_ref, sguv_ref,
                        kcarry, vcarry, run_ref):
    @pl.when(j == 0)
    def _():
        kcarry[...] = jnp.zeros_like(kcarry)
        vcarry[...] = jnp.zeros_like(vcarry)

    x = x_ref[0]
    h = _rms_rows(x, gmix_ref[...]).astype(BF16)
    qn, kn, v, u, vm, ga, gb = _project(h, w_in_ref, qg_ref[...], kg_ref[...], sgug_ref[...],
                                        seg64_ref[...], seg128_ref[...])
    k_ext = jnp.concatenate([kcarry[...], kn.astype(BF16)], axis=0)
    v_ext = jnp.concatenate([vcarry[...], v.astype(BF16)], axis=0)
    kcarry[...] = k_ext[TOKEN_BLOCK:]
    vcarry[...] = v_ext[TOKEN_BLOCK:]

    tri_mask = (lax.broadcasted_iota(jnp.int32, (MLP_CHUNK, MLP_CHUNK), 0)
                >= lax.broadcasted_iota(jnp.int32, (MLP_CHUNK, MLP_CHUNK), 1))
    att_rows, sgu_rows = [], []
    for pm in range(TOKEN_BLOCK // PAIR):
        r0 = pm * PAIR
        q_stack = _stack_heads(qn[r0:r0 + PAIR])
        k_band = k_ext[r0:r0 + BAND]
        v_band = v_ext[r0:r0 + BAND]
        s = _dot_nt(q_stack, k_band)
        first = jnp.where(j == 0, 0, 1) if pm == 0 else 1
        probs = []
        for head in range(N_HEADS):
            logit = s[head * PAIR:(head + 1) * PAIR] * ATTN_SCALE + bias_ref[first, head]
            probs.append(_softmax_rows([logit], sinks_ref[head])[0].astype(BF16))
        o = _dot(jnp.concatenate(probs, axis=0), v_band)
        att_rows.append(_unstack_heads(o, PAIR))
        cols = []
        for g in range(MLP_GROUPS):
            wm = jnp.where(tri_mask, wsp_ref[g], 0.0).astype(BF16)
            cols.append(_dot(wm, vm[r0:r0 + PAIR, g * LANES:(g + 1) * LANES].astype(BF16)))
        mixed = jnp.concatenate(cols, axis=-1) + bsp_ref[...]
        sgu_rows.append(u[r0:r0 + PAIR] * mixed)
    att = jnp.concatenate(att_rows, axis=0)
    sgu = jnp.concatenate(sgu_rows, axis=0)

    x1 = x + _merge(att, sgu, ga, gb, wba_ref, wbs_ref, wout_ref)
    h2 = _rms_rows(x1, gffn_ref[...])
    idx4, w4, r4 = _route(h2.astype(BF16), wr_ref, br_ref, tri_ref, run_ref)

    x1_ref[0] = x1
    _store_token_major(h2_ref, h2)
    idx_ref[...] = idx4
    gw_ref[...] = w4
    rank_ref[...] = r4
    cnt_ref[...] = run_ref[...]
    kwin_ref[0] = kn[TOKEN_BLOCK - WINDOW:]
    vwin_ref[0] = v[TOKEN_BLOCK - WINDOW:]
    sguv_ref[0] = vm[TOKEN_BLOCK - MLP_CHUNK:]


def _mixer_sample_body(sinks_ref, x_ref, ck_ref, cv_ref, w_in_ref, gmix_ref, qg_ref, kg_ref, sgug_ref, seg64_ref,
                       seg128_ref, biasc_ref, biasn_ref, wsp_ref, bsp_ref, wba_ref, wbs_ref, wout_ref, gffn_ref,
                       wr_ref, br_ref, tri_ref, h2_in_ref, idx_in_ref, gw_in_ref, rank_in_ref,
                       x1_ref, h2_ref, idx_ref, gw_ref, rank_ref, cnt_ref, kwin_ref, vwin_ref, sguv_ref,
                       run_ref, *, n_streams, n_new):
    del h2_in_ref, idx_in_ref, gw_in_ref, rank_in_ref
    run_ref[...] = jnp.zeros_like(run_ref)
    x = x_ref[...]
    h = _rms_rows(x, gmix_ref[...]).astype(BF16)
    qn, kn, v, u, vm, ga, gb = _project(h, w_in_ref, qg_ref[...], kg_ref[...], sgug_ref[...],
                                        seg64_ref[...], seg128_ref[...])
    n_cached = ck_ref.shape[1]
    att_rows = []
    for s_i in range(n_streams):
        r0 = s_i * n_new
        q_stack = _stack_heads(qn[r0:r0 + n_new])
        k_new = kn[r0:r0 + n_new]
        v_new = v[r0:r0 + n_new]
        s_c = _dot_nt(q_stack, ck_ref[s_i].astype(BF16))
        s_n = _dot_nt(q_stack, k_new.astype(BF16))
        pc, pn = [], []
        for head in range(N_HEADS):
            rows = slice(head * n_new, (head + 1) * n_new)
            lc = s_c[rows] * ATTN_SCALE + biasc_ref[rows]
            ln = s_n[rows] * ATTN_SCALE + biasn_ref[rows]
            p_c, p_n = _softmax_rows([lc, ln], sinks_ref[head])
            pc.append(p_c.astype(BF16))
            pn.append(p_n.astype(BF16))
        o = (_dot(jnp.concatenate(pc, axis=0), cv_ref[s_i].astype(BF16))
             + _dot(jnp.concatenate(pn, axis=0), v_new.astype(BF16)))
        att_rows.append(_unstack_heads(o, n_new))
        kwin_ref[s_i, 0:n_cached - n_new] = ck_ref[s_i, n_new:n_cached]
        kwin_ref[s_i, n_cached - n_new:n_cached] = k_new
        vwin_ref[s_i, 0:n_cached - n_new] = cv_ref[s_i, n_new:n_cached]
        vwin_ref[s_i, n_cached - n_new:n_cached] = v_new
    att = jnp.concatenate(att_rows, axis=0)

    rows = n_streams * n_new
    ri = lax.broadcasted_iota(jnp.int32, (rows, rows), 0)
    ci = lax.broadcasted_iota(jnp.int32, (rows, rows), 1)
    keep = jnp.logical_and(ri // n_new == ci // n_new, ri % n_new >= ci % n_new)
    cols = []
    for g in range(MLP_GROUPS):
        wm = jnp.where(keep, wsp_ref[g], 0.0).astype(BF16)
        cols.append(_dot(wm, vm[:, g * LANES:(g + 1) * LANES].astype(BF16)))
    sgu = u * (jnp.concatenate(cols, axis=-1) + bsp_ref[...])

    x1 = x + _merge(att, sgu, ga, gb, wba_ref, wbs_ref, wout_ref)
    h2 = _rms_rows(x1, gffn_ref[...])
    idx4, w4, r4 = _route(h2.astype(BF16), wr_ref, br_ref, tri_ref, run_ref)
    x1_ref[...] = x1
    _store_token_major(h2_ref, h2)
    idx_ref[...] = idx4
    gw_ref[...] = w4
    rank_ref[...] = r4
    cnt_ref[...] = run_ref[...]
    sguv_ref[...] = vm


def _row_copy_in(h2_hbm, tok_ref, xbuf, sem, r):
    src = pl.multiple_of(tok_ref[0, 0, r], ROW_TILE)
    return pltpu.make_async_copy(h2_hbm.at[pl.ds(src, ROW_TILE), :], xbuf.at[_tile_rows(r), :], sem)


def _row_copy_out(ybuf, dst_ref, yg_hbm, sem, r):
    dst = pl.multiple_of(dst_ref[0, 0, r], ROW_TILE)
    return pltpu.make_async_copy(ybuf.at[_tile_rows(r), :], yg_hbm.at[pl.ds(dst, ROW_TILE), :], sem)


def _tile_rows(r):
    start = r * ROW_TILE
    return pl.ds(start if isinstance(r, int) else pl.multiple_of(start, ROW_TILE), ROW_TILE)


def _expert_tile(tok_next_ref, dst_prev_ref, h2_hbm, bgu_ref, bd_ref, yg_hbm, wgu_bf, wd_bf,
                 x_cur, x_next, y_cur, y_prev, sem_in_cur, sem_in_next, sem_out):
    pltpu.make_async_copy(h2_hbm.at[pl.ds(0, EXPERT_TILE * ROW_TILE), :], x_cur, sem_in_cur).wait()
    for r in range(EXPERT_TILE):
        _row_copy_in(h2_hbm, tok_next_ref, x_next, sem_in_next, r).start()
    for r in range(EXPERT_TILE):
        _row_copy_out(y_prev, dst_prev_ref, yg_hbm, sem_out, r).start()
    xb = _load_token_major(x_cur, EXPERT_TILE).astype(BF16)
    hgu = _dot(xb, wgu_bf[...]) + bgu_ref[0]
    glu = jnp.minimum(hgu[:, :D_EXPERT], SWIGLU_LIMIT)
    lin = jnp.clip(hgu[:, D_EXPERT:], -SWIGLU_LIMIT, SWIGLU_LIMIT)
    act = glu * jax.nn.sigmoid(glu * SWIGLU_ALPHA) * (lin + 1.0)
    _store_token_major(y_cur, _dot(act.astype(BF16), wd_bf[...]) + bd_ref[0])
    pltpu.make_async_copy(y_prev, yg_hbm.at[pl.ds(0, EXPERT_TILE * ROW_TILE), :], sem_out).wait()


def _weight_copies(wgu_hbm, wd_hbm, wgu_stage, wd_stage, sem_w, expert):
    return (pltpu.make_async_copy(wgu_hbm.at[expert], wgu_stage, sem_w.at[0]),
            pltpu.make_async_copy(wd_hbm.at[expert], wd_stage, sem_w.at[1]))


def _expert_body(te_ref, nxt_ref, nused_ref, tok_first_ref, tok_next_ref, dst_prev_ref, h2_hbm, wgu_hbm, bgu_ref,
                 wd_hbm, bd_ref, yg_hbm, xbuf0, xbuf1, ybuf0, ybuf1, wgu_stage, wd_stage, wgu_bf, wd_bf,
                 sem_in, sem_out, sem_w):
    i = pl.program_id(0)
    n_used = nused_ref[0]
    xbufs, ybufs = (xbuf0, xbuf1), (ybuf0, ybuf1)

    @pl.when(i == 0)
    def _():
        ybuf1[...] = jnp.zeros_like(ybuf1)
        for cp in _weight_copies(wgu_hbm, wd_hbm, wgu_stage, wd_stage, sem_w, te_ref[0]):
            cp.start(priority=1)

        def first_rows(r, carry):
            _row_copy_in(h2_hbm, tok_first_ref, xbuf0, sem_in.at[0], r).start()
            return carry
        lax.fori_loop(0, EXPERT_TILE, first_rows, 0)

    @pl.when(jnp.logical_and(i < n_used, jnp.logical_or(i == 0, te_ref[i] != te_ref[jnp.maximum(i - 1, 0)])))
    def _():
        for cp in _weight_copies(wgu_hbm, wd_hbm, wgu_stage, wd_stage, sem_w, te_ref[i]):
            cp.wait()
        wgu_bf[...] = wgu_stage[...].astype(BF16)
        wd_bf[...] = wd_stage[...].astype(BF16)

        @pl.when(nxt_ref[i] >= 0)
        def _():
            for cp in _weight_copies(wgu_hbm, wd_hbm, wgu_stage, wd_stage, sem_w, nxt_ref[i]):
                cp.start(priority=1)

    for par in range(2):
        @pl.when(jnp.logical_and(i < n_used, i % 2 == par))
        def _(par=par):
            _expert_tile(tok_next_ref, dst_prev_ref, h2_hbm, bgu_ref, bd_ref, yg_hbm, wgu_bf, wd_bf,
                         xbufs[par], xbufs[1 - par], ybufs[par], ybufs[1 - par],
                         sem_in.at[par], sem_in.at[1 - par], sem_out)

        @pl.when(jnp.logical_and(i == n_used, i % 2 == par))
        def _(par=par):
            pltpu.make_async_copy(h2_hbm.at[pl.ds(0, EXPERT_TILE * ROW_TILE), :], xbufs[par], sem_in.at[par]).wait()

            def last_rows(r, carry):
                _row_copy_out(ybufs[1 - par], dst_prev_ref, yg_hbm, sem_out, r).start()
                return carry
            lax.fori_loop(0, EXPERT_TILE, last_rows, 0)
            pltpu.make_async_copy(ybufs[1 - par], yg_hbm.at[pl.ds(0, EXPERT_TILE * ROW_TILE), :], sem_out).wait()


def _combine_body(x1_ref, gw_ref, y0_ref, y1_ref, y2_ref, y3_ref, out_ref):
    gw = gw_ref[...]
    rows = x1_ref.shape[0]
    for s in range(ROW_TILE):
        cols = slice(s * LANES, (s + 1) * LANES)
        acc = x1_ref[:, cols]
        for k, y_ref in enumerate((y0_ref, y1_ref, y2_ref, y3_ref)):
            acc = acc + gw[:, k:k + 1] * y_ref[pl.ds(s, rows, stride=ROW_TILE), :]
        out_ref[:, cols] = acc


def _const_spec(shape):
    nd = len(shape)
    return pl.BlockSpec(shape, lambda *_: (0,) * nd, pipeline_mode=pl.Buffered(1))


def _q_perm():
    cols = np.arange(ATTN_WIDTH)
    j, half, d = cols // LANES, (cols % LANES) // HEAD_DIM, cols % HEAD_DIM
    return (j + Q_PER_KV * half) * HEAD_DIM + d


def _alibi_slopes():
    return 2.0 ** (-8.0 * np.arange(1, N_HEADS + 1) / N_HEADS)


def _prompt_bias():
    qi = np.arange(PAIR)[:, None]
    kj = np.arange(BAND)[None, :]
    dist = np.abs(qi + WINDOW - kj).astype(np.float64)
    cq, ck = qi // CHUNK, kj // CHUNK
    in_band = (ck >= cq) & (ck <= cq + WINDOW // CHUNK)
    base = -_alibi_slopes()[:, None, None] * dist[None]
    later = np.where(in_band[None], base, NEG_INF)
    first = np.where((kj >= WINDOW)[None], later, NEG_INF)
    return np.stack([first, later]).astype(np.float32)


def _sample_bias(n_new, n_cached):
    qi = np.arange(n_new)[:, None]
    dc = np.abs(qi + n_cached - np.arange(n_cached)[None, :]).astype(np.float64)
    dn = np.abs(qi - np.arange(n_new)[None, :]).astype(np.float64)
    sl = _alibi_slopes()[:, None, None]
    bc = (-sl * dc[None]).reshape(N_HEADS * n_new, n_cached)
    bn = (-sl * dn[None]).reshape(N_HEADS * n_new, n_new)
    return bc.astype(np.float32), bn.astype(np.float32)


def kernel(x_prompt, x_sample, cache_k_win, cache_v_win, g_mix, w_in, q_norm_g, k_norm_g, attn_sinks, sgu_norm_g,
           w_spatial, b_spatial, w_branch_attn, w_branch_sgu, w_out, g_ffn, w_router, b_router, w_gate_up,
           b_gate_up, w_down, b_down):
    n_b, seq, _ = x_prompt.shape
    n_streams, n_new, _ = x_sample.shape
    n_cached = cache_k_win.shape[2]
    n_prompt = n_b * seq
    n_sample = n_streams * n_new
    n_tok = n_prompt + n_sample
    assert seq % TOKEN_BLOCK == 0 and n_sample == TOKEN_BLOCK and n_cached == WINDOW
    blocks_per_seq = seq // TOKEN_BLOCK
    n_prompt_blocks = n_prompt // TOKEN_BLOCK

    perm = _q_perm()
    w_in_l = w_in[0]
    w_in_b = jnp.concatenate([w_in_l[:, perm], w_in_l[:, Q_END:]], axis=1).astype(BF16)
    wba_b = w_branch_attn[0][perm, :].astype(BF16)
    wbs_b = w_branch_sgu[0].astype(BF16)
    wout_b = w_out[0].astype(BF16)
    wr_b = w_router[0].astype(BF16)
    br = b_router[0].reshape(1, N_EXPERTS).astype(F32)
    gmix = g_mix[0].reshape(1, D_MODEL)
    gffn = g_ffn[0].reshape(1, D_MODEL)
    qg = jnp.tile(q_norm_g[0], LANES // HEAD_DIM).reshape(1, LANES)
    kg = jnp.tile(k_norm_g[0], LANES // HEAD_DIM).reshape(1, LANES)
    sgug = sgu_norm_g[0].reshape(1, MLP_WIDTH)
    sinks = attn_sinks[0].astype(F32)
    lane_seg = np.arange(LANES) // HEAD_DIM
    seg64 = jnp.asarray((lane_seg[:, None] == lane_seg[None, :]) / HEAD_DIM, BF16)
    seg128 = jnp.full((LANES, LANES), 1.0 / MLP_GROUP_DIM, BF16)
    tri = jnp.asarray(np.tril(np.ones((TOKEN_BLOCK, TOKEN_BLOCK)), -1), BF16)
    wsp = w_spatial[0]
    bsp_p = jnp.repeat(b_spatial[0].T, MLP_GROUP_DIM, axis=1)
    wsp_s = jnp.tile(wsp[:, :n_new, :n_new], (1, n_streams, n_streams))
    bsp_s = jnp.tile(jnp.repeat(b_spatial[0][:, :n_new].T, MLP_GROUP_DIM, axis=1), (n_streams, 1))
    bias_p = jnp.asarray(_prompt_bias())
    bias_c, bias_n = (jnp.asarray(a) for a in _sample_bias(n_new, n_cached))

    smem = pl.BlockSpec(memory_space=pltpu.SMEM)
    tb = TOKEN_BLOCK
    last_blk = n_prompt_blocks - 1
    seq_of = lambda i: jnp.minimum(i, last_blk) // blocks_per_seq
    x_spec = pl.BlockSpec((1, tb, D_MODEL), lambda i: (seq_of(i), jnp.minimum(i, last_blk) % blocks_per_seq, 0))
    tok_spec = lambda width: pl.BlockSpec((tb, width), lambda i: (i, 0))
    win_spec = lambda width: pl.BlockSpec((1, WINDOW, width), lambda i: (seq_of(i), 0, 0))
    route_shapes = [jax.ShapeDtypeStruct((n_tok, TOP_K), jnp.int32), jax.ShapeDtypeStruct((n_tok, TOP_K), F32),
                    jax.ShapeDtypeStruct((n_tok, TOP_K), jnp.int32)]

    x1p, h2, idx_p, gw_p, rank_p, cnt_p, kwin_p, vwin_p, sguv_p = pl.pallas_call(
        functools.partial(_mixer_prompt_body, blocks_per_seq=blocks_per_seq, n_blocks=n_prompt_blocks),
        grid=(n_prompt_blocks + 1,),
        in_specs=[smem, x_spec,
                  _const_spec((D_MODEL, IN_COLS)), _const_spec((1, D_MODEL)), _const_spec((1, LANES)),
                  _const_spec((1, LANES)), _const_spec((1, MLP_WIDTH)), _const_spec((LANES, LANES)),
                  _const_spec((LANES, LANES)), _const_spec((2, N_HEADS, PAIR, BAND)),
                  _const_spec((MLP_GROUPS, MLP_CHUNK, MLP_CHUNK)), _const_spec((MLP_CHUNK, MLP_WIDTH)),
                  _const_spec((ATTN_WIDTH, D_MODEL)), _const_spec((MLP_WIDTH, D_MODEL)),
                  _const_spec((D_MODEL, D_MODEL)), _const_spec((1, D_MODEL)), _const_spec((D_MODEL, N_EXPERTS)),
                  _const_spec((1, N_EXPERTS)), _const_spec((tb, tb))],
        out_specs=[x_spec,
                   pl.BlockSpec((tb * ROW_TILE, LANES), lambda i: (i, 0)), tok_spec(TOP_K), tok_spec(TOP_K), tok_spec(TOP_K),
                   pl.BlockSpec((1, N_EXPERTS), lambda i: (0, 0)),
                   win_spec(KV_WIDTH), win_spec(KV_WIDTH), win_spec(MLP_WIDTH)],
        out_shape=[jax.ShapeDtypeStruct((n_b, seq, D_MODEL), F32), jax.ShapeDtypeStruct((n_tok * ROW_TILE, LANES), F32),
                   *route_shapes, jax.ShapeDtypeStruct((1, N_EXPERTS), F32),
                   jax.ShapeDtypeStruct((n_b, WINDOW, KV_WIDTH), F32),
                   jax.ShapeDtypeStruct((n_b, WINDOW, KV_WIDTH), F32),
                   jax.ShapeDtypeStruct((n_b, MLP_CHUNK, MLP_WIDTH), F32)],
        scratch_shapes=[pltpu.VMEM((WINDOW, KV_WIDTH), BF16), pltpu.VMEM((WINDOW, KV_WIDTH), BF16),
                        pltpu.VMEM((1, N_EXPERTS), F32)],
        compiler_params=pltpu.CompilerParams(dimension_semantics=("arbitrary",), vmem_limit_bytes=VMEM_LIMIT),
        name="mixer_prompt",
    )(sinks, x_prompt, w_in_b, gmix, qg, kg, sgug, seg64, seg128, bias_p, wsp, bsp_p, wba_b, wbs_b, wout_b,
      gffn, wr_b, br, tri)

    full = lambda shape: pl.BlockSpec(shape, lambda i: (0,) * len(shape))
    last_tok = lambda width: pl.BlockSpec((tb, width), lambda i: (n_prompt_blocks, 0))
    any_spec = pl.BlockSpec(memory_space=pl.ANY)
    ck = cache_k_win[0].reshape(n_streams, n_cached, KV_WIDTH)
    cv = cache_v_win[0].reshape(n_streams, n_cached, KV_WIDTH)
    x1s, h2, idx_a, gw_a, rank_a, cnt_s, kwin_s, vwin_s, sguv_s = pl.pallas_call(
        functools.partial(_mixer_sample_body, n_streams=n_streams, n_new=n_new),
        grid=(1,),
        in_specs=[smem, full((n_sample, D_MODEL)), full((n_streams, n_cached, KV_WIDTH)),
                  full((n_streams, n_cached, KV_WIDTH)),
                  full((D_MODEL, IN_COLS)), full((1, D_MODEL)), full((1, LANES)), full((1, LANES)),
                  full((1, MLP_WIDTH)), full((LANES, LANES)), full((LANES, LANES)),
                  full((N_HEADS * n_new, n_cached)), full((N_HEADS * n_new, n_new)),
                  full((MLP_GROUPS, n_sample, n_sample)), full((n_sample, MLP_WIDTH)),
                  full((ATTN_WIDTH, D_MODEL)), full((MLP_WIDTH, D_MODEL)), full((D_MODEL, D_MODEL)),
                  full((1, D_MODEL)), full((D_MODEL, N_EXPERTS)), full((1, N_EXPERTS)), full((tb, tb)),
                  any_spec, any_spec, any_spec, any_spec],
        out_specs=[full((n_sample, D_MODEL)), pl.BlockSpec((tb * ROW_TILE, LANES), lambda i: (n_prompt_blocks, 0)), last_tok(TOP_K), last_tok(TOP_K), last_tok(TOP_K),
                   full((1, N_EXPERTS)), full((n_streams, n_cached, KV_WIDTH)),
                   full((n_streams, n_cached, KV_WIDTH)), full((n_sample, MLP_WIDTH))],
        out_shape=[jax.ShapeDtypeStruct((n_sample, D_MODEL), F32), jax.ShapeDtypeStruct((n_tok * ROW_TILE, LANES), F32),
                   *route_shapes, jax.ShapeDtypeStruct((1, N_EXPERTS), F32),
                   jax.ShapeDtypeStruct((n_streams, n_cached, KV_WIDTH), F32),
                   jax.ShapeDtypeStruct((n_streams, n_cached, KV_WIDTH), F32),
                   jax.ShapeDtypeStruct((n_sample, MLP_WIDTH), F32)],
        input_output_aliases={22: 1, 23: 2, 24: 3, 25: 4},
        scratch_shapes=[pltpu.VMEM((1, N_EXPERTS), F32)],
        compiler_params=pltpu.CompilerParams(dimension_semantics=("arbitrary",), vmem_limit_bytes=VMEM_LIMIT),
        name="mixer_sample",
    )(sinks, x_sample.reshape(n_sample, D_MODEL), ck, cv, w_in_b, gmix, qg, kg, sgug, seg64, seg128, bias_c,
      bias_n, wsp_s, bsp_s, wba_b, wbs_b, wout_b, gffn, wr_b, br, tri, h2, idx_p, gw_p, rank_p)

    tm = EXPERT_TILE
    cnt_prompt = cnt_p[0].astype(jnp.int32)
    cnt_all = cnt_prompt + cnt_s[0].astype(jnp.int32)
    padded = ((cnt_all + tm - 1) // tm) * tm
    pad_end = jnp.cumsum(padded)
    n_pairs = TOP_K * n_tok
    n_tiles = n_pairs // tm + N_EXPERTS
    key_span = 1 << 17
    assert key_span >= n_pairs and 2 * N_EXPERTS * key_span < 2 ** 31
    pair_code = (jnp.arange(n_tok, dtype=jnp.int32)[:, None] * TOP_K + jnp.arange(TOP_K, dtype=jnp.int32)[None, :])
    real_keys = (idx_a * (2 * key_span) + pair_code).reshape(-1)
    pad_j = jnp.arange(tm, dtype=jnp.int32)[None, :]
    pad_keys = jnp.where(pad_j < (padded - cnt_all)[:, None],
                         (2 * jnp.arange(N_EXPERTS, dtype=jnp.int32)[:, None] + 1) * key_span + pad_j,
                         jnp.iinfo(jnp.int32).max).reshape(-1)
    keys = jnp.sort(jnp.concatenate([real_keys, pad_keys]))
    is_real = (keys // key_span) % 2 == 0
    code = keys % key_span
    tok_sorted = jnp.where(is_real, code // TOP_K, 0)
    spare = n_pairs + jnp.arange(tm, dtype=jnp.int32)
    dst_sorted = jnp.where(is_real, (code % TOP_K) * n_tok + code // TOP_K, jnp.tile(spare, n_tiles))
    tok_tiles = tok_sorted.reshape(n_tiles, 1, tm)
    n_steps = n_tiles + 1
    tok_next = jnp.concatenate([tok_tiles[1:], jnp.zeros((2, 1, tm), jnp.int32)], axis=0)
    dst_prev = jnp.concatenate([spare.reshape(1, 1, tm), dst_sorted.reshape(n_tiles, 1, tm)], axis=0)
    n_used = (pad_end[-1] // tm).astype(jnp.int32)
    tile_start = jnp.minimum(jnp.arange(n_steps, dtype=jnp.int32), n_used - 1) * tm
    tile_expert = jnp.sum((pad_end[None, :] <= tile_start[:, None]).astype(jnp.int32), axis=1)
    tile_expert = jnp.minimum(tile_expert, N_EXPERTS - 1)
    next_run_tile = pad_end[tile_expert] // tm
    next_expert = jnp.where(next_run_tile < n_used, tile_expert[jnp.minimum(next_run_tile, n_steps - 1)], -1)

    any_spec = pl.BlockSpec(memory_space=pl.ANY)
    yg = pl.pallas_call(
        _expert_body,
        grid_spec=pltpu.PrefetchScalarGridSpec(
            num_scalar_prefetch=3,
            grid=(n_steps,),
            in_specs=[pl.BlockSpec((1, 1, tm), lambda i, te, nx, nu: (0, 0, 0), memory_space=pltpu.SMEM),
                      pl.BlockSpec((1, 1, tm), lambda i, te, nx, nu: (i, 0, 0), memory_space=pltpu.SMEM),
                      pl.BlockSpec((1, 1, tm), lambda i, te, nx, nu: (i, 0, 0), memory_space=pltpu.SMEM),
                      any_spec, any_spec,
                      pl.BlockSpec((1, 1, 2 * D_EXPERT), lambda i, te, nx, nu: (te[i], 0, 0)),
                      any_spec,
                      pl.BlockSpec((1, 1, D_MODEL), lambda i, te, nx, nu: (te[i], 0, 0))],
            out_specs=any_spec,
            scratch_shapes=[pltpu.VMEM((tm * ROW_TILE, LANES), F32), pltpu.VMEM((tm * ROW_TILE, LANES), F32),
                            pltpu.VMEM((tm * ROW_TILE, LANES), F32), pltpu.VMEM((tm * ROW_TILE, LANES), F32),
                            pltpu.VMEM((D_MODEL, 2 * D_EXPERT), F32), pltpu.VMEM((D_EXPERT, D_MODEL), F32),
                            pltpu.VMEM((D_MODEL, 2 * D_EXPERT), BF16), pltpu.VMEM((D_EXPERT, D_MODEL), BF16),
                            pltpu.SemaphoreType.DMA((2,)), pltpu.SemaphoreType.DMA(()),
                            pltpu.SemaphoreType.DMA((2,))]),
        out_shape=jax.ShapeDtypeStruct(((n_pairs + tm) * ROW_TILE, LANES), F32),
        compiler_params=pltpu.CompilerParams(dimension_semantics=("arbitrary",), vmem_limit_bytes=VMEM_LIMIT),
        name="moe_experts",
    )(tile_expert, next_expert.astype(jnp.int32), n_used.reshape(1), tok_tiles[0:1] * ROW_TILE, tok_next * ROW_TILE,
      dst_prev * ROW_TILE, h2, w_gate_up[0], b_gate_up[0].reshape(N_EXPERTS, 1, 2 * D_EXPERT), w_down[0],
      b_down[0].reshape(N_EXPERTS, 1, D_MODEL))

    n_tok_blocks = n_tok // tb

    def combine(x1, first_block, n_blocks):
        y_spec = lambda k: pl.BlockSpec((tb * ROW_TILE, LANES), lambda i: (k * n_tok_blocks + first_block + i, 0))
        return pl.pallas_call(
            _combine_body,
            grid=(n_blocks,),
            in_specs=[pl.BlockSpec((tb, D_MODEL), lambda i: (i, 0)),
                      pl.BlockSpec((tb, TOP_K), lambda i: (first_block + i, 0)),
                      y_spec(0), y_spec(1), y_spec(2), y_spec(3)],
            out_specs=pl.BlockSpec((tb, D_MODEL), lambda i: (i, 0)),
            out_shape=jax.ShapeDtypeStruct(x1.shape, F32),
            compiler_params=pltpu.CompilerParams(dimension_semantics=("arbitrary",)),
            name="moe_combine",
        )(x1, gw_a, yg, yg, yg, yg)

    y_prompt = combine(x1p.reshape(n_prompt, D_MODEL), 0, n_prompt_blocks).reshape(n_b, seq, D_MODEL)
    y_sample = combine(x1s, n_prompt_blocks, 1).reshape(n_streams, n_new, D_MODEL)

    kv_shape = (N_KV_HEADS, HEAD_DIM)
    sg_shape = (MLP_GROUPS, MLP_GROUP_DIM)
    return (y_prompt, y_sample,
            kwin_p.reshape(1, n_b, WINDOW, *kv_shape), vwin_p.reshape(1, n_b, WINDOW, *kv_shape),
            kwin_s.reshape(1, n_streams, n_cached, *kv_shape), vwin_s.reshape(1, n_streams, n_cached, *kv_shape),
            sguv_p.reshape(1, n_b, MLP_CHUNK, *sg_shape), sguv_s.reshape(1, n_streams, n_new, *sg_shape))
```

```python
import functools

import jax
import jax.numpy as jnp
import numpy as np
from jax import lax
from jax.experimental import pallas as pl
from jax.experimental.pallas import tpu as pltpu

D_MODEL = 1024
CHUNK = 64
WINDOW = 128
HEAD_DIM = 64
N_HEADS = 8
N_KV_HEADS = 2
Q_PER_KV = N_HEADS // N_KV_HEADS
ATTN_WIDTH = N_HEADS * HEAD_DIM
KV_WIDTH = N_KV_HEADS * HEAD_DIM
ATTN_SCALE = HEAD_DIM ** -0.5
MLP_CHUNK = 128
MLP_GROUPS = 8
MLP_WIDTH = D_MODEL
MLP_GROUP_DIM = MLP_WIDTH // MLP_GROUPS
N_EXPERTS = 32
TOP_K = 4
D_EXPERT = D_MODEL
SWIGLU_ALPHA = 1.702
SWIGLU_LIMIT = 7.0
EPS = 1e-6
NEG_INF = -1e30
Q_END = ATTN_WIDTH
K_END = Q_END + KV_WIDTH
V_END = K_END + KV_WIDTH
U_END = V_END + MLP_WIDTH
VM_END = U_END + MLP_WIDTH
GA_END = VM_END + D_MODEL
IN_COLS = GA_END + D_MODEL

LANES = 128
ROW_TILE = D_MODEL // LANES
TOKEN_BLOCK = 256
PAIR = 2 * CHUNK
BAND = PAIR + WINDOW
EXPERT_TILE = 256
VMEM_LIMIT = 56 * 1024 * 1024

F32 = jnp.float32
BF16 = jnp.bfloat16


def _dot(a, b):
    return jnp.dot(a, b, preferred_element_type=F32)


def _dot_nt(a, b):
    return lax.dot_general(a, b, (((1,), (1,)), ((), ())), preferred_element_type=F32)


def _store_token_major(ref, val):
    rows = val.shape[0]
    for s in range(ROW_TILE):
        ref[pl.ds(s, rows, stride=ROW_TILE), :] = val[:, s * LANES:(s + 1) * LANES]


def _load_token_major(ref, rows):
    return jnp.concatenate([ref[pl.ds(s, rows, stride=ROW_TILE), :] for s in range(ROW_TILE)], axis=-1)


def _segment_mean(sq, seg):
    hi = sq.astype(BF16)
    lo = (sq - hi.astype(F32)).astype(BF16)
    return _dot(jnp.concatenate([hi, lo], axis=-1), seg)


def _rms_rows(x, gain):
    ms = jnp.mean(x * x, axis=-1, keepdims=True)
    return (x * lax.rsqrt(ms + EPS)) * gain


def _project(h, w_in_ref, qg, kg, sgug, seg64, seg128, between=None):
    qkv = _dot(h, w_in_ref[:, 0:V_END])
    qk_cols = []
    for c in range(K_END // LANES):
        blk = qkv[:, c * LANES:(c + 1) * LANES]
        ms = _segment_mean(blk * blk, seg64)
        g = qg if c < Q_END // LANES else kg
        qk_cols.append((blk * lax.rsqrt(ms + EPS)) * g)
    qn = jnp.concatenate(qk_cols[:Q_END // LANES], axis=-1)
    kn = qk_cols[Q_END // LANES]
    v = qkv[:, K_END:V_END]
    u = jax.nn.gelu(_dot(h, w_in_ref[:, V_END:U_END]))
    vg = jax.nn.gelu(_dot(h, w_in_ref[:, U_END:VM_END]))
    vm_cols = []
    for g in range(MLP_GROUPS):
        blk = vg[:, g * LANES:(g + 1) * LANES]
        ms = _segment_mean(blk * blk, seg128)
        vm_cols.append((blk * lax.rsqrt(ms + EPS)) * sgug[:, g * LANES:(g + 1) * LANES])
    vm = jnp.concatenate(vm_cols, axis=-1)
    if between is not None:
        between()
    ga = jax.nn.sigmoid(_dot(h, w_in_ref[:, VM_END:GA_END]))
    gb = jax.nn.sigmoid(_dot(h, w_in_ref[:, GA_END:IN_COLS]))
    return qn, kn, v, u, vm, ga, gb


def _stack_heads(q_rows):
    lane = lax.broadcasted_iota(jnp.int32, (q_rows.shape[0], LANES), 1)
    blocks = []
    for head in range(N_HEADS):
        j, half = head % Q_PER_KV, head // Q_PER_KV
        col = q_rows[:, j * LANES:(j + 1) * LANES]
        keep = (lane < HEAD_DIM) if half == 0 else (lane >= HEAD_DIM)
        blocks.append(jnp.where(keep, col, 0.0))
    return jnp.concatenate(blocks, axis=0).astype(BF16)


def _unstack_heads(o, rows):
    lane = lax.broadcasted_iota(jnp.int32, (rows, LANES), 1)
    cols = []
    for j in range(Q_PER_KV):
        lo = o[j * rows:(j + 1) * rows]
        hi = o[(j + Q_PER_KV) * rows:(j + Q_PER_KV + 1) * rows]
        cols.append(jnp.where(lane < HEAD_DIM, lo, hi))
    return jnp.concatenate(cols, axis=-1)


def _merge(att, sgu, ga, gb, wba_ref, wbs_ref, wout_ref):
    m = ga * _dot(att.astype(BF16), wba_ref[...]) + gb * _dot(sgu.astype(BF16), wbs_ref[...])
    return _dot(m.astype(BF16), wout_ref[...])


def _route(h2, wr_ref, br_ref, tri_ref, run_ref, live=None):
    rows = h2.shape[0]
    logits = _dot(h2, wr_ref[...]) + br_ref[...]
    eidx = lax.broadcasted_iota(jnp.int32, (rows, N_EXPERTS), 1).astype(F32)
    work = logits
    vals, picks, onehots = [], [], []
    for _ in range(TOP_K):
        m = jnp.max(work, axis=-1, keepdims=True)
        sel = jnp.min(jnp.where(work == m, eidx, float(N_EXPERTS)), axis=-1, keepdims=True)
        oh = eidx == sel
        vals.append(m)
        picks.append(sel)
        onehots.append(oh)
        work = jnp.where(oh, -jnp.inf, work)
    exps = [jnp.exp(v - vals[0]) for v in vals]
    den = exps[0] + exps[1] + exps[2] + exps[3]
    mask = jnp.zeros((rows, N_EXPERTS), F32)
    for oh in onehots:
        mask = mask + jnp.where(oh, 1.0, 0.0)
    before = _dot(tri_ref[...], mask.astype(BF16)) + run_ref[...]
    ranks = [jnp.sum(jnp.where(oh, before, 0.0), axis=-1, keepdims=True) for oh in onehots]
    added = jnp.sum(mask, axis=0, keepdims=True)
    run_ref[...] = run_ref[...] + (added if live is None else added * live)
    k4 = lax.broadcasted_iota(jnp.int32, (rows, TOP_K), 1)
    idx4 = jnp.zeros((rows, TOP_K), jnp.int32)
    w4 = jnp.zeros((rows, TOP_K), F32)
    r4 = jnp.zeros((rows, TOP_K), jnp.int32)
    for k in range(TOP_K):
        idx4 = jnp.where(k4 == k, picks[k].astype(jnp.int32), idx4)
        w4 = jnp.where(k4 == k, exps[k] / den, w4)
        r4 = jnp.where(k4 == k, ranks[k].astype(jnp.int32), r4)
    return idx4, w4, r4


def _softmax_rows(parts, sink):
    m = sink
    for l in parts:
        m = jnp.maximum(m, jnp.max(l, axis=-1, keepdims=True))
    es = [jnp.exp(l - m) for l in parts]
    den = jnp.exp(sink - m)
    for e in es:
        den = den + jnp.sum(e, axis=-1, keepdims=True)
    return [e / den for e in es]


def _mixer_prompt_body(sinks_ref, x_ref, w_in_ref, gmix_ref, qg_ref, kg_ref, sgug_ref, seg64_ref, seg128_ref,
                       bias_ref, wsp_ref, bsp_ref, wba_ref, wbs_ref, wout_ref, gffn_ref, wr_ref, br_ref, tri_ref,
                       x1_ref, h2_ref, idx_ref, gw_ref, rank_ref, cnt_ref, kwin_ref, vwin_ref, sguv_ref,
                       kcarry, vcarry, run_ref, h2_prev_ref, *, blocks_per_seq, n_blocks):
    i = pl.program_id(0)

    @pl.when(i == 0)
    def _():
        run_ref[...] = jnp.zeros_like(run_ref)
        h2_prev_ref[...] = jnp.zeros_like(h2_prev_ref)

    def route_previous(live):
        idx4, w4, r4 = _route(h2_prev_ref[...], wr_ref, br_ref, tri_ref, run_ref, live)
        idx_ref[...] = idx4
        gw_ref[...] = w4
        rank_ref[...] = r4
        cnt_ref[...] = run_ref[...]

    @pl.when(i == n_blocks)
    def _():
        route_previous(None)
        h2_ref[...] = jnp.zeros_like(h2_ref)

    @pl.when(i == n_blocks + 1)
    def _():
        idx_ref[...] = jnp.zeros_like(idx_ref)
        gw_ref[...] = jnp.zeros_like(gw_ref)
        rank_ref[...] = jnp.zeros_like(rank_ref)

    @pl.when(jnp.logical_and(i < n_blocks, i % blocks_per_seq == 0))
    def _():
        kcarry[...] = jnp.zeros_like(kcarry)
        vcarry[...] = jnp.zeros_like(vcarry)

    @pl.when(i < n_blocks)
    def _():
        _mixer_prompt_block(i % blocks_per_seq, sinks_ref, x_ref, w_in_ref, gmix_ref, qg_ref, kg_ref, sgug_ref,
                            seg64_ref, seg128_ref, bias_ref, wsp_ref, bsp_ref, wba_ref, wbs_ref, wout_ref, gffn_ref,
                            x1_ref, h2_ref, kwin_ref, vwin_ref, sguv_ref, kcarry, vcarry, h2_prev_ref,
                            functools.partial(route_previous, jnp.where(i > 0, 1.0, 0.0)))


def _mixer_prompt_block(j, sinks_ref, x_ref, w_in_ref, gmix_ref, qg_ref, kg_ref, sgug_ref, seg64_ref, seg128_ref,
                        bias_ref, wsp_ref, bsp_ref, wba_ref, wbs_ref, wout_ref, gffn_ref,
                        x1_ref, h2_ref, kwin_ref, vwin_ref, sguv_ref, kcarry, vcarry, h2_prev_ref, route_previous):

    x = x_ref[0]
    h = _rms_rows(x, gmix_ref[...]).astype(BF16)
    qn, kn, v, u, vm, ga, gb = _project(h, w_in_ref, qg_ref[...], kg_ref[...], sgug_ref[...],
                                        seg64_ref[...], seg128_ref[...], between=route_previous)
    k_ext = jnp.concatenate([kcarry[...], kn.astype(BF16)], axis=0)
    v_ext = jnp.concatenate([vcarry[...], v.astype(BF16)], axis=0)
    kcarry[...] = k_ext[TOKEN_BLOCK:]
    vcarry[...] = v_ext[TOKEN_BLOCK:]

    tri_mask = (lax.broadcasted_iota(jnp.int32, (MLP_CHUNK, MLP_CHUNK), 0)
                >= lax.broadcasted_iota(jnp.int32, (MLP_CHUNK, MLP_CHUNK), 1))
    att_rows, sgu_rows = [], []
    for pm in range(TOKEN_BLOCK // PAIR):
        r0 = pm * PAIR
        q_stack = _stack_heads(qn[r0:r0 + PAIR])
        k_band = k_ext[r0:r0 + BAND]
        v_band = v_ext[r0:r0 + BAND]
        s = _dot_nt(q_stack, k_band)
        first = jnp.where(j == 0, 0, 1) if pm == 0 else 1
        probs = []
        for head in range(N_HEADS):
            logit = s[head * PAIR:(head + 1) * PAIR] * ATTN_SCALE + bias_ref[first, head]
            probs.append(_softmax_rows([logit], sinks_ref[head])[0].astype(BF16))
        o = _dot(jnp.concatenate(probs, axis=0), v_band)
        att_rows.append(_unstack_heads(o, PAIR))
        cols = []
        for g in range(MLP_GROUPS):
            wm = jnp.where(tri_mask, wsp_ref[g], 0.0).astype(BF16)
            cols.append(_dot(wm, vm[r0:r0 + PAIR, g * LANES:(g + 1) * LANES].astype(BF16)))
        mixed = jnp.concatenate(cols, axis=-1) + bsp_ref[...]
        sgu_rows.append(u[r0:r0 + PAIR] * mixed)
    att = jnp.concatenate(att_rows, axis=0)
    sgu = jnp.concatenate(sgu_rows, axis=0)

    x1 = x + _merge(att, sgu, ga, gb, wba_ref, wbs_ref, wout_ref)
    h2 = _rms_rows(x1, gffn_ref[...])
    h2_prev_ref[...] = h2.astype(BF16)

    x1_ref[0] = x1
    _store_token_major(h2_ref, h2)
    kwin_ref[0] = kn[TOKEN_BLOCK - WINDOW:]
    vwin_ref[0] = v[TOKEN_BLOCK - WINDOW:]
    sguv_ref[0] = vm[TOKEN_BLOCK - MLP_CHUNK:]


def _mixer_sample_body(sinks_ref, x_ref, ck_ref, cv_ref, w_in_ref, gmix_ref, qg_ref, kg_ref, sgug_ref, seg64_ref,
                       seg128_ref, biasc_ref, biasn_ref, wsp_ref, bsp_ref, wba_ref, wbs_ref, wout_ref, gffn_ref,
                       wr_ref, br_ref, tri_ref, h2_in_ref, idx_in_ref, gw_in_ref, rank_in_ref,
                       x1_ref, h2_ref, idx_ref, gw_ref, rank_ref, cnt_ref, kwin_ref, vwin_ref, sguv_ref,
                       run_ref, *, n_streams, n_new):
    del h2_in_ref, idx_in_ref, gw_in_ref, rank_in_ref
    run_ref[...] = jnp.zeros_like(run_ref)
    x = x_ref[...]
    h = _rms_rows(x, gmix_ref[...]).astype(BF16)
    qn, kn, v, u, vm, ga, gb = _project(h, w_in_ref, qg_ref[...], kg_ref[...], sgug_ref[...],
                                        seg64_ref[...], seg128_ref[...])
    n_cached = ck_ref.shape[1]
    att_rows = []
    for s_i in range(n_streams):
        r0 = s_i * n_new
        q_stack = _stack_heads(qn[r0:r0 + n_new])
        k_new = kn[r0:r0 + n_new]
        v_new = v[r0:r0 + n_new]
        s_c = _dot_nt(q_stack, ck_ref[s_i].astype(BF16))
        s_n = _dot_nt(q_stack, k_new.astype(BF16))
        pc, pn = [], []
        for head in range(N_HEADS):
            rows = slice(head * n_new, (head + 1) * n_new)
            lc = s_c[rows] * ATTN_SCALE + biasc_ref[rows]
            ln = s_n[rows] * ATTN_SCALE + biasn_ref[rows]
            p_c, p_n = _softmax_rows([lc, ln], sinks_ref[head])
            pc.append(p_c.astype(BF16))
            pn.append(p_n.astype(BF16))
        o = (_dot(jnp.concatenate(pc, axis=0), cv_ref[s_i].astype(BF16))
             + _dot(jnp.concatenate(pn, axis=0), v_new.astype(BF16)))
        att_rows.append(_unstack_heads(o, n_new))
        kwin_ref[s_i, 0:n_cached - n_new] = ck_ref[s_i, n_new:n_cached]
        kwin_ref[s_i, n_cached - n_new:n_cached] = k_new
        vwin_ref[s_i, 0:n_cached - n_new] = cv_ref[s_i, n_new:n_cached]
        vwin_ref[s_i, n_cached - n_new:n_cached] = v_new
    att = jnp.concatenate(att_rows, axis=0)

    rows = n_streams * n_new
    ri = lax.broadcasted_iota(jnp.int32, (rows, rows), 0)
    ci = lax.broadcasted_iota(jnp.int32, (rows, rows), 1)
    keep = jnp.logical_and(ri // n_new == ci // n_new, ri % n_new >= ci % n_new)
    cols = []
    for g in range(MLP_GROUPS):
        wm = jnp.where(keep, wsp_ref[g], 0.0).astype(BF16)
        cols.append(_dot(wm, vm[:, g * LANES:(g + 1) * LANES].astype(BF16)))
    sgu = u * (jnp.concatenate(cols, axis=-1) + bsp_ref[...])

    x1 = x + _merge(att, sgu, ga, gb, wba_ref, wbs_ref, wout_ref)
    h2 = _rms_rows(x1, gffn_ref[...])
    idx4, w4, r4 = _route(h2.astype(BF16), wr_ref, br_ref, tri_ref, run_ref)
    x1_ref[...] = x1
    _store_token_major(h2_ref, h2)
    idx_ref[...] = idx4
    gw_ref[...] = w4
    rank_ref[...] = r4
    cnt_ref[...] = run_ref[...]
    sguv_ref[...] = vm


def _row_copy_in(h2_hbm, tok_ref, xbuf, sem, r):
    src = pl.multiple_of(tok_ref[0, 0, r], ROW_TILE)
    return pltpu.make_async_copy(h2_hbm.at[pl.ds(src, ROW_TILE), :], xbuf.at[_tile_rows(r), :], sem)


def _row_copy_out(ybuf, dst_ref, yg_hbm, sem, r):
    dst = pl.multiple_of(dst_ref[0, 0, r], ROW_TILE)
    return pltpu.make_async_copy(ybuf.at[_tile_rows(r), :], yg_hbm.at[pl.ds(dst, ROW_TILE), :], sem)


def _tile_rows(r):
    start = r * ROW_TILE
    return pl.ds(start if isinstance(r, int) else pl.multiple_of(start, ROW_TILE), ROW_TILE)


def _expert_tile(tok_next_ref, dst_prev_ref, h2_hbm, bgu_ref, bd_ref, yg_hbm, wgu_bf, wd_bf,
                 x_cur, x_next, y_cur, y_prev, sem_in_cur, sem_in_next, sem_out):
    pltpu.make_async_copy(h2_hbm.at[pl.ds(0, EXPERT_TILE * ROW_TILE), :], x_cur, sem_in_cur).wait()
    for r in range(EXPERT_TILE):
        _row_copy_in(h2_hbm, tok_next_ref, x_next, sem_in_next, r).start()
    for r in range(EXPERT_TILE):
        _row_copy_out(y_prev, dst_prev_ref, yg_hbm, sem_out, r).start()
    xb = _load_token_major(x_cur, EXPERT_TILE).astype(BF16)
    hgu = _dot(xb, wgu_bf[...]) + bgu_ref[0]
    glu = jnp.minimum(hgu[:, :D_EXPERT], SWIGLU_LIMIT)
    lin = jnp.clip(hgu[:, D_EXPERT:], -SWIGLU_LIMIT, SWIGLU_LIMIT)
    act = glu * jax.nn.sigmoid(glu * SWIGLU_ALPHA) * (lin + 1.0)
    _store_token_major(y_cur, _dot(act.astype(BF16), wd_bf[...]) + bd_ref[0])
    pltpu.make_async_copy(y_prev, yg_hbm.at[pl.ds(0, EXPERT_TILE * ROW_TILE), :], sem_out).wait()


def _weight_copies(wgu_hbm, wd_hbm, wgu_stage, wd_stage, sem_w, expert):
    return (pltpu.make_async_copy(wgu_hbm.at[expert], wgu_stage, sem_w.at[0]),
            pltpu.make_async_copy(wd_hbm.at[expert], wd_stage, sem_w.at[1]))


def _expert_body(te_ref, nxt_ref, nused_ref, tok_first_ref, tok_next_ref, dst_prev_ref, h2_hbm, wgu_hbm, bgu_ref,
                 wd_hbm, bd_ref, yg_hbm, xbuf0, xbuf1, ybuf0, ybuf1, wgu_stage, wd_stage, wgu_bf, wd_bf,
                 sem_in, sem_out, sem_w):
    i = pl.program_id(0)
    n_used = nused_ref[0]
    xbufs, ybufs = (xbuf0, xbuf1), (ybuf0, ybuf1)

    @pl.when(i == 0)
    def _():
        ybuf1[...] = jnp.zeros_like(ybuf1)
        for cp in _weight_copies(wgu_hbm, wd_hbm, wgu_stage, wd_stage, sem_w, te_ref[0]):
            cp.start(priority=1)

        def first_rows(r, carry):
            _row_copy_in(h2_hbm, tok_first_ref, xbuf0, sem_in.at[0], r).start()
            return carry
        lax.fori_loop(0, EXPERT_TILE, first_rows, 0)

    @pl.when(jnp.logical_and(i < n_used, jnp.logical_or(i == 0, te_ref[i] != te_ref[jnp.maximum(i - 1, 0)])))
    def _():
        for cp in _weight_copies(wgu_hbm, wd_hbm, wgu_stage, wd_stage, sem_w, te_ref[i]):
            cp.wait()
        wgu_bf[...] = wgu_stage[...].astype(BF16)
        wd_bf[...] = wd_stage[...].astype(BF16)

        @pl.when(nxt_ref[i] >= 0)
        def _():
            for cp in _weight_copies(wgu_hbm, wd_hbm, wgu_stage, wd_stage, sem_w, nxt_ref[i]):
                cp.start(priority=1)

    for par in range(2):
        @pl.when(jnp.logical_and(i < n_used, i % 2 == par))
        def _(par=par):
            _expert_tile(tok_next_ref, dst_prev_ref, h2_hbm, bgu_ref, bd_ref, yg_hbm, wgu_bf, wd_bf,
                         xbufs[par], xbufs[1 - par], ybufs[par], ybufs[1 - par],
                         sem_in.at[par], sem_in.at[1 - par], sem_out)

        @pl.when(jnp.logical_and(i == n_used, i % 2 == par))
        def _(par=par):
            pltpu.make_async_copy(h2_hbm.at[pl.ds(0, EXPERT_TILE * ROW_TILE), :], xbufs[par], sem_in.at[par]).wait()

            def last_rows(r, carry):
                _row_copy_out(ybufs[1 - par], dst_prev_ref, yg_hbm, sem_out, r).start()
                return carry
            lax.fori_loop(0, EXPERT_TILE, last_rows, 0)
            pltpu.make_async_copy(ybufs[1 - par], yg_hbm.at[pl.ds(0, EXPERT_TILE * ROW_TILE), :], sem_out).wait()


def _combine_body(x1_ref, gw_ref, y0_ref, y1_ref, y2_ref, y3_ref, out_ref):
    gw = gw_ref[...]
    rows = x1_ref.shape[0]
    for s in range(ROW_TILE):
        cols = slice(s * LANES, (s + 1) * LANES)
        acc = x1_ref[:, cols]
        for k, y_ref in enumerate((y0_ref, y1_ref, y2_ref, y3_ref)):
            acc = acc + gw[:, k:k + 1] * y_ref[pl.ds(s, rows, stride=ROW_TILE), :]
        out_ref[:, cols] = acc


def _const_spec(shape):
    nd = len(shape)
    return pl.BlockSpec(shape, lambda *_: (0,) * nd, pipeline_mode=pl.Buffered(1))


def _q_perm():
    cols = np.arange(ATTN_WIDTH)
    j, half, d = cols // LANES, (cols % LANES) // HEAD_DIM, cols % HEAD_DIM
    return (j + Q_PER_KV * half) * HEAD_DIM + d


def _alibi_slopes():
    return 2.0 ** (-8.0 * np.arange(1, N_HEADS + 1) / N_HEADS)


def _prompt_bias():
    qi = np.arange(PAIR)[:, None]
    kj = np.arange(BAND)[None, :]
    dist = np.abs(qi + WINDOW - kj).astype(np.float64)
    cq, ck = qi // CHUNK, kj // CHUNK
    in_band = (ck >= cq) & (ck <= cq + WINDOW // CHUNK)
    base = -_alibi_slopes()[:, None, None] * dist[None]
    later = np.where(in_band[None], base, NEG_INF)
    first = np.where((kj >= WINDOW)[None], later, NEG_INF)
    return np.stack([first, later]).astype(np.float32)


def _sample_bias(n_new, n_cached):
    qi = np.arange(n_new)[:, None]
    dc = np.abs(qi + n_cached - np.arange(n_cached)[None, :]).astype(np.float64)
    dn = np.abs(qi - np.arange(n_new)[None, :]).astype(np.float64)
    sl = _alibi_slopes()[:, None, None]
    bc = (-sl * dc[None]).reshape(N_HEADS * n_new, n_cached)
    bn = (-sl * dn[None]).reshape(N_HEADS * n_new, n_new)
    return bc.astype(np.float32), bn.astype(np.float32)


def kernel(x_prompt, x_sample, cache_k_win, cache_v_win, g_mix, w_in, q_norm_g, k_norm_g, attn_sinks, sgu_norm_g,
           w_spatial, b_spatial, w_branch_attn, w_branch_sgu, w_out, g_ffn, w_router, b_router, w_gate_up,
           b_gate_up, w_down, b_down):
    n_b, seq, _ = x_prompt.shape
    n_streams, n_new, _ = x_sample.shape
    n_cached = cache_k_win.shape[2]
    n_prompt = n_b * seq
    n_sample = n_streams * n_new
    n_tok = n_prompt + n_sample
    assert seq % TOKEN_BLOCK == 0 and n_sample == TOKEN_BLOCK and n_cached == WINDOW
    blocks_per_seq = seq // TOKEN_BLOCK
    n_prompt_blocks = n_prompt // TOKEN_BLOCK

    perm = _q_perm()
    w_in_l = w_in[0]
    w_in_b = jnp.concatenate([w_in_l[:, perm], w_in_l[:, Q_END:]], axis=1).astype(BF16)
    wba_b = w_branch_attn[0][perm, :].astype(BF16)
    wbs_b = w_branch_sgu[0].astype(BF16)
    wout_b = w_out[0].astype(BF16)
    wr_b = w_router[0].astype(BF16)
    br = b_router[0].reshape(1, N_EXPERTS).astype(F32)
    gmix = g_mix[0].reshape(1, D_MODEL)
    gffn = g_ffn[0].reshape(1, D_MODEL)
    qg = jnp.tile(q_norm_g[0], LANES // HEAD_DIM).reshape(1, LANES)
    kg = jnp.tile(k_norm_g[0], LANES // HEAD_DIM).reshape(1, LANES)
    sgug = sgu_norm_g[0].reshape(1, MLP_WIDTH)
    sinks = attn_sinks[0].astype(F32)
    lane_seg = np.arange(LANES) // HEAD_DIM
    seg64 = jnp.asarray(np.tile((lane_seg[:, None] == lane_seg[None, :]) / HEAD_DIM, (2, 1)), BF16)
    seg128 = jnp.full((2 * LANES, LANES), 1.0 / MLP_GROUP_DIM, BF16)
    tri = jnp.asarray(np.tril(np.ones((TOKEN_BLOCK, TOKEN_BLOCK)), -1), BF16)
    wsp = w_spatial[0]
    bsp_p = jnp.repeat(b_spatial[0].T, MLP_GROUP_DIM, axis=1)
    wsp_s = jnp.tile(wsp[:, :n_new, :n_new], (1, n_streams, n_streams))
    bsp_s = jnp.tile(jnp.repeat(b_spatial[0][:, :n_new].T, MLP_GROUP_DIM, axis=1), (n_streams, 1))
    bias_p = jnp.asarray(_prompt_bias())
    bias_c, bias_n = (jnp.asarray(a) for a in _sample_bias(n_new, n_cached))

    smem = pl.BlockSpec(memory_space=pltpu.SMEM)
    tb = TOKEN_BLOCK
    last_blk = n_prompt_blocks - 1
    seq_of = lambda i: jnp.minimum(i, last_blk) // blocks_per_seq
    x_spec = pl.BlockSpec((1, tb, D_MODEL), lambda i: (seq_of(i), jnp.minimum(i, last_blk) % blocks_per_seq, 0))
    routed_blk = lambda i: jnp.where(i > n_prompt_blocks, n_prompt_blocks, jnp.clip(i - 1, 0, last_blk))
    tok_spec = lambda width: pl.BlockSpec((tb, width), lambda i: (routed_blk(i), 0))
    win_spec = lambda width: pl.BlockSpec((1, WINDOW, width), lambda i: (seq_of(i), 0, 0))
    route_shapes = [jax.ShapeDtypeStruct((n_tok, TOP_K), jnp.int32), jax.ShapeDtypeStruct((n_tok, TOP_K), F32),
                    jax.ShapeDtypeStruct((n_tok, TOP_K), jnp.int32)]

    x1p, h2, idx_p, gw_p, rank_p, cnt_p, kwin_p, vwin_p, sguv_p = pl.pallas_call(
        functools.partial(_mixer_prompt_body, blocks_per_seq=blocks_per_seq, n_blocks=n_prompt_blocks),
        grid=(n_prompt_blocks + 2,),
        in_specs=[smem, x_spec,
                  _const_spec((D_MODEL, IN_COLS)), _const_spec((1, D_MODEL)), _const_spec((1, LANES)),
                  _const_spec((1, LANES)), _const_spec((1, MLP_WIDTH)), _const_spec((2 * LANES, LANES)),
                  _const_spec((2 * LANES, LANES)), _const_spec((2, N_HEADS, PAIR, BAND)),
                  _const_spec((MLP_GROUPS, MLP_CHUNK, MLP_CHUNK)), _const_spec((MLP_CHUNK, MLP_WIDTH)),
                  _const_spec((ATTN_WIDTH, D_MODEL)), _const_spec((MLP_WIDTH, D_MODEL)),
                  _const_spec((D_MODEL, D_MODEL)), _const_spec((1, D_MODEL)), _const_spec((D_MODEL, N_EXPERTS)),
                  _const_spec((1, N_EXPERTS)), _const_spec((tb, tb))],
        out_specs=[x_spec,
                   pl.BlockSpec((tb * ROW_TILE, LANES), lambda i: (jnp.minimum(i, n_prompt_blocks), 0)),
                   tok_spec(TOP_K), tok_spec(TOP_K), tok_spec(TOP_K),
                   pl.BlockSpec((1, N_EXPERTS), lambda i: (0, 0)),
                   win_spec(KV_WIDTH), win_spec(KV_WIDTH), win_spec(MLP_WIDTH)],
        out_shape=[jax.ShapeDtypeStruct((n_b, seq, D_MODEL), F32), jax.ShapeDtypeStruct((n_tok * ROW_TILE, LANES), F32),
                   *route_shapes, jax.ShapeDtypeStruct((1, N_EXPERTS), F32),
                   jax.ShapeDtypeStruct((n_b, WINDOW, KV_WIDTH), F32),
                   jax.ShapeDtypeStruct((n_b, WINDOW, KV_WIDTH), F32),
                   jax.ShapeDtypeStruct((n_b, MLP_CHUNK, MLP_WIDTH), F32)],
        scratch_shapes=[pltpu.VMEM((WINDOW, KV_WIDTH), BF16), pltpu.VMEM((WINDOW, KV_WIDTH), BF16),
                        pltpu.VMEM((1, N_EXPERTS), F32), pltpu.VMEM((tb, D_MODEL), BF16)],
        compiler_params=pltpu.CompilerParams(dimension_semantics=("arbitrary",), vmem_limit_bytes=VMEM_LIMIT),
        name="mixer_prompt",
    )(sinks, x_prompt, w_in_b, gmix, qg, kg, sgug, seg64, seg128, bias_p, wsp, bsp_p, wba_b, wbs_b, wout_b,
      gffn, wr_b, br, tri)

    full = lambda shape: pl.BlockSpec(shape, lambda i: (0,) * len(shape))
    last_tok = lambda width: pl.BlockSpec((tb, width), lambda i: (n_prompt_blocks, 0))
    any_spec = pl.BlockSpec(memory_space=pl.ANY)
    ck = cache_k_win[0].reshape(n_streams, n_cached, KV_WIDTH)
    cv = cache_v_win[0].reshape(n_streams, n_cached, KV_WIDTH)
    x1s, h2, idx_a, gw_a, rank_a, cnt_s, kwin_s, vwin_s, sguv_s = pl.pallas_call(
        functools.partial(_mixer_sample_body, n_streams=n_streams, n_new=n_new),
        grid=(1,),
        in_specs=[smem, full((n_sample, D_MODEL)), full((n_streams, n_cached, KV_WIDTH)),
                  full((n_streams, n_cached, KV_WIDTH)),
                  full((D_MODEL, IN_COLS)), full((1, D_MODEL)), full((1, LANES)), full((1, LANES)),
                  full((1, MLP_WIDTH)), full((2 * LANES, LANES)), full((2 * LANES, LANES)),
                  full((N_HEADS * n_new, n_cached)), full((N_HEADS * n_new, n_new)),
                  full((MLP_GROUPS, n_sample, n_sample)), full((n_sample, MLP_WIDTH)),
                  full((ATTN_WIDTH, D_MODEL)), full((MLP_WIDTH, D_MODEL)), full((D_MODEL, D_MODEL)),
                  full((1, D_MODEL)), full((D_MODEL, N_EXPERTS)), full((1, N_EXPERTS)), full((tb, tb)),
                  any_spec, any_spec, any_spec, any_spec],
        out_specs=[full((n_sample, D_MODEL)), pl.BlockSpec((tb * ROW_TILE, LANES), lambda i: (n_prompt_blocks, 0)), last_tok(TOP_K), last_tok(TOP_K), last_tok(TOP_K),
                   full((1, N_EXPERTS)), full((n_streams, n_cached, KV_WIDTH)),
                   full((n_streams, n_cached, KV_WIDTH)), full((n_sample, MLP_WIDTH))],
        out_shape=[jax.ShapeDtypeStruct((n_sample, D_MODEL), F32), jax.ShapeDtypeStruct((n_tok * ROW_TILE, LANES), F32),
                   *route_shapes, jax.ShapeDtypeStruct((1, N_EXPERTS), F32),
                   jax.ShapeDtypeStruct((n_streams, n_cached, KV_WIDTH), F32),
                   jax.ShapeDtypeStruct((n_streams, n_cached, KV_WIDTH), F32),
                   jax.ShapeDtypeStruct((n_sample, MLP_WIDTH), F32)],
        input_output_aliases={22: 1, 23: 2, 24: 3, 25: 4},
        scratch_shapes=[pltpu.VMEM((1, N_EXPERTS), F32)],
        compiler_params=pltpu.CompilerParams(dimension_semantics=("arbitrary",), vmem_limit_bytes=VMEM_LIMIT),
        name="mixer_sample",
    )(sinks, x_sample.reshape(n_sample, D_MODEL), ck, cv, w_in_b, gmix, qg, kg, sgug, seg64, seg128, bias_c,
      bias_n, wsp_s, bsp_s, wba_b, wbs_b, wout_b, gffn, wr_b, br, tri, h2, idx_p, gw_p, rank_p)

    tm = EXPERT_TILE
    cnt_prompt = cnt_p[0].astype(jnp.int32)
    cnt_all = cnt_prompt + cnt_s[0].astype(jnp.int32)
    padded = ((cnt_all + tm - 1) // tm) * tm
    pad_end = jnp.cumsum(padded)
    n_pairs = TOP_K * n_tok
    n_tiles = n_pairs // tm + N_EXPERTS
    key_span = 1 << 17
    assert key_span >= n_pairs and 2 * N_EXPERTS * key_span < 2 ** 31
    pair_code = (jnp.arange(n_tok, dtype=jnp.int32)[:, None] * TOP_K + jnp.arange(TOP_K, dtype=jnp.int32)[None, :])
    real_keys = (idx_a * (2 * key_span) + pair_code).reshape(-1)
    pad_j = jnp.arange(tm, dtype=jnp.int32)[None, :]
    pad_keys = jnp.where(pad_j < (padded - cnt_all)[:, None],
                         (2 * jnp.arange(N_EXPERTS, dtype=jnp.int32)[:, None] + 1) * key_span + pad_j,
                         jnp.iinfo(jnp.int32).max).reshape(-1)
    keys = jnp.sort(jnp.concatenate([real_keys, pad_keys]))
    is_real = (keys // key_span) % 2 == 0
    code = keys % key_span
    tok_sorted = jnp.where(is_real, code // TOP_K, 0)
    spare = n_pairs + jnp.arange(tm, dtype=jnp.int32)
    dst_sorted = jnp.where(is_real, (code % TOP_K) * n_tok + code // TOP_K, jnp.tile(spare, n_tiles))
    tok_tiles = tok_sorted.reshape(n_tiles, 1, tm)
    n_steps = n_tiles + 1
    tok_next = jnp.concatenate([tok_tiles[1:], jnp.zeros((2, 1, tm), jnp.int32)], axis=0)
    dst_prev = jnp.concatenate([spare.reshape(1, 1, tm), dst_sorted.reshape(n_tiles, 1, tm)], axis=0)
    n_used = (pad_end[-1] // tm).astype(jnp.int32)
    tile_start = jnp.minimum(jnp.arange(n_steps, dtype=jnp.int32), n_used - 1) * tm
    tile_expert = jnp.sum((pad_end[None, :] <= tile_start[:, None]).astype(jnp.int32), axis=1)
    tile_expert = jnp.minimum(tile_expert, N_EXPERTS - 1)
    next_run_tile = pad_end[tile_expert] // tm
    next_expert = jnp.where(next_run_tile < n_used, tile_expert[jnp.minimum(next_run_tile, n_steps - 1)], -1)

    any_spec = pl.BlockSpec(memory_space=pl.ANY)
    yg = pl.pallas_call(
        _expert_body,
        grid_spec=pltpu.PrefetchScalarGridSpec(
            num_scalar_prefetch=3,
            grid=(n_steps,),
            in_specs=[pl.BlockSpec((1, 1, tm), lambda i, te, nx, nu: (0, 0, 0), memory_space=pltpu.SMEM),
                      pl.BlockSpec((1, 1, tm), lambda i, te, nx, nu: (i, 0, 0), memory_space=pltpu.SMEM),
                      pl.BlockSpec((1, 1, tm), lambda i, te, nx, nu: (i, 0, 0), memory_space=pltpu.SMEM),
                      any_spec, any_spec,
                      pl.BlockSpec((1, 1, 2 * D_EXPERT), lambda i, te, nx, nu: (te[i], 0, 0)),
                      any_spec,
                      pl.BlockSpec((1, 1, D_MODEL), lambda i, te, nx, nu: (te[i], 0, 0))],
            out_specs=any_spec,
            scratch_shapes=[pltpu.VMEM((tm * ROW_TILE, LANES), F32), pltpu.VMEM((tm * ROW_TILE, LANES), F32),
                            pltpu.VMEM((tm * ROW_TILE, LANES), F32), pltpu.VMEM((tm * ROW_TILE, LANES), F32),
                            pltpu.VMEM((D_MODEL, 2 * D_EXPERT), F32), pltpu.VMEM((D_EXPERT, D_MODEL), F32),
                            pltpu.VMEM((D_MODEL, 2 * D_EXPERT), BF16), pltpu.VMEM((D_EXPERT, D_MODEL), BF16),
                            pltpu.SemaphoreType.DMA((2,)), pltpu.SemaphoreType.DMA(()),
                            pltpu.SemaphoreType.DMA((2,))]),
        out_shape=jax.ShapeDtypeStruct(((n_pairs + tm) * ROW_TILE, LANES), F32),
        compiler_params=pltpu.CompilerParams(dimension_semantics=("arbitrary",), vmem_limit_bytes=VMEM_LIMIT),
        name="moe_experts",
    )(tile_expert, next_expert.astype(jnp.int32), n_used.reshape(1), tok_tiles[0:1] * ROW_TILE, tok_next * ROW_TILE,
      dst_prev * ROW_TILE, h2, w_gate_up[0], b_gate_up[0].reshape(N_EXPERTS, 1, 2 * D_EXPERT), w_down[0],
      b_down[0].reshape(N_EXPERTS, 1, D_MODEL))

    n_tok_blocks = n_tok // tb

    def combine(x1, first_block, n_blocks):
        y_spec = lambda k: pl.BlockSpec((tb * ROW_TILE, LANES), lambda i: (k * n_tok_blocks + first_block + i, 0))
        return pl.pallas_call(
            _combine_body,
            grid=(n_blocks,),
            in_specs=[pl.BlockSpec((tb, D_MODEL), lambda i: (i, 0)),
                      pl.BlockSpec((tb, TOP_K), lambda i: (first_block + i, 0)),
                      y_spec(0), y_spec(1), y_spec(2), y_spec(3)],
            out_specs=pl.BlockSpec((tb, D_MODEL), lambda i: (i, 0)),
            out_shape=jax.ShapeDtypeStruct(x1.shape, F32),
            compiler_params=pltpu.CompilerParams(dimension_semantics=("arbitrary",)),
            name="moe_combine",
        )(x1, gw_a, yg, yg, yg, yg)

    y_prompt = combine(x1p.reshape(n_prompt, D_MODEL), 0, n_prompt_blocks).reshape(n_b, seq, D_MODEL)
    y_sample = combine(x1s, n_prompt_blocks, 1).reshape(n_streams, n_new, D_MODEL)

    kv_shape = (N_KV_HEADS, HEAD_DIM)
    sg_shape = (MLP_GROUPS, MLP_GROUP_DIM)
    return (y_prompt, y_sample,
            kwin_p.reshape(1, n_b, WINDOW, *kv_shape), vwin_p.reshape(1, n_b, WINDOW, *kv_shape),
            kwin_s.reshape(1, n_streams, n_cached, *kv_shape), vwin_s.reshape(1, n_streams, n_cached, *kv_shape),
            sguv_p.reshape(1, n_b, MLP_CHUNK, *sg_shape), sguv_s.reshape(1, n_streams, n_new, *sg_shape))
```

```python
import functools

import jax
import jax.numpy as jnp
import numpy as np
from jax import lax
from jax.experimental import pallas as pl
from jax.experimental.pallas import tpu as pltpu

D_MODEL = 1024
CHUNK = 64
WINDOW = 128
HEAD_DIM = 64
N_HEADS = 8
N_KV_HEADS = 2
Q_PER_KV = N_HEADS // N_KV_HEADS
ATTN_WIDTH = N_HEADS * HEAD_DIM
KV_WIDTH = N_KV_HEADS * HEAD_DIM
ATTN_SCALE = HEAD_DIM ** -0.5
MLP_CHUNK = 128
MLP_GROUPS = 8
MLP_WIDTH = D_MODEL
MLP_GROUP_DIM = MLP_WIDTH // MLP_GROUPS
N_EXPERTS = 32
TOP_K = 4
D_EXPERT = D_MODEL
SWIGLU_ALPHA = 1.702
SWIGLU_LIMIT = 7.0
EPS = 1e-6
NEG_INF = -1e30
Q_END = ATTN_WIDTH
K_END = Q_END + KV_WIDTH
V_END = K_END + KV_WIDTH
U_END = V_END + MLP_WIDTH
VM_END = U_END + MLP_WIDTH
GA_END = VM_END + D_MODEL
IN_COLS = GA_END + D_MODEL

LANES = 128
ROW_TILE = D_MODEL // LANES
TOKEN_BLOCK = 256
PAIR = 2 * CHUNK
BAND = PAIR + WINDOW
EXPERT_TILE = 256
VMEM_LIMIT = 56 * 1024 * 1024

F32 = jnp.float32
BF16 = jnp.bfloat16


def _dot(a, b):
    return jnp.dot(a, b, preferred_element_type=F32)


def _dot_nt(a, b):
    return lax.dot_general(a, b, (((1,), (1,)), ((), ())), preferred_element_type=F32)


def _store_token_major(ref, val):
    rows = val.shape[0]
    for s in range(ROW_TILE):
        ref[pl.ds(s, rows, stride=ROW_TILE), :] = val[:, s * LANES:(s + 1) * LANES]


def _load_token_major(ref, rows):
    return jnp.concatenate([ref[pl.ds(s, rows, stride=ROW_TILE), :] for s in range(ROW_TILE)], axis=-1)


def _segment_mean(sq, seg):
    hi = sq.astype(BF16)
    lo = (sq - hi.astype(F32)).astype(BF16)
    return _dot(jnp.concatenate([hi, lo], axis=-1), seg)


def _rms_rows(x, gain):
    ms = jnp.mean(x * x, axis=-1, keepdims=True)
    return (x * lax.rsqrt(ms + EPS)) * gain


def _project(h, w_in_ref, qg, kg, sgug, seg64, seg128, between=None):
    qkv = _dot(h, w_in_ref[:, 0:V_END])
    qk_cols = []
    for c in range(K_END // LANES):
        blk = qkv[:, c * LANES:(c + 1) * LANES]
        ms = _segment_mean(blk * blk, seg64)
        g = qg if c < Q_END // LANES else kg
        qk_cols.append((blk * lax.rsqrt(ms + EPS)) * g)
    qn = jnp.concatenate(qk_cols[:Q_END // LANES], axis=-1)
    kn = qk_cols[Q_END // LANES]
    v = qkv[:, K_END:V_END]
    u = jax.nn.gelu(_dot(h, w_in_ref[:, V_END:U_END]))
    vg = jax.nn.gelu(_dot(h, w_in_ref[:, U_END:VM_END]))
    vm_cols = []
    for g in range(MLP_GROUPS):
        blk = vg[:, g * LANES:(g + 1) * LANES]
        ms = _segment_mean(blk * blk, seg128)
        vm_cols.append((blk * lax.rsqrt(ms + EPS)) * sgug[:, g * LANES:(g + 1) * LANES])
    vm = jnp.concatenate(vm_cols, axis=-1)
    if between is not None:
        between()
    ga = jax.nn.sigmoid(_dot(h, w_in_ref[:, VM_END:GA_END]))
    gb = jax.nn.sigmoid(_dot(h, w_in_ref[:, GA_END:IN_COLS]))
    return qn, kn, v, u, vm, ga, gb


def _stack_heads(q_rows):
    lane = lax.broadcasted_iota(jnp.int32, (q_rows.shape[0], LANES), 1)
    blocks = []
    for head in range(N_HEADS):
        j, half = head % Q_PER_KV, head // Q_PER_KV
        col = q_rows[:, j * LANES:(j + 1) * LANES]
        keep = (lane < HEAD_DIM) if half == 0 else (lane >= HEAD_DIM)
        blocks.append(jnp.where(keep, col, 0.0))
    return jnp.concatenate(blocks, axis=0).astype(BF16)


def _unstack_heads(o, rows):
    lane = lax.broadcasted_iota(jnp.int32, (rows, LANES), 1)
    cols = []
    for j in range(Q_PER_KV):
        lo = o[j * rows:(j + 1) * rows]
        hi = o[(j + Q_PER_KV) * rows:(j + Q_PER_KV + 1) * rows]
        cols.append(jnp.where(lane < HEAD_DIM, lo, hi))
    return jnp.concatenate(cols, axis=-1)


def _merge(att, sgu, ga, gb, wba_ref, wbs_ref, wout_ref):
    m = ga * _dot(att.astype(BF16), wba_ref[...]) + gb * _dot(sgu.astype(BF16), wbs_ref[...])
    return _dot(m.astype(BF16), wout_ref[...])


def _route(h2, wr_ref, br_ref, tri_ref):
    rows = h2.shape[0]
    logits = _dot(h2, wr_ref[...]) + br_ref[...]
    eidx = lax.broadcasted_iota(jnp.int32, (rows, N_EXPERTS), 1).astype(F32)
    work = logits
    vals, picks, onehots = [], [], []
    for _ in range(TOP_K):
        m = jnp.max(work, axis=-1, keepdims=True)
        sel = jnp.min(jnp.where(work == m, eidx, float(N_EXPERTS)), axis=-1, keepdims=True)
        oh = eidx == sel
        vals.append(m)
        picks.append(sel)
        onehots.append(oh)
        work = jnp.where(oh, -jnp.inf, work)
    exps = [jnp.exp(v - vals[0]) for v in vals]
    den = exps[0] + exps[1] + exps[2] + exps[3]
    mask = jnp.zeros((rows, N_EXPERTS), F32)
    for oh in onehots:
        mask = mask + jnp.where(oh, 1.0, 0.0)
    before = _dot(tri_ref[...], mask.astype(BF16))
    counts = jnp.sum(mask, axis=0, keepdims=True)
    k4 = lax.broadcasted_iota(jnp.int32, (rows, TOP_K), 1)
    w4 = jnp.zeros((rows, TOP_K), F32)
    pos4 = jnp.zeros((rows, TOP_K), jnp.int32)
    for k in range(TOP_K):
        lower_experts = jnp.sum(jnp.where(eidx < picks[k], counts, 0.0), axis=-1, keepdims=True)
        rank = jnp.sum(jnp.where(onehots[k], before, 0.0), axis=-1, keepdims=True)
        w4 = jnp.where(k4 == k, exps[k] / den, w4)
        pos4 = jnp.where(k4 == k, (lower_experts + rank).astype(jnp.int32), pos4)
    return w4, pos4, counts


def _pair_selector(pos4, values=None):
    rows = pos4.shape[0]
    col = lax.broadcasted_iota(jnp.int32, (rows, TOP_K * rows), 1)
    sel = jnp.zeros((rows, TOP_K * rows), F32)
    for k in range(TOP_K):
        sel = sel + jnp.where(col == pos4[:, k:k + 1], 1.0 if values is None else values[:, k:k + 1], 0.0)
    return sel


def _route_and_dispatch(h2, wr_ref, br_ref, tri_ref, gw_ref, pos_ref, cnt_ref, xc_ref):
    w4, pos4, counts = _route(h2, wr_ref, br_ref, tri_ref)
    gw_ref[...] = w4
    pos_ref[...] = pos4
    cnt_ref[0] = counts
    select = _pair_selector(pos4).astype(BF16)
    pairs = lax.dot_general(select, h2, (((0,), (0,)), ((), ())), preferred_element_type=F32)
    _store_token_major(xc_ref, pairs)


def _softmax_rows(parts, sink):
    m = sink
    for l in parts:
        m = jnp.maximum(m, jnp.max(l, axis=-1, keepdims=True))
    es = [jnp.exp(l - m) for l in parts]
    den = jnp.exp(sink - m)
    for e in es:
        den = den + jnp.sum(e, axis=-1, keepdims=True)
    return [e / den for e in es]


def _mixer_prompt_body(sinks_ref, x_ref, w_in_ref, gmix_ref, qg_ref, kg_ref, sgug_ref, seg64_ref, seg128_ref,
                       bias_ref, wsp_ref, bsp_ref, wba_ref, wbs_ref, wout_ref, gffn_ref, wr_ref, br_ref, tri_ref,
                       x1_ref, xc_ref, gw_ref, pos_ref, cnt_ref, kwin_ref, vwin_ref, sguv_ref,
                       kcarry, vcarry, h2_prev_ref, *, blocks_per_seq, n_blocks):
    i = pl.program_id(0)

    @pl.when(i == 0)
    def _():
        h2_prev_ref[...] = jnp.zeros_like(h2_prev_ref)

    def route_previous():
        _route_and_dispatch(h2_prev_ref[...], wr_ref, br_ref, tri_ref, gw_ref, pos_ref, cnt_ref, xc_ref)

    @pl.when(i == n_blocks)
    def _():
        route_previous()

    @pl.when(i == n_blocks + 1)
    def _():
        xc_ref[...] = jnp.zeros_like(xc_ref)
        gw_ref[...] = jnp.zeros_like(gw_ref)
        pos_ref[...] = jnp.zeros_like(pos_ref)
        cnt_ref[...] = jnp.zeros_like(cnt_ref)

    @pl.when(jnp.logical_and(i < n_blocks, i % blocks_per_seq == 0))
    def _():
        kcarry[...] = jnp.zeros_like(kcarry)
        vcarry[...] = jnp.zeros_like(vcarry)

    @pl.when(i < n_blocks)
    def _():
        _mixer_prompt_block(i % blocks_per_seq, sinks_ref, x_ref, w_in_ref, gmix_ref, qg_ref, kg_ref, sgug_ref,
                            seg64_ref, seg128_ref, bias_ref, wsp_ref, bsp_ref, wba_ref, wbs_ref, wout_ref, gffn_ref,
                            x1_ref, kwin_ref, vwin_ref, sguv_ref, kcarry, vcarry, h2_prev_ref, route_previous)


def _mixer_prompt_block(j, sinks_ref, x_ref, w_in_ref, gmix_ref, qg_ref, kg_ref, sgug_ref, seg64_ref, seg128_ref,
                        bias_ref, wsp_ref, bsp_ref, wba_ref, wbs_ref, wout_ref, gffn_ref,
                        x1_ref, kwin_ref, vwin_ref, sguv_ref, kcarry, vcarry, h2_prev_ref, route_previous):

    x = x_ref[0]
    h = _rms_rows(x, gmix_ref[...]).astype(BF16)
    qn, kn, v, u, vm, ga, gb = _project(h, w_in_ref, qg_ref[...], kg_ref[...], sgug_ref[...],
                                        seg64_ref[...], seg128_ref[...], between=route_previous)
    k_ext = jnp.concatenate([kcarry[...], kn.astype(BF16)], axis=0)
    v_ext = jnp.concatenate([vcarry[...], v.astype(BF16)], axis=0)
    kcarry[...] = k_ext[TOKEN_BLOCK:]
    vcarry[...] = v_ext[TOKEN_BLOCK:]

    tri_mask = (lax.broadcasted_iota(jnp.int32, (MLP_CHUNK, MLP_CHUNK), 0)
                >= lax.broadcasted_iota(jnp.int32, (MLP_CHUNK, MLP_CHUNK), 1))
    att_rows, sgu_rows = [], []
    for pm in range(TOKEN_BLOCK // PAIR):
        r0 = pm * PAIR
        q_stack = _stack_heads(qn[r0:r0 + PAIR])
        k_band = k_ext[r0:r0 + BAND]
        v_band = v_ext[r0:r0 + BAND]
        s = _dot_nt(q_stack, k_band)
        first = jnp.where(j == 0, 0, 1) if pm == 0 else 1
        probs = []
        for head in range(N_HEADS):
            logit = s[head * PAIR:(head + 1) * PAIR] * ATTN_SCALE + bias_ref[first, head]
            probs.append(_softmax_rows([logit], sinks_ref[head])[0].astype(BF16))
        o = _dot(jnp.concatenate(probs, axis=0), v_band)
        att_rows.append(_unstack_heads(o, PAIR))
        cols = []
        for g in range(MLP_GROUPS):
            wm = jnp.where(tri_mask, wsp_ref[g], 0.0).astype(BF16)
            cols.append(_dot(wm, vm[r0:r0 + PAIR, g * LANES:(g + 1) * LANES].astype(BF16)))
        mixed = jnp.concatenate(cols, axis=-1) + bsp_ref[...]
        sgu_rows.append(u[r0:r0 + PAIR] * mixed)
    att = jnp.concatenate(att_rows, axis=0)
    sgu = jnp.concatenate(sgu_rows, axis=0)

    x1 = x + _merge(att, sgu, ga, gb, wba_ref, wbs_ref, wout_ref)
    h2 = _rms_rows(x1, gffn_ref[...])
    h2_prev_ref[...] = h2.astype(BF16)

    x1_ref[0] = x1
    kwin_ref[0] = kn[TOKEN_BLOCK - WINDOW:]
    vwin_ref[0] = v[TOKEN_BLOCK - WINDOW:]
    sguv_ref[0] = vm[TOKEN_BLOCK - MLP_CHUNK:]


def _mixer_sample_body(sinks_ref, x_ref, ck_ref, cv_ref, w_in_ref, gmix_ref, qg_ref, kg_ref, sgug_ref, seg64_ref,
                       seg128_ref, biasc_ref, biasn_ref, wsp_ref, bsp_ref, wba_ref, wbs_ref, wout_ref, gffn_ref,
                       wr_ref, br_ref, tri_ref, xc_in_ref, gw_in_ref, pos_in_ref, cnt_in_ref,
                       x1_ref, xc_ref, gw_ref, pos_ref, cnt_ref, kwin_ref, vwin_ref, sguv_ref, *, n_streams, n_new):
    del xc_in_ref, gw_in_ref, pos_in_ref, cnt_in_ref
    x = x_ref[...]
    h = _rms_rows(x, gmix_ref[...]).astype(BF16)
    qn, kn, v, u, vm, ga, gb = _project(h, w_in_ref, qg_ref[...], kg_ref[...], sgug_ref[...],
                                        seg64_ref[...], seg128_ref[...])
    n_cached = ck_ref.shape[1]
    att_rows = []
    for s_i in range(n_streams):
        r0 = s_i * n_new
        q_stack = _stack_heads(qn[r0:r0 + n_new])
        k_new = kn[r0:r0 + n_new]
        v_new = v[r0:r0 + n_new]
        s_c = _dot_nt(q_stack, ck_ref[s_i].astype(BF16))
        s_n = _dot_nt(q_stack, k_new.astype(BF16))
        pc, pn = [], []
        for head in range(N_HEADS):
            rows = slice(head * n_new, (head + 1) * n_new)
            lc = s_c[rows] * ATTN_SCALE + biasc_ref[rows]
            ln = s_n[rows] * ATTN_SCALE + biasn_ref[rows]
            p_c, p_n = _softmax_rows([lc, ln], sinks_ref[head])
            pc.append(p_c.astype(BF16))
            pn.append(p_n.astype(BF16))
        o = (_dot(jnp.concatenate(pc, axis=0), cv_ref[s_i].astype(BF16))
             + _dot(jnp.concatenate(pn, axis=0), v_new.astype(BF16)))
        att_rows.append(_unstack_heads(o, n_new))
        kwin_ref[s_i, 0:n_cached - n_new] = ck_ref[s_i, n_new:n_cached]
        kwin_ref[s_i, n_cached - n_new:n_cached] = k_new
        vwin_ref[s_i, 0:n_cached - n_new] = cv_ref[s_i, n_new:n_cached]
        vwin_ref[s_i, n_cached - n_new:n_cached] = v_new
    att = jnp.concatenate(att_rows, axis=0)

    rows = n_streams * n_new
    ri = lax.broadcasted_iota(jnp.int32, (rows, rows), 0)
    ci = lax.broadcasted_iota(jnp.int32, (rows, rows), 1)
    keep = jnp.logical_and(ri // n_new == ci // n_new, ri % n_new >= ci % n_new)
    cols = []
    for g in range(MLP_GROUPS):
        wm = jnp.where(keep, wsp_ref[g], 0.0).astype(BF16)
        cols.append(_dot(wm, vm[:, g * LANES:(g + 1) * LANES].astype(BF16)))
    sgu = u * (jnp.concatenate(cols, axis=-1) + bsp_ref[...])

    x1 = x + _merge(att, sgu, ga, gb, wba_ref, wbs_ref, wout_ref)
    h2 = _rms_rows(x1, gffn_ref[...])
    _route_and_dispatch(h2.astype(BF16), wr_ref, br_ref, tri_ref, gw_ref, pos_ref, cnt_ref, xc_ref)
    x1_ref[...] = x1
    sguv_ref[...] = vm


def _piece_copies(i_tile, lo_ref, src_ref, off_ref, len_ref, hbm, buf, sem, to_hbm):
    def one(p, carry):
        n = pl.multiple_of(len_ref[p], ROW_TILE)

        @pl.when(n > 0)
        def _():
            far = hbm.at[pl.ds(pl.multiple_of(src_ref[p], ROW_TILE), n), :]
            near = buf.at[pl.ds(pl.multiple_of(off_ref[p], ROW_TILE), n), :]
            if to_hbm:
                pltpu.make_async_copy(near, far, sem).start()
            else:
                pltpu.make_async_copy(far, near, sem).start()
        return carry
    lax.fori_loop(lo_ref[i_tile], lo_ref[i_tile + 1], one, 0)


def _wait_rows(hbm, buf, sem, n_rows):
    n = pl.multiple_of(n_rows, ROW_TILE)
    pltpu.make_async_copy(hbm.at[pl.ds(0, n), :], buf.at[pl.ds(0, n), :], sem).wait()


def _expert_mlp(x_cur, y_cur, bgu_ref, bd_ref, wgu_bf, wd_bf):
    xb = _load_token_major(x_cur, EXPERT_TILE).astype(BF16)
    hgu = _dot(xb, wgu_bf[...]) + bgu_ref[0]
    glu = jnp.minimum(hgu[:, :D_EXPERT], SWIGLU_LIMIT)
    lin = jnp.clip(hgu[:, D_EXPERT:], -SWIGLU_LIMIT, SWIGLU_LIMIT)
    act = glu * jax.nn.sigmoid(glu * SWIGLU_ALPHA) * (lin + 1.0)
    _store_token_major(y_cur, _dot(act.astype(BF16), wd_bf[...]) + bd_ref[0])


def _weight_copies(wgu_hbm, wd_hbm, wgu_stage, wd_stage, sem_w, expert):
    return (pltpu.make_async_copy(wgu_hbm.at[expert], wgu_stage, sem_w.at[0]),
            pltpu.make_async_copy(wd_hbm.at[expert], wd_stage, sem_w.at[1]))


def _expert_body(te_ref, nxt_ref, nused_ref, valid_ref, lo_ref, src_ref, off_ref, len_ref, xc_hbm, wgu_hbm, bgu_ref,
                 wd_hbm, bd_ref, yc_hbm, xbuf0, xbuf1, ybuf0, ybuf1, wgu_stage, wd_stage, wgu_bf, wd_bf,
                 sem_in, sem_out, sem_w):
    i = pl.program_id(0)
    n_used = nused_ref[0]
    xbufs, ybufs = (xbuf0, xbuf1), (ybuf0, ybuf1)
    pieces = (lo_ref, src_ref, off_ref, len_ref)

    @pl.when(i == 0)
    def _():
        xbuf0[...] = jnp.zeros_like(xbuf0)
        xbuf1[...] = jnp.zeros_like(xbuf1)
        for cp in _weight_copies(wgu_hbm, wd_hbm, wgu_stage, wd_stage, sem_w, te_ref[0]):
            cp.start(priority=1)
        _piece_copies(0, *pieces, xc_hbm, xbuf0, sem_in.at[0], to_hbm=False)

    @pl.when(jnp.logical_and(i < n_used, jnp.logical_or(i == 0, te_ref[i] != te_ref[jnp.maximum(i - 1, 0)])))
    def _():
        for cp in _weight_copies(wgu_hbm, wd_hbm, wgu_stage, wd_stage, sem_w, te_ref[i]):
            cp.wait()
        wgu_bf[...] = wgu_stage[...].astype(BF16)
        wd_bf[...] = wd_stage[...].astype(BF16)

        @pl.when(nxt_ref[i] >= 0)
        def _():
            for cp in _weight_copies(wgu_hbm, wd_hbm, wgu_stage, wd_stage, sem_w, nxt_ref[i]):
                cp.start(priority=1)

    for par in range(2):
        x_cur, x_next, y_cur, y_prev = xbufs[par], xbufs[1 - par], ybufs[par], ybufs[1 - par]

        @pl.when(jnp.logical_and(i < n_used, i % 2 == par))
        def _(x_cur=x_cur, x_next=x_next, y_cur=y_cur, y_prev=y_prev, par=par):
            _wait_rows(xc_hbm, x_cur, sem_in.at[par], valid_ref[i])

            @pl.when(i + 1 < n_used)
            def _():
                _piece_copies(i + 1, *pieces, xc_hbm, x_next, sem_in.at[1 - par], to_hbm=False)

            @pl.when(i > 0)
            def _():
                _piece_copies(i - 1, *pieces, yc_hbm, y_prev, sem_out, to_hbm=True)

            _expert_mlp(x_cur, y_cur, bgu_ref, bd_ref, wgu_bf, wd_bf)

            @pl.when(i > 0)
            def _():
                _wait_rows(yc_hbm, y_prev, sem_out, valid_ref[i - 1])

        @pl.when(jnp.logical_and(i == n_used, i % 2 == par))
        def _(y_prev=y_prev):
            _piece_copies(i - 1, *pieces, yc_hbm, y_prev, sem_out, to_hbm=True)
            _wait_rows(yc_hbm, y_prev, sem_out, valid_ref[i - 1])


def _combine_body(x1_ref, gw_ref, pos_ref, yc_ref, out_ref):
    rows = x1_ref.shape[0]
    y = _load_token_major(yc_ref, TOP_K * rows)
    w = _pair_selector(pos_ref[...], gw_ref[...])
    w_hi = w.astype(BF16)
    w_lo = (w - w_hi.astype(F32)).astype(BF16)
    y_hi = y.astype(BF16)
    y_lo = (y - y_hi.astype(F32)).astype(BF16)
    out_ref[...] = x1_ref[...] + (_dot(w_hi, y_hi) + _dot(w_lo, y_hi) + _dot(w_hi, y_lo))


def _const_spec(shape):
    nd = len(shape)
    return pl.BlockSpec(shape, lambda *_: (0,) * nd, pipeline_mode=pl.Buffered(1))


def _q_perm():
    cols = np.arange(ATTN_WIDTH)
    j, half, d = cols // LANES, (cols % LANES) // HEAD_DIM, cols % HEAD_DIM
    return (j + Q_PER_KV * half) * HEAD_DIM + d


def _alibi_slopes():
    return 2.0 ** (-8.0 * np.arange(1, N_HEADS + 1) / N_HEADS)


def _prompt_bias():
    qi = np.arange(PAIR)[:, None]
    kj = np.arange(BAND)[None, :]
    dist = np.abs(qi + WINDOW - kj).astype(np.float64)
    cq, ck = qi // CHUNK, kj // CHUNK
    in_band = (ck >= cq) & (ck <= cq + WINDOW // CHUNK)
    base = -_alibi_slopes()[:, None, None] * dist[None]
    later = np.where(in_band[None], base, NEG_INF)
    first = np.where((kj >= WINDOW)[None], later, NEG_INF)
    return np.stack([first, later]).astype(np.float32)


def _sample_bias(n_new, n_cached):
    qi = np.arange(n_new)[:, None]
    dc = np.abs(qi + n_cached - np.arange(n_cached)[None, :]).astype(np.float64)
    dn = np.abs(qi - np.arange(n_new)[None, :]).astype(np.float64)
    sl = _alibi_slopes()[:, None, None]
    bc = (-sl * dc[None]).reshape(N_HEADS * n_new, n_cached)
    bn = (-sl * dn[None]).reshape(N_HEADS * n_new, n_new)
    return bc.astype(np.float32), bn.astype(np.float32)


def kernel(x_prompt, x_sample, cache_k_win, cache_v_win, g_mix, w_in, q_norm_g, k_norm_g, attn_sinks, sgu_norm_g,
           w_spatial, b_spatial, w_branch_attn, w_branch_sgu, w_out, g_ffn, w_router, b_router, w_gate_up,
           b_gate_up, w_down, b_down):
    n_b, seq, _ = x_prompt.shape
    n_streams, n_new, _ = x_sample.shape
    n_cached = cache_k_win.shape[2]
    n_prompt = n_b * seq
    n_sample = n_streams * n_new
    n_tok = n_prompt + n_sample
    assert seq % TOKEN_BLOCK == 0 and n_sample == TOKEN_BLOCK and n_cached == WINDOW
    blocks_per_seq = seq // TOKEN_BLOCK
    n_prompt_blocks = n_prompt // TOKEN_BLOCK

    perm = _q_perm()
    w_in_l = w_in[0]
    w_in_b = jnp.concatenate([w_in_l[:, perm], w_in_l[:, Q_END:]], axis=1).astype(BF16)
    wba_b = w_branch_attn[0][perm, :].astype(BF16)
    wbs_b = w_branch_sgu[0].astype(BF16)
    wout_b = w_out[0].astype(BF16)
    wr_b = w_router[0].astype(BF16)
    br = b_router[0].reshape(1, N_EXPERTS).astype(F32)
    gmix = g_mix[0].reshape(1, D_MODEL)
    gffn = g_ffn[0].reshape(1, D_MODEL)
    qg = jnp.tile(q_norm_g[0], LANES // HEAD_DIM).reshape(1, LANES)
    kg = jnp.tile(k_norm_g[0], LANES // HEAD_DIM).reshape(1, LANES)
    sgug = sgu_norm_g[0].reshape(1, MLP_WIDTH)
    sinks = attn_sinks[0].astype(F32)
    lane_seg = np.arange(LANES) // HEAD_DIM
    seg64 = jnp.asarray(np.tile((lane_seg[:, None] == lane_seg[None, :]) / HEAD_DIM, (2, 1)), BF16)
    seg128 = jnp.full((2 * LANES, LANES), 1.0 / MLP_GROUP_DIM, BF16)
    tri = jnp.asarray(np.tril(np.ones((TOKEN_BLOCK, TOKEN_BLOCK)), -1), BF16)
    wsp = w_spatial[0]
    bsp_p = jnp.repeat(b_spatial[0].T, MLP_GROUP_DIM, axis=1)
    wsp_s = jnp.tile(wsp[:, :n_new, :n_new], (1, n_streams, n_streams))
    bsp_s = jnp.tile(jnp.repeat(b_spatial[0][:, :n_new].T, MLP_GROUP_DIM, axis=1), (n_streams, 1))
    bias_p = jnp.asarray(_prompt_bias())
    bias_c, bias_n = (jnp.asarray(a) for a in _sample_bias(n_new, n_cached))

    smem = pl.BlockSpec(memory_space=pltpu.SMEM)
    tb = TOKEN_BLOCK
    pb = TOP_K * tb
    n_tok_blocks = n_tok // tb
    last_blk = n_prompt_blocks - 1
    seq_of = lambda i: jnp.minimum(i, last_blk) // blocks_per_seq
    x_spec = pl.BlockSpec((1, tb, D_MODEL), lambda i: (seq_of(i), jnp.minimum(i, last_blk) % blocks_per_seq, 0))
    routed_blk = lambda i: jnp.where(i > n_prompt_blocks, n_prompt_blocks, jnp.clip(i - 1, 0, last_blk))
    win_spec = lambda width: pl.BlockSpec((1, WINDOW, width), lambda i: (seq_of(i), 0, 0))
    route_specs = lambda blk: [pl.BlockSpec((pb * ROW_TILE, LANES), lambda i: (blk(i), 0)),
                               pl.BlockSpec((tb, TOP_K), lambda i: (blk(i), 0)),
                               pl.BlockSpec((tb, TOP_K), lambda i: (blk(i), 0)),
                               pl.BlockSpec((1, 1, N_EXPERTS), lambda i: (blk(i), 0, 0))]
    route_shapes = [jax.ShapeDtypeStruct((n_tok * TOP_K * ROW_TILE, LANES), F32),
                    jax.ShapeDtypeStruct((n_tok, TOP_K), F32),
                    jax.ShapeDtypeStruct((n_tok, TOP_K), jnp.int32),
                    jax.ShapeDtypeStruct((n_tok_blocks, 1, N_EXPERTS), F32)]

    x1p, xc, gw_p, pos_p, cnt_p, kwin_p, vwin_p, sguv_p = pl.pallas_call(
        functools.partial(_mixer_prompt_body, blocks_per_seq=blocks_per_seq, n_blocks=n_prompt_blocks),
        grid=(n_prompt_blocks + 2,),
        in_specs=[smem, x_spec,
                  _const_spec((D_MODEL, IN_COLS)), _const_spec((1, D_MODEL)), _const_spec((1, LANES)),
                  _const_spec((1, LANES)), _const_spec((1, MLP_WIDTH)), _const_spec((2 * LANES, LANES)),
                  _const_spec((2 * LANES, LANES)), _const_spec((2, N_HEADS, PAIR, BAND)),
                  _const_spec((MLP_GROUPS, MLP_CHUNK, MLP_CHUNK)), _const_spec((MLP_CHUNK, MLP_WIDTH)),
                  _const_spec((ATTN_WIDTH, D_MODEL)), _const_spec((MLP_WIDTH, D_MODEL)),
                  _const_spec((D_MODEL, D_MODEL)), _const_spec((1, D_MODEL)), _const_spec((D_MODEL, N_EXPERTS)),
                  _const_spec((1, N_EXPERTS)), _const_spec((tb, tb))],
        out_specs=[x_spec, *route_specs(routed_blk),
                   win_spec(KV_WIDTH), win_spec(KV_WIDTH), win_spec(MLP_WIDTH)],
        out_shape=[jax.ShapeDtypeStruct((n_b, seq, D_MODEL), F32), *route_shapes,
                   jax.ShapeDtypeStruct((n_b, WINDOW, KV_WIDTH), F32),
                   jax.ShapeDtypeStruct((n_b, WINDOW, KV_WIDTH), F32),
                   jax.ShapeDtypeStruct((n_b, MLP_CHUNK, MLP_WIDTH), F32)],
        scratch_shapes=[pltpu.VMEM((WINDOW, KV_WIDTH), BF16), pltpu.VMEM((WINDOW, KV_WIDTH), BF16),
                        pltpu.VMEM((tb, D_MODEL), BF16)],
        compiler_params=pltpu.CompilerParams(dimension_semantics=("arbitrary",), vmem_limit_bytes=VMEM_LIMIT),
        name="mixer_prompt",
    )(sinks, x_prompt, w_in_b, gmix, qg, kg, sgug, seg64, seg128, bias_p, wsp, bsp_p, wba_b, wbs_b, wout_b,
      gffn, wr_b, br, tri)

    full = lambda shape: pl.BlockSpec(shape, lambda i: (0,) * len(shape))
    any_spec = pl.BlockSpec(memory_space=pl.ANY)
    ck = cache_k_win[0].reshape(n_streams, n_cached, KV_WIDTH)
    cv = cache_v_win[0].reshape(n_streams, n_cached, KV_WIDTH)
    x1s, xc, gw_a, pos_a, cnt_a, kwin_s, vwin_s, sguv_s = pl.pallas_call(
        functools.partial(_mixer_sample_body, n_streams=n_streams, n_new=n_new),
        grid=(1,),
        in_specs=[smem, full((n_sample, D_MODEL)), full((n_streams, n_cached, KV_WIDTH)),
                  full((n_streams, n_cached, KV_WIDTH)),
                  full((D_MODEL, IN_COLS)), full((1, D_MODEL)), full((1, LANES)), full((1, LANES)),
                  full((1, MLP_WIDTH)), full((2 * LANES, LANES)), full((2 * LANES, LANES)),
                  full((N_HEADS * n_new, n_cached)), full((N_HEADS * n_new, n_new)),
                  full((MLP_GROUPS, n_sample, n_sample)), full((n_sample, MLP_WIDTH)),
                  full((ATTN_WIDTH, D_MODEL)), full((MLP_WIDTH, D_MODEL)), full((D_MODEL, D_MODEL)),
                  full((1, D_MODEL)), full((D_MODEL, N_EXPERTS)), full((1, N_EXPERTS)), full((tb, tb)),
                  any_spec, any_spec, any_spec, any_spec],
        out_specs=[full((n_sample, D_MODEL)), *route_specs(lambda i: n_prompt_blocks),
                   full((n_streams, n_cached, KV_WIDTH)), full((n_streams, n_cached, KV_WIDTH)),
                   full((n_sample, MLP_WIDTH))],
        out_shape=[jax.ShapeDtypeStruct((n_sample, D_MODEL), F32), *route_shapes,
                   jax.ShapeDtypeStruct((n_streams, n_cached, KV_WIDTH), F32),
                   jax.ShapeDtypeStruct((n_streams, n_cached, KV_WIDTH), F32),
                   jax.ShapeDtypeStruct((n_sample, MLP_WIDTH), F32)],
        input_output_aliases={22: 1, 23: 2, 24: 3, 25: 4},
        compiler_params=pltpu.CompilerParams(dimension_semantics=("arbitrary",), vmem_limit_bytes=VMEM_LIMIT),
        name="mixer_sample",
    )(sinks, x_sample.reshape(n_sample, D_MODEL), ck, cv, w_in_b, gmix, qg, kg, sgug, seg64, seg128, bias_c,
      bias_n, wsp_s, bsp_s, wba_b, wbs_b, wout_b, gffn, wr_b, br, tri, xc, gw_p, pos_p, cnt_p)

    tm = EXPERT_TILE
    counts = cnt_a[:, 0, :].astype(jnp.int32)
    cnt_all = jnp.sum(counts, axis=0)
    padded = ((cnt_all + tm - 1) // tm) * tm
    pad_end = jnp.cumsum(padded)
    pad_off = pad_end - padded
    n_pairs = TOP_K * n_tok
    n_tiles = n_pairs // tm + N_EXPERTS
    n_steps = n_tiles + 1
    in_block = jnp.cumsum(counts, axis=1) - counts
    in_expert = jnp.cumsum(counts, axis=0) - counts
    run_src = (jnp.arange(n_tok_blocks, dtype=jnp.int32)[:, None] * pb + in_block).T.reshape(-1)
    run_start = (pad_off[None, :] + in_expert).T.reshape(-1)
    run_len = counts.T.reshape(-1)
    head = jnp.minimum(run_len, tm - run_start % tm)
    piece_start = jnp.stack([run_start, run_start + head], axis=1).reshape(-1)
    piece_src = jnp.stack([run_src, run_src + head], axis=1).reshape(-1)
    piece_len = jnp.stack([head, run_len - head], axis=1).reshape(-1)
    piece_tile = piece_start // tm
    tile_ids = jnp.arange(n_steps + 1, dtype=jnp.int32)
    piece_lo = jnp.sum((piece_tile[None, :] < tile_ids[:, None]).astype(jnp.int32), axis=1)
    n_used = (pad_end[-1] // tm).astype(jnp.int32)
    tile_start = jnp.minimum(tile_ids[:n_steps], n_used - 1) * tm
    tile_expert = jnp.sum((pad_end[None, :] <= tile_start[:, None]).astype(jnp.int32), axis=1)
    tile_expert = jnp.minimum(tile_expert, N_EXPERTS - 1)
    tile_valid = jnp.clip((pad_off + cnt_all)[tile_expert] - tile_start, 0, tm)
    next_run_tile = pad_end[tile_expert] // tm
    next_expert = jnp.where(next_run_tile < n_used, tile_expert[jnp.minimum(next_run_tile, n_steps - 1)], -1)

    prefetch = (tile_expert, next_expert, n_used.reshape(1), tile_valid * ROW_TILE, piece_lo,
                piece_src * ROW_TILE, (piece_start % tm) * ROW_TILE, piece_len * ROW_TILE)
    yc = pl.pallas_call(
        _expert_body,
        grid_spec=pltpu.PrefetchScalarGridSpec(
            num_scalar_prefetch=len(prefetch),
            grid=(n_steps,),
            in_specs=[any_spec, any_spec,
                      pl.BlockSpec((1, 1, 2 * D_EXPERT), lambda i, te, *_: (te[i], 0, 0)),
                      any_spec,
                      pl.BlockSpec((1, 1, D_MODEL), lambda i, te, *_: (te[i], 0, 0))],
            out_specs=any_spec,
            scratch_shapes=[pltpu.VMEM((tm * ROW_TILE, LANES), F32), pltpu.VMEM((tm * ROW_TILE, LANES), F32),
                            pltpu.VMEM((tm * ROW_TILE, LANES), F32), pltpu.VMEM((tm * ROW_TILE, LANES), F32),
                            pltpu.VMEM((D_MODEL, 2 * D_EXPERT), F32), pltpu.VMEM((D_EXPERT, D_MODEL), F32),
                            pltpu.VMEM((D_MODEL, 2 * D_EXPERT), BF16), pltpu.VMEM((D_EXPERT, D_MODEL), BF16),
                            pltpu.SemaphoreType.DMA((2,)), pltpu.SemaphoreType.DMA(()),
                            pltpu.SemaphoreType.DMA((2,))]),
        out_shape=jax.ShapeDtypeStruct((n_pairs * ROW_TILE, LANES), F32),
        compiler_params=pltpu.CompilerParams(dimension_semantics=("arbitrary",), vmem_limit_bytes=VMEM_LIMIT),
        name="moe_experts",
    )(*[p.astype(jnp.int32) for p in prefetch], xc, w_gate_up[0],
      b_gate_up[0].reshape(N_EXPERTS, 1, 2 * D_EXPERT), w_down[0], b_down[0].reshape(N_EXPERTS, 1, D_MODEL))

    def combine(x1, first_block, n_blocks):
        return pl.pallas_call(
            _combine_body,
            grid=(n_blocks,),
            in_specs=[pl.BlockSpec((tb, D_MODEL), lambda i: (i, 0)),
                      pl.BlockSpec((tb, TOP_K), lambda i: (first_block + i, 0)),
                      pl.BlockSpec((tb, TOP_K), lambda i: (first_block + i, 0)),
                      pl.BlockSpec((pb * ROW_TILE, LANES), lambda i: (first_block + i, 0))],
            out_specs=pl.BlockSpec((tb, D_MODEL), lambda i: (i, 0)),
            out_shape=jax.ShapeDtypeStruct(x1.shape, F32),
            compiler_params=pltpu.CompilerParams(dimension_semantics=("arbitrary",), vmem_limit_bytes=VMEM_LIMIT),
            name="moe_combine",
        )(x1, gw_a, pos_a, yc)

    y_prompt = combine(x1p.reshape(n_prompt, D_MODEL), 0, n_prompt_blocks).reshape(n_b, seq, D_MODEL)
    y_sample = combine(x1s, n_prompt_blocks, 1).reshape(n_streams, n_new, D_MODEL)

    kv_shape = (N_KV_HEADS, HEAD_DIM)
    sg_shape = (MLP_GROUPS, MLP_GROUP_DIM)
    return (y_prompt, y_sample,
            kwin_p.reshape(1, n_b, WINDOW, *kv_shape), vwin_p.reshape(1, n_b, WINDOW, *kv_shape),
            kwin_s.reshape(1, n_streams, n_cached, *kv_shape), vwin_s.reshape(1, n_streams, n_cached, *kv_shape),
            sguv_p.reshape(1, n_b, MLP_CHUNK, *sg_shape), sguv_s.reshape(1, n_streams, n_new, *sg_shape))
```

```python
import functools

import jax
import jax.numpy as jnp
import numpy as np
from jax import lax
from jax.experimental import pallas as pl
from jax.experimental.pallas import tpu as pltpu

D_MODEL = 1024
CHUNK = 64
WINDOW = 128
HEAD_DIM = 64
N_HEADS = 8
N_KV_HEADS = 2
Q_PER_KV = N_HEADS // N_KV_HEADS
ATTN_WIDTH = N_HEADS * HEAD_DIM
KV_WIDTH = N_KV_HEADS * HEAD_DIM
ATTN_SCALE = HEAD_DIM ** -0.5
MLP_CHUNK = 128
MLP_GROUPS = 8
MLP_WIDTH = D_MODEL
MLP_GROUP_DIM = MLP_WIDTH // MLP_GROUPS
N_EXPERTS = 32
TOP_K = 4
D_EXPERT = D_MODEL
SWIGLU_ALPHA = 1.702
SWIGLU_LIMIT = 7.0
EPS = 1e-6
NEG_INF = -1e30
Q_END = ATTN_WIDTH
K_END = Q_END + KV_WIDTH
V_END = K_END + KV_WIDTH
U_END = V_END + MLP_WIDTH
VM_END = U_END + MLP_WIDTH
GA_END = VM_END + D_MODEL
IN_COLS = GA_END + D_MODEL

LANES = 128
ROW_TILE = D_MODEL // LANES
TOKEN_BLOCK = 256
PAIR = 2 * CHUNK
BAND = PAIR + WINDOW
EXPERT_TILE = 256
VMEM_LIMIT = 56 * 1024 * 1024

F32 = jnp.float32
BF16 = jnp.bfloat16


def _dot(a, b):
    return jnp.dot(a, b, preferred_element_type=F32)


def _dot_nt(a, b):
    return lax.dot_general(a, b, (((1,), (1,)), ((), ())), preferred_element_type=F32)


def _store_token_major(ref, val):
    rows = val.shape[0]
    for s in range(ROW_TILE):
        ref[pl.ds(s, rows, stride=ROW_TILE), :] = val[:, s * LANES:(s + 1) * LANES]


def _load_token_major(ref, rows):
    return jnp.concatenate([ref[pl.ds(s, rows, stride=ROW_TILE), :] for s in range(ROW_TILE)], axis=-1)


def _segment_mean(sq, seg):
    hi = sq.astype(BF16)
    lo = (sq - hi.astype(F32)).astype(BF16)
    return _dot(jnp.concatenate([hi, lo], axis=-1), seg)


def _rms_rows(x, gain):
    ms = jnp.mean(x * x, axis=-1, keepdims=True)
    return (x * lax.rsqrt(ms + EPS)) * gain


def _project(h, w_in_ref, qg, kg, sgug, seg64, seg128, between=None):
    qkv = _dot(h, w_in_ref[:, 0:V_END])
    qk_cols = []
    for c in range(K_END // LANES):
        blk = qkv[:, c * LANES:(c + 1) * LANES]
        ms = _segment_mean(blk * blk, seg64)
        g = qg if c < Q_END // LANES else kg
        qk_cols.append((blk * lax.rsqrt(ms + EPS)) * g)
    qn = jnp.concatenate(qk_cols[:Q_END // LANES], axis=-1)
    kn = qk_cols[Q_END // LANES]
    v = qkv[:, K_END:V_END]
    u = jax.nn.gelu(_dot(h, w_in_ref[:, V_END:U_END]))
    vg = jax.nn.gelu(_dot(h, w_in_ref[:, U_END:VM_END]))
    vm_cols = []
    for g in range(MLP_GROUPS):
        blk = vg[:, g * LANES:(g + 1) * LANES]
        ms = _segment_mean(blk * blk, seg128)
        vm_cols.append((blk * lax.rsqrt(ms + EPS)) * sgug[:, g * LANES:(g + 1) * LANES])
    vm = jnp.concatenate(vm_cols, axis=-1)
    if between is not None:
        between()
    ga = jax.nn.sigmoid(_dot(h, w_in_ref[:, VM_END:GA_END]))
    gb = jax.nn.sigmoid(_dot(h, w_in_ref[:, GA_END:IN_COLS]))
    return qn, kn, v, u, vm, ga, gb


def _stack_heads(q_rows):
    lane = lax.broadcasted_iota(jnp.int32, (q_rows.shape[0], LANES), 1)
    blocks = []
    for head in range(N_HEADS):
        j, half = head % Q_PER_KV, head // Q_PER_KV
        col = q_rows[:, j * LANES:(j + 1) * LANES]
        keep = (lane < HEAD_DIM) if half == 0 else (lane >= HEAD_DIM)
        blocks.append(jnp.where(keep, col, 0.0))
    return jnp.concatenate(blocks, axis=0).astype(BF16)


def _unstack_heads(o, rows):
    lane = lax.broadcasted_iota(jnp.int32, (rows, LANES), 1)
    cols = []
    for j in range(Q_PER_KV):
        lo = o[j * rows:(j + 1) * rows]
        hi = o[(j + Q_PER_KV) * rows:(j + Q_PER_KV + 1) * rows]
        cols.append(jnp.where(lane < HEAD_DIM, lo, hi))
    return jnp.concatenate(cols, axis=-1)


def _merge(att, sgu, ga, gb, wba_ref, wbs_ref, wout_ref):
    m = ga * _dot(att.astype(BF16), wba_ref[...]) + gb * _dot(sgu.astype(BF16), wbs_ref[...])
    return _dot(m.astype(BF16), wout_ref[...])


def _route(h2, wr_ref, br_ref, tri_ref):
    rows = h2.shape[0]
    logits = _dot(h2, wr_ref[...]) + br_ref[...]
    eidx = lax.broadcasted_iota(jnp.int32, (rows, N_EXPERTS), 1).astype(F32)
    work = logits
    vals, picks, onehots = [], [], []
    for _ in range(TOP_K):
        m = jnp.max(work, axis=-1, keepdims=True)
        sel = jnp.min(jnp.where(work == m, eidx, float(N_EXPERTS)), axis=-1, keepdims=True)
        oh = eidx == sel
        vals.append(m)
        picks.append(sel)
        onehots.append(oh)
        work = jnp.where(oh, -jnp.inf, work)
    exps = [jnp.exp(v - vals[0]) for v in vals]
    den = exps[0] + exps[1] + exps[2] + exps[3]
    mask = jnp.zeros((rows, N_EXPERTS), F32)
    for oh in onehots:
        mask = mask + jnp.where(oh, 1.0, 0.0)
    before = _dot(tri_ref[...], mask.astype(BF16))
    counts = jnp.sum(mask, axis=0, keepdims=True)
    k4 = lax.broadcasted_iota(jnp.int32, (rows, TOP_K), 1)
    w4 = jnp.zeros((rows, TOP_K), F32)
    pos4 = jnp.zeros((rows, TOP_K), jnp.int32)
    for k in range(TOP_K):
        lower_experts = jnp.sum(jnp.where(eidx < picks[k], counts, 0.0), axis=-1, keepdims=True)
        rank = jnp.sum(jnp.where(onehots[k], before, 0.0), axis=-1, keepdims=True)
        w4 = jnp.where(k4 == k, exps[k] / den, w4)
        pos4 = jnp.where(k4 == k, (lower_experts + rank).astype(jnp.int32), pos4)
    return w4, pos4, counts


def _pair_selector(pos4, values=None):
    rows = pos4.shape[0]
    col = lax.broadcasted_iota(jnp.int32, (rows, TOP_K * rows), 1)
    sel = jnp.zeros((rows, TOP_K * rows), F32)
    for k in range(TOP_K):
        sel = sel + jnp.where(col == pos4[:, k:k + 1], 1.0 if values is None else values[:, k:k + 1], 0.0)
    return sel


def _route_block(h2, wr_ref, br_ref, tri_ref, gw_ref, pos_ref, cnt_ref):
    w4, pos4, counts = _route(h2, wr_ref, br_ref, tri_ref)
    gw_ref[...] = w4
    pos_ref[...] = pos4
    cnt_ref[0] = counts
    return pos4


def _dispatch_block(pos4, h2, xc_ref):
    select = _pair_selector(pos4).astype(BF16)
    pairs = lax.dot_general(select, h2, (((0,), (0,)), ((), ())), preferred_element_type=F32)
    _store_token_major(xc_ref, pairs)


def _route_and_dispatch(h2, wr_ref, br_ref, tri_ref, gw_ref, pos_ref, cnt_ref, xc_ref):
    _dispatch_block(_route_block(h2, wr_ref, br_ref, tri_ref, gw_ref, pos_ref, cnt_ref), h2, xc_ref)


def _softmax_rows(parts, sink):
    m = sink
    for l in parts:
        m = jnp.maximum(m, jnp.max(l, axis=-1, keepdims=True))
    es = [jnp.exp(l - m) for l in parts]
    den = jnp.exp(sink - m)
    for e in es:
        den = den + jnp.sum(e, axis=-1, keepdims=True)
    return [e / den for e in es]


def _mixer_prompt_body(sinks_ref, x_ref, w_in_ref, gmix_ref, qg_ref, kg_ref, sgug_ref, seg64_ref, seg128_ref,
                       bias_ref, wsp_ref, bsp_ref, wba_ref, wbs_ref, wout_ref, gffn_ref, wr_ref, br_ref, tri_ref,
                       x1_ref, xc_ref, gw_ref, pos_ref, cnt_ref, kwin_ref, vwin_ref, sguv_ref,
                       kcarry, vcarry, h2_prev_ref, *, blocks_per_seq, n_blocks):
    i = pl.program_id(0)

    @pl.when(i == 0)
    def _():
        h2_prev_ref[...] = jnp.zeros_like(h2_prev_ref)

    def route_previous():
        return _route_block(h2_prev_ref[...], wr_ref, br_ref, tri_ref, gw_ref, pos_ref, cnt_ref)

    def dispatch_previous(pos4):
        _dispatch_block(pos4, h2_prev_ref[...], xc_ref)

    @pl.when(i == n_blocks)
    def _():
        dispatch_previous(route_previous())

    @pl.when(i == n_blocks + 1)
    def _():
        xc_ref[...] = jnp.zeros_like(xc_ref)
        gw_ref[...] = jnp.zeros_like(gw_ref)
        pos_ref[...] = jnp.zeros_like(pos_ref)
        cnt_ref[...] = jnp.zeros_like(cnt_ref)

    @pl.when(jnp.logical_and(i < n_blocks, i % blocks_per_seq == 0))
    def _():
        kcarry[...] = jnp.zeros_like(kcarry)
        vcarry[...] = jnp.zeros_like(vcarry)

    @pl.when(i < n_blocks)
    def _():
        _mixer_prompt_block(i % blocks_per_seq, sinks_ref, x_ref, w_in_ref, gmix_ref, qg_ref, kg_ref, sgug_ref,
                            seg64_ref, seg128_ref, bias_ref, wsp_ref, bsp_ref, wba_ref, wbs_ref, wout_ref, gffn_ref,
                            x1_ref, kwin_ref, vwin_ref, sguv_ref, kcarry, vcarry, h2_prev_ref,
                            route_previous, dispatch_previous)


def _mixer_prompt_block(j, sinks_ref, x_ref, w_in_ref, gmix_ref, qg_ref, kg_ref, sgug_ref, seg64_ref, seg128_ref,
                        bias_ref, wsp_ref, bsp_ref, wba_ref, wbs_ref, wout_ref, gffn_ref,
                        x1_ref, kwin_ref, vwin_ref, sguv_ref, kcarry, vcarry, h2_prev_ref,
                        route_previous, dispatch_previous):
    routed = []
    x = x_ref[0]
    h = _rms_rows(x, gmix_ref[...]).astype(BF16)
    qn, kn, v, u, vm, ga, gb = _project(h, w_in_ref, qg_ref[...], kg_ref[...], sgug_ref[...],
                                        seg64_ref[...], seg128_ref[...],
                                        between=lambda: routed.append(route_previous()))
    k_ext = jnp.concatenate([kcarry[...], kn.astype(BF16)], axis=0)
    v_ext = jnp.concatenate([vcarry[...], v.astype(BF16)], axis=0)
    kcarry[...] = k_ext[TOKEN_BLOCK:]
    vcarry[...] = v_ext[TOKEN_BLOCK:]

    tri_mask = (lax.broadcasted_iota(jnp.int32, (MLP_CHUNK, MLP_CHUNK), 0)
                >= lax.broadcasted_iota(jnp.int32, (MLP_CHUNK, MLP_CHUNK), 1))
    att_rows, sgu_rows = [], []
    for pm in range(TOKEN_BLOCK // PAIR):
        r0 = pm * PAIR
        q_stack = _stack_heads(qn[r0:r0 + PAIR])
        k_band = k_ext[r0:r0 + BAND]
        v_band = v_ext[r0:r0 + BAND]
        s = _dot_nt(q_stack, k_band)
        first = jnp.where(j == 0, 0, 1) if pm == 0 else 1
        probs = []
        for head in range(N_HEADS):
            logit = s[head * PAIR:(head + 1) * PAIR] * ATTN_SCALE + bias_ref[first, head]
            probs.append(_softmax_rows([logit], sinks_ref[head])[0].astype(BF16))
        o = _dot(jnp.concatenate(probs, axis=0), v_band)
        att_rows.append(_unstack_heads(o, PAIR))
        cols = []
        for g in range(MLP_GROUPS):
            wm = jnp.where(tri_mask, wsp_ref[g], 0.0).astype(BF16)
            cols.append(_dot(wm, vm[r0:r0 + PAIR, g * LANES:(g + 1) * LANES].astype(BF16)))
        mixed = jnp.concatenate(cols, axis=-1) + bsp_ref[...]
        sgu_rows.append(u[r0:r0 + PAIR] * mixed)
        if pm == 0:
            dispatch_previous(routed[0])
    att = jnp.concatenate(att_rows, axis=0)
    sgu = jnp.concatenate(sgu_rows, axis=0)

    x1 = x + _merge(att, sgu, ga, gb, wba_ref, wbs_ref, wout_ref)
    h2 = _rms_rows(x1, gffn_ref[...])
    h2_prev_ref[...] = h2.astype(BF16)

    x1_ref[0] = x1
    kwin_ref[0] = kn[TOKEN_BLOCK - WINDOW:]
    vwin_ref[0] = v[TOKEN_BLOCK - WINDOW:]
    for g in range(MLP_GROUPS):
        sguv_ref[0, :, g, :] = vm[TOKEN_BLOCK - MLP_CHUNK:, g * LANES:(g + 1) * LANES]


def _mixer_sample_body(sinks_ref, x_ref, ck_ref, cv_ref, w_in_ref, gmix_ref, qg_ref, kg_ref, sgug_ref, seg64_ref,
                       seg128_ref, biasc_ref, biasn_ref, wsp_ref, bsp_ref, wba_ref, wbs_ref, wout_ref, gffn_ref,
                       wr_ref, br_ref, tri_ref, xc_in_ref, gw_in_ref, pos_in_ref, cnt_in_ref,
                       x1_ref, xc_ref, gw_ref, pos_ref, cnt_ref, kwin_ref, vwin_ref, sguv_ref, *, n_streams, n_new):
    del xc_in_ref, gw_in_ref, pos_in_ref, cnt_in_ref
    x = x_ref[...]
    h = _rms_rows(x, gmix_ref[...]).astype(BF16)
    qn, kn, v, u, vm, ga, gb = _project(h, w_in_ref, qg_ref[...], kg_ref[...], sgug_ref[...],
                                        seg64_ref[...], seg128_ref[...])
    n_cached = ck_ref.shape[1]
    att_rows = []
    for s_i in range(n_streams):
        r0 = s_i * n_new
        q_stack = _stack_heads(qn[r0:r0 + n_new])
        k_new = kn[r0:r0 + n_new]
        v_new = v[r0:r0 + n_new]
        s_c = _dot_nt(q_stack, ck_ref[s_i].astype(BF16))
        s_n = _dot_nt(q_stack, k_new.astype(BF16))
        pc, pn = [], []
        for head in range(N_HEADS):
            rows = slice(head * n_new, (head + 1) * n_new)
            lc = s_c[rows] * ATTN_SCALE + biasc_ref[rows]
            ln = s_n[rows] * ATTN_SCALE + biasn_ref[rows]
            p_c, p_n = _softmax_rows([lc, ln], sinks_ref[head])
            pc.append(p_c.astype(BF16))
            pn.append(p_n.astype(BF16))
        o = (_dot(jnp.concatenate(pc, axis=0), cv_ref[s_i].astype(BF16))
             + _dot(jnp.concatenate(pn, axis=0), v_new.astype(BF16)))
        att_rows.append(_unstack_heads(o, n_new))
        kwin_ref[s_i, 0:n_cached - n_new] = ck_ref[s_i, n_new:n_cached]
        kwin_ref[s_i, n_cached - n_new:n_cached] = k_new
        vwin_ref[s_i, 0:n_cached - n_new] = cv_ref[s_i, n_new:n_cached]
        vwin_ref[s_i, n_cached - n_new:n_cached] = v_new
    att = jnp.concatenate(att_rows, axis=0)

    rows = n_streams * n_new
    ri = lax.broadcasted_iota(jnp.int32, (rows, rows), 0)
    ci = lax.broadcasted_iota(jnp.int32, (rows, rows), 1)
    keep = jnp.logical_and(ri // n_new == ci // n_new, ri % n_new >= ci % n_new)
    expand = (lax.broadcasted_iota(jnp.int32, (rows, n_new), 0) % n_new
              == lax.broadcasted_iota(jnp.int32, (rows, n_new), 1)).astype(BF16)
    cols = []
    for g in range(MLP_GROUPS):
        tiled = _dot_nt(_dot(expand, wsp_ref[g].astype(BF16)).astype(BF16), expand)
        wm = jnp.where(keep, tiled, 0.0).astype(BF16)
        cols.append(_dot(wm, vm[:, g * LANES:(g + 1) * LANES].astype(BF16)))
    sgu = u * (jnp.concatenate(cols, axis=-1) + bsp_ref[...])

    x1 = x + _merge(att, sgu, ga, gb, wba_ref, wbs_ref, wout_ref)
    h2 = _rms_rows(x1, gffn_ref[...])
    _route_and_dispatch(h2.astype(BF16), wr_ref, br_ref, tri_ref, gw_ref, pos_ref, cnt_ref, xc_ref)
    x1_ref[...] = x1
    for g in range(MLP_GROUPS):
        sguv_ref[:, g, :] = vm[:, g * LANES:(g + 1) * LANES]


def _piece_copies(i_tile, lo_ref, src_ref, off_ref, len_ref, hbm, buf, sem, to_hbm):
    def one(p, carry):
        n = pl.multiple_of(len_ref[p], ROW_TILE)

        @pl.when(n > 0)
        def _():
            far = hbm.at[pl.ds(pl.multiple_of(src_ref[p], ROW_TILE), n), :]
            near = buf.at[pl.ds(pl.multiple_of(off_ref[p], ROW_TILE), n), :]
            if to_hbm:
                pltpu.make_async_copy(near, far, sem).start()
            else:
                pltpu.make_async_copy(far, near, sem).start()
        return carry
    lax.fori_loop(lo_ref[i_tile], lo_ref[i_tile + 1], one, 0)


def _wait_rows(hbm, buf, sem, n_rows):
    n = pl.multiple_of(n_rows, ROW_TILE)
    pltpu.make_async_copy(hbm.at[pl.ds(0, n), :], buf.at[pl.ds(0, n), :], sem).wait()


def _expert_mlp(x_cur, y_cur, bgu_ref, bd_ref, wgu_bf, wd_bf):
    xb = _load_token_major(x_cur, EXPERT_TILE).astype(BF16)
    hgu = _dot(xb, wgu_bf[...]) + bgu_ref[0]
    glu = jnp.minimum(hgu[:, :D_EXPERT], SWIGLU_LIMIT)
    lin = jnp.clip(hgu[:, D_EXPERT:], -SWIGLU_LIMIT, SWIGLU_LIMIT)
    act = glu * jax.nn.sigmoid(glu * SWIGLU_ALPHA) * (lin + 1.0)
    _store_token_major(y_cur, _dot(act.astype(BF16), wd_bf[...]) + bd_ref[0])


def _weight_copies(wgu_hbm, wd_hbm, wgu_stage, wd_stage, sem_w, expert):
    return (pltpu.make_async_copy(wgu_hbm.at[expert], wgu_stage, sem_w.at[0]),
            pltpu.make_async_copy(wd_hbm.at[expert], wd_stage, sem_w.at[1]))


def _expert_body(te_ref, nxt_ref, nused_ref, valid_ref, lo_ref, src_ref, off_ref, len_ref, xc_hbm, wgu_hbm, bgu_ref,
                 wd_hbm, bd_ref, yc_hbm, xbuf0, xbuf1, ybuf0, ybuf1, wgu_stage, wd_stage, wgu_bf, wd_bf,
                 sem_in, sem_out, sem_w):
    i = pl.program_id(0)
    n_used = nused_ref[0]
    xbufs, ybufs = (xbuf0, xbuf1), (ybuf0, ybuf1)
    pieces = (lo_ref, src_ref, off_ref, len_ref)

    @pl.when(i == 0)
    def _():
        xbuf0[...] = jnp.zeros_like(xbuf0)
        xbuf1[...] = jnp.zeros_like(xbuf1)
        for cp in _weight_copies(wgu_hbm, wd_hbm, wgu_stage, wd_stage, sem_w, te_ref[0]):
            cp.start(priority=1)
        _piece_copies(0, *pieces, xc_hbm, xbuf0, sem_in.at[0], to_hbm=False)

    @pl.when(jnp.logical_and(i < n_used, jnp.logical_or(i == 0, te_ref[i] != te_ref[jnp.maximum(i - 1, 0)])))
    def _():
        for cp in _weight_copies(wgu_hbm, wd_hbm, wgu_stage, wd_stage, sem_w, te_ref[i]):
            cp.wait()
        wgu_bf[...] = wgu_stage[...].astype(BF16)
        wd_bf[...] = wd_stage[...].astype(BF16)

        @pl.when(nxt_ref[i] >= 0)
        def _():
            for cp in _weight_copies(wgu_hbm, wd_hbm, wgu_stage, wd_stage, sem_w, nxt_ref[i]):
                cp.start(priority=1)

    for par in range(2):
        x_cur, x_next, y_cur, y_prev = xbufs[par], xbufs[1 - par], ybufs[par], ybufs[1 - par]

        @pl.when(jnp.logical_and(i < n_used, i % 2 == par))
        def _(x_cur=x_cur, x_next=x_next, y_cur=y_cur, y_prev=y_prev, par=par):
            _wait_rows(xc_hbm, x_cur, sem_in.at[par], valid_ref[i])

            @pl.when(i + 1 < n_used)
            def _():
                _piece_copies(i + 1, *pieces, xc_hbm, x_next, sem_in.at[1 - par], to_hbm=False)

            @pl.when(i > 0)
            def _():
                _piece_copies(i - 1, *pieces, yc_hbm, y_prev, sem_out, to_hbm=True)

            _expert_mlp(x_cur, y_cur, bgu_ref, bd_ref, wgu_bf, wd_bf)

            @pl.when(i > 0)
            def _():
                _wait_rows(yc_hbm, y_prev, sem_out, valid_ref[i - 1])

        @pl.when(jnp.logical_and(i == n_used, i % 2 == par))
        def _(y_prev=y_prev):
            _piece_copies(i - 1, *pieces, yc_hbm, y_prev, sem_out, to_hbm=True)
            _wait_rows(yc_hbm, y_prev, sem_out, valid_ref[i - 1])


def _combine_body(x1_ref, gw_ref, pos_ref, yc_ref, out_ref):
    rows = x1_ref.shape[0]
    y = _load_token_major(yc_ref, TOP_K * rows)
    w = _pair_selector(pos_ref[...], gw_ref[...])
    w_hi = w.astype(BF16)
    w_lo = (w - w_hi.astype(F32)).astype(BF16)
    y_hi = y.astype(BF16)
    y_lo = (y - y_hi.astype(F32)).astype(BF16)
    out_ref[...] = x1_ref[...] + (_dot(w_hi, y_hi) + _dot(w_lo, y_hi) + _dot(w_hi, y_lo))


def _const_spec(shape):
    nd = len(shape)
    return pl.BlockSpec(shape, lambda *_: (0,) * nd, pipeline_mode=pl.Buffered(1))


def _q_perm():
    cols = np.arange(ATTN_WIDTH)
    j, half, d = cols // LANES, (cols % LANES) // HEAD_DIM, cols % HEAD_DIM
    return (j + Q_PER_KV * half) * HEAD_DIM + d


def _alibi_slopes():
    return 2.0 ** (-8.0 * np.arange(1, N_HEADS + 1) / N_HEADS)


def _prompt_bias():
    qi = np.arange(PAIR)[:, None]
    kj = np.arange(BAND)[None, :]
    dist = np.abs(qi + WINDOW - kj).astype(np.float64)
    cq, ck = qi // CHUNK, kj // CHUNK
    in_band = (ck >= cq) & (ck <= cq + WINDOW // CHUNK)
    base = -_alibi_slopes()[:, None, None] * dist[None]
    later = np.where(in_band[None], base, NEG_INF)
    first = np.where((kj >= WINDOW)[None], later, NEG_INF)
    return np.stack([first, later]).astype(np.float32)


def _sample_bias(n_new, n_cached):
    qi = np.arange(n_new)[:, None]
    dc = np.abs(qi + n_cached - np.arange(n_cached)[None, :]).astype(np.float64)
    dn = np.abs(qi - np.arange(n_new)[None, :]).astype(np.float64)
    sl = _alibi_slopes()[:, None, None]
    bc = (-sl * dc[None]).reshape(N_HEADS * n_new, n_cached)
    bn = (-sl * dn[None]).reshape(N_HEADS * n_new, n_new)
    return bc.astype(np.float32), bn.astype(np.float32)


def kernel(x_prompt, x_sample, cache_k_win, cache_v_win, g_mix, w_in, q_norm_g, k_norm_g, attn_sinks, sgu_norm_g,
           w_spatial, b_spatial, w_branch_attn, w_branch_sgu, w_out, g_ffn, w_router, b_router, w_gate_up,
           b_gate_up, w_down, b_down):
    n_b, seq, _ = x_prompt.shape
    n_streams, n_new, _ = x_sample.shape
    n_cached = cache_k_win.shape[2]
    n_prompt = n_b * seq
    n_sample = n_streams * n_new
    n_tok = n_prompt + n_sample
    assert seq % TOKEN_BLOCK == 0 and n_sample == TOKEN_BLOCK and n_cached == WINDOW
    blocks_per_seq = seq // TOKEN_BLOCK
    n_prompt_blocks = n_prompt // TOKEN_BLOCK

    perm = _q_perm()
    w_in_l = w_in[0]
    w_in_b = jnp.concatenate([w_in_l[:, perm], w_in_l[:, Q_END:]], axis=1).astype(BF16)
    wba_b = w_branch_attn[0][perm, :].astype(BF16)
    wbs_b = w_branch_sgu[0].astype(BF16)
    wout_b = w_out[0].astype(BF16)
    wr_b = w_router[0].astype(BF16)
    br = b_router[0].reshape(1, N_EXPERTS).astype(F32)
    gmix = g_mix[0].reshape(1, D_MODEL)
    gffn = g_ffn[0].reshape(1, D_MODEL)
    qg = jnp.tile(q_norm_g[0], LANES // HEAD_DIM).reshape(1, LANES)
    kg = jnp.tile(k_norm_g[0], LANES // HEAD_DIM).reshape(1, LANES)
    sgug = sgu_norm_g[0].reshape(1, MLP_WIDTH)
    sinks = attn_sinks[0].astype(F32)
    lane_seg = np.arange(LANES) // HEAD_DIM
    seg64 = jnp.asarray(np.tile((lane_seg[:, None] == lane_seg[None, :]) / HEAD_DIM, (2, 1)), BF16)
    seg128 = jnp.full((2 * LANES, LANES), 1.0 / MLP_GROUP_DIM, BF16)
    tri = jnp.asarray(np.tril(np.ones((TOKEN_BLOCK, TOKEN_BLOCK)), -1), BF16)
    wsp = w_spatial[0]
    bsp_p = jnp.repeat(b_spatial[0].T, MLP_GROUP_DIM, axis=1)
    wsp_s = wsp[:, :n_new, :n_new]
    bsp_s = jnp.tile(jnp.repeat(b_spatial[0][:, :n_new].T, MLP_GROUP_DIM, axis=1), (n_streams, 1))
    bias_p = jnp.asarray(_prompt_bias())
    bias_c, bias_n = (jnp.asarray(a) for a in _sample_bias(n_new, n_cached))

    smem = pl.BlockSpec(memory_space=pltpu.SMEM)
    tb = TOKEN_BLOCK
    pb = TOP_K * tb
    n_tok_blocks = n_tok // tb
    last_blk = n_prompt_blocks - 1
    seq_of = lambda i: jnp.minimum(i, last_blk) // blocks_per_seq
    x_spec = pl.BlockSpec((1, tb, D_MODEL), lambda i: (seq_of(i), jnp.minimum(i, last_blk) % blocks_per_seq, 0))
    routed_blk = lambda i: jnp.where(i > n_prompt_blocks, n_prompt_blocks, jnp.clip(i - 1, 0, last_blk))
    win_spec = lambda width: pl.BlockSpec((1, WINDOW, width), lambda i: (seq_of(i), 0, 0))
    route_specs = lambda blk: [pl.BlockSpec((pb * ROW_TILE, LANES), lambda i: (blk(i), 0)),
                               pl.BlockSpec((tb, TOP_K), lambda i: (blk(i), 0)),
                               pl.BlockSpec((tb, TOP_K), lambda i: (blk(i), 0)),
                               pl.BlockSpec((1, 1, N_EXPERTS), lambda i: (blk(i), 0, 0))]
    route_shapes = [jax.ShapeDtypeStruct((n_tok * TOP_K * ROW_TILE, LANES), F32),
                    jax.ShapeDtypeStruct((n_tok, TOP_K), F32),
                    jax.ShapeDtypeStruct((n_tok, TOP_K), jnp.int32),
                    jax.ShapeDtypeStruct((n_tok_blocks, 1, N_EXPERTS), F32)]

    x1p, xc, gw_p, pos_p, cnt_p, kwin_p, vwin_p, sguv_p = pl.pallas_call(
        functools.partial(_mixer_prompt_body, blocks_per_seq=blocks_per_seq, n_blocks=n_prompt_blocks),
        grid=(n_prompt_blocks + 2,),
        in_specs=[smem, x_spec,
                  _const_spec((D_MODEL, IN_COLS)), _const_spec((1, D_MODEL)), _const_spec((1, LANES)),
                  _const_spec((1, LANES)), _const_spec((1, MLP_WIDTH)), _const_spec((2 * LANES, LANES)),
                  _const_spec((2 * LANES, LANES)), _const_spec((2, N_HEADS, PAIR, BAND)),
                  _const_spec((MLP_GROUPS, MLP_CHUNK, MLP_CHUNK)), _const_spec((MLP_CHUNK, MLP_WIDTH)),
                  _const_spec((ATTN_WIDTH, D_MODEL)), _const_spec((MLP_WIDTH, D_MODEL)),
                  _const_spec((D_MODEL, D_MODEL)), _const_spec((1, D_MODEL)), _const_spec((D_MODEL, N_EXPERTS)),
                  _const_spec((1, N_EXPERTS)), _const_spec((tb, tb))],
        out_specs=[x_spec, *route_specs(routed_blk),
                   win_spec(KV_WIDTH), win_spec(KV_WIDTH),
                   pl.BlockSpec((1, MLP_CHUNK, MLP_GROUPS, MLP_GROUP_DIM), lambda i: (seq_of(i), 0, 0, 0))],
        out_shape=[jax.ShapeDtypeStruct((n_b, seq, D_MODEL), F32), *route_shapes,
                   jax.ShapeDtypeStruct((n_b, WINDOW, KV_WIDTH), F32),
                   jax.ShapeDtypeStruct((n_b, WINDOW, KV_WIDTH), F32),
                   jax.ShapeDtypeStruct((n_b, MLP_CHUNK, MLP_GROUPS, MLP_GROUP_DIM), F32)],
        scratch_shapes=[pltpu.VMEM((WINDOW, KV_WIDTH), BF16), pltpu.VMEM((WINDOW, KV_WIDTH), BF16),
                        pltpu.VMEM((tb, D_MODEL), BF16)],
        compiler_params=pltpu.CompilerParams(dimension_semantics=("arbitrary",), vmem_limit_bytes=VMEM_LIMIT),
        name="mixer_prompt",
    )(sinks, x_prompt, w_in_b, gmix, qg, kg, sgug, seg64, seg128, bias_p, wsp, bsp_p, wba_b, wbs_b, wout_b,
      gffn, wr_b, br, tri)

    full = lambda shape: pl.BlockSpec(shape, lambda i: (0,) * len(shape))
    any_spec = pl.BlockSpec(memory_space=pl.ANY)
    ck = cache_k_win[0].reshape(n_streams, n_cached, KV_WIDTH)
    cv = cache_v_win[0].reshape(n_streams, n_cached, KV_WIDTH)
    x1s, xc, gw_a, pos_a, cnt_a, kwin_s, vwin_s, sguv_s = pl.pallas_call(
        functools.partial(_mixer_sample_body, n_streams=n_streams, n_new=n_new),
        grid=(1,),
        in_specs=[smem, full((n_sample, D_MODEL)), full((n_streams, n_cached, KV_WIDTH)),
                  full((n_streams, n_cached, KV_WIDTH)),
                  full((D_MODEL, IN_COLS)), full((1, D_MODEL)), full((1, LANES)), full((1, LANES)),
                  full((1, MLP_WIDTH)), full((2 * LANES, LANES)), full((2 * LANES, LANES)),
                  full((N_HEADS * n_new, n_cached)), full((N_HEADS * n_new, n_new)),
                  full((MLP_GROUPS, n_new, n_new)), full((n_sample, MLP_WIDTH)),
                  full((ATTN_WIDTH, D_MODEL)), full((MLP_WIDTH, D_MODEL)), full((D_MODEL, D_MODEL)),
                  full((1, D_MODEL)), full((D_MODEL, N_EXPERTS)), full((1, N_EXPERTS)), full((tb, tb)),
                  any_spec, any_spec, any_spec, any_spec],
        out_specs=[full((n_sample, D_MODEL)), *route_specs(lambda i: n_prompt_blocks),
                   full((n_streams, n_cached, KV_WIDTH)), full((n_streams, n_cached, KV_WIDTH)),
                   full((n_sample, MLP_GROUPS, MLP_GROUP_DIM))],
        out_shape=[jax.ShapeDtypeStruct((n_sample, D_MODEL), F32), *route_shapes,
                   jax.ShapeDtypeStruct((n_streams, n_cached, KV_WIDTH), F32),
                   jax.ShapeDtypeStruct((n_streams, n_cached, KV_WIDTH), F32),
                   jax.ShapeDtypeStruct((n_sample, MLP_GROUPS, MLP_GROUP_DIM), F32)],
        input_output_aliases={22: 1, 23: 2, 24: 3, 25: 4},
        compiler_params=pltpu.CompilerParams(dimension_semantics=("arbitrary",), vmem_limit_bytes=VMEM_LIMIT),
        name="mixer_sample",
    )(sinks, x_sample.reshape(n_sample, D_MODEL), ck, cv, w_in_b, gmix, qg, kg, sgug, seg64, seg128, bias_c,
      bias_n, wsp_s, bsp_s, wba_b, wbs_b, wout_b, gffn, wr_b, br, tri, xc, gw_p, pos_p, cnt_p)

    tm = EXPERT_TILE
    counts = cnt_a[:, 0, :].astype(jnp.int32)
    cnt_all = jnp.sum(counts, axis=0)
    padded = ((cnt_all + tm - 1) // tm) * tm
    pad_end = jnp.cumsum(padded)
    pad_off = pad_end - padded
    n_pairs = TOP_K * n_tok
    n_tiles = n_pairs // tm + N_EXPERTS
    n_steps = n_tiles + 1
    in_block = jnp.cumsum(counts, axis=1) - counts
    in_expert = jnp.cumsum(counts, axis=0) - counts
    run_src = (jnp.arange(n_tok_blocks, dtype=jnp.int32)[:, None] * pb + in_block).T.reshape(-1)
    run_start = (pad_off[None, :] + in_expert).T.reshape(-1)
    run_len = counts.T.reshape(-1)
    head = jnp.minimum(run_len, tm - run_start % tm)
    piece_start = jnp.stack([run_start, run_start + head], axis=1).reshape(-1)
    piece_src = jnp.stack([run_src, run_src + head], axis=1).reshape(-1)
    piece_len = jnp.stack([head, run_len - head], axis=1).reshape(-1)
    piece_tile = piece_start // tm
    tile_ids = jnp.arange(n_steps + 1, dtype=jnp.int32)
    piece_lo = jnp.sum((piece_tile[None, :] < tile_ids[:, None]).astype(jnp.int32), axis=1)
    n_used = (pad_end[-1] // tm).astype(jnp.int32)
    tile_start = jnp.minimum(tile_ids[:n_steps], n_used - 1) * tm
    tile_expert = jnp.sum((pad_end[None, :] <= tile_start[:, None]).astype(jnp.int32), axis=1)
    tile_expert = jnp.minimum(tile_expert, N_EXPERTS - 1)
    of_tile = tile_expert[:, None] == jnp.arange(N_EXPERTS, dtype=jnp.int32)[None, :]
    rows_end = jnp.sum(jnp.where(of_tile, (pad_off + cnt_all)[None, :], 0), axis=1)
    run_end = jnp.sum(jnp.where(of_tile, pad_end[None, :], 0), axis=1)
    tile_valid = jnp.clip(rows_end - tile_start, 0, tm)
    next_expert = jnp.sum((pad_end[None, :] <= run_end[:, None]).astype(jnp.int32), axis=1)
    next_expert = jnp.where(run_end < pad_end[-1], jnp.minimum(next_expert, N_EXPERTS - 1), -1)

    prefetch = (tile_expert, next_expert, n_used.reshape(1), tile_valid * ROW_TILE, piece_lo,
                piece_src * ROW_TILE, (piece_start % tm) * ROW_TILE, piece_len * ROW_TILE)
    yc = pl.pallas_call(
        _expert_body,
        grid_spec=pltpu.PrefetchScalarGridSpec(
            num_scalar_prefetch=len(prefetch),
            grid=(n_steps,),
            in_specs=[any_spec, any_spec,
                      pl.BlockSpec((1, 1, 2 * D_EXPERT), lambda i, te, *_: (te[i], 0, 0)),
                      any_spec,
                      pl.BlockSpec((1, 1, D_MODEL), lambda i, te, *_: (te[i], 0, 0))],
            out_specs=any_spec,
            scratch_shapes=[pltpu.VMEM((tm * ROW_TILE, LANES), F32), pltpu.VMEM((tm * ROW_TILE, LANES), F32),
                            pltpu.VMEM((tm * ROW_TILE, LANES), F32), pltpu.VMEM((tm * ROW_TILE, LANES), F32),
                            pltpu.VMEM((D_MODEL, 2 * D_EXPERT), F32), pltpu.VMEM((D_EXPERT, D_MODEL), F32),
                            pltpu.VMEM((D_MODEL, 2 * D_EXPERT), BF16), pltpu.VMEM((D_EXPERT, D_MODEL), BF16),
                            pltpu.SemaphoreType.DMA((2,)), pltpu.SemaphoreType.DMA(()),
                            pltpu.SemaphoreType.DMA((2,))]),
        out_shape=jax.ShapeDtypeStruct((n_pairs * ROW_TILE, LANES), F32),
        compiler_params=pltpu.CompilerParams(dimension_semantics=("arbitrary",), vmem_limit_bytes=VMEM_LIMIT),
        name="moe_experts",
    )(*[p.astype(jnp.int32) for p in prefetch], xc, w_gate_up[0],
      b_gate_up[0].reshape(N_EXPERTS, 1, 2 * D_EXPERT), w_down[0], b_down[0].reshape(N_EXPERTS, 1, D_MODEL))

    def combine(x1, first_block, n_blocks):
        return pl.pallas_call(
            _combine_body,
            grid=(n_blocks,),
            in_specs=[pl.BlockSpec((tb, D_MODEL), lambda i: (i, 0)),
                      pl.BlockSpec((tb, TOP_K), lambda i: (first_block + i, 0)),
                      pl.BlockSpec((tb, TOP_K), lambda i: (first_block + i, 0)),
                      pl.BlockSpec((pb * ROW_TILE, LANES), lambda i: (first_block + i, 0))],
            out_specs=pl.BlockSpec((tb, D_MODEL), lambda i: (i, 0)),
            out_shape=jax.ShapeDtypeStruct(x1.shape, F32),
            compiler_params=pltpu.CompilerParams(dimension_semantics=("arbitrary",), vmem_limit_bytes=VMEM_LIMIT),
            name="moe_combine",
        )(x1, gw_a, pos_a, yc)

    y_prompt = combine(x1p.reshape(n_prompt, D_MODEL), 0, n_prompt_blocks).reshape(n_b, seq, D_MODEL)
    y_sample = combine(x1s, n_prompt_blocks, 1).reshape(n_streams, n_new, D_MODEL)

    kv_shape = (N_KV_HEADS, HEAD_DIM)
    sg_shape = (MLP_GROUPS, MLP_GROUP_DIM)
    return (y_prompt, y_sample,
            kwin_p.reshape(1, n_b, WINDOW, *kv_shape), vwin_p.reshape(1, n_b, WINDOW, *kv_shape),
            kwin_s.reshape(1, n_streams, n_cached, *kv_shape), vwin_s.reshape(1, n_streams, n_cached, *kv_shape),
            sguv_p.reshape(1, n_b, MLP_CHUNK, *sg_shape), sguv_s.reshape(1, n_streams, n_new, *sg_shape))
```

```python
import functools

import jax
import jax.numpy as jnp
import numpy as np
from jax import lax
from jax.experimental import pallas as pl
from jax.experimental.pallas import tpu as pltpu

D_MODEL = 1024
CHUNK = 64
WINDOW = 128
HEAD_DIM = 64
N_HEADS = 8
N_KV_HEADS = 2
Q_PER_KV = N_HEADS // N_KV_HEADS
ATTN_WIDTH = N_HEADS * HEAD_DIM
KV_WIDTH = N_KV_HEADS * HEAD_DIM
ATTN_SCALE = HEAD_DIM ** -0.5
MLP_CHUNK = 128
MLP_GROUPS = 8
MLP_WIDTH = D_MODEL
MLP_GROUP_DIM = MLP_WIDTH // MLP_GROUPS
N_EXPERTS = 32
TOP_K = 4
D_EXPERT = D_MODEL
SWIGLU_ALPHA = 1.702
SWIGLU_LIMIT = 7.0
EPS = 1e-6
NEG_INF = -1e30
Q_END = ATTN_WIDTH
K_END = Q_END + KV_WIDTH
V_END = K_END + KV_WIDTH
U_END = V_END + MLP_WIDTH
VM_END = U_END + MLP_WIDTH
GA_END = VM_END + D_MODEL
IN_COLS = GA_END + D_MODEL

LANES = 128
ROW_TILE = D_MODEL // LANES
TOKEN_BLOCK = 256
PAIR = 2 * CHUNK
BAND = PAIR + WINDOW
EXPERT_TILE = 256
VMEM_LIMIT = 56 * 1024 * 1024

F32 = jnp.float32
BF16 = jnp.bfloat16


def _dot(a, b):
    return jnp.dot(a, b, preferred_element_type=F32)


def _dot_nt(a, b):
    return lax.dot_general(a, b, (((1,), (1,)), ((), ())), preferred_element_type=F32)


def _store_token_major(ref, val):
    rows = val.shape[0]
    for s in range(ROW_TILE):
        ref[pl.ds(s, rows, stride=ROW_TILE), :] = val[:, s * LANES:(s + 1) * LANES]


def _load_token_major(ref, rows):
    return jnp.concatenate([ref[pl.ds(s, rows, stride=ROW_TILE), :] for s in range(ROW_TILE)], axis=-1)


def _segment_mean(sq, seg):
    hi = sq.astype(BF16)
    lo = (sq - hi.astype(F32)).astype(BF16)
    return _dot(jnp.concatenate([hi, lo], axis=-1), seg)


def _rms_rows(x, gain):
    ms = jnp.mean(x * x, axis=-1, keepdims=True)
    return (x * lax.rsqrt(ms + EPS)) * gain


def _project(h, wq_ref, w_in_ref, qg, kg, sgug, seg64, seg128, between=None):
    qkv = jnp.concatenate([_dot(h, wq_ref[...]), _dot(h, w_in_ref[:, Q_END:V_END])], axis=-1)
    qk_cols = []
    for c in range(K_END // LANES):
        blk = qkv[:, c * LANES:(c + 1) * LANES]
        ms = _segment_mean(blk * blk, seg64)
        g = qg if c < Q_END // LANES else kg
        qk_cols.append((blk * lax.rsqrt(ms + EPS)) * g)
    qn = jnp.concatenate(qk_cols[:Q_END // LANES], axis=-1)
    kn = qk_cols[Q_END // LANES]
    v = qkv[:, K_END:V_END]
    u = jax.nn.gelu(_dot(h, w_in_ref[:, V_END:U_END]))
    vg = jax.nn.gelu(_dot(h, w_in_ref[:, U_END:VM_END]))
    vm_cols = []
    for g in range(MLP_GROUPS):
        blk = vg[:, g * LANES:(g + 1) * LANES]
        ms = _segment_mean(blk * blk, seg128)
        vm_cols.append((blk * lax.rsqrt(ms + EPS)) * sgug[:, g * LANES:(g + 1) * LANES])
    vm = jnp.concatenate(vm_cols, axis=-1)
    if between is not None:
        between()
    ga = jax.nn.sigmoid(_dot(h, w_in_ref[:, VM_END:GA_END]))
    gb = jax.nn.sigmoid(_dot(h, w_in_ref[:, GA_END:IN_COLS]))
    return qn, kn, v, u, vm, ga, gb


def _stack_heads(q_rows):
    lane = lax.broadcasted_iota(jnp.int32, (q_rows.shape[0], LANES), 1)
    blocks = []
    for head in range(N_HEADS):
        j, half = head % Q_PER_KV, head // Q_PER_KV
        col = q_rows[:, j * LANES:(j + 1) * LANES]
        keep = (lane < HEAD_DIM) if half == 0 else (lane >= HEAD_DIM)
        blocks.append(jnp.where(keep, col, 0.0))
    return jnp.concatenate(blocks, axis=0).astype(BF16)


def _unstack_heads(o, rows):
    lane = lax.broadcasted_iota(jnp.int32, (rows, LANES), 1)
    cols = []
    for j in range(Q_PER_KV):
        lo = o[j * rows:(j + 1) * rows]
        hi = o[(j + Q_PER_KV) * rows:(j + Q_PER_KV + 1) * rows]
        cols.append(jnp.where(lane < HEAD_DIM, lo, hi))
    return jnp.concatenate(cols, axis=-1)


def _merge(att, sgu, ga, gb, wba_ref, wbs_ref, wout_ref):
    m = ga * _dot(att.astype(BF16), wba_ref[...]) + gb * _dot(sgu.astype(BF16), wbs_ref[...])
    return _dot(m.astype(BF16), wout_ref[...])


def _route(h2, wr_ref, br_ref, tri_ref):
    rows = h2.shape[0]
    logits = _dot(h2, wr_ref[...]) + br_ref[...]
    eidx = lax.broadcasted_iota(jnp.int32, (rows, N_EXPERTS), 1).astype(F32)
    work = logits
    vals, picks, onehots = [], [], []
    for _ in range(TOP_K):
        m = jnp.max(work, axis=-1, keepdims=True)
        sel = jnp.min(jnp.where(work == m, eidx, float(N_EXPERTS)), axis=-1, keepdims=True)
        oh = eidx == sel
        vals.append(m)
        picks.append(sel)
        onehots.append(oh)
        work = jnp.where(oh, -jnp.inf, work)
    exps = [jnp.exp(v - vals[0]) for v in vals]
    den = exps[0] + exps[1] + exps[2] + exps[3]
    mask = jnp.zeros((rows, N_EXPERTS), F32)
    for oh in onehots:
        mask = mask + jnp.where(oh, 1.0, 0.0)
    before = _dot(tri_ref[...], mask.astype(BF16))
    counts = jnp.sum(mask, axis=0, keepdims=True)
    k4 = lax.broadcasted_iota(jnp.int32, (rows, TOP_K), 1)
    w4 = jnp.zeros((rows, TOP_K), F32)
    pos4 = jnp.zeros((rows, TOP_K), jnp.int32)
    for k in range(TOP_K):
        lower_experts = jnp.sum(jnp.where(eidx < picks[k], counts, 0.0), axis=-1, keepdims=True)
        rank = jnp.sum(jnp.where(onehots[k], before, 0.0), axis=-1, keepdims=True)
        w4 = jnp.where(k4 == k, exps[k] / den, w4)
        pos4 = jnp.where(k4 == k, (lower_experts + rank).astype(jnp.int32), pos4)
    return w4, pos4, counts


def _pair_selector(pos4, values=None):
    rows = pos4.shape[0]
    col = lax.broadcasted_iota(jnp.int32, (rows, TOP_K * rows), 1)
    sel = jnp.zeros((rows, TOP_K * rows), F32)
    for k in range(TOP_K):
        sel = sel + jnp.where(col == pos4[:, k:k + 1], 1.0 if values is None else values[:, k:k + 1], 0.0)
    return sel


def _route_block(h2, wr_ref, br_ref, tri_ref, gw_ref, pos_ref, cnt_ref):
    w4, pos4, counts = _route(h2, wr_ref, br_ref, tri_ref)
    gw_ref[...] = w4
    pos_ref[...] = pos4
    cnt_ref[0] = counts
    return pos4


def _dispatch_block(pos4, h2, xc_ref):
    select = _pair_selector(pos4).astype(BF16)
    pairs = lax.dot_general(select, h2, (((0,), (0,)), ((), ())), preferred_element_type=F32)
    _store_token_major(xc_ref, pairs)


def _route_and_dispatch(h2, wr_ref, br_ref, tri_ref, gw_ref, pos_ref, cnt_ref, xc_ref):
    _dispatch_block(_route_block(h2, wr_ref, br_ref, tri_ref, gw_ref, pos_ref, cnt_ref), h2, xc_ref)


def _softmax_rows(parts, sink):
    m = sink
    for l in parts:
        m = jnp.maximum(m, jnp.max(l, axis=-1, keepdims=True))
    es = [jnp.exp(l - m) for l in parts]
    den = jnp.exp(sink - m)
    for e in es:
        den = den + jnp.sum(e, axis=-1, keepdims=True)
    return [e / den for e in es]


def _mixer_prompt_body(sinks_ref, x_ref, wq_ref, w_in_ref, gmix_ref, qg_ref, kg_ref, sgug_ref, seg64_ref, seg128_ref,
                       bias_ref, wsp_ref, bsp_ref, wba_ref, wbs_ref, wout_ref, gffn_ref, wr_ref, br_ref, tri_ref,
                       x1_ref, xc_ref, gw_ref, pos_ref, cnt_ref, kwin_ref, vwin_ref, sguv_ref,
                       kcarry, vcarry, h2_prev_ref, *, blocks_per_seq, n_blocks):
    i = pl.program_id(0)

    @pl.when(i == 0)
    def _():
        h2_prev_ref[...] = jnp.zeros_like(h2_prev_ref)

    def route_previous():
        return _route_block(h2_prev_ref[...], wr_ref, br_ref, tri_ref, gw_ref, pos_ref, cnt_ref)

    def dispatch_previous(pos4):
        _dispatch_block(pos4, h2_prev_ref[...], xc_ref)

    @pl.when(i == n_blocks)
    def _():
        dispatch_previous(route_previous())

    @pl.when(i == n_blocks + 1)
    def _():
        xc_ref[...] = jnp.zeros_like(xc_ref)
        gw_ref[...] = jnp.zeros_like(gw_ref)
        pos_ref[...] = jnp.zeros_like(pos_ref)
        cnt_ref[...] = jnp.zeros_like(cnt_ref)

    @pl.when(jnp.logical_and(i < n_blocks, i % blocks_per_seq == 0))
    def _():
        kcarry[...] = jnp.zeros_like(kcarry)
        vcarry[...] = jnp.zeros_like(vcarry)

    @pl.when(i < n_blocks)
    def _():
        _mixer_prompt_block(i % blocks_per_seq, sinks_ref, x_ref, wq_ref, w_in_ref, gmix_ref, qg_ref, kg_ref, sgug_ref,
                            seg64_ref, seg128_ref, bias_ref, wsp_ref, bsp_ref, wba_ref, wbs_ref, wout_ref, gffn_ref,
                            x1_ref, kwin_ref, vwin_ref, sguv_ref, kcarry, vcarry, h2_prev_ref,
                            route_previous, dispatch_previous)


def _mixer_prompt_block(j, sinks_ref, x_ref, wq_ref, w_in_ref, gmix_ref, qg_ref, kg_ref, sgug_ref, seg64_ref, seg128_ref,
                        bias_ref, wsp_ref, bsp_ref, wba_ref, wbs_ref, wout_ref, gffn_ref,
                        x1_ref, kwin_ref, vwin_ref, sguv_ref, kcarry, vcarry, h2_prev_ref,
                        route_previous, dispatch_previous):
    routed = []
    x = x_ref[0]
    h = _rms_rows(x, gmix_ref[...]).astype(BF16)
    qn, kn, v, u, vm, ga, gb = _project(h, wq_ref, w_in_ref, qg_ref[...], kg_ref[...], sgug_ref[...],
                                        seg64_ref[...], seg128_ref[...],
                                        between=lambda: routed.append(route_previous()))
    k_ext = jnp.concatenate([kcarry[...], kn.astype(BF16)], axis=0)
    v_ext = jnp.concatenate([vcarry[...], v.astype(BF16)], axis=0)
    kcarry[...] = k_ext[TOKEN_BLOCK:]
    vcarry[...] = v_ext[TOKEN_BLOCK:]

    tri_mask = (lax.broadcasted_iota(jnp.int32, (MLP_CHUNK, MLP_CHUNK), 0)
                >= lax.broadcasted_iota(jnp.int32, (MLP_CHUNK, MLP_CHUNK), 1))
    att_rows, sgu_rows = [], []
    for pm in range(TOKEN_BLOCK // PAIR):
        r0 = pm * PAIR
        q_stack = _stack_heads(qn[r0:r0 + PAIR])
        k_band = k_ext[r0:r0 + BAND]
        v_band = v_ext[r0:r0 + BAND]
        s = _dot_nt(q_stack, k_band)
        first = jnp.where(j == 0, 0, 1) if pm == 0 else 1
        probs = []
        for head in range(N_HEADS):
            logit = s[head * PAIR:(head + 1) * PAIR] * ATTN_SCALE + bias_ref[first, head]
            probs.append(_softmax_rows([logit], sinks_ref[head])[0].astype(BF16))
        o = _dot(jnp.concatenate(probs, axis=0), v_band)
        att_rows.append(_unstack_heads(o, PAIR))
        cols = []
        for g in range(MLP_GROUPS):
            wm = jnp.where(tri_mask, wsp_ref[g], 0.0).astype(BF16)
            cols.append(_dot(wm, vm[r0:r0 + PAIR, g * LANES:(g + 1) * LANES].astype(BF16)))
        mixed = jnp.concatenate(cols, axis=-1) + bsp_ref[...]
        sgu_rows.append(u[r0:r0 + PAIR] * mixed)
        if pm == 0:
            dispatch_previous(routed[0])
    att = jnp.concatenate(att_rows, axis=0)
    sgu = jnp.concatenate(sgu_rows, axis=0)

    x1 = x + _merge(att, sgu, ga, gb, wba_ref, wbs_ref, wout_ref)
    h2 = _rms_rows(x1, gffn_ref[...])
    h2_prev_ref[...] = h2.astype(BF16)

    x1_ref[0] = x1
    kwin_ref[0] = kn[TOKEN_BLOCK - WINDOW:]
    vwin_ref[0] = v[TOKEN_BLOCK - WINDOW:]
    for g in range(MLP_GROUPS):
        sguv_ref[0, :, g, :] = vm[TOKEN_BLOCK - MLP_CHUNK:, g * LANES:(g + 1) * LANES]


def _mixer_sample_body(sinks_ref, x_ref, ck_ref, cv_ref, wq_ref, w_in_ref, gmix_ref, qg_ref, kg_ref, sgug_ref, seg64_ref,
                       seg128_ref, biasc_ref, biasn_ref, wsp_ref, bsp_ref, wba_ref, wbs_ref, wout_ref, gffn_ref,
                       wr_ref, br_ref, tri_ref, xc_in_ref, gw_in_ref, pos_in_ref, cnt_in_ref,
                       x1_ref, xc_ref, gw_ref, pos_ref, cnt_ref, kwin_ref, vwin_ref, sguv_ref, *, n_streams, n_new):
    del xc_in_ref, gw_in_ref, pos_in_ref, cnt_in_ref
    x = x_ref[...]
    h = _rms_rows(x, gmix_ref[...]).astype(BF16)
    qn, kn, v, u, vm, ga, gb = _project(h, wq_ref, w_in_ref, qg_ref[...], kg_ref[...], sgug_ref[...],
                                        seg64_ref[...], seg128_ref[...])
    n_cached = ck_ref.shape[1]
    att_rows = []
    for s_i in range(n_streams):
        r0 = s_i * n_new
        q_stack = _stack_heads(qn[r0:r0 + n_new])
        k_new = kn[r0:r0 + n_new]
        v_new = v[r0:r0 + n_new]
        s_c = _dot_nt(q_stack, ck_ref[s_i].astype(BF16))
        s_n = _dot_nt(q_stack, k_new.astype(BF16))
        pc, pn = [], []
        for head in range(N_HEADS):
            rows = slice(head * n_new, (head + 1) * n_new)
            lc = s_c[rows] * ATTN_SCALE + biasc_ref[rows]
            ln = s_n[rows] * ATTN_SCALE + biasn_ref[rows]
            p_c, p_n = _softmax_rows([lc, ln], sinks_ref[head])
            pc.append(p_c.astype(BF16))
            pn.append(p_n.astype(BF16))
        o = (_dot(jnp.concatenate(pc, axis=0), cv_ref[s_i].astype(BF16))
             + _dot(jnp.concatenate(pn, axis=0), v_new.astype(BF16)))
        att_rows.append(_unstack_heads(o, n_new))
        kwin_ref[s_i, 0:n_cached - n_new] = ck_ref[s_i, n_new:n_cached]
        kwin_ref[s_i, n_cached - n_new:n_cached] = k_new
        vwin_ref[s_i, 0:n_cached - n_new] = cv_ref[s_i, n_new:n_cached]
        vwin_ref[s_i, n_cached - n_new:n_cached] = v_new
    att = jnp.concatenate(att_rows, axis=0)

    rows = n_streams * n_new
    ri = lax.broadcasted_iota(jnp.int32, (rows, rows), 0)
    ci = lax.broadcasted_iota(jnp.int32, (rows, rows), 1)
    keep = jnp.logical_and(ri // n_new == ci // n_new, ri % n_new >= ci % n_new)
    expand = (lax.broadcasted_iota(jnp.int32, (rows, n_new), 0) % n_new
              == lax.broadcasted_iota(jnp.int32, (rows, n_new), 1)).astype(BF16)
    cols = []
    for g in range(MLP_GROUPS):
        tiled = _dot_nt(_dot(expand, wsp_ref[g].astype(BF16)).astype(BF16), expand)
        wm = jnp.where(keep, tiled, 0.0).astype(BF16)
        cols.append(_dot(wm, vm[:, g * LANES:(g + 1) * LANES].astype(BF16)))
    sgu = u * (jnp.concatenate(cols, axis=-1) + bsp_ref[...])

    x1 = x + _merge(att, sgu, ga, gb, wba_ref, wbs_ref, wout_ref)
    h2 = _rms_rows(x1, gffn_ref[...])
    _route_and_dispatch(h2.astype(BF16), wr_ref, br_ref, tri_ref, gw_ref, pos_ref, cnt_ref, xc_ref)
    x1_ref[...] = x1
    for g in range(MLP_GROUPS):
        sguv_ref[:, g, :] = vm[:, g * LANES:(g + 1) * LANES]


def _piece_copies(i_tile, lo_ref, src_ref, off_ref, len_ref, hbm, buf, sem, to_hbm):
    def one(p, carry):
        n = pl.multiple_of(len_ref[p], ROW_TILE)

        @pl.when(n > 0)
        def _():
            far = hbm.at[pl.ds(pl.multiple_of(src_ref[p], ROW_TILE), n), :]
            near = buf.at[pl.ds(pl.multiple_of(off_ref[p], ROW_TILE), n), :]
            if to_hbm:
                pltpu.make_async_copy(near, far, sem).start()
            else:
                pltpu.make_async_copy(far, near, sem).start()
        return carry
    lax.fori_loop(lo_ref[i_tile], lo_ref[i_tile + 1], one, 0)


def _wait_rows(hbm, buf, sem, n_rows):
    n = pl.multiple_of(n_rows, ROW_TILE)
    pltpu.make_async_copy(hbm.at[pl.ds(0, n), :], buf.at[pl.ds(0, n), :], sem).wait()


def _expert_mlp(x_cur, y_cur, bgu_ref, bd_ref, wgu_bf, wd_bf):
    xb = _load_token_major(x_cur, EXPERT_TILE).astype(BF16)
    hgu = _dot(xb, wgu_bf[...]) + bgu_ref[0]
    glu = jnp.minimum(hgu[:, :D_EXPERT], SWIGLU_LIMIT)
    lin = jnp.clip(hgu[:, D_EXPERT:], -SWIGLU_LIMIT, SWIGLU_LIMIT)
    act = glu * jax.nn.sigmoid(glu * SWIGLU_ALPHA) * (lin + 1.0)
    _store_token_major(y_cur, _dot(act.astype(BF16), wd_bf[...]) + bd_ref[0])


def _weight_copies(wgu_hbm, wd_hbm, wgu_stage, wd_stage, sem_w, expert):
    return (pltpu.make_async_copy(wgu_hbm.at[expert], wgu_stage, sem_w.at[0]),
            pltpu.make_async_copy(wd_hbm.at[expert], wd_stage, sem_w.at[1]))


def _expert_body(te_ref, nxt_ref, nused_ref, valid_ref, lo_ref, src_ref, off_ref, len_ref, xc_hbm, wgu_hbm, bgu_ref,
                 wd_hbm, bd_ref, yc_hbm, xbuf0, xbuf1, ybuf0, ybuf1, wgu_stage, wd_stage, wgu_bf, wd_bf,
                 sem_in, sem_out, sem_w):
    i = pl.program_id(0)
    n_used = nused_ref[0]
    xbufs, ybufs = (xbuf0, xbuf1), (ybuf0, ybuf1)
    pieces = (lo_ref, src_ref, off_ref, len_ref)

    @pl.when(i == 0)
    def _():
        xbuf0[...] = jnp.zeros_like(xbuf0)
        xbuf1[...] = jnp.zeros_like(xbuf1)
        for cp in _weight_copies(wgu_hbm, wd_hbm, wgu_stage, wd_stage, sem_w, te_ref[0]):
            cp.start(priority=1)
        _piece_copies(0, *pieces, xc_hbm, xbuf0, sem_in.at[0], to_hbm=False)

    @pl.when(jnp.logical_and(i < n_used, jnp.logical_or(i == 0, te_ref[i] != te_ref[jnp.maximum(i - 1, 0)])))
    def _():
        for cp in _weight_copies(wgu_hbm, wd_hbm, wgu_stage, wd_stage, sem_w, te_ref[i]):
            cp.wait()
        wgu_bf[...] = wgu_stage[...].astype(BF16)
        wd_bf[...] = wd_stage[...].astype(BF16)

        @pl.when(nxt_ref[i] >= 0)
        def _():
            for cp in _weight_copies(wgu_hbm, wd_hbm, wgu_stage, wd_stage, sem_w, nxt_ref[i]):
                cp.start(priority=1)

    for par in range(2):
        x_cur, x_next, y_cur, y_prev = xbufs[par], xbufs[1 - par], ybufs[par], ybufs[1 - par]

        @pl.when(jnp.logical_and(i < n_used, i % 2 == par))
        def _(x_cur=x_cur, x_next=x_next, y_cur=y_cur, y_prev=y_prev, par=par):
            _wait_rows(xc_hbm, x_cur, sem_in.at[par], valid_ref[i])

            @pl.when(i + 1 < n_used)
            def _():
                _piece_copies(i + 1, *pieces, xc_hbm, x_next, sem_in.at[1 - par], to_hbm=False)

            @pl.when(i > 0)
            def _():
                _piece_copies(i - 1, *pieces, yc_hbm, y_prev, sem_out, to_hbm=True)

            _expert_mlp(x_cur, y_cur, bgu_ref, bd_ref, wgu_bf, wd_bf)

            @pl.when(i > 0)
            def _():
                _wait_rows(yc_hbm, y_prev, sem_out, valid_ref[i - 1])

        @pl.when(jnp.logical_and(i == n_used, i % 2 == par))
        def _(y_prev=y_prev):
            _piece_copies(i - 1, *pieces, yc_hbm, y_prev, sem_out, to_hbm=True)
            _wait_rows(yc_hbm, y_prev, sem_out, valid_ref[i - 1])


def _combine_body(x1_ref, gw_ref, pos_ref, yc_ref, out_ref):
    rows = x1_ref.shape[0]
    y = _load_token_major(yc_ref, TOP_K * rows)
    w = _pair_selector(pos_ref[...], gw_ref[...])
    w_hi = w.astype(BF16)
    w_lo = (w - w_hi.astype(F32)).astype(BF16)
    y_hi = y.astype(BF16)
    y_lo = (y - y_hi.astype(F32)).astype(BF16)
    out_ref[...] = x1_ref[...] + (_dot(w_hi, y_hi) + _dot(w_lo, y_hi) + _dot(w_hi, y_lo))


def _const_spec(shape):
    nd = len(shape)
    return pl.BlockSpec(shape, lambda *_: (0,) * nd, pipeline_mode=pl.Buffered(1))


def _q_perm():
    cols = np.arange(ATTN_WIDTH)
    j, half, d = cols // LANES, (cols % LANES) // HEAD_DIM, cols % HEAD_DIM
    return (j + Q_PER_KV * half) * HEAD_DIM + d


def _alibi_slopes():
    return 2.0 ** (-8.0 * np.arange(1, N_HEADS + 1) / N_HEADS)


def _prompt_bias():
    qi = np.arange(PAIR)[:, None]
    kj = np.arange(BAND)[None, :]
    dist = np.abs(qi + WINDOW - kj).astype(np.float64)
    cq, ck = qi // CHUNK, kj // CHUNK
    in_band = (ck >= cq) & (ck <= cq + WINDOW // CHUNK)
    base = -_alibi_slopes()[:, None, None] * dist[None]
    later = np.where(in_band[None], base, NEG_INF)
    first = np.where((kj >= WINDOW)[None], later, NEG_INF)
    return np.stack([first, later]).astype(np.float32)


def _sample_bias(n_new, n_cached):
    qi = np.arange(n_new)[:, None]
    dc = np.abs(qi + n_cached - np.arange(n_cached)[None, :]).astype(np.float64)
    dn = np.abs(qi - np.arange(n_new)[None, :]).astype(np.float64)
    sl = _alibi_slopes()[:, None, None]
    bc = (-sl * dc[None]).reshape(N_HEADS * n_new, n_cached)
    bn = (-sl * dn[None]).reshape(N_HEADS * n_new, n_new)
    return bc.astype(np.float32), bn.astype(np.float32)


def kernel(x_prompt, x_sample, cache_k_win, cache_v_win, g_mix, w_in, q_norm_g, k_norm_g, attn_sinks, sgu_norm_g,
           w_spatial, b_spatial, w_branch_attn, w_branch_sgu, w_out, g_ffn, w_router, b_router, w_gate_up,
           b_gate_up, w_down, b_down):
    n_b, seq, _ = x_prompt.shape
    n_streams, n_new, _ = x_sample.shape
    n_cached = cache_k_win.shape[2]
    n_prompt = n_b * seq
    n_sample = n_streams * n_new
    n_tok = n_prompt + n_sample
    assert seq % TOKEN_BLOCK == 0 and n_sample == TOKEN_BLOCK and n_cached == WINDOW
    blocks_per_seq = seq // TOKEN_BLOCK
    n_prompt_blocks = n_prompt // TOKEN_BLOCK

    perm = _q_perm()
    w_in_l = w_in[0]
    w_in_b = w_in_l.astype(BF16)
    halves = N_HEADS // Q_PER_KV
    wq_b = (w_in_l[:, :Q_END].reshape(D_MODEL, halves, Q_PER_KV, HEAD_DIM).transpose(0, 2, 1, 3)
            .reshape(D_MODEL, Q_END).astype(BF16))
    wba_b = w_branch_attn[0][perm, :].astype(BF16)
    wbs_b = w_branch_sgu[0].astype(BF16)
    wout_b = w_out[0].astype(BF16)
    wr_b = w_router[0].astype(BF16)
    br = b_router[0].reshape(1, N_EXPERTS).astype(F32)
    gmix = g_mix[0].reshape(1, D_MODEL)
    gffn = g_ffn[0].reshape(1, D_MODEL)
    qg = jnp.tile(q_norm_g[0], LANES // HEAD_DIM).reshape(1, LANES)
    kg = jnp.tile(k_norm_g[0], LANES // HEAD_DIM).reshape(1, LANES)
    sgug = sgu_norm_g[0].reshape(1, MLP_WIDTH)
    sinks = attn_sinks[0].astype(F32)
    lane_seg = np.arange(LANES) // HEAD_DIM
    seg64 = jnp.asarray(np.tile((lane_seg[:, None] == lane_seg[None, :]) / HEAD_DIM, (2, 1)), BF16)
    seg128 = jnp.full((2 * LANES, LANES), 1.0 / MLP_GROUP_DIM, BF16)
    tri = jnp.asarray(np.tril(np.ones((TOKEN_BLOCK, TOKEN_BLOCK)), -1), BF16)
    wsp = w_spatial[0]
    bsp_p = jnp.repeat(b_spatial[0].T, MLP_GROUP_DIM, axis=1)
    wsp_s = wsp[:, :n_new, :n_new]
    bsp_s = jnp.tile(jnp.repeat(b_spatial[0][:, :n_new].T, MLP_GROUP_DIM, axis=1), (n_streams, 1))
    bias_p = jnp.asarray(_prompt_bias())
    bias_c, bias_n = (jnp.asarray(a) for a in _sample_bias(n_new, n_cached))

    smem = pl.BlockSpec(memory_space=pltpu.SMEM)
    tb = TOKEN_BLOCK
    pb = TOP_K * tb
    n_tok_blocks = n_tok // tb
    last_blk = n_prompt_blocks - 1
    seq_of = lambda i: jnp.minimum(i, last_blk) // blocks_per_seq
    x_spec = pl.BlockSpec((1, tb, D_MODEL), lambda i: (seq_of(i), jnp.minimum(i, last_blk) % blocks_per_seq, 0))
    routed_blk = lambda i: jnp.where(i > n_prompt_blocks, n_prompt_blocks, jnp.clip(i - 1, 0, last_blk))
    win_spec = lambda width: pl.BlockSpec((1, WINDOW, width), lambda i: (seq_of(i), 0, 0))
    route_specs = lambda blk: [pl.BlockSpec((pb * ROW_TILE, LANES), lambda i: (blk(i), 0)),
                               pl.BlockSpec((tb, TOP_K), lambda i: (blk(i), 0)),
                               pl.BlockSpec((tb, TOP_K), lambda i: (blk(i), 0)),
                               pl.BlockSpec((1, 1, N_EXPERTS), lambda i: (blk(i), 0, 0))]
    route_shapes = [jax.ShapeDtypeStruct((n_tok * TOP_K * ROW_TILE, LANES), F32),
                    jax.ShapeDtypeStruct((n_tok, TOP_K), F32),
                    jax.ShapeDtypeStruct((n_tok, TOP_K), jnp.int32),
                    jax.ShapeDtypeStruct((n_tok_blocks, 1, N_EXPERTS), F32)]

    x1p, xc, gw_p, pos_p, cnt_p, kwin_p, vwin_p, sguv_p = pl.pallas_call(
        functools.partial(_mixer_prompt_body, blocks_per_seq=blocks_per_seq, n_blocks=n_prompt_blocks),
        grid=(n_prompt_blocks + 2,),
        in_specs=[smem, x_spec, _const_spec((D_MODEL, Q_END)),
                  _const_spec((D_MODEL, IN_COLS)), _const_spec((1, D_MODEL)), _const_spec((1, LANES)),
                  _const_spec((1, LANES)), _const_spec((1, MLP_WIDTH)), _const_spec((2 * LANES, LANES)),
                  _const_spec((2 * LANES, LANES)), _const_spec((2, N_HEADS, PAIR, BAND)),
                  _const_spec((MLP_GROUPS, MLP_CHUNK, MLP_CHUNK)), _const_spec((MLP_CHUNK, MLP_WIDTH)),
                  _const_spec((ATTN_WIDTH, D_MODEL)), _const_spec((MLP_WIDTH, D_MODEL)),
                  _const_spec((D_MODEL, D_MODEL)), _const_spec((1, D_MODEL)), _const_spec((D_MODEL, N_EXPERTS)),
                  _const_spec((1, N_EXPERTS)), _const_spec((tb, tb))],
        out_specs=[x_spec, *route_specs(routed_blk),
                   win_spec(KV_WIDTH), win_spec(KV_WIDTH),
                   pl.BlockSpec((1, MLP_CHUNK, MLP_GROUPS, MLP_GROUP_DIM), lambda i: (seq_of(i), 0, 0, 0))],
        out_shape=[jax.ShapeDtypeStruct((n_b, seq, D_MODEL), F32), *route_shapes,
                   jax.ShapeDtypeStruct((n_b, WINDOW, KV_WIDTH), F32),
                   jax.ShapeDtypeStruct((n_b, WINDOW, KV_WIDTH), F32),
                   jax.ShapeDtypeStruct((n_b, MLP_CHUNK, MLP_GROUPS, MLP_GROUP_DIM), F32)],
        scratch_shapes=[pltpu.VMEM((WINDOW, KV_WIDTH), BF16), pltpu.VMEM((WINDOW, KV_WIDTH), BF16),
                        pltpu.VMEM((tb, D_MODEL), BF16)],
        compiler_params=pltpu.CompilerParams(dimension_semantics=("arbitrary",), vmem_limit_bytes=VMEM_LIMIT),
        name="mixer_prompt",
    )(sinks, x_prompt, wq_b, w_in_b, gmix, qg, kg, sgug, seg64, seg128, bias_p, wsp, bsp_p, wba_b, wbs_b, wout_b,
      gffn, wr_b, br, tri)

    full = lambda shape: pl.BlockSpec(shape, lambda i: (0,) * len(shape))
    any_spec = pl.BlockSpec(memory_space=pl.ANY)
    ck = cache_k_win[0].reshape(n_streams, n_cached, KV_WIDTH)
    cv = cache_v_win[0].reshape(n_streams, n_cached, KV_WIDTH)
    x1s, xc, gw_a, pos_a, cnt_a, kwin_s, vwin_s, sguv_s = pl.pallas_call(
        functools.partial(_mixer_sample_body, n_streams=n_streams, n_new=n_new),
        grid=(1,),
        in_specs=[smem, full((n_sample, D_MODEL)), full((n_streams, n_cached, KV_WIDTH)),
                  full((n_streams, n_cached, KV_WIDTH)),
                  full((D_MODEL, Q_END)), full((D_MODEL, IN_COLS)), full((1, D_MODEL)), full((1, LANES)),
                  full((1, LANES)),
                  full((1, MLP_WIDTH)), full((2 * LANES, LANES)), full((2 * LANES, LANES)),
                  full((N_HEADS * n_new, n_cached)), full((N_HEADS * n_new, n_new)),
                  full((MLP_GROUPS, n_new, n_new)), full((n_sample, MLP_WIDTH)),
                  full((ATTN_WIDTH, D_MODEL)), full((MLP_WIDTH, D_MODEL)), full((D_MODEL, D_MODEL)),
                  full((1, D_MODEL)), full((D_MODEL, N_EXPERTS)), full((1, N_EXPERTS)), full((tb, tb)),
                  any_spec, any_spec, any_spec, any_spec],
        out_specs=[full((n_sample, D_MODEL)), *route_specs(lambda i: n_prompt_blocks),
                   full((n_streams, n_cached, KV_WIDTH)), full((n_streams, n_cached, KV_WIDTH)),
                   full((n_sample, MLP_GROUPS, MLP_GROUP_DIM))],
        out_shape=[jax.ShapeDtypeStruct((n_sample, D_MODEL), F32), *route_shapes,
                   jax.ShapeDtypeStruct((n_streams, n_cached, KV_WIDTH), F32),
                   jax.ShapeDtypeStruct((n_streams, n_cached, KV_WIDTH), F32),
                   jax.ShapeDtypeStruct((n_sample, MLP_GROUPS, MLP_GROUP_DIM), F32)],
        input_output_aliases={23: 1, 24: 2, 25: 3, 26: 4},
        compiler_params=pltpu.CompilerParams(dimension_semantics=("arbitrary",), vmem_limit_bytes=VMEM_LIMIT),
        name="mixer_sample",
    )(sinks, x_sample.reshape(n_sample, D_MODEL), ck, cv, wq_b, w_in_b, gmix, qg, kg, sgug, seg64, seg128, bias_c,
      bias_n, wsp_s, bsp_s, wba_b, wbs_b, wout_b, gffn, wr_b, br, tri, xc, gw_p, pos_p, cnt_p)

    tm = EXPERT_TILE
    counts = cnt_a[:, 0, :].astype(jnp.int32)
    cnt_all = jnp.sum(counts, axis=0)
    padded = ((cnt_all + tm - 1) // tm) * tm
    pad_end = jnp.cumsum(padded)
    pad_off = pad_end - padded
    n_pairs = TOP_K * n_tok
    n_tiles = n_pairs // tm + N_EXPERTS
    n_steps = n_tiles + 1
    in_block = jnp.cumsum(counts, axis=1) - counts
    in_expert = jnp.cumsum(counts, axis=0) - counts
    run_src = (jnp.arange(n_tok_blocks, dtype=jnp.int32)[:, None] * pb + in_block).T.reshape(-1)
    run_start = (pad_off[None, :] + in_expert).T.reshape(-1)
    run_len = counts.T.reshape(-1)
    head = jnp.minimum(run_len, tm - run_start % tm)
    piece_start = jnp.stack([run_start, run_start + head], axis=1).reshape(-1)
    piece_src = jnp.stack([run_src, run_src + head], axis=1).reshape(-1)
    piece_len = jnp.stack([head, run_len - head], axis=1).reshape(-1)
    piece_tile = piece_start // tm
    tile_ids = jnp.arange(n_steps + 1, dtype=jnp.int32)
    piece_lo = jnp.sum((piece_tile[None, :] < tile_ids[:, None]).astype(jnp.int32), axis=1)
    n_used = (pad_end[-1] // tm).astype(jnp.int32)
    tile_start = jnp.minimum(tile_ids[:n_steps], n_used - 1) * tm
    tile_expert = jnp.sum((pad_end[None, :] <= tile_start[:, None]).astype(jnp.int32), axis=1)
    tile_expert = jnp.minimum(tile_expert, N_EXPERTS - 1)
    of_tile = tile_expert[:, None] == jnp.arange(N_EXPERTS, dtype=jnp.int32)[None, :]
    rows_end = jnp.sum(jnp.where(of_tile, (pad_off + cnt_all)[None, :], 0), axis=1)
    run_end = jnp.sum(jnp.where(of_tile, pad_end[None, :], 0), axis=1)
    tile_valid = jnp.clip(rows_end - tile_start, 0, tm)
    next_expert = jnp.sum((pad_end[None, :] <= run_end[:, None]).astype(jnp.int32), axis=1)
    next_expert = jnp.where(run_end < pad_end[-1], jnp.minimum(next_expert, N_EXPERTS - 1), -1)

    prefetch = (tile_expert, next_expert, n_used.reshape(1), tile_valid * ROW_TILE, piece_lo,
                piece_src * ROW_TILE, (piece_start % tm) * ROW_TILE, piece_len * ROW_TILE)
    yc = pl.pallas_call(
        _expert_body,
        grid_spec=pltpu.PrefetchScalarGridSpec(
            num_scalar_prefetch=len(prefetch),
            grid=(n_steps,),
            in_specs=[any_spec, any_spec,
                      pl.BlockSpec((1, 1, 2 * D_EXPERT), lambda i, te, *_: (te[i], 0, 0)),
                      any_spec,
                      pl.BlockSpec((1, 1, D_MODEL), lambda i, te, *_: (te[i], 0, 0))],
            out_specs=any_spec,
            scratch_shapes=[pltpu.VMEM((tm * ROW_TILE, LANES), F32), pltpu.VMEM((tm * ROW_TILE, LANES), F32),
                            pltpu.VMEM((tm * ROW_TILE, LANES), F32), pltpu.VMEM((tm * ROW_TILE, LANES), F32),
                            pltpu.VMEM((D_MODEL, 2 * D_EXPERT), F32), pltpu.VMEM((D_EXPERT, D_MODEL), F32),
                            pltpu.VMEM((D_MODEL, 2 * D_EXPERT), BF16), pltpu.VMEM((D_EXPERT, D_MODEL), BF16),
                            pltpu.SemaphoreType.DMA((2,)), pltpu.SemaphoreType.DMA(()),
                            pltpu.SemaphoreType.DMA((2,))]),
        out_shape=jax.ShapeDtypeStruct((n_pairs * ROW_TILE, LANES), F32),
        compiler_params=pltpu.CompilerParams(dimension_semantics=("arbitrary",), vmem_limit_bytes=VMEM_LIMIT),
        name="moe_experts",
    )(*[p.astype(jnp.int32) for p in prefetch], xc, w_gate_up[0],
      b_gate_up[0].reshape(N_EXPERTS, 1, 2 * D_EXPERT), w_down[0], b_down[0].reshape(N_EXPERTS, 1, D_MODEL))

    def combine(x1, first_block, n_blocks):
        return pl.pallas_call(
            _combine_body,
            grid=(n_blocks,),
            in_specs=[pl.BlockSpec((tb, D_MODEL), lambda i: (i, 0)),
                      pl.BlockSpec((tb, TOP_K), lambda i: (first_block + i, 0)),
                      pl.BlockSpec((tb, TOP_K), lambda i: (first_block + i, 0)),
                      pl.BlockSpec((pb * ROW_TILE, LANES), lambda i: (first_block + i, 0))],
            out_specs=pl.BlockSpec((tb, D_MODEL), lambda i: (i, 0)),
            out_shape=jax.ShapeDtypeStruct(x1.shape, F32),
            compiler_params=pltpu.CompilerParams(dimension_semantics=("arbitrary",), vmem_limit_bytes=VMEM_LIMIT),
            name="moe_combine",
        )(x1, gw_a, pos_a, yc)

    y_prompt = combine(x1p.reshape(n_prompt, D_MODEL), 0, n_prompt_blocks).reshape(n_b, seq, D_MODEL)
    y_sample = combine(x1s, n_prompt_blocks, 1).reshape(n_streams, n_new, D_MODEL)

    kv_shape = (N_KV_HEADS, HEAD_DIM)
    sg_shape = (MLP_GROUPS, MLP_GROUP_DIM)
    return (y_prompt, y_sample,
            kwin_p.reshape(1, n_b, WINDOW, *kv_shape), vwin_p.reshape(1, n_b, WINDOW, *kv_shape),
            kwin_s.reshape(1, n_streams, n_cached, *kv_shape), vwin_s.reshape(1, n_streams, n_cached, *kv_shape),
            sguv_p.reshape(1, n_b, MLP_CHUNK, *sg_shape), sguv_s.reshape(1, n_streams, n_new, *sg_shape))
```

```python
import functools

import jax
import jax.numpy as jnp
import numpy as np
from jax import lax
from jax.experimental import pallas as pl
from jax.experimental.pallas import tpu as pltpu

D_MODEL = 1024
CHUNK = 64
WINDOW = 128
HEAD_DIM = 64
N_HEADS = 8
N_KV_HEADS = 2
Q_PER_KV = N_HEADS // N_KV_HEADS
ATTN_WIDTH = N_HEADS * HEAD_DIM
KV_WIDTH = N_KV_HEADS * HEAD_DIM
ATTN_SCALE = HEAD_DIM ** -0.5
MLP_CHUNK = 128
MLP_GROUPS = 8
MLP_WIDTH = D_MODEL
MLP_GROUP_DIM = MLP_WIDTH // MLP_GROUPS
N_EXPERTS = 32
TOP_K = 4
D_EXPERT = D_MODEL
SWIGLU_ALPHA = 1.702
SWIGLU_LIMIT = 7.0
EPS = 1e-6
NEG_INF = -1e30
Q_END = ATTN_WIDTH
K_END = Q_END + KV_WIDTH
V_END = K_END + KV_WIDTH
U_END = V_END + MLP_WIDTH
VM_END = U_END + MLP_WIDTH
GA_END = VM_END + D_MODEL
IN_COLS = GA_END + D_MODEL

LANES = 128
ROW_TILE = D_MODEL // LANES
TOKEN_BLOCK = 256
PAIR = 2 * CHUNK
BAND = PAIR + WINDOW
EXPERT_TILE = 512
VMEM_LIMIT = 56 * 1024 * 1024

F32 = jnp.float32
BF16 = jnp.bfloat16


def _dot(a, b):
    return jnp.dot(a, b, preferred_element_type=F32)


def _dot_nt(a, b):
    return lax.dot_general(a, b, (((1,), (1,)), ((), ())), preferred_element_type=F32)


def _store_token_major(ref, val):
    rows = val.shape[0]
    for s in range(ROW_TILE):
        ref[pl.ds(s, rows, stride=ROW_TILE), :] = val[:, s * LANES:(s + 1) * LANES]


def _load_token_major(ref, rows):
    return jnp.concatenate([ref[pl.ds(s, rows, stride=ROW_TILE), :] for s in range(ROW_TILE)], axis=-1)


def _segment_mean(sq, seg):
    hi = sq.astype(BF16)
    lo = (sq - hi.astype(F32)).astype(BF16)
    return _dot(jnp.concatenate([hi, lo], axis=-1), seg)


def _rms_rows(x, gain):
    ms = jnp.mean(x * x, axis=-1, keepdims=True)
    return (x * lax.rsqrt(ms + EPS)) * gain


def _project(h, wq_ref, w_in_ref, qg, kg, sgug, seg64, seg128, between=None):
    qkv = jnp.concatenate([_dot(h, wq_ref[...]), _dot(h, w_in_ref[:, Q_END:V_END])], axis=-1)
    qk_cols = []
    for c in range(K_END // LANES):
        blk = qkv[:, c * LANES:(c + 1) * LANES]
        ms = _segment_mean(blk * blk, seg64)
        g = qg if c < Q_END // LANES else kg
        qk_cols.append((blk * lax.rsqrt(ms + EPS)) * g)
    qn = jnp.concatenate(qk_cols[:Q_END // LANES], axis=-1)
    kn = qk_cols[Q_END // LANES]
    v = qkv[:, K_END:V_END]
    u = jax.nn.gelu(_dot(h, w_in_ref[:, V_END:U_END]))
    vg = jax.nn.gelu(_dot(h, w_in_ref[:, U_END:VM_END]))
    vm_cols = []
    for g in range(MLP_GROUPS):
        blk = vg[:, g * LANES:(g + 1) * LANES]
        ms = _segment_mean(blk * blk, seg128)
        vm_cols.append((blk * lax.rsqrt(ms + EPS)) * sgug[:, g * LANES:(g + 1) * LANES])
    vm = jnp.concatenate(vm_cols, axis=-1)
    if between is not None:
        between()
    ga = jax.nn.sigmoid(_dot(h, w_in_ref[:, VM_END:GA_END]))
    gb = jax.nn.sigmoid(_dot(h, w_in_ref[:, GA_END:IN_COLS]))
    return qn, kn, v, u, vm, ga, gb


def _stack_heads(q_rows):
    lane = lax.broadcasted_iota(jnp.int32, (q_rows.shape[0], LANES), 1)
    blocks = []
    for head in range(N_HEADS):
        j, half = head % Q_PER_KV, head // Q_PER_KV
        col = q_rows[:, j * LANES:(j + 1) * LANES]
        keep = (lane < HEAD_DIM) if half == 0 else (lane >= HEAD_DIM)
        blocks.append(jnp.where(keep, col, 0.0))
    return jnp.concatenate(blocks, axis=0).astype(BF16)


def _unstack_heads(o, rows):
    lane = lax.broadcasted_iota(jnp.int32, (rows, LANES), 1)
    cols = []
    for j in range(Q_PER_KV):
        lo = o[j * rows:(j + 1) * rows]
        hi = o[(j + Q_PER_KV) * rows:(j + Q_PER_KV + 1) * rows]
        cols.append(jnp.where(lane < HEAD_DIM, lo, hi))
    return jnp.concatenate(cols, axis=-1)


def _merge(att, sgu, ga, gb, wba_ref, wbs_ref, wout_ref):
    m = ga * _dot(att.astype(BF16), wba_ref[...]) + gb * _dot(sgu.astype(BF16), wbs_ref[...])
    return _dot(m.astype(BF16), wout_ref[...])


def _route(h2, wr_ref, br_ref, tri_ref):
    rows = h2.shape[0]
    logits = _dot(h2, wr_ref[...]) + br_ref[...]
    eidx = lax.broadcasted_iota(jnp.int32, (rows, N_EXPERTS), 1).astype(F32)
    work = logits
    vals, picks, onehots = [], [], []
    for _ in range(TOP_K):
        m = jnp.max(work, axis=-1, keepdims=True)
        sel = jnp.min(jnp.where(work == m, eidx, float(N_EXPERTS)), axis=-1, keepdims=True)
        oh = eidx == sel
        vals.append(m)
        picks.append(sel)
        onehots.append(oh)
        work = jnp.where(oh, -jnp.inf, work)
    exps = [jnp.exp(v - vals[0]) for v in vals]
    den = exps[0] + exps[1] + exps[2] + exps[3]
    mask = jnp.zeros((rows, N_EXPERTS), F32)
    for oh in onehots:
        mask = mask + jnp.where(oh, 1.0, 0.0)
    before = _dot(tri_ref[...], mask.astype(BF16))
    counts = jnp.sum(mask, axis=0, keepdims=True)
    k4 = lax.broadcasted_iota(jnp.int32, (rows, TOP_K), 1)
    w4 = jnp.zeros((rows, TOP_K), F32)
    pos4 = jnp.zeros((rows, TOP_K), jnp.int32)
    for k in range(TOP_K):
        lower_experts = jnp.sum(jnp.where(eidx < picks[k], counts, 0.0), axis=-1, keepdims=True)
        rank = jnp.sum(jnp.where(onehots[k], before, 0.0), axis=-1, keepdims=True)
        w4 = jnp.where(k4 == k, exps[k] / den, w4)
        pos4 = jnp.where(k4 == k, (lower_experts + rank).astype(jnp.int32), pos4)
    return w4, pos4, counts


def _pair_selector(pos4, values=None):
    rows = pos4.shape[0]
    col = lax.broadcasted_iota(jnp.int32, (rows, TOP_K * rows), 1)
    sel = jnp.zeros((rows, TOP_K * rows), F32)
    for k in range(TOP_K):
        sel = sel + jnp.where(col == pos4[:, k:k + 1], 1.0 if values is None else values[:, k:k + 1], 0.0)
    return sel


def _route_block(h2, wr_ref, br_ref, tri_ref, gw_ref, pos_ref, cnt_ref):
    w4, pos4, counts = _route(h2, wr_ref, br_ref, tri_ref)
    gw_ref[...] = w4
    pos_ref[...] = pos4
    cnt_ref[0] = counts
    return pos4


def _dispatch_block(pos4, h2, xc_ref):
    select = _pair_selector(pos4).astype(BF16)
    pairs = lax.dot_general(select, h2, (((0,), (0,)), ((), ())), preferred_element_type=F32)
    _store_token_major(xc_ref, pairs)


def _route_and_dispatch(h2, wr_ref, br_ref, tri_ref, gw_ref, pos_ref, cnt_ref, xc_ref):
    _dispatch_block(_route_block(h2, wr_ref, br_ref, tri_ref, gw_ref, pos_ref, cnt_ref), h2, xc_ref)


def _softmax_rows(parts, sink):
    m = sink
    for l in parts:
        m = jnp.maximum(m, jnp.max(l, axis=-1, keepdims=True))
    es = [jnp.exp(l - m) for l in parts]
    den = jnp.exp(sink - m)
    for e in es:
        den = den + jnp.sum(e, axis=-1, keepdims=True)
    return [e / den for e in es]


def _mixer_prompt_body(sinks_ref, x_ref, wq_ref, w_in_ref, gmix_ref, qg_ref, kg_ref, sgug_ref, seg64_ref, seg128_ref,
                       bias_ref, wsp_ref, bsp_ref, wba_ref, wbs_ref, wout_ref, gffn_ref, wr_ref, br_ref, tri_ref,
                       x1_ref, xc_ref, gw_ref, pos_ref, cnt_ref, kwin_ref, vwin_ref, sguv_ref,
                       kcarry, vcarry, h2_prev_ref, *, blocks_per_seq, n_blocks):
    i = pl.program_id(0)

    @pl.when(i == 0)
    def _():
        h2_prev_ref[...] = jnp.zeros_like(h2_prev_ref)

    def route_previous():
        return _route_block(h2_prev_ref[...], wr_ref, br_ref, tri_ref, gw_ref, pos_ref, cnt_ref)

    def dispatch_previous(pos4):
        _dispatch_block(pos4, h2_prev_ref[...], xc_ref)

    @pl.when(i == n_blocks)
    def _():
        dispatch_previous(route_previous())

    @pl.when(i == n_blocks + 1)
    def _():
        xc_ref[...] = jnp.zeros_like(xc_ref)
        gw_ref[...] = jnp.zeros_like(gw_ref)
        pos_ref[...] = jnp.zeros_like(pos_ref)
        cnt_ref[...] = jnp.zeros_like(cnt_ref)

    @pl.when(jnp.logical_and(i < n_blocks, i % blocks_per_seq == 0))
    def _():
        kcarry[...] = jnp.zeros_like(kcarry)
        vcarry[...] = jnp.zeros_like(vcarry)

    @pl.when(i < n_blocks)
    def _():
        _mixer_prompt_block(i % blocks_per_seq, sinks_ref, x_ref, wq_ref, w_in_ref, gmix_ref, qg_ref, kg_ref, sgug_ref,
                            seg64_ref, seg128_ref, bias_ref, wsp_ref, bsp_ref, wba_ref, wbs_ref, wout_ref, gffn_ref,
                            x1_ref, kwin_ref, vwin_ref, sguv_ref, kcarry, vcarry, h2_prev_ref,
                            route_previous, dispatch_previous)


def _mixer_prompt_block(j, sinks_ref, x_ref, wq_ref, w_in_ref, gmix_ref, qg_ref, kg_ref, sgug_ref, seg64_ref, seg128_ref,
                        bias_ref, wsp_ref, bsp_ref, wba_ref, wbs_ref, wout_ref, gffn_ref,
                        x1_ref, kwin_ref, vwin_ref, sguv_ref, kcarry, vcarry, h2_prev_ref,
                        route_previous, dispatch_previous):
    routed = []
    x = x_ref[0]
    h = _rms_rows(x, gmix_ref[...]).astype(BF16)
    qn, kn, v, u, vm, ga, gb = _project(h, wq_ref, w_in_ref, qg_ref[...], kg_ref[...], sgug_ref[...],
                                        seg64_ref[...], seg128_ref[...],
                                        between=lambda: routed.append(route_previous()))
    k_ext = jnp.concatenate([kcarry[...], kn.astype(BF16)], axis=0)
    v_ext = jnp.concatenate([vcarry[...], v.astype(BF16)], axis=0)
    kcarry[...] = k_ext[TOKEN_BLOCK:]
    vcarry[...] = v_ext[TOKEN_BLOCK:]

    tri_mask = (lax.broadcasted_iota(jnp.int32, (MLP_CHUNK, MLP_CHUNK), 0)
                >= lax.broadcasted_iota(jnp.int32, (MLP_CHUNK, MLP_CHUNK), 1))
    att_rows, sgu_rows = [], []
    for pm in range(TOKEN_BLOCK // PAIR):
        r0 = pm * PAIR
        q_stack = _stack_heads(qn[r0:r0 + PAIR])
        k_band = k_ext[r0:r0 + BAND]
        v_band = v_ext[r0:r0 + BAND]
        s = _dot_nt(q_stack, k_band)
        first = jnp.where(j == 0, 0, 1) if pm == 0 else 1
        probs = []
        for head in range(N_HEADS):
            logit = s[head * PAIR:(head + 1) * PAIR] * ATTN_SCALE + bias_ref[first, head]
            probs.append(_softmax_rows([logit], sinks_ref[head])[0].astype(BF16))
        o = _dot(jnp.concatenate(probs, axis=0), v_band)
        att_rows.append(_unstack_heads(o, PAIR))
        cols = []
        for g in range(MLP_GROUPS):
            wm = jnp.where(tri_mask, wsp_ref[g], 0.0).astype(BF16)
            cols.append(_dot(wm, vm[r0:r0 + PAIR, g * LANES:(g + 1) * LANES].astype(BF16)))
        mixed = jnp.concatenate(cols, axis=-1) + bsp_ref[...]
        sgu_rows.append(u[r0:r0 + PAIR] * mixed)
        if pm == 0:
            dispatch_previous(routed[0])
    att = jnp.concatenate(att_rows, axis=0)
    sgu = jnp.concatenate(sgu_rows, axis=0)

    x1 = x + _merge(att, sgu, ga, gb, wba_ref, wbs_ref, wout_ref)
    h2 = _rms_rows(x1, gffn_ref[...])
    h2_prev_ref[...] = h2.astype(BF16)

    x1_ref[0] = x1
    kwin_ref[0] = kn[TOKEN_BLOCK - WINDOW:]
    vwin_ref[0] = v[TOKEN_BLOCK - WINDOW:]
    for g in range(MLP_GROUPS):
        sguv_ref[0, :, g, :] = vm[TOKEN_BLOCK - MLP_CHUNK:, g * LANES:(g + 1) * LANES]


def _mixer_sample_body(sinks_ref, x_ref, ck_ref, cv_ref, wq_ref, w_in_ref, gmix_ref, qg_ref, kg_ref, sgug_ref, seg64_ref,
                       seg128_ref, biasc_ref, biasn_ref, wsp_ref, bsp_ref, wba_ref, wbs_ref, wout_ref, gffn_ref,
                       wr_ref, br_ref, tri_ref, xc_in_ref, gw_in_ref, pos_in_ref, cnt_in_ref,
                       x1_ref, xc_ref, gw_ref, pos_ref, cnt_ref, kwin_ref, vwin_ref, sguv_ref, *, n_streams, n_new):
    del xc_in_ref, gw_in_ref, pos_in_ref, cnt_in_ref
    x = x_ref[...]
    h = _rms_rows(x, gmix_ref[...]).astype(BF16)
    qn, kn, v, u, vm, ga, gb = _project(h, wq_ref, w_in_ref, qg_ref[...], kg_ref[...], sgug_ref[...],
                                        seg64_ref[...], seg128_ref[...])
    n_cached = ck_ref.shape[1]
    att_rows = []
    for s_i in range(n_streams):
        r0 = s_i * n_new
        q_stack = _stack_heads(qn[r0:r0 + n_new])
        k_new = kn[r0:r0 + n_new]
        v_new = v[r0:r0 + n_new]
        s_c = _dot_nt(q_stack, ck_ref[s_i].astype(BF16))
        s_n = _dot_nt(q_stack, k_new.astype(BF16))
        pc, pn = [], []
        for head in range(N_HEADS):
            rows = slice(head * n_new, (head + 1) * n_new)
            lc = s_c[rows] * ATTN_SCALE + biasc_ref[rows]
            ln = s_n[rows] * ATTN_SCALE + biasn_ref[rows]
            p_c, p_n = _softmax_rows([lc, ln], sinks_ref[head])
            pc.append(p_c.astype(BF16))
            pn.append(p_n.astype(BF16))
        o = (_dot(jnp.concatenate(pc, axis=0), cv_ref[s_i].astype(BF16))
             + _dot(jnp.concatenate(pn, axis=0), v_new.astype(BF16)))
        att_rows.append(_unstack_heads(o, n_new))
        kwin_ref[s_i, 0:n_cached - n_new] = ck_ref[s_i, n_new:n_cached]
        kwin_ref[s_i, n_cached - n_new:n_cached] = k_new
        vwin_ref[s_i, 0:n_cached - n_new] = cv_ref[s_i, n_new:n_cached]
        vwin_ref[s_i, n_cached - n_new:n_cached] = v_new
    att = jnp.concatenate(att_rows, axis=0)

    rows = n_streams * n_new
    ri = lax.broadcasted_iota(jnp.int32, (rows, rows), 0)
    ci = lax.broadcasted_iota(jnp.int32, (rows, rows), 1)
    keep = jnp.logical_and(ri // n_new == ci // n_new, ri % n_new >= ci % n_new)
    expand = (lax.broadcasted_iota(jnp.int32, (rows, n_new), 0) % n_new
              == lax.broadcasted_iota(jnp.int32, (rows, n_new), 1)).astype(BF16)
    cols = []
    for g in range(MLP_GROUPS):
        tiled = _dot_nt(_dot(expand, wsp_ref[g].astype(BF16)).astype(BF16), expand)
        wm = jnp.where(keep, tiled, 0.0).astype(BF16)
        cols.append(_dot(wm, vm[:, g * LANES:(g + 1) * LANES].astype(BF16)))
    sgu = u * (jnp.concatenate(cols, axis=-1) + bsp_ref[...])

    x1 = x + _merge(att, sgu, ga, gb, wba_ref, wbs_ref, wout_ref)
    h2 = _rms_rows(x1, gffn_ref[...])
    _route_and_dispatch(h2.astype(BF16), wr_ref, br_ref, tri_ref, gw_ref, pos_ref, cnt_ref, xc_ref)
    x1_ref[...] = x1
    for g in range(MLP_GROUPS):
        sguv_ref[:, g, :] = vm[:, g * LANES:(g + 1) * LANES]


def _piece_copies(i_tile, lo_ref, src_ref, off_ref, len_ref, hbm, buf, sem, to_hbm):
    def one(p, carry):
        n = pl.multiple_of(len_ref[p], ROW_TILE)

        @pl.when(n > 0)
        def _():
            far = hbm.at[pl.ds(pl.multiple_of(src_ref[p], ROW_TILE), n), :]
            near = buf.at[pl.ds(pl.multiple_of(off_ref[p], ROW_TILE), n), :]
            if to_hbm:
                pltpu.make_async_copy(near, far, sem).start()
            else:
                pltpu.make_async_copy(far, near, sem).start()
        return carry
    lax.fori_loop(lo_ref[i_tile], lo_ref[i_tile + 1], one, 0)


def _wait_rows(hbm, buf, sem, n_rows):
    n = pl.multiple_of(n_rows, ROW_TILE)
    pltpu.make_async_copy(hbm.at[pl.ds(0, n), :], buf.at[pl.ds(0, n), :], sem).wait()


def _expert_mlp(x_cur, y_cur, bgu_ref, bd_ref, wgu_bf, wd_bf):
    xb = _load_token_major(x_cur, EXPERT_TILE).astype(BF16)
    hgu = _dot(xb, wgu_bf[...]) + bgu_ref[0]
    glu = jnp.minimum(hgu[:, :D_EXPERT], SWIGLU_LIMIT)
    lin = jnp.clip(hgu[:, D_EXPERT:], -SWIGLU_LIMIT, SWIGLU_LIMIT)
    act = glu * jax.nn.sigmoid(glu * SWIGLU_ALPHA) * (lin + 1.0)
    _store_token_major(y_cur, _dot(act.astype(BF16), wd_bf[...]) + bd_ref[0])


def _weight_copies(wgu_hbm, wd_hbm, wgu_stage, wd_stage, sem_w, expert):
    return (pltpu.make_async_copy(wgu_hbm.at[expert], wgu_stage, sem_w.at[0]),
            pltpu.make_async_copy(wd_hbm.at[expert], wd_stage, sem_w.at[1]))


def _expert_body(te_ref, nxt_ref, nused_ref, valid_ref, lo_ref, src_ref, off_ref, len_ref, xc_hbm, wgu_hbm, bgu_ref,
                 wd_hbm, bd_ref, yc_hbm, xbuf0, xbuf1, ybuf0, ybuf1, wgu_stage, wd_stage, wgu_bf, wd_bf,
                 sem_in, sem_out, sem_w):
    i = pl.program_id(0)
    n_used = nused_ref[0]
    xbufs, ybufs = (xbuf0, xbuf1), (ybuf0, ybuf1)
    pieces = (lo_ref, src_ref, off_ref, len_ref)

    @pl.when(i == 0)
    def _():
        xbuf0[...] = jnp.zeros_like(xbuf0)
        xbuf1[...] = jnp.zeros_like(xbuf1)
        for cp in _weight_copies(wgu_hbm, wd_hbm, wgu_stage, wd_stage, sem_w, te_ref[0]):
            cp.start(priority=1)
        _piece_copies(0, *pieces, xc_hbm, xbuf0, sem_in.at[0], to_hbm=False)

    @pl.when(jnp.logical_and(i < n_used, jnp.logical_or(i == 0, te_ref[i] != te_ref[jnp.maximum(i - 1, 0)])))
    def _():
        for cp in _weight_copies(wgu_hbm, wd_hbm, wgu_stage, wd_stage, sem_w, te_ref[i]):
            cp.wait()
        wgu_bf[...] = wgu_stage[...].astype(BF16)
        wd_bf[...] = wd_stage[...].astype(BF16)

        @pl.when(nxt_ref[i] >= 0)
        def _():
            for cp in _weight_copies(wgu_hbm, wd_hbm, wgu_stage, wd_stage, sem_w, nxt_ref[i]):
                cp.start(priority=1)

    for par in range(2):
        x_cur, x_next, y_cur, y_prev = xbufs[par], xbufs[1 - par], ybufs[par], ybufs[1 - par]

        @pl.when(jnp.logical_and(i < n_used, i % 2 == par))
        def _(x_cur=x_cur, x_next=x_next, y_cur=y_cur, y_prev=y_prev, par=par):
            _wait_rows(xc_hbm, x_cur, sem_in.at[par], valid_ref[i])

            @pl.when(i + 1 < n_used)
            def _():
                _piece_copies(i + 1, *pieces, xc_hbm, x_next, sem_in.at[1 - par], to_hbm=False)

            @pl.when(i > 0)
            def _():
                _piece_copies(i - 1, *pieces, yc_hbm, y_prev, sem_out, to_hbm=True)

            _expert_mlp(x_cur, y_cur, bgu_ref, bd_ref, wgu_bf, wd_bf)

            @pl.when(i > 0)
            def _():
                _wait_rows(yc_hbm, y_prev, sem_out, valid_ref[i - 1])

        @pl.when(jnp.logical_and(i == n_used, i % 2 == par))
        def _(y_prev=y_prev):
            _piece_copies(i - 1, *pieces, yc_hbm, y_prev, sem_out, to_hbm=True)
            _wait_rows(yc_hbm, y_prev, sem_out, valid_ref[i - 1])


def _combine_body(pos_ref, gw_ref, x1_ref, yc_ref, out_ref, acc_ref):
    rows = x1_ref.shape[0]
    _store_token_major(acc_ref, x1_ref[...])
    for t in range(rows):
        row = acc_ref[pl.ds(t * ROW_TILE, ROW_TILE), :]
        for k in range(TOP_K):
            flat = t * TOP_K + k
            at = pl.multiple_of(pos_ref[0, flat // LANES, flat % LANES], ROW_TILE)
            row = row + gw_ref[0, flat // LANES, flat % LANES] * yc_ref[pl.ds(at, ROW_TILE), :]
        acc_ref[pl.ds(t * ROW_TILE, ROW_TILE), :] = row
    out_ref[...] = _load_token_major(acc_ref, rows)


def _const_spec(shape):
    nd = len(shape)
    return pl.BlockSpec(shape, lambda *_: (0,) * nd, pipeline_mode=pl.Buffered(1))


def _q_perm():
    cols = np.arange(ATTN_WIDTH)
    j, half, d = cols // LANES, (cols % LANES) // HEAD_DIM, cols % HEAD_DIM
    return (j + Q_PER_KV * half) * HEAD_DIM + d


def _alibi_slopes():
    return 2.0 ** (-8.0 * np.arange(1, N_HEADS + 1) / N_HEADS)


def _prompt_bias():
    qi = np.arange(PAIR)[:, None]
    kj = np.arange(BAND)[None, :]
    dist = np.abs(qi + WINDOW - kj).astype(np.float64)
    cq, ck = qi // CHUNK, kj // CHUNK
    in_band = (ck >= cq) & (ck <= cq + WINDOW // CHUNK)
    base = -_alibi_slopes()[:, None, None] * dist[None]
    later = np.where(in_band[None], base, NEG_INF)
    first = np.where((kj >= WINDOW)[None], later, NEG_INF)
    return np.stack([first, later]).astype(np.float32)


def _sample_bias(n_new, n_cached):
    qi = np.arange(n_new)[:, None]
    dc = np.abs(qi + n_cached - np.arange(n_cached)[None, :]).astype(np.float64)
    dn = np.abs(qi - np.arange(n_new)[None, :]).astype(np.float64)
    sl = _alibi_slopes()[:, None, None]
    bc = (-sl * dc[None]).reshape(N_HEADS * n_new, n_cached)
    bn = (-sl * dn[None]).reshape(N_HEADS * n_new, n_new)
    return bc.astype(np.float32), bn.astype(np.float32)


def kernel(x_prompt, x_sample, cache_k_win, cache_v_win, g_mix, w_in, q_norm_g, k_norm_g, attn_sinks, sgu_norm_g,
           w_spatial, b_spatial, w_branch_attn, w_branch_sgu, w_out, g_ffn, w_router, b_router, w_gate_up,
           b_gate_up, w_down, b_down):
    n_b, seq, _ = x_prompt.shape
    n_streams, n_new, _ = x_sample.shape
    n_cached = cache_k_win.shape[2]
    n_prompt = n_b * seq
    n_sample = n_streams * n_new
    n_tok = n_prompt + n_sample
    assert seq % TOKEN_BLOCK == 0 and n_sample == TOKEN_BLOCK and n_cached == WINDOW
    blocks_per_seq = seq // TOKEN_BLOCK
    n_prompt_blocks = n_prompt // TOKEN_BLOCK

    perm = _q_perm()
    w_in_l = w_in[0]
    w_in_b = w_in_l.astype(BF16)
    halves = N_HEADS // Q_PER_KV
    wq_b = (w_in_l[:, :Q_END].reshape(D_MODEL, halves, Q_PER_KV, HEAD_DIM).transpose(0, 2, 1, 3)
            .reshape(D_MODEL, Q_END).astype(BF16))
    wba_b = w_branch_attn[0][perm, :].astype(BF16)
    wbs_b = w_branch_sgu[0].astype(BF16)
    wout_b = w_out[0].astype(BF16)
    wr_b = w_router[0].astype(BF16)
    br = b_router[0].reshape(1, N_EXPERTS).astype(F32)
    gmix = g_mix[0].reshape(1, D_MODEL)
    gffn = g_ffn[0].reshape(1, D_MODEL)
    qg = jnp.tile(q_norm_g[0], LANES // HEAD_DIM).reshape(1, LANES)
    kg = jnp.tile(k_norm_g[0], LANES // HEAD_DIM).reshape(1, LANES)
    sgug = sgu_norm_g[0].reshape(1, MLP_WIDTH)
    sinks = attn_sinks[0].astype(F32)
    lane_seg = np.arange(LANES) // HEAD_DIM
    seg64 = jnp.asarray(np.tile((lane_seg[:, None] == lane_seg[None, :]) / HEAD_DIM, (2, 1)), BF16)
    seg128 = jnp.full((2 * LANES, LANES), 1.0 / MLP_GROUP_DIM, BF16)
    tri = jnp.asarray(np.tril(np.ones((TOKEN_BLOCK, TOKEN_BLOCK)), -1), BF16)
    wsp = w_spatial[0]
    bsp_p = jnp.repeat(b_spatial[0].T, MLP_GROUP_DIM, axis=1)
    wsp_s = wsp[:, :n_new, :n_new]
    bsp_s = jnp.tile(jnp.repeat(b_spatial[0][:, :n_new].T, MLP_GROUP_DIM, axis=1), (n_streams, 1))
    bias_p = jnp.asarray(_prompt_bias())
    bias_c, bias_n = (jnp.asarray(a) for a in _sample_bias(n_new, n_cached))

    smem = pl.BlockSpec(memory_space=pltpu.SMEM)
    tb = TOKEN_BLOCK
    pb = TOP_K * tb
    n_tok_blocks = n_tok // tb
    last_blk = n_prompt_blocks - 1
    seq_of = lambda i: jnp.minimum(i, last_blk) // blocks_per_seq
    x_spec = pl.BlockSpec((1, tb, D_MODEL), lambda i: (seq_of(i), jnp.minimum(i, last_blk) % blocks_per_seq, 0))
    routed_blk = lambda i: jnp.where(i > n_prompt_blocks, n_prompt_blocks, jnp.clip(i - 1, 0, last_blk))
    win_spec = lambda width: pl.BlockSpec((1, WINDOW, width), lambda i: (seq_of(i), 0, 0))
    route_specs = lambda blk: [pl.BlockSpec((pb * ROW_TILE, LANES), lambda i: (blk(i), 0)),
                               pl.BlockSpec((tb, TOP_K), lambda i: (blk(i), 0)),
                               pl.BlockSpec((tb, TOP_K), lambda i: (blk(i), 0)),
                               pl.BlockSpec((1, 1, N_EXPERTS), lambda i: (blk(i), 0, 0))]
    route_shapes = [jax.ShapeDtypeStruct((n_tok * TOP_K * ROW_TILE, LANES), F32),
                    jax.ShapeDtypeStruct((n_tok, TOP_K), F32),
                    jax.ShapeDtypeStruct((n_tok, TOP_K), jnp.int32),
                    jax.ShapeDtypeStruct((n_tok_blocks, 1, N_EXPERTS), F32)]

    x1p, xc, gw_p, pos_p, cnt_p, kwin_p, vwin_p, sguv_p = pl.pallas_call(
        functools.partial(_mixer_prompt_body, blocks_per_seq=blocks_per_seq, n_blocks=n_prompt_blocks),
        grid=(n_prompt_blocks + 2,),
        in_specs=[smem, x_spec, _const_spec((D_MODEL, Q_END)),
                  _const_spec((D_MODEL, IN_COLS)), _const_spec((1, D_MODEL)), _const_spec((1, LANES)),
                  _const_spec((1, LANES)), _const_spec((1, MLP_WIDTH)), _const_spec((2 * LANES, LANES)),
                  _const_spec((2 * LANES, LANES)), _const_spec((2, N_HEADS, PAIR, BAND)),
                  _const_spec((MLP_GROUPS, MLP_CHUNK, MLP_CHUNK)), _const_spec((MLP_CHUNK, MLP_WIDTH)),
                  _const_spec((ATTN_WIDTH, D_MODEL)), _const_spec((MLP_WIDTH, D_MODEL)),
                  _const_spec((D_MODEL, D_MODEL)), _const_spec((1, D_MODEL)), _const_spec((D_MODEL, N_EXPERTS)),
                  _const_spec((1, N_EXPERTS)), _const_spec((tb, tb))],
        out_specs=[x_spec, *route_specs(routed_blk),
                   win_spec(KV_WIDTH), win_spec(KV_WIDTH),
                   pl.BlockSpec((1, MLP_CHUNK, MLP_GROUPS, MLP_GROUP_DIM), lambda i: (seq_of(i), 0, 0, 0))],
        out_shape=[jax.ShapeDtypeStruct((n_b, seq, D_MODEL), F32), *route_shapes,
                   jax.ShapeDtypeStruct((n_b, WINDOW, KV_WIDTH), F32),
                   jax.ShapeDtypeStruct((n_b, WINDOW, KV_WIDTH), F32),
                   jax.ShapeDtypeStruct((n_b, MLP_CHUNK, MLP_GROUPS, MLP_GROUP_DIM), F32)],
        scratch_shapes=[pltpu.VMEM((WINDOW, KV_WIDTH), BF16), pltpu.VMEM((WINDOW, KV_WIDTH), BF16),
                        pltpu.VMEM((tb, D_MODEL), BF16)],
        compiler_params=pltpu.CompilerParams(dimension_semantics=("arbitrary",), vmem_limit_bytes=VMEM_LIMIT),
        name="mixer_prompt",
    )(sinks, x_prompt, wq_b, w_in_b, gmix, qg, kg, sgug, seg64, seg128, bias_p, wsp, bsp_p, wba_b, wbs_b, wout_b,
      gffn, wr_b, br, tri)

    full = lambda shape: pl.BlockSpec(shape, lambda i: (0,) * len(shape))
    any_spec = pl.BlockSpec(memory_space=pl.ANY)
    ck = cache_k_win[0].reshape(n_streams, n_cached, KV_WIDTH)
    cv = cache_v_win[0].reshape(n_streams, n_cached, KV_WIDTH)
    x1s, xc, gw_a, pos_a, cnt_a, kwin_s, vwin_s, sguv_s = pl.pallas_call(
        functools.partial(_mixer_sample_body, n_streams=n_streams, n_new=n_new),
        grid=(1,),
        in_specs=[smem, full((n_sample, D_MODEL)), full((n_streams, n_cached, KV_WIDTH)),
                  full((n_streams, n_cached, KV_WIDTH)),
                  full((D_MODEL, Q_END)), full((D_MODEL, IN_COLS)), full((1, D_MODEL)), full((1, LANES)),
                  full((1, LANES)),
                  full((1, MLP_WIDTH)), full((2 * LANES, LANES)), full((2 * LANES, LANES)),
                  full((N_HEADS * n_new, n_cached)), full((N_HEADS * n_new, n_new)),
                  full((MLP_GROUPS, n_new, n_new)), full((n_sample, MLP_WIDTH)),
                  full((ATTN_WIDTH, D_MODEL)), full((MLP_WIDTH, D_MODEL)), full((D_MODEL, D_MODEL)),
                  full((1, D_MODEL)), full((D_MODEL, N_EXPERTS)), full((1, N_EXPERTS)), full((tb, tb)),
                  any_spec, any_spec, any_spec, any_spec],
        out_specs=[full((n_sample, D_MODEL)), *route_specs(lambda i: n_prompt_blocks),
                   full((n_streams, n_cached, KV_WIDTH)), full((n_streams, n_cached, KV_WIDTH)),
                   full((n_sample, MLP_GROUPS, MLP_GROUP_DIM))],
        out_shape=[jax.ShapeDtypeStruct((n_sample, D_MODEL), F32), *route_shapes,
                   jax.ShapeDtypeStruct((n_streams, n_cached, KV_WIDTH), F32),
                   jax.ShapeDtypeStruct((n_streams, n_cached, KV_WIDTH), F32),
                   jax.ShapeDtypeStruct((n_sample, MLP_GROUPS, MLP_GROUP_DIM), F32)],
        input_output_aliases={23: 1, 24: 2, 25: 3, 26: 4},
        compiler_params=pltpu.CompilerParams(dimension_semantics=("arbitrary",), vmem_limit_bytes=VMEM_LIMIT),
        name="mixer_sample",
    )(sinks, x_sample.reshape(n_sample, D_MODEL), ck, cv, wq_b, w_in_b, gmix, qg, kg, sgug, seg64, seg128, bias_c,
      bias_n, wsp_s, bsp_s, wba_b, wbs_b, wout_b, gffn, wr_b, br, tri, xc, gw_p, pos_p, cnt_p)

    tm = EXPERT_TILE
    counts = cnt_a[:, 0, :].astype(jnp.int32)
    cnt_all = jnp.sum(counts, axis=0)
    padded = ((cnt_all + tm - 1) // tm) * tm
    pad_end = jnp.cumsum(padded)
    pad_off = pad_end - padded
    n_pairs = TOP_K * n_tok
    n_tiles = n_pairs // tm + N_EXPERTS
    n_steps = n_tiles + 1
    in_block = jnp.cumsum(counts, axis=1) - counts
    in_expert = jnp.cumsum(counts, axis=0) - counts
    run_src = (jnp.arange(n_tok_blocks, dtype=jnp.int32)[:, None] * pb + in_block).T.reshape(-1)
    run_start = (pad_off[None, :] + in_expert).T.reshape(-1)
    run_len = counts.T.reshape(-1)
    head = jnp.minimum(run_len, tm - run_start % tm)
    piece_start = jnp.stack([run_start, run_start + head], axis=1).reshape(-1)
    piece_src = jnp.stack([run_src, run_src + head], axis=1).reshape(-1)
    piece_len = jnp.stack([head, run_len - head], axis=1).reshape(-1)
    piece_tile = piece_start // tm
    tile_ids = jnp.arange(n_steps + 1, dtype=jnp.int32)
    piece_lo = jnp.sum((piece_tile[None, :] < tile_ids[:, None]).astype(jnp.int32), axis=1)
    n_used = (pad_end[-1] // tm).astype(jnp.int32)
    tile_start = jnp.minimum(tile_ids[:n_steps], n_used - 1) * tm
    tile_expert = jnp.sum((pad_end[None, :] <= tile_start[:, None]).astype(jnp.int32), axis=1)
    tile_expert = jnp.minimum(tile_expert, N_EXPERTS - 1)
    of_tile = tile_expert[:, None] == jnp.arange(N_EXPERTS, dtype=jnp.int32)[None, :]
    rows_end = jnp.sum(jnp.where(of_tile, (pad_off + cnt_all)[None, :], 0), axis=1)
    run_end = jnp.sum(jnp.where(of_tile, pad_end[None, :], 0), axis=1)
    tile_valid = jnp.clip(rows_end - tile_start, 0, tm)
    next_expert = jnp.sum((pad_end[None, :] <= run_end[:, None]).astype(jnp.int32), axis=1)
    next_expert = jnp.where(run_end < pad_end[-1], jnp.minimum(next_expert, N_EXPERTS - 1), -1)

    prefetch = (tile_expert, next_expert, n_used.reshape(1), tile_valid * ROW_TILE, piece_lo,
                piece_src * ROW_TILE, (piece_start % tm) * ROW_TILE, piece_len * ROW_TILE)
    yc = pl.pallas_call(
        _expert_body,
        grid_spec=pltpu.PrefetchScalarGridSpec(
            num_scalar_prefetch=len(prefetch),
            grid=(n_steps,),
            in_specs=[any_spec, any_spec,
                      pl.BlockSpec((1, 1, 2 * D_EXPERT), lambda i, te, *_: (te[i], 0, 0)),
                      any_spec,
                      pl.BlockSpec((1, 1, D_MODEL), lambda i, te, *_: (te[i], 0, 0))],
            out_specs=any_spec,
            scratch_shapes=[pltpu.VMEM((tm * ROW_TILE, LANES), F32), pltpu.VMEM((tm * ROW_TILE, LANES), F32),
                            pltpu.VMEM((tm * ROW_TILE, LANES), F32), pltpu.VMEM((tm * ROW_TILE, LANES), F32),
                            pltpu.VMEM((D_MODEL, 2 * D_EXPERT), F32), pltpu.VMEM((D_EXPERT, D_MODEL), F32),
                            pltpu.VMEM((D_MODEL, 2 * D_EXPERT), BF16), pltpu.VMEM((D_EXPERT, D_MODEL), BF16),
                            pltpu.SemaphoreType.DMA((2,)), pltpu.SemaphoreType.DMA(()),
                            pltpu.SemaphoreType.DMA((2,))]),
        out_shape=jax.ShapeDtypeStruct((n_pairs * ROW_TILE, LANES), F32),
        compiler_params=pltpu.CompilerParams(dimension_semantics=("arbitrary",), vmem_limit_bytes=VMEM_LIMIT),
        name="moe_experts",
    )(*[p.astype(jnp.int32) for p in prefetch], xc, w_gate_up[0],
      b_gate_up[0].reshape(N_EXPERTS, 1, 2 * D_EXPERT), w_down[0], b_down[0].reshape(N_EXPERTS, 1, D_MODEL))

    pos_tiles = (pos_a * ROW_TILE).reshape(n_tok_blocks, pb // LANES, LANES)
    gw_tiles = gw_a.reshape(n_tok_blocks, pb // LANES, LANES)

    def combine(x1, first_block, n_blocks):
        scalars = pl.BlockSpec((1, pb // LANES, LANES), lambda i: (first_block + i, 0, 0), memory_space=pltpu.SMEM)
        return pl.pallas_call(
            _combine_body,
            grid=(n_blocks,),
            in_specs=[scalars, scalars,
                      pl.BlockSpec((tb, D_MODEL), lambda i: (i, 0)),
                      pl.BlockSpec((pb * ROW_TILE, LANES), lambda i: (first_block + i, 0))],
            out_specs=pl.BlockSpec((tb, D_MODEL), lambda i: (i, 0)),
            out_shape=jax.ShapeDtypeStruct(x1.shape, F32),
            scratch_shapes=[pltpu.VMEM((tb * ROW_TILE, LANES), F32)],
            compiler_params=pltpu.CompilerParams(dimension_semantics=("arbitrary",), vmem_limit_bytes=VMEM_LIMIT),
            name="moe_combine",
        )(pos_tiles, gw_tiles, x1, yc)

    y_prompt = combine(x1p.reshape(n_prompt, D_MODEL), 0, n_prompt_blocks).reshape(n_b, seq, D_MODEL)
    y_sample = combine(x1s, n_prompt_blocks, 1).reshape(n_streams, n_new, D_MODEL)

    kv_shape = (N_KV_HEADS, HEAD_DIM)
    sg_shape = (MLP_GROUPS, MLP_GROUP_DIM)
    return (y_prompt, y_sample,
            kwin_p.reshape(1, n_b, WINDOW, *kv_shape), vwin_p.reshape(1, n_b, WINDOW, *kv_shape),
            kwin_s.reshape(1, n_streams, n_cached, *kv_shape), vwin_s.reshape(1, n_streams, n_cached, *kv_shape),
            sguv_p.reshape(1, n_b, MLP_CHUNK, *sg_shape), sguv_s.reshape(1, n_streams, n_new, *sg_shape))
```

```python
import functools

import jax
import jax.numpy as jnp
import numpy as np
from jax import lax
from jax.experimental import pallas as pl
from jax.experimental.pallas import tpu as pltpu

D_MODEL = 1024
CHUNK = 64
WINDOW = 128
HEAD_DIM = 64
N_HEADS = 8
N_KV_HEADS = 2
Q_PER_KV = N_HEADS // N_KV_HEADS
ATTN_WIDTH = N_HEADS * HEAD_DIM
KV_WIDTH = N_KV_HEADS * HEAD_DIM
ATTN_SCALE = HEAD_DIM ** -0.5
MLP_CHUNK = 128
MLP_GROUPS = 8
MLP_WIDTH = D_MODEL
MLP_GROUP_DIM = MLP_WIDTH // MLP_GROUPS
N_EXPERTS = 32
TOP_K = 4
D_EXPERT = D_MODEL
SWIGLU_ALPHA = 1.702
SWIGLU_LIMIT = 7.0
EPS = 1e-6
NEG_INF = -1e30
Q_END = ATTN_WIDTH
K_END = Q_END + KV_WIDTH
V_END = K_END + KV_WIDTH
U_END = V_END + MLP_WIDTH
VM_END = U_END + MLP_WIDTH
GA_END = VM_END + D_MODEL
IN_COLS = GA_END + D_MODEL

LANES = 128
ROW_TILE = D_MODEL // LANES
TOKEN_BLOCK = 256
PAIR = 2 * CHUNK
BAND = PAIR + WINDOW
EXPERT_TILE = 512
VMEM_LIMIT = 56 * 1024 * 1024

F32 = jnp.float32
BF16 = jnp.bfloat16


def _dot(a, b):
    return jnp.dot(a, b, preferred_element_type=F32)


def _dot_nt(a, b):
    return lax.dot_general(a, b, (((1,), (1,)), ((), ())), preferred_element_type=F32)


def _store_token_major(ref, val):
    rows = val.shape[0]
    for s in range(ROW_TILE):
        ref[pl.ds(s, rows, stride=ROW_TILE), :] = val[:, s * LANES:(s + 1) * LANES]


def _load_token_major(ref, rows):
    return jnp.concatenate([ref[pl.ds(s, rows, stride=ROW_TILE), :] for s in range(ROW_TILE)], axis=-1)


def _segment_mean(sq, seg):
    hi = sq.astype(BF16)
    lo = (sq - hi.astype(F32)).astype(BF16)
    return _dot(jnp.concatenate([hi, lo], axis=-1), seg)


def _rms_rows(x, gain):
    ms = jnp.mean(x * x, axis=-1, keepdims=True)
    return (x * lax.rsqrt(ms + EPS)) * gain


def _project(h, wq_ref, w_in_ref, qg, kg, sgug, seg64, seg128, between=None):
    qkv = jnp.concatenate([_dot(h, wq_ref[...]), _dot(h, w_in_ref[:, Q_END:V_END])], axis=-1)
    qk_cols = []
    for c in range(K_END // LANES):
        blk = qkv[:, c * LANES:(c + 1) * LANES]
        ms = _segment_mean(blk * blk, seg64)
        g = qg if c < Q_END // LANES else kg
        qk_cols.append((blk * lax.rsqrt(ms + EPS)) * g)
    qn = jnp.concatenate(qk_cols[:Q_END // LANES], axis=-1)
    kn = qk_cols[Q_END // LANES]
    v = qkv[:, K_END:V_END]
    u = jax.nn.gelu(_dot(h, w_in_ref[:, V_END:U_END]))
    vg = jax.nn.gelu(_dot(h, w_in_ref[:, U_END:VM_END]))
    vm_cols = []
    for g in range(MLP_GROUPS):
        blk = vg[:, g * LANES:(g + 1) * LANES]
        ms = _segment_mean(blk * blk, seg128)
        vm_cols.append((blk * lax.rsqrt(ms + EPS)) * sgug[:, g * LANES:(g + 1) * LANES])
    vm = jnp.concatenate(vm_cols, axis=-1)
    if between is not None:
        between()
    ga = jax.nn.sigmoid(_dot(h, w_in_ref[:, VM_END:GA_END]))
    gb = jax.nn.sigmoid(_dot(h, w_in_ref[:, GA_END:IN_COLS]))
    return qn, kn, v, u, vm, ga, gb


def _stack_heads(q_rows):
    lane = lax.broadcasted_iota(jnp.int32, (q_rows.shape[0], LANES), 1)
    blocks = []
    for head in range(N_HEADS):
        j, half = head % Q_PER_KV, head // Q_PER_KV
        col = q_rows[:, j * LANES:(j + 1) * LANES]
        keep = (lane < HEAD_DIM) if half == 0 else (lane >= HEAD_DIM)
        blocks.append(jnp.where(keep, col, 0.0))
    return jnp.concatenate(blocks, axis=0).astype(BF16)


def _unstack_heads(o, rows):
    lane = lax.broadcasted_iota(jnp.int32, (rows, LANES), 1)
    cols = []
    for j in range(Q_PER_KV):
        lo = o[j * rows:(j + 1) * rows]
        hi = o[(j + Q_PER_KV) * rows:(j + Q_PER_KV + 1) * rows]
        cols.append(jnp.where(lane < HEAD_DIM, lo, hi))
    return jnp.concatenate(cols, axis=-1)


def _merge(att, sgu, ga, gb, wba_ref, wbs_ref, wout_ref):
    m = ga * _dot(att.astype(BF16), wba_ref[...]) + gb * _dot(sgu.astype(BF16), wbs_ref[...])
    return _dot(m.astype(BF16), wout_ref[...])


def _route(h2, wr_ref, br_ref, tri_ref):
    rows = h2.shape[0]
    logits = _dot(h2, wr_ref[...]) + br_ref[...]
    eidx = lax.broadcasted_iota(jnp.int32, (rows, N_EXPERTS), 1).astype(F32)
    work = logits
    vals, picks, onehots = [], [], []
    for _ in range(TOP_K):
        m = jnp.max(work, axis=-1, keepdims=True)
        sel = jnp.min(jnp.where(work == m, eidx, float(N_EXPERTS)), axis=-1, keepdims=True)
        oh = eidx == sel
        vals.append(m)
        picks.append(sel)
        onehots.append(oh)
        work = jnp.where(oh, -jnp.inf, work)
    exps = [jnp.exp(v - vals[0]) for v in vals]
    den = exps[0] + exps[1] + exps[2] + exps[3]
    mask = jnp.zeros((rows, N_EXPERTS), F32)
    for oh in onehots:
        mask = mask + jnp.where(oh, 1.0, 0.0)
    before = _dot(tri_ref[...], mask.astype(BF16))
    counts = jnp.sum(mask, axis=0, keepdims=True)
    k4 = lax.broadcasted_iota(jnp.int32, (rows, TOP_K), 1)
    w4 = jnp.zeros((rows, TOP_K), F32)
    pos4 = jnp.zeros((rows, TOP_K), jnp.int32)
    for k in range(TOP_K):
        lower_experts = jnp.sum(jnp.where(eidx < picks[k], counts, 0.0), axis=-1, keepdims=True)
        rank = jnp.sum(jnp.where(onehots[k], before, 0.0), axis=-1, keepdims=True)
        w4 = jnp.where(k4 == k, exps[k] / den, w4)
        pos4 = jnp.where(k4 == k, (lower_experts + rank).astype(jnp.int32), pos4)
    return w4, pos4, counts


def _pair_selector(pos4, values=None):
    rows = pos4.shape[0]
    col = lax.broadcasted_iota(jnp.int32, (rows, TOP_K * rows), 1)
    sel = jnp.zeros((rows, TOP_K * rows), F32)
    for k in range(TOP_K):
        sel = sel + jnp.where(col == pos4[:, k:k + 1], 1.0 if values is None else values[:, k:k + 1], 0.0)
    return sel


def _route_block(h2, wr_ref, br_ref, tri_ref, gw_ref, pos_ref, cnt_ref):
    w4, pos4, counts = _route(h2, wr_ref, br_ref, tri_ref)
    gw_ref[...] = w4
    pos_ref[...] = pos4 * ROW_TILE
    cnt_ref[0] = counts
    return pos4


def _dispatch_block(pos4, h2, xc_ref):
    select = _pair_selector(pos4).astype(BF16)
    pairs = lax.dot_general(select, h2, (((0,), (0,)), ((), ())), preferred_element_type=F32)
    _store_token_major(xc_ref, pairs)


def _route_and_dispatch(h2, wr_ref, br_ref, tri_ref, gw_ref, pos_ref, cnt_ref, xc_ref):
    _dispatch_block(_route_block(h2, wr_ref, br_ref, tri_ref, gw_ref, pos_ref, cnt_ref), h2, xc_ref)


def _softmax_rows(parts, sink):
    m = sink
    for l in parts:
        m = jnp.maximum(m, jnp.max(l, axis=-1, keepdims=True))
    es = [jnp.exp(l - m) for l in parts]
    den = jnp.exp(sink - m)
    for e in es:
        den = den + jnp.sum(e, axis=-1, keepdims=True)
    return [e / den for e in es]


def _mixer_prompt_body(sinks_ref, x_ref, wq_ref, w_in_ref, gmix_ref, qg_ref, kg_ref, sgug_ref, seg64_ref, seg128_ref,
                       bias_ref, wsp_ref, bsp_ref, wba_ref, wbs_ref, wout_ref, gffn_ref, wr_ref, br_ref, tri_ref,
                       x1_ref, xc_ref, gw_ref, pos_ref, cnt_ref, kwin_ref, vwin_ref, sguv_ref,
                       kcarry, vcarry, h2_prev_ref, *, blocks_per_seq, n_blocks):
    i = pl.program_id(0)

    @pl.when(i == 0)
    def _():
        h2_prev_ref[...] = jnp.zeros_like(h2_prev_ref)

    def route_previous():
        return _route_block(h2_prev_ref[...], wr_ref, br_ref, tri_ref, gw_ref, pos_ref, cnt_ref)

    def dispatch_previous(pos4):
        _dispatch_block(pos4, h2_prev_ref[...], xc_ref)

    @pl.when(i == n_blocks)
    def _():
        dispatch_previous(route_previous())

    @pl.when(i == n_blocks + 1)
    def _():
        xc_ref[...] = jnp.zeros_like(xc_ref)
        gw_ref[...] = jnp.zeros_like(gw_ref)
        pos_ref[...] = jnp.zeros_like(pos_ref)
        cnt_ref[...] = jnp.zeros_like(cnt_ref)

    @pl.when(jnp.logical_and(i < n_blocks, i % blocks_per_seq == 0))
    def _():
        kcarry[...] = jnp.zeros_like(kcarry)
        vcarry[...] = jnp.zeros_like(vcarry)

    @pl.when(i < n_blocks)
    def _():
        _mixer_prompt_block(i % blocks_per_seq, sinks_ref, x_ref, wq_ref, w_in_ref, gmix_ref, qg_ref, kg_ref, sgug_ref,
                            seg64_ref, seg128_ref, bias_ref, wsp_ref, bsp_ref, wba_ref, wbs_ref, wout_ref, gffn_ref,
                            x1_ref, kwin_ref, vwin_ref, sguv_ref, kcarry, vcarry, h2_prev_ref,
                            route_previous, dispatch_previous)


def _mixer_prompt_block(j, sinks_ref, x_ref, wq_ref, w_in_ref, gmix_ref, qg_ref, kg_ref, sgug_ref, seg64_ref, seg128_ref,
                        bias_ref, wsp_ref, bsp_ref, wba_ref, wbs_ref, wout_ref, gffn_ref,
                        x1_ref, kwin_ref, vwin_ref, sguv_ref, kcarry, vcarry, h2_prev_ref,
                        route_previous, dispatch_previous):
    routed = []
    x = x_ref[0]
    h = _rms_rows(x, gmix_ref[...]).astype(BF16)
    qn, kn, v, u, vm, ga, gb = _project(h, wq_ref, w_in_ref, qg_ref[...], kg_ref[...], sgug_ref[...],
                                        seg64_ref[...], seg128_ref[...],
                                        between=lambda: routed.append(route_previous()))
    k_ext = jnp.concatenate([kcarry[...], kn.astype(BF16)], axis=0)
    v_ext = jnp.concatenate([vcarry[...], v.astype(BF16)], axis=0)
    kcarry[...] = k_ext[TOKEN_BLOCK:]
    vcarry[...] = v_ext[TOKEN_BLOCK:]

    tri_mask = (lax.broadcasted_iota(jnp.int32, (MLP_CHUNK, MLP_CHUNK), 0)
                >= lax.broadcasted_iota(jnp.int32, (MLP_CHUNK, MLP_CHUNK), 1))
    att_rows, sgu_rows = [], []
    for pm in range(TOKEN_BLOCK // PAIR):
        r0 = pm * PAIR
        q_stack = _stack_heads(qn[r0:r0 + PAIR])
        k_band = k_ext[r0:r0 + BAND]
        v_band = v_ext[r0:r0 + BAND]
        s = _dot_nt(q_stack, k_band)
        first = jnp.where(j == 0, 0, 1) if pm == 0 else 1
        probs = []
        for head in range(N_HEADS):
            logit = s[head * PAIR:(head + 1) * PAIR] * ATTN_SCALE + bias_ref[first, head]
            probs.append(_softmax_rows([logit], sinks_ref[head])[0].astype(BF16))
        o = _dot(jnp.concatenate(probs, axis=0), v_band)
        att_rows.append(_unstack_heads(o, PAIR))
        cols = []
        for g in range(MLP_GROUPS):
            wm = jnp.where(tri_mask, wsp_ref[g], 0.0).astype(BF16)
            cols.append(_dot(wm, vm[r0:r0 + PAIR, g * LANES:(g + 1) * LANES].astype(BF16)))
        mixed = jnp.concatenate(cols, axis=-1) + bsp_ref[...]
        sgu_rows.append(u[r0:r0 + PAIR] * mixed)
        if pm == 0:
            dispatch_previous(routed[0])
    att = jnp.concatenate(att_rows, axis=0)
    sgu = jnp.concatenate(sgu_rows, axis=0)

    x1 = x + _merge(att, sgu, ga, gb, wba_ref, wbs_ref, wout_ref)
    h2 = _rms_rows(x1, gffn_ref[...])
    h2_prev_ref[...] = h2.astype(BF16)

    x1_ref[0] = x1
    kwin_ref[0] = kn[TOKEN_BLOCK - WINDOW:]
    vwin_ref[0] = v[TOKEN_BLOCK - WINDOW:]
    for g in range(MLP_GROUPS):
        sguv_ref[0, :, g, :] = vm[TOKEN_BLOCK - MLP_CHUNK:, g * LANES:(g + 1) * LANES]


def _mixer_sample_body(sinks_ref, x_ref, ck_ref, cv_ref, wq_ref, w_in_ref, gmix_ref, qg_ref, kg_ref, sgug_ref, seg64_ref,
                       seg128_ref, biasc_ref, biasn_ref, wsp_ref, bsp_ref, wba_ref, wbs_ref, wout_ref, gffn_ref,
                       wr_ref, br_ref, tri_ref, xc_in_ref, gw_in_ref, pos_in_ref, cnt_in_ref,
                       x1_ref, xc_ref, gw_ref, pos_ref, cnt_ref, kwin_ref, vwin_ref, sguv_ref, *, n_streams, n_new):
    del xc_in_ref, gw_in_ref, pos_in_ref, cnt_in_ref
    x = x_ref[...]
    h = _rms_rows(x, gmix_ref[...]).astype(BF16)
    qn, kn, v, u, vm, ga, gb = _project(h, wq_ref, w_in_ref, qg_ref[...], kg_ref[...], sgug_ref[...],
                                        seg64_ref[...], seg128_ref[...])
    n_cached = ck_ref.shape[1]
    att_rows = []
    for s_i in range(n_streams):
        r0 = s_i * n_new
        q_stack = _stack_heads(qn[r0:r0 + n_new])
        k_new = kn[r0:r0 + n_new]
        v_new = v[r0:r0 + n_new]
        s_c = _dot_nt(q_stack, ck_ref[s_i].astype(BF16))
        s_n = _dot_nt(q_stack, k_new.astype(BF16))
        pc, pn = [], []
        for head in range(N_HEADS):
            rows = slice(head * n_new, (head + 1) * n_new)
            lc = s_c[rows] * ATTN_SCALE + biasc_ref[rows]
            ln = s_n[rows] * ATTN_SCALE + biasn_ref[rows]
            p_c, p_n = _softmax_rows([lc, ln], sinks_ref[head])
            pc.append(p_c.astype(BF16))
            pn.append(p_n.astype(BF16))
        o = (_dot(jnp.concatenate(pc, axis=0), cv_ref[s_i].astype(BF16))
             + _dot(jnp.concatenate(pn, axis=0), v_new.astype(BF16)))
        att_rows.append(_unstack_heads(o, n_new))
        kwin_ref[s_i, 0:n_cached - n_new] = ck_ref[s_i, n_new:n_cached]
        kwin_ref[s_i, n_cached - n_new:n_cached] = k_new
        vwin_ref[s_i, 0:n_cached - n_new] = cv_ref[s_i, n_new:n_cached]
        vwin_ref[s_i, n_cached - n_new:n_cached] = v_new
    att = jnp.concatenate(att_rows, axis=0)

    rows = n_streams * n_new
    ri = lax.broadcasted_iota(jnp.int32, (rows, rows), 0)
    ci = lax.broadcasted_iota(jnp.int32, (rows, rows), 1)
    keep = jnp.logical_and(ri // n_new == ci // n_new, ri % n_new >= ci % n_new)
    expand = (lax.broadcasted_iota(jnp.int32, (rows, n_new), 0) % n_new
              == lax.broadcasted_iota(jnp.int32, (rows, n_new), 1)).astype(BF16)
    cols = []
    for g in range(MLP_GROUPS):
        tiled = _dot_nt(_dot(expand, wsp_ref[g].astype(BF16)).astype(BF16), expand)
        wm = jnp.where(keep, tiled, 0.0).astype(BF16)
        cols.append(_dot(wm, vm[:, g * LANES:(g + 1) * LANES].astype(BF16)))
    sgu = u * (jnp.concatenate(cols, axis=-1) + bsp_ref[...])

    x1 = x + _merge(att, sgu, ga, gb, wba_ref, wbs_ref, wout_ref)
    h2 = _rms_rows(x1, gffn_ref[...])
    _route_and_dispatch(h2.astype(BF16), wr_ref, br_ref, tri_ref, gw_ref, pos_ref, cnt_ref, xc_ref)
    x1_ref[...] = x1
    for g in range(MLP_GROUPS):
        sguv_ref[:, g, :] = vm[:, g * LANES:(g + 1) * LANES]


def _piece_copies(i_tile, lo_ref, src_ref, off_ref, len_ref, hbm, buf, sem, to_hbm):
    def one(p, carry):
        n = pl.multiple_of(len_ref[p], ROW_TILE)

        @pl.when(n > 0)
        def _():
            far = hbm.at[pl.ds(pl.multiple_of(src_ref[p], ROW_TILE), n), :]
            near = buf.at[pl.ds(pl.multiple_of(off_ref[p], ROW_TILE), n), :]
            if to_hbm:
                pltpu.make_async_copy(near, far, sem).start()
            else:
                pltpu.make_async_copy(far, near, sem).start()
        return carry
    lax.fori_loop(lo_ref[i_tile], lo_ref[i_tile + 1], one, 0)


def _wait_rows(hbm, buf, sem, n_rows):
    n = pl.multiple_of(n_rows, ROW_TILE)
    pltpu.make_async_copy(hbm.at[pl.ds(0, n), :], buf.at[pl.ds(0, n), :], sem).wait()


def _expert_mlp(x_cur, y_cur, bgu_ref, bd_ref, wgu_bf, wd_bf):
    xb = _load_token_major(x_cur, EXPERT_TILE).astype(BF16)
    hgu = _dot(xb, wgu_bf[...]) + bgu_ref[0]
    glu = jnp.minimum(hgu[:, :D_EXPERT], SWIGLU_LIMIT)
    lin = jnp.clip(hgu[:, D_EXPERT:], -SWIGLU_LIMIT, SWIGLU_LIMIT)
    act = glu * jax.nn.sigmoid(glu * SWIGLU_ALPHA) * (lin + 1.0)
    _store_token_major(y_cur, _dot(act.astype(BF16), wd_bf[...]) + bd_ref[0])


def _weight_copies(wgu_hbm, wd_hbm, wgu_stage, wd_stage, sem_w, expert):
    return (pltpu.make_async_copy(wgu_hbm.at[expert], wgu_stage, sem_w.at[0]),
            pltpu.make_async_copy(wd_hbm.at[expert], wd_stage, sem_w.at[1]))


def _expert_body(te_ref, nxt_ref, nused_ref, valid_ref, lo_ref, src_ref, off_ref, len_ref, xc_hbm, wgu_hbm, bgu_ref,
                 wd_hbm, bd_ref, yc_hbm, xbuf0, xbuf1, ybuf0, ybuf1, wgu_stage, wd_stage, wgu_bf, wd_bf,
                 sem_in, sem_out, sem_w):
    i = pl.program_id(0)
    n_used = nused_ref[0]
    xbufs, ybufs = (xbuf0, xbuf1), (ybuf0, ybuf1)
    pieces = (lo_ref, src_ref, off_ref, len_ref)

    @pl.when(i == 0)
    def _():
        xbuf0[...] = jnp.zeros_like(xbuf0)
        xbuf1[...] = jnp.zeros_like(xbuf1)
        for cp in _weight_copies(wgu_hbm, wd_hbm, wgu_stage, wd_stage, sem_w, te_ref[0]):
            cp.start(priority=1)
        _piece_copies(0, *pieces, xc_hbm, xbuf0, sem_in.at[0], to_hbm=False)

    @pl.when(jnp.logical_and(i < n_used, jnp.logical_or(i == 0, te_ref[i] != te_ref[jnp.maximum(i - 1, 0)])))
    def _():
        for cp in _weight_copies(wgu_hbm, wd_hbm, wgu_stage, wd_stage, sem_w, te_ref[i]):
            cp.wait()
        wgu_bf[...] = wgu_stage[...].astype(BF16)
        wd_bf[...] = wd_stage[...].astype(BF16)

        @pl.when(nxt_ref[i] >= 0)
        def _():
            for cp in _weight_copies(wgu_hbm, wd_hbm, wgu_stage, wd_stage, sem_w, nxt_ref[i]):
                cp.start(priority=1)

    for par in range(2):
        x_cur, x_next, y_cur, y_prev = xbufs[par], xbufs[1 - par], ybufs[par], ybufs[1 - par]

        @pl.when(jnp.logical_and(i < n_used, i % 2 == par))
        def _(x_cur=x_cur, x_next=x_next, y_cur=y_cur, y_prev=y_prev, par=par):
            _wait_rows(xc_hbm, x_cur, sem_in.at[par], valid_ref[i])

            @pl.when(i + 1 < n_used)
            def _():
                _piece_copies(i + 1, *pieces, xc_hbm, x_next, sem_in.at[1 - par], to_hbm=False)

            @pl.when(i > 0)
            def _():
                _piece_copies(i - 1, *pieces, yc_hbm, y_prev, sem_out, to_hbm=True)

            _expert_mlp(x_cur, y_cur, bgu_ref, bd_ref, wgu_bf, wd_bf)

            @pl.when(i > 0)
            def _():
                _wait_rows(yc_hbm, y_prev, sem_out, valid_ref[i - 1])

        @pl.when(jnp.logical_and(i == n_used, i % 2 == par))
        def _(y_prev=y_prev):
            _piece_copies(i - 1, *pieces, yc_hbm, y_prev, sem_out, to_hbm=True)
            _wait_rows(yc_hbm, y_prev, sem_out, valid_ref[i - 1])


def _combine_body(pos_ref, gw_ref, x1_ref, yc_ref, out_ref, acc_ref):
    rows = x1_ref.shape[0]
    _store_token_major(acc_ref, x1_ref[...])
    for t in range(rows):
        row = acc_ref[pl.ds(t * ROW_TILE, ROW_TILE), :]
        for k in range(TOP_K):
            at = pl.multiple_of(pos_ref[t, k], ROW_TILE)
            row = row + gw_ref[t, k] * yc_ref[pl.ds(at, ROW_TILE), :]
        acc_ref[pl.ds(t * ROW_TILE, ROW_TILE), :] = row
    out_ref[...] = _load_token_major(acc_ref, rows)


def _const_spec(shape):
    nd = len(shape)
    return pl.BlockSpec(shape, lambda *_: (0,) * nd, pipeline_mode=pl.Buffered(1))


def _q_perm():
    cols = np.arange(ATTN_WIDTH)
    j, half, d = cols // LANES, (cols % LANES) // HEAD_DIM, cols % HEAD_DIM
    return (j + Q_PER_KV * half) * HEAD_DIM + d


def _alibi_slopes():
    return 2.0 ** (-8.0 * np.arange(1, N_HEADS + 1) / N_HEADS)


def _prompt_bias():
    qi = np.arange(PAIR)[:, None]
    kj = np.arange(BAND)[None, :]
    dist = np.abs(qi + WINDOW - kj).astype(np.float64)
    cq, ck = qi // CHUNK, kj // CHUNK
    in_band = (ck >= cq) & (ck <= cq + WINDOW // CHUNK)
    base = -_alibi_slopes()[:, None, None] * dist[None]
    later = np.where(in_band[None], base, NEG_INF)
    first = np.where((kj >= WINDOW)[None], later, NEG_INF)
    return np.stack([first, later]).astype(np.float32)


def _sample_bias(n_new, n_cached):
    qi = np.arange(n_new)[:, None]
    dc = np.abs(qi + n_cached - np.arange(n_cached)[None, :]).astype(np.float64)
    dn = np.abs(qi - np.arange(n_new)[None, :]).astype(np.float64)
    sl = _alibi_slopes()[:, None, None]
    bc = (-sl * dc[None]).reshape(N_HEADS * n_new, n_cached)
    bn = (-sl * dn[None]).reshape(N_HEADS * n_new, n_new)
    return bc.astype(np.float32), bn.astype(np.float32)


def kernel(x_prompt, x_sample, cache_k_win, cache_v_win, g_mix, w_in, q_norm_g, k_norm_g, attn_sinks, sgu_norm_g,
           w_spatial, b_spatial, w_branch_attn, w_branch_sgu, w_out, g_ffn, w_router, b_router, w_gate_up,
           b_gate_up, w_down, b_down):
    n_b, seq, _ = x_prompt.shape
    n_streams, n_new, _ = x_sample.shape
    n_cached = cache_k_win.shape[2]
    n_prompt = n_b * seq
    n_sample = n_streams * n_new
    n_tok = n_prompt + n_sample
    assert seq % TOKEN_BLOCK == 0 and n_sample == TOKEN_BLOCK and n_cached == WINDOW
    blocks_per_seq = seq // TOKEN_BLOCK
    n_prompt_blocks = n_prompt // TOKEN_BLOCK

    perm = _q_perm()
    w_in_l = w_in[0]
    w_in_b = w_in_l.astype(BF16)
    halves = N_HEADS // Q_PER_KV
    wq_b = (w_in_l[:, :Q_END].reshape(D_MODEL, halves, Q_PER_KV, HEAD_DIM).transpose(0, 2, 1, 3)
            .reshape(D_MODEL, Q_END).astype(BF16))
    wba_b = w_branch_attn[0][perm, :].astype(BF16)
    wbs_b = w_branch_sgu[0].astype(BF16)
    wout_b = w_out[0].astype(BF16)
    wr_b = w_router[0].astype(BF16)
    br = b_router[0].reshape(1, N_EXPERTS).astype(F32)
    gmix = g_mix[0].reshape(1, D_MODEL)
    gffn = g_ffn[0].reshape(1, D_MODEL)
    qg = jnp.tile(q_norm_g[0], LANES // HEAD_DIM).reshape(1, LANES)
    kg = jnp.tile(k_norm_g[0], LANES // HEAD_DIM).reshape(1, LANES)
    sgug = sgu_norm_g[0].reshape(1, MLP_WIDTH)
    sinks = attn_sinks[0].astype(F32)
    lane_seg = np.arange(LANES) // HEAD_DIM
    seg64 = jnp.asarray(np.tile((lane_seg[:, None] == lane_seg[None, :]) / HEAD_DIM, (2, 1)), BF16)
    seg128 = jnp.full((2 * LANES, LANES), 1.0 / MLP_GROUP_DIM, BF16)
    tri = jnp.asarray(np.tril(np.ones((TOKEN_BLOCK, TOKEN_BLOCK)), -1), BF16)
    wsp = w_spatial[0]
    bsp_p = jnp.repeat(b_spatial[0].T, MLP_GROUP_DIM, axis=1)
    wsp_s = wsp[:, :n_new, :n_new]
    bsp_s = jnp.tile(jnp.repeat(b_spatial[0][:, :n_new].T, MLP_GROUP_DIM, axis=1), (n_streams, 1))
    bias_p = jnp.asarray(_prompt_bias())
    bias_c, bias_n = (jnp.asarray(a) for a in _sample_bias(n_new, n_cached))

    smem = pl.BlockSpec(memory_space=pltpu.SMEM)
    tb = TOKEN_BLOCK
    pb = TOP_K * tb
    n_tok_blocks = n_tok // tb
    last_blk = n_prompt_blocks - 1
    seq_of = lambda i: jnp.minimum(i, last_blk) // blocks_per_seq
    x_spec = pl.BlockSpec((1, tb, D_MODEL), lambda i: (seq_of(i), jnp.minimum(i, last_blk) % blocks_per_seq, 0))
    routed_blk = lambda i: jnp.where(i > n_prompt_blocks, n_prompt_blocks, jnp.clip(i - 1, 0, last_blk))
    win_spec = lambda width: pl.BlockSpec((1, WINDOW, width), lambda i: (seq_of(i), 0, 0))
    route_specs = lambda blk: [pl.BlockSpec((pb * ROW_TILE, LANES), lambda i: (blk(i), 0)),
                               pl.BlockSpec((tb, TOP_K), lambda i: (blk(i), 0)),
                               pl.BlockSpec((tb, TOP_K), lambda i: (blk(i), 0)),
                               pl.BlockSpec((1, 1, N_EXPERTS), lambda i: (blk(i), 0, 0))]
    route_shapes = [jax.ShapeDtypeStruct((n_tok * TOP_K * ROW_TILE, LANES), F32),
                    jax.ShapeDtypeStruct((n_tok, TOP_K), F32),
                    jax.ShapeDtypeStruct((n_tok, TOP_K), jnp.int32),
                    jax.ShapeDtypeStruct((n_tok_blocks, 1, N_EXPERTS), F32)]

    x1p, xc, gw_p, pos_p, cnt_p, kwin_p, vwin_p, sguv_p = pl.pallas_call(
        functools.partial(_mixer_prompt_body, blocks_per_seq=blocks_per_seq, n_blocks=n_prompt_blocks),
        grid=(n_prompt_blocks + 2,),
        in_specs=[smem, x_spec, _const_spec((D_MODEL, Q_END)),
                  _const_spec((D_MODEL, IN_COLS)), _const_spec((1, D_MODEL)), _const_spec((1, LANES)),
                  _const_spec((1, LANES)), _const_spec((1, MLP_WIDTH)), _const_spec((2 * LANES, LANES)),
                  _const_spec((2 * LANES, LANES)), _const_spec((2, N_HEADS, PAIR, BAND)),
                  _const_spec((MLP_GROUPS, MLP_CHUNK, MLP_CHUNK)), _const_spec((MLP_CHUNK, MLP_WIDTH)),
                  _const_spec((ATTN_WIDTH, D_MODEL)), _const_spec((MLP_WIDTH, D_MODEL)),
                  _const_spec((D_MODEL, D_MODEL)), _const_spec((1, D_MODEL)), _const_spec((D_MODEL, N_EXPERTS)),
                  _const_spec((1, N_EXPERTS)), _const_spec((tb, tb))],
        out_specs=[x_spec, *route_specs(routed_blk),
                   win_spec(KV_WIDTH), win_spec(KV_WIDTH),
                   pl.BlockSpec((1, MLP_CHUNK, MLP_GROUPS, MLP_GROUP_DIM), lambda i: (seq_of(i), 0, 0, 0))],
        out_shape=[jax.ShapeDtypeStruct((n_b, seq, D_MODEL), F32), *route_shapes,
                   jax.ShapeDtypeStruct((n_b, WINDOW, KV_WIDTH), F32),
                   jax.ShapeDtypeStruct((n_b, WINDOW, KV_WIDTH), F32),
                   jax.ShapeDtypeStruct((n_b, MLP_CHUNK, MLP_GROUPS, MLP_GROUP_DIM), F32)],
        scratch_shapes=[pltpu.VMEM((WINDOW, KV_WIDTH), BF16), pltpu.VMEM((WINDOW, KV_WIDTH), BF16),
                        pltpu.VMEM((tb, D_MODEL), BF16)],
        compiler_params=pltpu.CompilerParams(dimension_semantics=("arbitrary",), vmem_limit_bytes=VMEM_LIMIT),
        name="mixer_prompt",
    )(sinks, x_prompt, wq_b, w_in_b, gmix, qg, kg, sgug, seg64, seg128, bias_p, wsp, bsp_p, wba_b, wbs_b, wout_b,
      gffn, wr_b, br, tri)

    full = lambda shape: pl.BlockSpec(shape, lambda i: (0,) * len(shape))
    any_spec = pl.BlockSpec(memory_space=pl.ANY)
    ck = cache_k_win[0].reshape(n_streams, n_cached, KV_WIDTH)
    cv = cache_v_win[0].reshape(n_streams, n_cached, KV_WIDTH)
    x1s, xc, gw_a, pos_a, cnt_a, kwin_s, vwin_s, sguv_s = pl.pallas_call(
        functools.partial(_mixer_sample_body, n_streams=n_streams, n_new=n_new),
        grid=(1,),
        in_specs=[smem, full((n_sample, D_MODEL)), full((n_streams, n_cached, KV_WIDTH)),
                  full((n_streams, n_cached, KV_WIDTH)),
                  full((D_MODEL, Q_END)), full((D_MODEL, IN_COLS)), full((1, D_MODEL)), full((1, LANES)),
                  full((1, LANES)),
                  full((1, MLP_WIDTH)), full((2 * LANES, LANES)), full((2 * LANES, LANES)),
                  full((N_HEADS * n_new, n_cached)), full((N_HEADS * n_new, n_new)),
                  full((MLP_GROUPS, n_new, n_new)), full((n_sample, MLP_WIDTH)),
                  full((ATTN_WIDTH, D_MODEL)), full((MLP_WIDTH, D_MODEL)), full((D_MODEL, D_MODEL)),
                  full((1, D_MODEL)), full((D_MODEL, N_EXPERTS)), full((1, N_EXPERTS)), full((tb, tb)),
                  any_spec, any_spec, any_spec, any_spec],
        out_specs=[full((n_sample, D_MODEL)), *route_specs(lambda i: n_prompt_blocks),
                   full((n_streams, n_cached, KV_WIDTH)), full((n_streams, n_cached, KV_WIDTH)),
                   full((n_sample, MLP_GROUPS, MLP_GROUP_DIM))],
        out_shape=[jax.ShapeDtypeStruct((n_sample, D_MODEL), F32), *route_shapes,
                   jax.ShapeDtypeStruct((n_streams, n_cached, KV_WIDTH), F32),
                   jax.ShapeDtypeStruct((n_streams, n_cached, KV_WIDTH), F32),
                   jax.ShapeDtypeStruct((n_sample, MLP_GROUPS, MLP_GROUP_DIM), F32)],
        input_output_aliases={23: 1, 24: 2, 25: 3, 26: 4},
        compiler_params=pltpu.CompilerParams(dimension_semantics=("arbitrary",), vmem_limit_bytes=VMEM_LIMIT),
        name="mixer_sample",
    )(sinks, x_sample.reshape(n_sample, D_MODEL), ck, cv, wq_b, w_in_b, gmix, qg, kg, sgug, seg64, seg128, bias_c,
      bias_n, wsp_s, bsp_s, wba_b, wbs_b, wout_b, gffn, wr_b, br, tri, xc, gw_p, pos_p, cnt_p)

    tm = EXPERT_TILE
    counts = cnt_a[:, 0, :].astype(jnp.int32)
    cnt_all = jnp.sum(counts, axis=0)
    padded = ((cnt_all + tm - 1) // tm) * tm
    pad_end = jnp.cumsum(padded)
    pad_off = pad_end - padded
    n_pairs = TOP_K * n_tok
    n_tiles = n_pairs // tm + N_EXPERTS
    n_steps = n_tiles + 1
    in_block = jnp.cumsum(counts, axis=1) - counts
    in_expert = jnp.cumsum(counts, axis=0) - counts
    run_src = (jnp.arange(n_tok_blocks, dtype=jnp.int32)[:, None] * pb + in_block).T.reshape(-1)
    run_start = (pad_off[None, :] + in_expert).T.reshape(-1)
    run_len = counts.T.reshape(-1)
    head = jnp.minimum(run_len, tm - run_start % tm)
    piece_start = jnp.stack([run_start, run_start + head], axis=1).reshape(-1)
    piece_src = jnp.stack([run_src, run_src + head], axis=1).reshape(-1)
    piece_len = jnp.stack([head, run_len - head], axis=1).reshape(-1)
    piece_tile = piece_start // tm
    tile_ids = jnp.arange(n_steps + 1, dtype=jnp.int32)
    piece_lo = jnp.sum((piece_tile[None, :] < tile_ids[:, None]).astype(jnp.int32), axis=1)
    n_used = (pad_end[-1] // tm).astype(jnp.int32)
    tile_start = jnp.minimum(tile_ids[:n_steps], n_used - 1) * tm
    tile_expert = jnp.sum((pad_end[None, :] <= tile_start[:, None]).astype(jnp.int32), axis=1)
    tile_expert = jnp.minimum(tile_expert, N_EXPERTS - 1)
    of_tile = tile_expert[:, None] == jnp.arange(N_EXPERTS, dtype=jnp.int32)[None, :]
    rows_end = jnp.sum(jnp.where(of_tile, (pad_off + cnt_all)[None, :], 0), axis=1)
    run_end = jnp.sum(jnp.where(of_tile, pad_end[None, :], 0), axis=1)
    tile_valid = jnp.clip(rows_end - tile_start, 0, tm)
    next_expert = jnp.sum((pad_end[None, :] <= run_end[:, None]).astype(jnp.int32), axis=1)
    next_expert = jnp.where(run_end < pad_end[-1], jnp.minimum(next_expert, N_EXPERTS - 1), -1)

    prefetch = (tile_expert, next_expert, n_used.reshape(1), tile_valid * ROW_TILE, piece_lo,
                piece_src * ROW_TILE, (piece_start % tm) * ROW_TILE, piece_len * ROW_TILE)
    yc = pl.pallas_call(
        _expert_body,
        grid_spec=pltpu.PrefetchScalarGridSpec(
            num_scalar_prefetch=len(prefetch),
            grid=(n_steps,),
            in_specs=[any_spec, any_spec,
                      pl.BlockSpec((1, 1, 2 * D_EXPERT), lambda i, te, *_: (te[i], 0, 0)),
                      any_spec,
                      pl.BlockSpec((1, 1, D_MODEL), lambda i, te, *_: (te[i], 0, 0))],
            out_specs=any_spec,
            scratch_shapes=[pltpu.VMEM((tm * ROW_TILE, LANES), F32), pltpu.VMEM((tm * ROW_TILE, LANES), F32),
                            pltpu.VMEM((tm * ROW_TILE, LANES), F32), pltpu.VMEM((tm * ROW_TILE, LANES), F32),
                            pltpu.VMEM((D_MODEL, 2 * D_EXPERT), F32), pltpu.VMEM((D_EXPERT, D_MODEL), F32),
                            pltpu.VMEM((D_MODEL, 2 * D_EXPERT), BF16), pltpu.VMEM((D_EXPERT, D_MODEL), BF16),
                            pltpu.SemaphoreType.DMA((2,)), pltpu.SemaphoreType.DMA(()),
                            pltpu.SemaphoreType.DMA((2,))]),
        out_shape=jax.ShapeDtypeStruct((n_pairs * ROW_TILE, LANES), F32),
        compiler_params=pltpu.CompilerParams(dimension_semantics=("arbitrary",), vmem_limit_bytes=VMEM_LIMIT),
        name="moe_experts",
    )(*[p.astype(jnp.int32) for p in prefetch], xc, w_gate_up[0],
      b_gate_up[0].reshape(N_EXPERTS, 1, 2 * D_EXPERT), w_down[0], b_down[0].reshape(N_EXPERTS, 1, D_MODEL))

    def combine(x1, first_block, n_blocks):
        scalars = pl.BlockSpec((tb, TOP_K), lambda i: (first_block + i, 0), memory_space=pltpu.SMEM)
        return pl.pallas_call(
            _combine_body,
            grid=(n_blocks,),
            in_specs=[scalars, scalars,
                      pl.BlockSpec((tb, D_MODEL), lambda i: (i, 0)),
                      pl.BlockSpec((pb * ROW_TILE, LANES), lambda i: (first_block + i, 0))],
            out_specs=pl.BlockSpec((tb, D_MODEL), lambda i: (i, 0)),
            out_shape=jax.ShapeDtypeStruct(x1.shape, F32),
            scratch_shapes=[pltpu.VMEM((tb * ROW_TILE, LANES), F32)],
            compiler_params=pltpu.CompilerParams(dimension_semantics=("arbitrary",), vmem_limit_bytes=VMEM_LIMIT),
            name="moe_combine",
        )(pos_a, gw_a, x1, yc)

    y_prompt = combine(x1p.reshape(n_prompt, D_MODEL), 0, n_prompt_blocks).reshape(n_b, seq, D_MODEL)
    y_sample = combine(x1s, n_prompt_blocks, 1).reshape(n_streams, n_new, D_MODEL)

    kv_shape = (N_KV_HEADS, HEAD_DIM)
    sg_shape = (MLP_GROUPS, MLP_GROUP_DIM)
    return (y_prompt, y_sample,
            kwin_p.reshape(1, n_b, WINDOW, *kv_shape), vwin_p.reshape(1, n_b, WINDOW, *kv_shape),
            kwin_s.reshape(1, n_streams, n_cached, *kv_shape), vwin_s.reshape(1, n_streams, n_cached, *kv_shape),
            sguv_p.reshape(1, n_b, MLP_CHUNK, *sg_shape), sguv_s.reshape(1, n_streams, n_new, *sg_shape))
```

```python
import functools

import jax
import jax.numpy as jnp
import numpy as np
from jax import lax
from jax.experimental import pallas as pl
from jax.experimental.pallas import tpu as pltpu

D_MODEL = 1024
CHUNK = 64
WINDOW = 128
HEAD_DIM = 64
N_HEADS = 8
N_KV_HEADS = 2
Q_PER_KV = N_HEADS // N_KV_HEADS
ATTN_WIDTH = N_HEADS * HEAD_DIM
KV_WIDTH = N_KV_HEADS * HEAD_DIM
ATTN_SCALE = HEAD_DIM ** -0.5
MLP_CHUNK = 128
MLP_GROUPS = 8
MLP_WIDTH = D_MODEL
MLP_GROUP_DIM = MLP_WIDTH // MLP_GROUPS
N_EXPERTS = 32
TOP_K = 4
D_EXPERT = D_MODEL
SWIGLU_ALPHA = 1.702
SWIGLU_LIMIT = 7.0
EPS = 1e-6
NEG_INF = -1e30
Q_END = ATTN_WIDTH
K_END = Q_END + KV_WIDTH
V_END = K_END + KV_WIDTH
U_END = V_END + MLP_WIDTH
VM_END = U_END + MLP_WIDTH
GA_END = VM_END + D_MODEL
IN_COLS = GA_END + D_MODEL

LANES = 128
ROW_TILE = D_MODEL // LANES
TOKEN_BLOCK = 256
PAIR = 2 * CHUNK
BAND = PAIR + WINDOW
EXPERT_TILE = 512
VMEM_LIMIT = 56 * 1024 * 1024

F32 = jnp.float32
BF16 = jnp.bfloat16


def _dot(a, b):
    return jnp.dot(a, b, preferred_element_type=F32)


def _dot_nt(a, b):
    return lax.dot_general(a, b, (((1,), (1,)), ((), ())), preferred_element_type=F32)


def _store_token_major(ref, val):
    rows = val.shape[0]
    for s in range(ROW_TILE):
        ref[pl.ds(s, rows, stride=ROW_TILE), :] = val[:, s * LANES:(s + 1) * LANES]


def _load_token_major(ref, rows):
    return jnp.concatenate([ref[pl.ds(s, rows, stride=ROW_TILE), :] for s in range(ROW_TILE)], axis=-1)


def _segment_mean(sq, seg):
    hi = sq.astype(BF16)
    lo = (sq - hi.astype(F32)).astype(BF16)
    return _dot(jnp.concatenate([hi, lo], axis=-1), seg)


def _rms_rows(x, gain):
    ms = jnp.mean(x * x, axis=-1, keepdims=True)
    return (x * lax.rsqrt(ms + EPS)) * gain


def _project(h, wq_ref, w_in_ref, qg, kg, sgug, seg64, seg128, between=None):
    qkv = jnp.concatenate([_dot(h, wq_ref[...]), _dot(h, w_in_ref[:, Q_END:V_END])], axis=-1)
    qk_cols = []
    for c in range(K_END // LANES):
        blk = qkv[:, c * LANES:(c + 1) * LANES]
        ms = _segment_mean(blk * blk, seg64)
        g = qg if c < Q_END // LANES else kg
        qk_cols.append((blk * lax.rsqrt(ms + EPS)) * g)
    qn = jnp.concatenate(qk_cols[:Q_END // LANES], axis=-1)
    kn = qk_cols[Q_END // LANES]
    v = qkv[:, K_END:V_END]
    u = jax.nn.gelu(_dot(h, w_in_ref[:, V_END:U_END]))
    vg = jax.nn.gelu(_dot(h, w_in_ref[:, U_END:VM_END]))
    vm_cols = []
    for g in range(MLP_GROUPS):
        blk = vg[:, g * LANES:(g + 1) * LANES]
        ms = _segment_mean(blk * blk, seg128)
        vm_cols.append((blk * lax.rsqrt(ms + EPS)) * sgug[:, g * LANES:(g + 1) * LANES])
    vm = jnp.concatenate(vm_cols, axis=-1)
    if between is not None:
        between()
    ga = jax.nn.sigmoid(_dot(h, w_in_ref[:, VM_END:GA_END]))
    gb = jax.nn.sigmoid(_dot(h, w_in_ref[:, GA_END:IN_COLS]))
    return qn, kn, v, u, vm, ga, gb


def _stack_heads(q_rows):
    lane = lax.broadcasted_iota(jnp.int32, (q_rows.shape[0], LANES), 1)
    blocks = []
    for head in range(N_HEADS):
        j, half = head % Q_PER_KV, head // Q_PER_KV
        col = q_rows[:, j * LANES:(j + 1) * LANES]
        keep = (lane < HEAD_DIM) if half == 0 else (lane >= HEAD_DIM)
        blocks.append(jnp.where(keep, col, 0.0))
    return jnp.concatenate(blocks, axis=0).astype(BF16)


def _unstack_heads(o, rows):
    lane = lax.broadcasted_iota(jnp.int32, (rows, LANES), 1)
    cols = []
    for j in range(Q_PER_KV):
        lo = o[j * rows:(j + 1) * rows]
        hi = o[(j + Q_PER_KV) * rows:(j + Q_PER_KV + 1) * rows]
        cols.append(jnp.where(lane < HEAD_DIM, lo, hi))
    return jnp.concatenate(cols, axis=-1)


def _merge(att, sgu, ga, gb, wba_ref, wbs_ref, wout_ref):
    m = ga * _dot(att.astype(BF16), wba_ref[...]) + gb * _dot(sgu.astype(BF16), wbs_ref[...])
    return _dot(m.astype(BF16), wout_ref[...])


def _route(h2, wr_ref, br_ref, tri_ref):
    rows = h2.shape[0]
    logits = _dot(h2, wr_ref[...]) + br_ref[...]
    eidx = lax.broadcasted_iota(jnp.int32, (rows, N_EXPERTS), 1).astype(F32)
    work = logits
    vals, picks, onehots = [], [], []
    for _ in range(TOP_K):
        m = jnp.max(work, axis=-1, keepdims=True)
        sel = jnp.min(jnp.where(work == m, eidx, float(N_EXPERTS)), axis=-1, keepdims=True)
        oh = eidx == sel
        vals.append(m)
        picks.append(sel)
        onehots.append(oh)
        work = jnp.where(oh, -jnp.inf, work)
    exps = [jnp.exp(v - vals[0]) for v in vals]
    den = exps[0] + exps[1] + exps[2] + exps[3]
    mask = jnp.zeros((rows, N_EXPERTS), F32)
    for oh in onehots:
        mask = mask + jnp.where(oh, 1.0, 0.0)
    before = _dot(tri_ref[...], mask.astype(BF16))
    counts = jnp.sum(mask, axis=0, keepdims=True)
    k4 = lax.broadcasted_iota(jnp.int32, (rows, TOP_K), 1)
    w4 = jnp.zeros((rows, TOP_K), F32)
    pos4 = jnp.zeros((rows, TOP_K), jnp.int32)
    for k in range(TOP_K):
        lower_experts = jnp.sum(jnp.where(eidx < picks[k], counts, 0.0), axis=-1, keepdims=True)
        rank = jnp.sum(jnp.where(onehots[k], before, 0.0), axis=-1, keepdims=True)
        w4 = jnp.where(k4 == k, exps[k] / den, w4)
        pos4 = jnp.where(k4 == k, (lower_experts + rank).astype(jnp.int32), pos4)
    return w4, pos4, counts


def _pair_selector(pos4, values=None):
    rows = pos4.shape[0]
    col = lax.broadcasted_iota(jnp.int32, (rows, TOP_K * rows), 1)
    sel = jnp.zeros((rows, TOP_K * rows), F32)
    for k in range(TOP_K):
        sel = sel + jnp.where(col == pos4[:, k:k + 1], 1.0 if values is None else values[:, k:k + 1], 0.0)
    return sel


def _route_block(h2, wr_ref, br_ref, tri_ref, gw_ref, pos_ref, cnt_ref):
    w4, pos4, counts = _route(h2, wr_ref, br_ref, tri_ref)
    gw_ref[...] = w4
    pos_ref[...] = pos4
    cnt_ref[0] = counts
    return pos4


def _dispatch_block(pos4, h2, xc_ref):
    select = _pair_selector(pos4).astype(BF16)
    pairs = lax.dot_general(select, h2, (((0,), (0,)), ((), ())), preferred_element_type=F32)
    _store_token_major(xc_ref, pairs)


def _route_and_dispatch(h2, wr_ref, br_ref, tri_ref, gw_ref, pos_ref, cnt_ref, xc_ref):
    _dispatch_block(_route_block(h2, wr_ref, br_ref, tri_ref, gw_ref, pos_ref, cnt_ref), h2, xc_ref)


def _softmax_rows(parts, sink):
    m = sink
    for l in parts:
        m = jnp.maximum(m, jnp.max(l, axis=-1, keepdims=True))
    es = [jnp.exp(l - m) for l in parts]
    den = jnp.exp(sink - m)
    for e in es:
        den = den + jnp.sum(e, axis=-1, keepdims=True)
    return [e / den for e in es]


def _mixer_prompt_body(sinks_ref, x_ref, wq_ref, w_in_ref, gmix_ref, qg_ref, kg_ref, sgug_ref, seg64_ref, seg128_ref,
                       bias_ref, wsp_ref, bsp_ref, wba_ref, wbs_ref, wout_ref, gffn_ref, wr_ref, br_ref, tri_ref,
                       x1_ref, xc_ref, gw_ref, pos_ref, cnt_ref, kwin_ref, vwin_ref, sguv_ref,
                       kcarry, vcarry, h2_prev_ref, *, blocks_per_seq, n_blocks):
    i = pl.program_id(0)

    @pl.when(i == 0)
    def _():
        h2_prev_ref[...] = jnp.zeros_like(h2_prev_ref)

    def route_previous():
        return _route_block(h2_prev_ref[...], wr_ref, br_ref, tri_ref, gw_ref, pos_ref, cnt_ref)

    def dispatch_previous(pos4):
        _dispatch_block(pos4, h2_prev_ref[...], xc_ref)

    @pl.when(i == n_blocks)
    def _():
        dispatch_previous(route_previous())

    @pl.when(i == n_blocks + 1)
    def _():
        xc_ref[...] = jnp.zeros_like(xc_ref)
        gw_ref[...] = jnp.zeros_like(gw_ref)
        pos_ref[...] = jnp.zeros_like(pos_ref)
        cnt_ref[...] = jnp.zeros_like(cnt_ref)

    @pl.when(jnp.logical_and(i < n_blocks, i % blocks_per_seq == 0))
    def _():
        kcarry[...] = jnp.zeros_like(kcarry)
        vcarry[...] = jnp.zeros_like(vcarry)

    @pl.when(i < n_blocks)
    def _():
        _mixer_prompt_block(i % blocks_per_seq, sinks_ref, x_ref, wq_ref, w_in_ref, gmix_ref, qg_ref, kg_ref, sgug_ref,
                            seg64_ref, seg128_ref, bias_ref, wsp_ref, bsp_ref, wba_ref, wbs_ref, wout_ref, gffn_ref,
                            x1_ref, kwin_ref, vwin_ref, sguv_ref, kcarry, vcarry, h2_prev_ref,
                            route_previous, dispatch_previous)


def _mixer_prompt_block(j, sinks_ref, x_ref, wq_ref, w_in_ref, gmix_ref, qg_ref, kg_ref, sgug_ref, seg64_ref, seg128_ref,
                        bias_ref, wsp_ref, bsp_ref, wba_ref, wbs_ref, wout_ref, gffn_ref,
                        x1_ref, kwin_ref, vwin_ref, sguv_ref, kcarry, vcarry, h2_prev_ref,
                        route_previous, dispatch_previous):
    routed = []
    x = x_ref[0]
    h = _rms_rows(x, gmix_ref[...]).astype(BF16)
    qn, kn, v, u, vm, ga, gb = _project(h, wq_ref, w_in_ref, qg_ref[...], kg_ref[...], sgug_ref[...],
                                        seg64_ref[...], seg128_ref[...],
                                        between=lambda: routed.append(route_previous()))
    k_ext = jnp.concatenate([kcarry[...], kn.astype(BF16)], axis=0)
    v_ext = jnp.concatenate([vcarry[...], v.astype(BF16)], axis=0)
    kcarry[...] = k_ext[TOKEN_BLOCK:]
    vcarry[...] = v_ext[TOKEN_BLOCK:]

    tri_mask = (lax.broadcasted_iota(jnp.int32, (MLP_CHUNK, MLP_CHUNK), 0)
                >= lax.broadcasted_iota(jnp.int32, (MLP_CHUNK, MLP_CHUNK), 1))
    att_rows, sgu_rows = [], []
    for pm in range(TOKEN_BLOCK // PAIR):
        r0 = pm * PAIR
        q_stack = _stack_heads(qn[r0:r0 + PAIR])
        k_band = k_ext[r0:r0 + BAND]
        v_band = v_ext[r0:r0 + BAND]
        s = _dot_nt(q_stack, k_band)
        first = jnp.where(j == 0, 0, 1) if pm == 0 else 1
        probs = []
        for head in range(N_HEADS):
            logit = s[head * PAIR:(head + 1) * PAIR] * ATTN_SCALE + bias_ref[first, head]
            probs.append(_softmax_rows([logit], sinks_ref[head])[0].astype(BF16))
        o = _dot(jnp.concatenate(probs, axis=0), v_band)
        att_rows.append(_unstack_heads(o, PAIR))
        cols = []
        for g in range(MLP_GROUPS):
            wm = jnp.where(tri_mask, wsp_ref[g], 0.0).astype(BF16)
            cols.append(_dot(wm, vm[r0:r0 + PAIR, g * LANES:(g + 1) * LANES].astype(BF16)))
        mixed = jnp.concatenate(cols, axis=-1) + bsp_ref[...]
        sgu_rows.append(u[r0:r0 + PAIR] * mixed)
        if pm == 0:
            dispatch_previous(routed[0])
    att = jnp.concatenate(att_rows, axis=0)
    sgu = jnp.concatenate(sgu_rows, axis=0)

    x1 = x + _merge(att, sgu, ga, gb, wba_ref, wbs_ref, wout_ref)
    h2 = _rms_rows(x1, gffn_ref[...])
    h2_prev_ref[...] = h2.astype(BF16)

    x1_ref[0] = x1
    kwin_ref[0] = kn[TOKEN_BLOCK - WINDOW:]
    vwin_ref[0] = v[TOKEN_BLOCK - WINDOW:]
    for g in range(MLP_GROUPS):
        sguv_ref[0, :, g, :] = vm[TOKEN_BLOCK - MLP_CHUNK:, g * LANES:(g + 1) * LANES]


def _mixer_sample_body(sinks_ref, x_ref, ck_ref, cv_ref, wq_ref, w_in_ref, gmix_ref, qg_ref, kg_ref, sgug_ref, seg64_ref,
                       seg128_ref, biasc_ref, biasn_ref, wsp_ref, bsp_ref, wba_ref, wbs_ref, wout_ref, gffn_ref,
                       wr_ref, br_ref, tri_ref, xc_in_ref, gw_in_ref, pos_in_ref, cnt_in_ref,
                       x1_ref, xc_ref, gw_ref, pos_ref, cnt_ref, kwin_ref, vwin_ref, sguv_ref, *, n_streams, n_new):
    del xc_in_ref, gw_in_ref, pos_in_ref, cnt_in_ref
    x = x_ref[...]
    h = _rms_rows(x, gmix_ref[...]).astype(BF16)
    qn, kn, v, u, vm, ga, gb = _project(h, wq_ref, w_in_ref, qg_ref[...], kg_ref[...], sgug_ref[...],
                                        seg64_ref[...], seg128_ref[...])
    n_cached = ck_ref.shape[1]
    att_rows = []
    for s_i in range(n_streams):
        r0 = s_i * n_new
        q_stack = _stack_heads(qn[r0:r0 + n_new])
        k_new = kn[r0:r0 + n_new]
        v_new = v[r0:r0 + n_new]
        s_c = _dot_nt(q_stack, ck_ref[s_i].astype(BF16))
        s_n = _dot_nt(q_stack, k_new.astype(BF16))
        pc, pn = [], []
        for head in range(N_HEADS):
            rows = slice(head * n_new, (head + 1) * n_new)
            lc = s_c[rows] * ATTN_SCALE + biasc_ref[rows]
            ln = s_n[rows] * ATTN_SCALE + biasn_ref[rows]
            p_c, p_n = _softmax_rows([lc, ln], sinks_ref[head])
            pc.append(p_c.astype(BF16))
            pn.append(p_n.astype(BF16))
        o = (_dot(jnp.concatenate(pc, axis=0), cv_ref[s_i].astype(BF16))
             + _dot(jnp.concatenate(pn, axis=0), v_new.astype(BF16)))
        att_rows.append(_unstack_heads(o, n_new))
        kwin_ref[s_i, 0:n_cached - n_new] = ck_ref[s_i, n_new:n_cached]
        kwin_ref[s_i, n_cached - n_new:n_cached] = k_new
        vwin_ref[s_i, 0:n_cached - n_new] = cv_ref[s_i, n_new:n_cached]
        vwin_ref[s_i, n_cached - n_new:n_cached] = v_new
    att = jnp.concatenate(att_rows, axis=0)

    rows = n_streams * n_new
    ri = lax.broadcasted_iota(jnp.int32, (rows, rows), 0)
    ci = lax.broadcasted_iota(jnp.int32, (rows, rows), 1)
    keep = jnp.logical_and(ri // n_new == ci // n_new, ri % n_new >= ci % n_new)
    expand = (lax.broadcasted_iota(jnp.int32, (rows, n_new), 0) % n_new
              == lax.broadcasted_iota(jnp.int32, (rows, n_new), 1)).astype(BF16)
    cols = []
    for g in range(MLP_GROUPS):
        tiled = _dot_nt(_dot(expand, wsp_ref[g].astype(BF16)).astype(BF16), expand)
        wm = jnp.where(keep, tiled, 0.0).astype(BF16)
        cols.append(_dot(wm, vm[:, g * LANES:(g + 1) * LANES].astype(BF16)))
    sgu = u * (jnp.concatenate(cols, axis=-1) + bsp_ref[...])

    x1 = x + _merge(att, sgu, ga, gb, wba_ref, wbs_ref, wout_ref)
    h2 = _rms_rows(x1, gffn_ref[...])
    _route_and_dispatch(h2.astype(BF16), wr_ref, br_ref, tri_ref, gw_ref, pos_ref, cnt_ref, xc_ref)
    x1_ref[...] = x1
    for g in range(MLP_GROUPS):
        sguv_ref[:, g, :] = vm[:, g * LANES:(g + 1) * LANES]


def _piece_copies(i_tile, lo_ref, src_ref, off_ref, len_ref, hbm, buf, sem, to_hbm):
    def one(p, carry):
        n = pl.multiple_of(len_ref[p], ROW_TILE)

        @pl.when(n > 0)
        def _():
            far = hbm.at[pl.ds(pl.multiple_of(src_ref[p], ROW_TILE), n), :]
            near = buf.at[pl.ds(pl.multiple_of(off_ref[p], ROW_TILE), n), :]
            if to_hbm:
                pltpu.make_async_copy(near, far, sem).start(priority=1)
            else:
                pltpu.make_async_copy(far, near, sem).start()
        return carry
    lax.fori_loop(lo_ref[i_tile], lo_ref[i_tile + 1], one, 0)


def _wait_rows(hbm, buf, sem, n_rows):
    n = pl.multiple_of(n_rows, ROW_TILE)
    pltpu.make_async_copy(hbm.at[pl.ds(0, n), :], buf.at[pl.ds(0, n), :], sem).wait()


def _expert_mlp(x_cur, y_cur, bgu_ref, bd_ref, wgu_bf, wd_bf):
    xb = _load_token_major(x_cur, EXPERT_TILE).astype(BF16)
    hgu = _dot(xb, wgu_bf[...]) + bgu_ref[0]
    glu = jnp.minimum(hgu[:, :D_EXPERT], SWIGLU_LIMIT)
    lin = jnp.clip(hgu[:, D_EXPERT:], -SWIGLU_LIMIT, SWIGLU_LIMIT)
    act = glu * jax.nn.sigmoid(glu * SWIGLU_ALPHA) * (lin + 1.0)
    _store_token_major(y_cur, _dot(act.astype(BF16), wd_bf[...]) + bd_ref[0])


def _weight_copies(wgu_hbm, wd_hbm, wgu_stage, wd_stage, sem_w, expert):
    return (pltpu.make_async_copy(wgu_hbm.at[expert], wgu_stage, sem_w.at[0]),
            pltpu.make_async_copy(wd_hbm.at[expert], wd_stage, sem_w.at[1]))


def _expert_body(te_ref, nxt_ref, nused_ref, valid_ref, lo_ref, src_ref, off_ref, len_ref, xc_hbm, wgu_hbm, bgu_ref,
                 wd_hbm, bd_ref, yc_hbm, xbuf0, xbuf1, ybuf0, ybuf1, wgu_stage, wd_stage, wgu_bf, wd_bf,
                 sem_in, sem_out, sem_w):
    i = pl.program_id(0)
    n_used = nused_ref[0]
    xbufs, ybufs = (xbuf0, xbuf1), (ybuf0, ybuf1)
    pieces = (lo_ref, src_ref, off_ref, len_ref)

    @pl.when(i == 0)
    def _():
        xbuf0[...] = jnp.zeros_like(xbuf0)
        xbuf1[...] = jnp.zeros_like(xbuf1)
        for cp in _weight_copies(wgu_hbm, wd_hbm, wgu_stage, wd_stage, sem_w, te_ref[0]):
            cp.start(priority=1)
        _piece_copies(0, *pieces, xc_hbm, xbuf0, sem_in.at[0], to_hbm=False)

    @pl.when(jnp.logical_and(i < n_used, jnp.logical_or(i == 0, te_ref[i] != te_ref[jnp.maximum(i - 1, 0)])))
    def _():
        for cp in _weight_copies(wgu_hbm, wd_hbm, wgu_stage, wd_stage, sem_w, te_ref[i]):
            cp.wait()
        wgu_bf[...] = wgu_stage[...].astype(BF16)
        wd_bf[...] = wd_stage[...].astype(BF16)

        @pl.when(nxt_ref[i] >= 0)
        def _():
            for cp in _weight_copies(wgu_hbm, wd_hbm, wgu_stage, wd_stage, sem_w, nxt_ref[i]):
                cp.start(priority=1)

    for par in range(2):
        x_cur, x_next, y_cur, y_prev = xbufs[par], xbufs[1 - par], ybufs[par], ybufs[1 - par]

        @pl.when(jnp.logical_and(i < n_used, i % 2 == par))
        def _(x_cur=x_cur, x_next=x_next, y_cur=y_cur, y_prev=y_prev, par=par):
            _wait_rows(xc_hbm, x_cur, sem_in.at[par], valid_ref[i])

            @pl.when(i + 1 < n_used)
            def _():
                _piece_copies(i + 1, *pieces, xc_hbm, x_next, sem_in.at[1 - par], to_hbm=False)

            @pl.when(i > 0)
            def _():
                _piece_copies(i - 1, *pieces, yc_hbm, y_prev, sem_out, to_hbm=True)

            _expert_mlp(x_cur, y_cur, bgu_ref, bd_ref, wgu_bf, wd_bf)

            @pl.when(i > 0)
            def _():
                _wait_rows(yc_hbm, y_prev, sem_out, valid_ref[i - 1])

        @pl.when(jnp.logical_and(i == n_used, i % 2 == par))
        def _(y_prev=y_prev):
            _piece_copies(i - 1, *pieces, yc_hbm, y_prev, sem_out, to_hbm=True)
            _wait_rows(yc_hbm, y_prev, sem_out, valid_ref[i - 1])


def _combine_body(pos_ref, gw_ref, x1_ref, yc_ref, out_ref, acc_ref):
    rows = x1_ref.shape[0]
    _store_token_major(acc_ref, x1_ref[...])
    for t in range(rows):
        row = acc_ref[pl.ds(t * ROW_TILE, ROW_TILE), :]
        for k in range(TOP_K):
            flat = t * TOP_K + k
            at = pl.multiple_of(pos_ref[0, flat // LANES, flat % LANES], ROW_TILE)
            row = row + gw_ref[0, flat // LANES, flat % LANES] * yc_ref[pl.ds(at, ROW_TILE), :]
        acc_ref[pl.ds(t * ROW_TILE, ROW_TILE), :] = row
    out_ref[...] = _load_token_major(acc_ref, rows)


def _const_spec(shape):
    nd = len(shape)
    return pl.BlockSpec(shape, lambda *_: (0,) * nd, pipeline_mode=pl.Buffered(1))


def _q_perm():
    cols = np.arange(ATTN_WIDTH)
    j, half, d = cols // LANES, (cols % LANES) // HEAD_DIM, cols % HEAD_DIM
    return (j + Q_PER_KV * half) * HEAD_DIM + d


def _alibi_slopes():
    return 2.0 ** (-8.0 * np.arange(1, N_HEADS + 1) / N_HEADS)


def _prompt_bias():
    qi = np.arange(PAIR)[:, None]
    kj = np.arange(BAND)[None, :]
    dist = np.abs(qi + WINDOW - kj).astype(np.float64)
    cq, ck = qi // CHUNK, kj // CHUNK
    in_band = (ck >= cq) & (ck <= cq + WINDOW // CHUNK)
    base = -_alibi_slopes()[:, None, None] * dist[None]
    later = np.where(in_band[None], base, NEG_INF)
    first = np.where((kj >= WINDOW)[None], later, NEG_INF)
    return np.stack([first, later]).astype(np.float32)


def _sample_bias(n_new, n_cached):
    qi = np.arange(n_new)[:, None]
    dc = np.abs(qi + n_cached - np.arange(n_cached)[None, :]).astype(np.float64)
    dn = np.abs(qi - np.arange(n_new)[None, :]).astype(np.float64)
    sl = _alibi_slopes()[:, None, None]
    bc = (-sl * dc[None]).reshape(N_HEADS * n_new, n_cached)
    bn = (-sl * dn[None]).reshape(N_HEADS * n_new, n_new)
    return bc.astype(np.float32), bn.astype(np.float32)


def kernel(x_prompt, x_sample, cache_k_win, cache_v_win, g_mix, w_in, q_norm_g, k_norm_g, attn_sinks, sgu_norm_g,
           w_spatial, b_spatial, w_branch_attn, w_branch_sgu, w_out, g_ffn, w_router, b_router, w_gate_up,
           b_gate_up, w_down, b_down):
    n_b, seq, _ = x_prompt.shape
    n_streams, n_new, _ = x_sample.shape
    n_cached = cache_k_win.shape[2]
    n_prompt = n_b * seq
    n_sample = n_streams * n_new
    n_tok = n_prompt + n_sample
    assert seq % TOKEN_BLOCK == 0 and n_sample == TOKEN_BLOCK and n_cached == WINDOW
    blocks_per_seq = seq // TOKEN_BLOCK
    n_prompt_blocks = n_prompt // TOKEN_BLOCK

    perm = _q_perm()
    w_in_l = w_in[0]
    w_in_b = w_in_l.astype(BF16)
    halves = N_HEADS // Q_PER_KV
    wq_b = (w_in_l[:, :Q_END].reshape(D_MODEL, halves, Q_PER_KV, HEAD_DIM).transpose(0, 2, 1, 3)
            .reshape(D_MODEL, Q_END).astype(BF16))
    wba_b = w_branch_attn[0][perm, :].astype(BF16)
    wbs_b = w_branch_sgu[0].astype(BF16)
    wout_b = w_out[0].astype(BF16)
    wr_b = w_router[0].astype(BF16)
    br = b_router[0].reshape(1, N_EXPERTS).astype(F32)
    gmix = g_mix[0].reshape(1, D_MODEL)
    gffn = g_ffn[0].reshape(1, D_MODEL)
    qg = jnp.tile(q_norm_g[0], LANES // HEAD_DIM).reshape(1, LANES)
    kg = jnp.tile(k_norm_g[0], LANES // HEAD_DIM).reshape(1, LANES)
    sgug = sgu_norm_g[0].reshape(1, MLP_WIDTH)
    sinks = attn_sinks[0].astype(F32)
    lane_seg = np.arange(LANES) // HEAD_DIM
    seg64 = jnp.asarray(np.tile((lane_seg[:, None] == lane_seg[None, :]) / HEAD_DIM, (2, 1)), BF16)
    seg128 = jnp.full((2 * LANES, LANES), 1.0 / MLP_GROUP_DIM, BF16)
    tri = jnp.asarray(np.tril(np.ones((TOKEN_BLOCK, TOKEN_BLOCK)), -1), BF16)
    wsp = w_spatial[0]
    bsp_p = jnp.repeat(b_spatial[0].T, MLP_GROUP_DIM, axis=1)
    wsp_s = wsp[:, :n_new, :n_new]
    bsp_s = jnp.tile(jnp.repeat(b_spatial[0][:, :n_new].T, MLP_GROUP_DIM, axis=1), (n_streams, 1))
    bias_p = jnp.asarray(_prompt_bias())
    bias_c, bias_n = (jnp.asarray(a) for a in _sample_bias(n_new, n_cached))

    smem = pl.BlockSpec(memory_space=pltpu.SMEM)
    tb = TOKEN_BLOCK
    pb = TOP_K * tb
    n_tok_blocks = n_tok // tb
    last_blk = n_prompt_blocks - 1
    seq_of = lambda i: jnp.minimum(i, last_blk) // blocks_per_seq
    x_spec = pl.BlockSpec((1, tb, D_MODEL), lambda i: (seq_of(i), jnp.minimum(i, last_blk) % blocks_per_seq, 0))
    routed_blk = lambda i: jnp.where(i > n_prompt_blocks, n_prompt_blocks, jnp.clip(i - 1, 0, last_blk))
    win_spec = lambda width: pl.BlockSpec((1, WINDOW, width), lambda i: (seq_of(i), 0, 0))
    route_specs = lambda blk: [pl.BlockSpec((pb * ROW_TILE, LANES), lambda i: (blk(i), 0)),
                               pl.BlockSpec((tb, TOP_K), lambda i: (blk(i), 0)),
                               pl.BlockSpec((tb, TOP_K), lambda i: (blk(i), 0)),
                               pl.BlockSpec((1, 1, N_EXPERTS), lambda i: (blk(i), 0, 0))]
    route_shapes = [jax.ShapeDtypeStruct((n_tok * TOP_K * ROW_TILE, LANES), F32),
                    jax.ShapeDtypeStruct((n_tok, TOP_K), F32),
                    jax.ShapeDtypeStruct((n_tok, TOP_K), jnp.int32),
                    jax.ShapeDtypeStruct((n_tok_blocks, 1, N_EXPERTS), F32)]

    x1p, xc, gw_p, pos_p, cnt_p, kwin_p, vwin_p, sguv_p = pl.pallas_call(
        functools.partial(_mixer_prompt_body, blocks_per_seq=blocks_per_seq, n_blocks=n_prompt_blocks),
        grid=(n_prompt_blocks + 2,),
        in_specs=[smem, x_spec, _const_spec((D_MODEL, Q_END)),
                  _const_spec((D_MODEL, IN_COLS)), _const_spec((1, D_MODEL)), _const_spec((1, LANES)),
                  _const_spec((1, LANES)), _const_spec((1, MLP_WIDTH)), _const_spec((2 * LANES, LANES)),
                  _const_spec((2 * LANES, LANES)), _const_spec((2, N_HEADS, PAIR, BAND)),
                  _const_spec((MLP_GROUPS, MLP_CHUNK, MLP_CHUNK)), _const_spec((MLP_CHUNK, MLP_WIDTH)),
                  _const_spec((ATTN_WIDTH, D_MODEL)), _const_spec((MLP_WIDTH, D_MODEL)),
                  _const_spec((D_MODEL, D_MODEL)), _const_spec((1, D_MODEL)), _const_spec((D_MODEL, N_EXPERTS)),
                  _const_spec((1, N_EXPERTS)), _const_spec((tb, tb))],
        out_specs=[x_spec, *route_specs(routed_blk),
                   win_spec(KV_WIDTH), win_spec(KV_WIDTH),
                   pl.BlockSpec((1, MLP_CHUNK, MLP_GROUPS, MLP_GROUP_DIM), lambda i: (seq_of(i), 0, 0, 0))],
        out_shape=[jax.ShapeDtypeStruct((n_b, seq, D_MODEL), F32), *route_shapes,
                   jax.ShapeDtypeStruct((n_b, WINDOW, KV_WIDTH), F32),
                   jax.ShapeDtypeStruct((n_b, WINDOW, KV_WIDTH), F32),
                   jax.ShapeDtypeStruct((n_b, MLP_CHUNK, MLP_GROUPS, MLP_GROUP_DIM), F32)],
        scratch_shapes=[pltpu.VMEM((WINDOW, KV_WIDTH), BF16), pltpu.VMEM((WINDOW, KV_WIDTH), BF16),
                        pltpu.VMEM((tb, D_MODEL), BF16)],
        compiler_params=pltpu.CompilerParams(dimension_semantics=("arbitrary",), vmem_limit_bytes=VMEM_LIMIT),
        name="mixer_prompt",
    )(sinks, x_prompt, wq_b, w_in_b, gmix, qg, kg, sgug, seg64, seg128, bias_p, wsp, bsp_p, wba_b, wbs_b, wout_b,
      gffn, wr_b, br, tri)

    full = lambda shape: pl.BlockSpec(shape, lambda i: (0,) * len(shape))
    any_spec = pl.BlockSpec(memory_space=pl.ANY)
    ck = cache_k_win[0].reshape(n_streams, n_cached, KV_WIDTH)
    cv = cache_v_win[0].reshape(n_streams, n_cached, KV_WIDTH)
    x1s, xc, gw_a, pos_a, cnt_a, kwin_s, vwin_s, sguv_s = pl.pallas_call(
        functools.partial(_mixer_sample_body, n_streams=n_streams, n_new=n_new),
        grid=(1,),
        in_specs=[smem, full((n_sample, D_MODEL)), full((n_streams, n_cached, KV_WIDTH)),
                  full((n_streams, n_cached, KV_WIDTH)),
                  full((D_MODEL, Q_END)), full((D_MODEL, IN_COLS)), full((1, D_MODEL)), full((1, LANES)),
                  full((1, LANES)),
                  full((1, MLP_WIDTH)), full((2 * LANES, LANES)), full((2 * LANES, LANES)),
                  full((N_HEADS * n_new, n_cached)), full((N_HEADS * n_new, n_new)),
                  full((MLP_GROUPS, n_new, n_new)), full((n_sample, MLP_WIDTH)),
                  full((ATTN_WIDTH, D_MODEL)), full((MLP_WIDTH, D_MODEL)), full((D_MODEL, D_MODEL)),
                  full((1, D_MODEL)), full((D_MODEL, N_EXPERTS)), full((1, N_EXPERTS)), full((tb, tb)),
                  any_spec, any_spec, any_spec, any_spec],
        out_specs=[full((n_sample, D_MODEL)), *route_specs(lambda i: n_prompt_blocks),
                   full((n_streams, n_cached, KV_WIDTH)), full((n_streams, n_cached, KV_WIDTH)),
                   full((n_sample, MLP_GROUPS, MLP_GROUP_DIM))],
        out_shape=[jax.ShapeDtypeStruct((n_sample, D_MODEL), F32), *route_shapes,
                   jax.ShapeDtypeStruct((n_streams, n_cached, KV_WIDTH), F32),
                   jax.ShapeDtypeStruct((n_streams, n_cached, KV_WIDTH), F32),
                   jax.ShapeDtypeStruct((n_sample, MLP_GROUPS, MLP_GROUP_DIM), F32)],
        input_output_aliases={23: 1, 24: 2, 25: 3, 26: 4},
        compiler_params=pltpu.CompilerParams(dimension_semantics=("arbitrary",), vmem_limit_bytes=VMEM_LIMIT),
        name="mixer_sample",
    )(sinks, x_sample.reshape(n_sample, D_MODEL), ck, cv, wq_b, w_in_b, gmix, qg, kg, sgug, seg64, seg128, bias_c,
      bias_n, wsp_s, bsp_s, wba_b, wbs_b, wout_b, gffn, wr_b, br, tri, xc, gw_p, pos_p, cnt_p)

    tm = EXPERT_TILE
    counts = cnt_a[:, 0, :].astype(jnp.int32)
    cnt_all = jnp.sum(counts, axis=0)
    padded = ((cnt_all + tm - 1) // tm) * tm
    pad_end = jnp.cumsum(padded)
    pad_off = pad_end - padded
    n_pairs = TOP_K * n_tok
    n_tiles = n_pairs // tm + N_EXPERTS
    n_steps = n_tiles + 1
    in_block = jnp.cumsum(counts, axis=1) - counts
    in_expert = jnp.cumsum(counts, axis=0) - counts
    run_src = (jnp.arange(n_tok_blocks, dtype=jnp.int32)[:, None] * pb + in_block).T.reshape(-1)
    run_start = (pad_off[None, :] + in_expert).T.reshape(-1)
    run_len = counts.T.reshape(-1)
    head = jnp.minimum(run_len, tm - run_start % tm)
    piece_start = jnp.stack([run_start, run_start + head], axis=1).reshape(-1)
    piece_src = jnp.stack([run_src, run_src + head], axis=1).reshape(-1)
    piece_len = jnp.stack([head, run_len - head], axis=1).reshape(-1)
    piece_tile = piece_start // tm
    tile_ids = jnp.arange(n_steps + 1, dtype=jnp.int32)
    piece_lo = jnp.sum((piece_tile[None, :] < tile_ids[:, None]).astype(jnp.int32), axis=1)
    n_used = (pad_end[-1] // tm).astype(jnp.int32)
    tile_start = jnp.minimum(tile_ids[:n_steps], n_used - 1) * tm
    tile_expert = jnp.sum((pad_end[None, :] <= tile_start[:, None]).astype(jnp.int32), axis=1)
    tile_expert = jnp.minimum(tile_expert, N_EXPERTS - 1)
    of_tile = tile_expert[:, None] == jnp.arange(N_EXPERTS, dtype=jnp.int32)[None, :]
    rows_end = jnp.sum(jnp.where(of_tile, (pad_off + cnt_all)[None, :], 0), axis=1)
    run_end = jnp.sum(jnp.where(of_tile, pad_end[None, :], 0), axis=1)
    tile_valid = jnp.clip(rows_end - tile_start, 0, tm)
    next_expert = jnp.sum((pad_end[None, :] <= run_end[:, None]).astype(jnp.int32), axis=1)
    next_expert = jnp.where(run_end < pad_end[-1], jnp.minimum(next_expert, N_EXPERTS - 1), -1)

    prefetch = (tile_expert, next_expert, n_used.reshape(1), tile_valid * ROW_TILE, piece_lo,
                piece_src * ROW_TILE, (piece_start % tm) * ROW_TILE, piece_len * ROW_TILE)
    yc = pl.pallas_call(
        _expert_body,
        grid_spec=pltpu.PrefetchScalarGridSpec(
            num_scalar_prefetch=len(prefetch),
            grid=(n_steps,),
            in_specs=[any_spec, any_spec,
                      pl.BlockSpec((1, 1, 2 * D_EXPERT), lambda i, te, *_: (te[i], 0, 0)),
                      any_spec,
                      pl.BlockSpec((1, 1, D_MODEL), lambda i, te, *_: (te[i], 0, 0))],
            out_specs=any_spec,
            scratch_shapes=[pltpu.VMEM((tm * ROW_TILE, LANES), F32), pltpu.VMEM((tm * ROW_TILE, LANES), F32),
                            pltpu.VMEM((tm * ROW_TILE, LANES), F32), pltpu.VMEM((tm * ROW_TILE, LANES), F32),
                            pltpu.VMEM((D_MODEL, 2 * D_EXPERT), F32), pltpu.VMEM((D_EXPERT, D_MODEL), F32),
                            pltpu.VMEM((D_MODEL, 2 * D_EXPERT), BF16), pltpu.VMEM((D_EXPERT, D_MODEL), BF16),
                            pltpu.SemaphoreType.DMA((2,)), pltpu.SemaphoreType.DMA(()),
                            pltpu.SemaphoreType.DMA((2,))]),
        out_shape=jax.ShapeDtypeStruct((n_pairs * ROW_TILE, LANES), F32),
        compiler_params=pltpu.CompilerParams(dimension_semantics=("arbitrary",), vmem_limit_bytes=VMEM_LIMIT),
        name="moe_experts",
    )(*[p.astype(jnp.int32) for p in prefetch], xc, w_gate_up[0],
      b_gate_up[0].reshape(N_EXPERTS, 1, 2 * D_EXPERT), w_down[0], b_down[0].reshape(N_EXPERTS, 1, D_MODEL))

    pos_tiles = (pos_a * ROW_TILE).reshape(n_tok_blocks, pb // LANES, LANES)
    gw_tiles = gw_a.reshape(n_tok_blocks, pb // LANES, LANES)

    def combine(x1, first_block, n_blocks):
        scalars = pl.BlockSpec((1, pb // LANES, LANES), lambda i: (first_block + i, 0, 0), memory_space=pltpu.SMEM)
        return pl.pallas_call(
            _combine_body,
            grid=(n_blocks,),
            in_specs=[scalars, scalars,
                      pl.BlockSpec((tb, D_MODEL), lambda i: (i, 0)),
                      pl.BlockSpec((pb * ROW_TILE, LANES), lambda i: (first_block + i, 0))],
            out_specs=pl.BlockSpec((tb, D_MODEL), lambda i: (i, 0)),
            out_shape=jax.ShapeDtypeStruct(x1.shape, F32),
            scratch_shapes=[pltpu.VMEM((tb * ROW_TILE, LANES), F32)],
            compiler_params=pltpu.CompilerParams(dimension_semantics=("arbitrary",), vmem_limit_bytes=VMEM_LIMIT),
            name="moe_combine",
        )(pos_tiles, gw_tiles, x1, yc)

    y_prompt = combine(x1p.reshape(n_prompt, D_MODEL), 0, n_prompt_blocks).reshape(n_b, seq, D_MODEL)
    y_sample = combine(x1s, n_prompt_blocks, 1).reshape(n_streams, n_new, D_MODEL)

    kv_shape = (N_KV_HEADS, HEAD_DIM)
    sg_shape = (MLP_GROUPS, MLP_GROUP_DIM)
    return (y_prompt, y_sample,
            kwin_p.reshape(1, n_b, WINDOW, *kv_shape), vwin_p.reshape(1, n_b, WINDOW, *kv_shape),
            kwin_s.reshape(1, n_streams, n_cached, *kv_shape), vwin_s.reshape(1, n_streams, n_cached, *kv_shape),
            sguv_p.reshape(1, n_b, MLP_CHUNK, *sg_shape), sguv_s.reshape(1, n_streams, n_new, *sg_shape))
```

```python
import functools

import jax
import jax.numpy as jnp
import numpy as np
from jax import lax
from jax.experimental import pallas as pl
from jax.experimental.pallas import tpu as pltpu

D_MODEL = 1024
CHUNK = 64
WINDOW = 128
HEAD_DIM = 64
N_HEADS = 8
N_KV_HEADS = 2
Q_PER_KV = N_HEADS // N_KV_HEADS
ATTN_WIDTH = N_HEADS * HEAD_DIM
KV_WIDTH = N_KV_HEADS * HEAD_DIM
ATTN_SCALE = HEAD_DIM ** -0.5
MLP_CHUNK = 128
MLP_GROUPS = 8
MLP_WIDTH = D_MODEL
MLP_GROUP_DIM = MLP_WIDTH // MLP_GROUPS
N_EXPERTS = 32
TOP_K = 4
D_EXPERT = D_MODEL
SWIGLU_ALPHA = 1.702
SWIGLU_LIMIT = 7.0
EPS = 1e-6
NEG_INF = -1e30
Q_END = ATTN_WIDTH
K_END = Q_END + KV_WIDTH
V_END = K_END + KV_WIDTH
U_END = V_END + MLP_WIDTH
VM_END = U_END + MLP_WIDTH
GA_END = VM_END + D_MODEL
IN_COLS = GA_END + D_MODEL

LANES = 128
ROW_TILE = D_MODEL // LANES
TOKEN_BLOCK = 256
PAIR = 2 * CHUNK
BAND = PAIR + WINDOW
EXPERT_TILE = 512
TILE_QUARTER = EXPERT_TILE // 4
VMEM_LIMIT = 56 * 1024 * 1024

F32 = jnp.float32
BF16 = jnp.bfloat16


def _dot(a, b):
    return jnp.dot(a, b, preferred_element_type=F32)


def _dot_nt(a, b):
    return lax.dot_general(a, b, (((1,), (1,)), ((), ())), preferred_element_type=F32)


def _store_token_major(ref, val):
    rows = val.shape[0]
    for s in range(ROW_TILE):
        ref[pl.ds(s, rows, stride=ROW_TILE), :] = val[:, s * LANES:(s + 1) * LANES]


def _load_token_major(ref, rows):
    return jnp.concatenate([ref[pl.ds(s, rows, stride=ROW_TILE), :] for s in range(ROW_TILE)], axis=-1)


def _segment_mean(sq, seg):
    hi = sq.astype(BF16)
    lo = (sq - hi.astype(F32)).astype(BF16)
    return _dot(jnp.concatenate([hi, lo], axis=-1), seg)


def _rms_rows(x, gain):
    ms = jnp.mean(x * x, axis=-1, keepdims=True)
    return (x * lax.rsqrt(ms + EPS)) * gain


def _project(h, wq_ref, w_in_ref, qg, kg, sgug, seg64, seg128, between=None):
    qkv = jnp.concatenate([_dot(h, wq_ref[...]), _dot(h, w_in_ref[:, Q_END:V_END])], axis=-1)
    qk_cols = []
    for c in range(K_END // LANES):
        blk = qkv[:, c * LANES:(c + 1) * LANES]
        ms = _segment_mean(blk * blk, seg64)
        g = qg if c < Q_END // LANES else kg
        qk_cols.append((blk * lax.rsqrt(ms + EPS)) * g)
    qn = jnp.concatenate(qk_cols[:Q_END // LANES], axis=-1)
    kn = qk_cols[Q_END // LANES]
    v = qkv[:, K_END:V_END]
    u = jax.nn.gelu(_dot(h, w_in_ref[:, V_END:U_END]))
    vg = jax.nn.gelu(_dot(h, w_in_ref[:, U_END:VM_END]))
    vm_cols = []
    for g in range(MLP_GROUPS):
        blk = vg[:, g * LANES:(g + 1) * LANES]
        ms = _segment_mean(blk * blk, seg128)
        vm_cols.append((blk * lax.rsqrt(ms + EPS)) * sgug[:, g * LANES:(g + 1) * LANES])
    vm = jnp.concatenate(vm_cols, axis=-1)
    if between is not None:
        between()
    ga = jax.nn.sigmoid(_dot(h, w_in_ref[:, VM_END:GA_END]))
    gb = jax.nn.sigmoid(_dot(h, w_in_ref[:, GA_END:IN_COLS]))
    return qn, kn, v, u, vm, ga, gb


def _stack_heads(q_rows):
    lane = lax.broadcasted_iota(jnp.int32, (q_rows.shape[0], LANES), 1)
    blocks = []
    for head in range(N_HEADS):
        j, half = head % Q_PER_KV, head // Q_PER_KV
        col = q_rows[:, j * LANES:(j + 1) * LANES]
        keep = (lane < HEAD_DIM) if half == 0 else (lane >= HEAD_DIM)
        blocks.append(jnp.where(keep, col, 0.0))
    return jnp.concatenate(blocks, axis=0).astype(BF16)


def _unstack_heads(o, rows):
    lane = lax.broadcasted_iota(jnp.int32, (rows, LANES), 1)
    cols = []
    for j in range(Q_PER_KV):
        lo = o[j * rows:(j + 1) * rows]
        hi = o[(j + Q_PER_KV) * rows:(j + Q_PER_KV + 1) * rows]
        cols.append(jnp.where(lane < HEAD_DIM, lo, hi))
    return jnp.concatenate(cols, axis=-1)


def _merge(att, sgu, ga, gb, wba_ref, wbs_ref, wout_ref):
    m = ga * _dot(att.astype(BF16), wba_ref[...]) + gb * _dot(sgu.astype(BF16), wbs_ref[...])
    return _dot(m.astype(BF16), wout_ref[...])


def _route(h2, wr_ref, br_ref, tri_ref):
    rows = h2.shape[0]
    logits = _dot(h2, wr_ref[...]) + br_ref[...]
    eidx = lax.broadcasted_iota(jnp.int32, (rows, N_EXPERTS), 1).astype(F32)
    work = logits
    vals, picks, onehots = [], [], []
    for _ in range(TOP_K):
        m = jnp.max(work, axis=-1, keepdims=True)
        sel = jnp.min(jnp.where(work == m, eidx, float(N_EXPERTS)), axis=-1, keepdims=True)
        oh = eidx == sel
        vals.append(m)
        picks.append(sel)
        onehots.append(oh)
        work = jnp.where(oh, -jnp.inf, work)
    exps = [jnp.exp(v - vals[0]) for v in vals]
    den = exps[0] + exps[1] + exps[2] + exps[3]
    mask = jnp.zeros((rows, N_EXPERTS), F32)
    for oh in onehots:
        mask = mask + jnp.where(oh, 1.0, 0.0)
    before = _dot(tri_ref[...], mask.astype(BF16))
    counts = jnp.sum(mask, axis=0, keepdims=True)
    k4 = lax.broadcasted_iota(jnp.int32, (rows, TOP_K), 1)
    w4 = jnp.zeros((rows, TOP_K), F32)
    pos4 = jnp.zeros((rows, TOP_K), jnp.int32)
    for k in range(TOP_K):
        lower_experts = jnp.sum(jnp.where(eidx < picks[k], counts, 0.0), axis=-1, keepdims=True)
        rank = jnp.sum(jnp.where(onehots[k], before, 0.0), axis=-1, keepdims=True)
        w4 = jnp.where(k4 == k, exps[k] / den, w4)
        pos4 = jnp.where(k4 == k, (lower_experts + rank).astype(jnp.int32), pos4)
    return w4, pos4, counts


def _pair_selector(pos4, values=None):
    rows = pos4.shape[0]
    col = lax.broadcasted_iota(jnp.int32, (rows, TOP_K * rows), 1)
    sel = jnp.zeros((rows, TOP_K * rows), F32)
    for k in range(TOP_K):
        sel = sel + jnp.where(col == pos4[:, k:k + 1], 1.0 if values is None else values[:, k:k + 1], 0.0)
    return sel


def _route_block(h2, wr_ref, br_ref, tri_ref, gw_ref, pos_ref, cnt_ref):
    w4, pos4, counts = _route(h2, wr_ref, br_ref, tri_ref)
    gw_ref[...] = w4
    pos_ref[...] = pos4
    cnt_ref[0] = counts
    return pos4


def _dispatch_block(pos4, h2, xc_ref):
    select = _pair_selector(pos4).astype(BF16)
    pairs = lax.dot_general(select, h2, (((0,), (0,)), ((), ())), preferred_element_type=F32)
    _store_token_major(xc_ref, pairs)


def _route_and_dispatch(h2, wr_ref, br_ref, tri_ref, gw_ref, pos_ref, cnt_ref, xc_ref):
    _dispatch_block(_route_block(h2, wr_ref, br_ref, tri_ref, gw_ref, pos_ref, cnt_ref), h2, xc_ref)


def _softmax_rows(parts, sink):
    m = sink
    for l in parts:
        m = jnp.maximum(m, jnp.max(l, axis=-1, keepdims=True))
    es = [jnp.exp(l - m) for l in parts]
    den = jnp.exp(sink - m)
    for e in es:
        den = den + jnp.sum(e, axis=-1, keepdims=True)
    return [e / den for e in es]


def _mixer_prompt_body(sinks_ref, x_ref, wq_ref, w_in_ref, gmix_ref, qg_ref, kg_ref, sgug_ref, seg64_ref, seg128_ref,
                       bias_ref, wsp_ref, bsp_ref, wba_ref, wbs_ref, wout_ref, gffn_ref, wr_ref, br_ref, tri_ref,
                       x1_ref, xc_ref, gw_ref, pos_ref, cnt_ref, kwin_ref, vwin_ref, sguv_ref,
                       kcarry, vcarry, h2_prev_ref, *, blocks_per_seq, n_blocks):
    i = pl.program_id(0)

    @pl.when(i == 0)
    def _():
        h2_prev_ref[...] = jnp.zeros_like(h2_prev_ref)

    def route_previous():
        return _route_block(h2_prev_ref[...], wr_ref, br_ref, tri_ref, gw_ref, pos_ref, cnt_ref)

    def dispatch_previous(pos4):
        _dispatch_block(pos4, h2_prev_ref[...], xc_ref)

    @pl.when(i == n_blocks)
    def _():
        dispatch_previous(route_previous())

    @pl.when(i == n_blocks + 1)
    def _():
        xc_ref[...] = jnp.zeros_like(xc_ref)
        gw_ref[...] = jnp.zeros_like(gw_ref)
        pos_ref[...] = jnp.zeros_like(pos_ref)
        cnt_ref[...] = jnp.zeros_like(cnt_ref)

    @pl.when(jnp.logical_and(i < n_blocks, i % blocks_per_seq == 0))
    def _():
        kcarry[...] = jnp.zeros_like(kcarry)
        vcarry[...] = jnp.zeros_like(vcarry)

    @pl.when(i < n_blocks)
    def _():
        _mixer_prompt_block(i % blocks_per_seq, sinks_ref, x_ref, wq_ref, w_in_ref, gmix_ref, qg_ref, kg_ref, sgug_ref,
                            seg64_ref, seg128_ref, bias_ref, wsp_ref, bsp_ref, wba_ref, wbs_ref, wout_ref, gffn_ref,
                            x1_ref, kwin_ref, vwin_ref, sguv_ref, kcarry, vcarry, h2_prev_ref,
                            route_previous, dispatch_previous)


def _mixer_prompt_block(j, sinks_ref, x_ref, wq_ref, w_in_ref, gmix_ref, qg_ref, kg_ref, sgug_ref, seg64_ref, seg128_ref,
                        bias_ref, wsp_ref, bsp_ref, wba_ref, wbs_ref, wout_ref, gffn_ref,
                        x1_ref, kwin_ref, vwin_ref, sguv_ref, kcarry, vcarry, h2_prev_ref,
                        route_previous, dispatch_previous):
    routed = []
    x = x_ref[0]
    h = _rms_rows(x, gmix_ref[...]).astype(BF16)
    qn, kn, v, u, vm, ga, gb = _project(h, wq_ref, w_in_ref, qg_ref[...], kg_ref[...], sgug_ref[...],
                                        seg64_ref[...], seg128_ref[...],
                                        between=lambda: routed.append(route_previous()))
    k_ext = jnp.concatenate([kcarry[...], kn.astype(BF16)], axis=0)
    v_ext = jnp.concatenate([vcarry[...], v.astype(BF16)], axis=0)
    kcarry[...] = k_ext[TOKEN_BLOCK:]
    vcarry[...] = v_ext[TOKEN_BLOCK:]

    tri_mask = (lax.broadcasted_iota(jnp.int32, (MLP_CHUNK, MLP_CHUNK), 0)
                >= lax.broadcasted_iota(jnp.int32, (MLP_CHUNK, MLP_CHUNK), 1))
    att_rows, sgu_rows = [], []
    for pm in range(TOKEN_BLOCK // PAIR):
        r0 = pm * PAIR
        q_stack = _stack_heads(qn[r0:r0 + PAIR])
        k_band = k_ext[r0:r0 + BAND]
        v_band = v_ext[r0:r0 + BAND]
        s = _dot_nt(q_stack, k_band)
        first = jnp.where(j == 0, 0, 1) if pm == 0 else 1
        probs = []
        for head in range(N_HEADS):
            logit = s[head * PAIR:(head + 1) * PAIR] * ATTN_SCALE + bias_ref[first, head]
            probs.append(_softmax_rows([logit], sinks_ref[head])[0].astype(BF16))
        o = _dot(jnp.concatenate(probs, axis=0), v_band)
        att_rows.append(_unstack_heads(o, PAIR))
        cols = []
        for g in range(MLP_GROUPS):
            wm = jnp.where(tri_mask, wsp_ref[g], 0.0).astype(BF16)
            cols.append(_dot(wm, vm[r0:r0 + PAIR, g * LANES:(g + 1) * LANES].astype(BF16)))
        mixed = jnp.concatenate(cols, axis=-1) + bsp_ref[...]
        sgu_rows.append(u[r0:r0 + PAIR] * mixed)
        if pm == 0:
            dispatch_previous(routed[0])
    att = jnp.concatenate(att_rows, axis=0)
    sgu = jnp.concatenate(sgu_rows, axis=0)

    x1 = x + _merge(att, sgu, ga, gb, wba_ref, wbs_ref, wout_ref)
    h2 = _rms_rows(x1, gffn_ref[...])
    h2_prev_ref[...] = h2.astype(BF16)

    x1_ref[0] = x1
    kwin_ref[0] = kn[TOKEN_BLOCK - WINDOW:]
    vwin_ref[0] = v[TOKEN_BLOCK - WINDOW:]
    for g in range(MLP_GROUPS):
        sguv_ref[0, :, g, :] = vm[TOKEN_BLOCK - MLP_CHUNK:, g * LANES:(g + 1) * LANES]


def _mixer_sample_body(sinks_ref, x_ref, ck_ref, cv_ref, wq_ref, w_in_ref, gmix_ref, qg_ref, kg_ref, sgug_ref, seg64_ref,
                       seg128_ref, biasc_ref, biasn_ref, wsp_ref, bsp_ref, wba_ref, wbs_ref, wout_ref, gffn_ref,
                       wr_ref, br_ref, tri_ref, xc_in_ref, gw_in_ref, pos_in_ref, cnt_in_ref,
                       x1_ref, xc_ref, gw_ref, pos_ref, cnt_ref, kwin_ref, vwin_ref, sguv_ref, *, n_streams, n_new):
    del xc_in_ref, gw_in_ref, pos_in_ref, cnt_in_ref
    x = x_ref[...]
    h = _rms_rows(x, gmix_ref[...]).astype(BF16)
    qn, kn, v, u, vm, ga, gb = _project(h, wq_ref, w_in_ref, qg_ref[...], kg_ref[...], sgug_ref[...],
                                        seg64_ref[...], seg128_ref[...])
    n_cached = ck_ref.shape[1]
    att_rows = []
    for s_i in range(n_streams):
        r0 = s_i * n_new
        q_stack = _stack_heads(qn[r0:r0 + n_new])
        k_new = kn[r0:r0 + n_new]
        v_new = v[r0:r0 + n_new]
        s_c = _dot_nt(q_stack, ck_ref[s_i].astype(BF16))
        s_n = _dot_nt(q_stack, k_new.astype(BF16))
        pc, pn = [], []
        for head in range(N_HEADS):
            rows = slice(head * n_new, (head + 1) * n_new)
            lc = s_c[rows] * ATTN_SCALE + biasc_ref[rows]
            ln = s_n[rows] * ATTN_SCALE + biasn_ref[rows]
            p_c, p_n = _softmax_rows([lc, ln], sinks_ref[head])
            pc.append(p_c.astype(BF16))
            pn.append(p_n.astype(BF16))
        o = (_dot(jnp.concatenate(pc, axis=0), cv_ref[s_i].astype(BF16))
             + _dot(jnp.concatenate(pn, axis=0), v_new.astype(BF16)))
        att_rows.append(_unstack_heads(o, n_new))
        kwin_ref[s_i, 0:n_cached - n_new] = ck_ref[s_i, n_new:n_cached]
        kwin_ref[s_i, n_cached - n_new:n_cached] = k_new
        vwin_ref[s_i, 0:n_cached - n_new] = cv_ref[s_i, n_new:n_cached]
        vwin_ref[s_i, n_cached - n_new:n_cached] = v_new
    att = jnp.concatenate(att_rows, axis=0)

    rows = n_streams * n_new
    ri = lax.broadcasted_iota(jnp.int32, (rows, rows), 0)
    ci = lax.broadcasted_iota(jnp.int32, (rows, rows), 1)
    keep = jnp.logical_and(ri // n_new == ci // n_new, ri % n_new >= ci % n_new)
    expand = (lax.broadcasted_iota(jnp.int32, (rows, n_new), 0) % n_new
              == lax.broadcasted_iota(jnp.int32, (rows, n_new), 1)).astype(BF16)
    cols = []
    for g in range(MLP_GROUPS):
        tiled = _dot_nt(_dot(expand, wsp_ref[g].astype(BF16)).astype(BF16), expand)
        wm = jnp.where(keep, tiled, 0.0).astype(BF16)
        cols.append(_dot(wm, vm[:, g * LANES:(g + 1) * LANES].astype(BF16)))
    sgu = u * (jnp.concatenate(cols, axis=-1) + bsp_ref[...])

    x1 = x + _merge(att, sgu, ga, gb, wba_ref, wbs_ref, wout_ref)
    h2 = _rms_rows(x1, gffn_ref[...])
    _route_and_dispatch(h2.astype(BF16), wr_ref, br_ref, tri_ref, gw_ref, pos_ref, cnt_ref, xc_ref)
    x1_ref[...] = x1
    for g in range(MLP_GROUPS):
        sguv_ref[:, g, :] = vm[:, g * LANES:(g + 1) * LANES]


def _piece_copies(i_tile, lo_ref, src_ref, off_ref, len_ref, hbm, buf, sem, to_hbm):
    def one(p, carry):
        n = pl.multiple_of(len_ref[p], ROW_TILE)

        @pl.when(n > 0)
        def _():
            far = hbm.at[pl.ds(pl.multiple_of(src_ref[p], ROW_TILE), n), :]
            near = buf.at[pl.ds(pl.multiple_of(off_ref[p], ROW_TILE), n), :]
            if to_hbm:
                pltpu.make_async_copy(near, far, sem).start()
            else:
                pltpu.make_async_copy(far, near, sem).start()
        return carry
    lax.fori_loop(lo_ref[i_tile], lo_ref[i_tile + 1], one, 0)


def _wait_rows(hbm, buf, sem, n_rows):
    n = pl.multiple_of(n_rows, ROW_TILE)
    pltpu.make_async_copy(hbm.at[pl.ds(0, n), :], buf.at[pl.ds(0, n), :], sem).wait()


def _expert_mlp(x_cur, y_cur, bgu_ref, bd_ref, wgu_bf, wd_bf, n_rows):
    rows = pl.ds(0, n_rows * ROW_TILE)
    xb = _load_token_major(x_cur.at[rows, :], n_rows).astype(BF16)
    hgu = _dot(xb, wgu_bf[...]) + bgu_ref[0]
    glu = jnp.minimum(hgu[:, :D_EXPERT], SWIGLU_LIMIT)
    lin = jnp.clip(hgu[:, D_EXPERT:], -SWIGLU_LIMIT, SWIGLU_LIMIT)
    act = glu * jax.nn.sigmoid(glu * SWIGLU_ALPHA) * (lin + 1.0)
    _store_token_major(y_cur.at[rows, :], _dot(act.astype(BF16), wd_bf[...]) + bd_ref[0])


def _weight_copies(wgu_hbm, wd_hbm, wgu_stage, wd_stage, sem_w, expert):
    return (pltpu.make_async_copy(wgu_hbm.at[expert], wgu_stage, sem_w.at[0]),
            pltpu.make_async_copy(wd_hbm.at[expert], wd_stage, sem_w.at[1]))


def _expert_body(te_ref, nxt_ref, nused_ref, valid_ref, lo_ref, src_ref, off_ref, len_ref, xc_hbm, wgu_hbm, bgu_ref,
                 wd_hbm, bd_ref, yc_hbm, xbuf0, xbuf1, ybuf0, ybuf1, wgu_stage, wd_stage, wgu_bf, wd_bf,
                 sem_in, sem_out, sem_w):
    i = pl.program_id(0)
    n_used = nused_ref[0]
    xbufs, ybufs = (xbuf0, xbuf1), (ybuf0, ybuf1)
    pieces = (lo_ref, src_ref, off_ref, len_ref)

    @pl.when(i == 0)
    def _():
        xbuf0[...] = jnp.zeros_like(xbuf0)
        xbuf1[...] = jnp.zeros_like(xbuf1)
        for cp in _weight_copies(wgu_hbm, wd_hbm, wgu_stage, wd_stage, sem_w, te_ref[0]):
            cp.start(priority=1)
        _piece_copies(0, *pieces, xc_hbm, xbuf0, sem_in.at[0], to_hbm=False)

    @pl.when(jnp.logical_and(i < n_used, jnp.logical_or(i == 0, te_ref[i] != te_ref[jnp.maximum(i - 1, 0)])))
    def _():
        for cp in _weight_copies(wgu_hbm, wd_hbm, wgu_stage, wd_stage, sem_w, te_ref[i]):
            cp.wait()
        wgu_bf[...] = wgu_stage[...].astype(BF16)
        wd_bf[...] = wd_stage[...].astype(BF16)

        @pl.when(nxt_ref[i] >= 0)
        def _():
            for cp in _weight_copies(wgu_hbm, wd_hbm, wgu_stage, wd_stage, sem_w, nxt_ref[i]):
                cp.start(priority=1)

    for par in range(2):
        x_cur, x_next, y_cur, y_prev = xbufs[par], xbufs[1 - par], ybufs[par], ybufs[1 - par]

        @pl.when(jnp.logical_and(i < n_used, i % 2 == par))
        def _(x_cur=x_cur, x_next=x_next, y_cur=y_cur, y_prev=y_prev, par=par):
            _wait_rows(xc_hbm, x_cur, sem_in.at[par], valid_ref[i])

            @pl.when(i + 1 < n_used)
            def _():
                _piece_copies(i + 1, *pieces, xc_hbm, x_next, sem_in.at[1 - par], to_hbm=False)

            @pl.when(i > 0)
            def _():
                _piece_copies(i - 1, *pieces, yc_hbm, y_prev, sem_out, to_hbm=True)

            quarters = (valid_ref[i] + (TILE_QUARTER * ROW_TILE - 1)) // (TILE_QUARTER * ROW_TILE)
            for q in range(1, EXPERT_TILE // TILE_QUARTER + 1):
                @pl.when(quarters == q)
                def _(q=q):
                    _expert_mlp(x_cur, y_cur, bgu_ref, bd_ref, wgu_bf, wd_bf, q * TILE_QUARTER)

            @pl.when(i > 0)
            def _():
                _wait_rows(yc_hbm, y_prev, sem_out, valid_ref[i - 1])

        @pl.when(jnp.logical_and(i == n_used, i % 2 == par))
        def _(y_prev=y_prev):
            _piece_copies(i - 1, *pieces, yc_hbm, y_prev, sem_out, to_hbm=True)
            _wait_rows(yc_hbm, y_prev, sem_out, valid_ref[i - 1])


def _combine_body(pos_ref, gw_ref, x1_ref, yc_ref, out_ref, acc_ref):
    rows = x1_ref.shape[0]
    _store_token_major(acc_ref, x1_ref[...])
    for t in range(rows):
        row = acc_ref[pl.ds(t * ROW_TILE, ROW_TILE), :]
        for k in range(TOP_K):
            flat = t * TOP_K + k
            at = pl.multiple_of(pos_ref[0, flat // LANES, flat % LANES], ROW_TILE)
            row = row + gw_ref[0, flat // LANES, flat % LANES] * yc_ref[pl.ds(at, ROW_TILE), :]
        acc_ref[pl.ds(t * ROW_TILE, ROW_TILE), :] = row
    out_ref[...] = _load_token_major(acc_ref, rows)


def _const_spec(shape):
    nd = len(shape)
    return pl.BlockSpec(shape, lambda *_: (0,) * nd, pipeline_mode=pl.Buffered(1))


def _q_perm():
    cols = np.arange(ATTN_WIDTH)
    j, half, d = cols // LANES, (cols % LANES) // HEAD_DIM, cols % HEAD_DIM
    return (j + Q_PER_KV * half) * HEAD_DIM + d


def _alibi_slopes():
    return 2.0 ** (-8.0 * np.arange(1, N_HEADS + 1) / N_HEADS)


def _prompt_bias():
    qi = np.arange(PAIR)[:, None]
    kj = np.arange(BAND)[None, :]
    dist = np.abs(qi + WINDOW - kj).astype(np.float64)
    cq, ck = qi // CHUNK, kj // CHUNK
    in_band = (ck >= cq) & (ck <= cq + WINDOW // CHUNK)
    base = -_alibi_slopes()[:, None, None] * dist[None]
    later = np.where(in_band[None], base, NEG_INF)
    first = np.where((kj >= WINDOW)[None], later, NEG_INF)
    return np.stack([first, later]).astype(np.float32)


def _sample_bias(n_new, n_cached):
    qi = np.arange(n_new)[:, None]
    dc = np.abs(qi + n_cached - np.arange(n_cached)[None, :]).astype(np.float64)
    dn = np.abs(qi - np.arange(n_new)[None, :]).astype(np.float64)
    sl = _alibi_slopes()[:, None, None]
    bc = (-sl * dc[None]).reshape(N_HEADS * n_new, n_cached)
    bn = (-sl * dn[None]).reshape(N_HEADS * n_new, n_new)
    return bc.astype(np.float32), bn.astype(np.float32)


def kernel(x_prompt, x_sample, cache_k_win, cache_v_win, g_mix, w_in, q_norm_g, k_norm_g, attn_sinks, sgu_norm_g,
           w_spatial, b_spatial, w_branch_attn, w_branch_sgu, w_out, g_ffn, w_router, b_router, w_gate_up,
           b_gate_up, w_down, b_down):
    n_b, seq, _ = x_prompt.shape
    n_streams, n_new, _ = x_sample.shape
    n_cached = cache_k_win.shape[2]
    n_prompt = n_b * seq
    n_sample = n_streams * n_new
    n_tok = n_prompt + n_sample
    assert seq % TOKEN_BLOCK == 0 and n_sample == TOKEN_BLOCK and n_cached == WINDOW
    blocks_per_seq = seq // TOKEN_BLOCK
    n_prompt_blocks = n_prompt // TOKEN_BLOCK

    perm = _q_perm()
    w_in_l = w_in[0]
    w_in_b = w_in_l.astype(BF16)
    halves = N_HEADS // Q_PER_KV
    wq_b = (w_in_l[:, :Q_END].reshape(D_MODEL, halves, Q_PER_KV, HEAD_DIM).transpose(0, 2, 1, 3)
            .reshape(D_MODEL, Q_END).astype(BF16))
    wba_b = w_branch_attn[0][perm, :].astype(BF16)
    wbs_b = w_branch_sgu[0].astype(BF16)
    wout_b = w_out[0].astype(BF16)
    wr_b = w_router[0].astype(BF16)
    br = b_router[0].reshape(1, N_EXPERTS).astype(F32)
    gmix = g_mix[0].reshape(1, D_MODEL)
    gffn = g_ffn[0].reshape(1, D_MODEL)
    qg = jnp.tile(q_norm_g[0], LANES // HEAD_DIM).reshape(1, LANES)
    kg = jnp.tile(k_norm_g[0], LANES // HEAD_DIM).reshape(1, LANES)
    sgug = sgu_norm_g[0].reshape(1, MLP_WIDTH)
    sinks = attn_sinks[0].astype(F32)
    lane_seg = np.arange(LANES) // HEAD_DIM
    seg64 = jnp.asarray(np.tile((lane_seg[:, None] == lane_seg[None, :]) / HEAD_DIM, (2, 1)), BF16)
    seg128 = jnp.full((2 * LANES, LANES), 1.0 / MLP_GROUP_DIM, BF16)
    tri = jnp.asarray(np.tril(np.ones((TOKEN_BLOCK, TOKEN_BLOCK)), -1), BF16)
    wsp = w_spatial[0]
    bsp_p = jnp.repeat(b_spatial[0].T, MLP_GROUP_DIM, axis=1)
    wsp_s = wsp[:, :n_new, :n_new]
    bsp_s = jnp.tile(jnp.repeat(b_spatial[0][:, :n_new].T, MLP_GROUP_DIM, axis=1), (n_streams, 1))
    bias_p = jnp.asarray(_prompt_bias())
    bias_c, bias_n = (jnp.asarray(a) for a in _sample_bias(n_new, n_cached))

    smem = pl.BlockSpec(memory_space=pltpu.SMEM)
    tb = TOKEN_BLOCK
    pb = TOP_K * tb
    n_tok_blocks = n_tok // tb
    last_blk = n_prompt_blocks - 1
    seq_of = lambda i: jnp.minimum(i, last_blk) // blocks_per_seq
    x_spec = pl.BlockSpec((1, tb, D_MODEL), lambda i: (seq_of(i), jnp.minimum(i, last_blk) % blocks_per_seq, 0))
    routed_blk = lambda i: jnp.where(i > n_prompt_blocks, n_prompt_blocks, jnp.clip(i - 1, 0, last_blk))
    win_spec = lambda width: pl.BlockSpec((1, WINDOW, width), lambda i: (seq_of(i), 0, 0))
    route_specs = lambda blk: [pl.BlockSpec((pb * ROW_TILE, LANES), lambda i: (blk(i), 0)),
                               pl.BlockSpec((tb, TOP_K), lambda i: (blk(i), 0)),
                               pl.BlockSpec((tb, TOP_K), lambda i: (blk(i), 0)),
                               pl.BlockSpec((1, 1, N_EXPERTS), lambda i: (blk(i), 0, 0))]
    route_shapes = [jax.ShapeDtypeStruct((n_tok * TOP_K * ROW_TILE, LANES), F32),
                    jax.ShapeDtypeStruct((n_tok, TOP_K), F32),
                    jax.ShapeDtypeStruct((n_tok, TOP_K), jnp.int32),
                    jax.ShapeDtypeStruct((n_tok_blocks, 1, N_EXPERTS), F32)]

    x1p, xc, gw_p, pos_p, cnt_p, kwin_p, vwin_p, sguv_p = pl.pallas_call(
        functools.partial(_mixer_prompt_body, blocks_per_seq=blocks_per_seq, n_blocks=n_prompt_blocks),
        grid=(n_prompt_blocks + 2,),
        in_specs=[smem, x_spec, _const_spec((D_MODEL, Q_END)),
                  _const_spec((D_MODEL, IN_COLS)), _const_spec((1, D_MODEL)), _const_spec((1, LANES)),
                  _const_spec((1, LANES)), _const_spec((1, MLP_WIDTH)), _const_spec((2 * LANES, LANES)),
                  _const_spec((2 * LANES, LANES)), _const_spec((2, N_HEADS, PAIR, BAND)),
                  _const_spec((MLP_GROUPS, MLP_CHUNK, MLP_CHUNK)), _const_spec((MLP_CHUNK, MLP_WIDTH)),
                  _const_spec((ATTN_WIDTH, D_MODEL)), _const_spec((MLP_WIDTH, D_MODEL)),
                  _const_spec((D_MODEL, D_MODEL)), _const_spec((1, D_MODEL)), _const_spec((D_MODEL, N_EXPERTS)),
                  _const_spec((1, N_EXPERTS)), _const_spec((tb, tb))],
        out_specs=[x_spec, *route_specs(routed_blk),
                   win_spec(KV_WIDTH), win_spec(KV_WIDTH),
                   pl.BlockSpec((1, MLP_CHUNK, MLP_GROUPS, MLP_GROUP_DIM), lambda i: (seq_of(i), 0, 0, 0))],
        out_shape=[jax.ShapeDtypeStruct((n_b, seq, D_MODEL), F32), *route_shapes,
                   jax.ShapeDtypeStruct((n_b, WINDOW, KV_WIDTH), F32),
                   jax.ShapeDtypeStruct((n_b, WINDOW, KV_WIDTH), F32),
                   jax.ShapeDtypeStruct((n_b, MLP_CHUNK, MLP_GROUPS, MLP_GROUP_DIM), F32)],
        scratch_shapes=[pltpu.VMEM((WINDOW, KV_WIDTH), BF16), pltpu.VMEM((WINDOW, KV_WIDTH), BF16),
                        pltpu.VMEM((tb, D_MODEL), BF16)],
        compiler_params=pltpu.CompilerParams(dimension_semantics=("arbitrary",), vmem_limit_bytes=VMEM_LIMIT),
        name="mixer_prompt",
    )(sinks, x_prompt, wq_b, w_in_b, gmix, qg, kg, sgug, seg64, seg128, bias_p, wsp, bsp_p, wba_b, wbs_b, wout_b,
      gffn, wr_b, br, tri)

    full = lambda shape: pl.BlockSpec(shape, lambda i: (0,) * len(shape))
    any_spec = pl.BlockSpec(memory_space=pl.ANY)
    ck = cache_k_win[0].reshape(n_streams, n_cached, KV_WIDTH)
    cv = cache_v_win[0].reshape(n_streams, n_cached, KV_WIDTH)
    x1s, xc, gw_a, pos_a, cnt_a, kwin_s, vwin_s, sguv_s = pl.pallas_call(
        functools.partial(_mixer_sample_body, n_streams=n_streams, n_new=n_new),
        grid=(1,),
        in_specs=[smem, full((n_sample, D_MODEL)), full((n_streams, n_cached, KV_WIDTH)),
                  full((n_streams, n_cached, KV_WIDTH)),
                  full((D_MODEL, Q_END)), full((D_MODEL, IN_COLS)), full((1, D_MODEL)), full((1, LANES)),
                  full((1, LANES)),
                  full((1, MLP_WIDTH)), full((2 * LANES, LANES)), full((2 * LANES, LANES)),
                  full((N_HEADS * n_new, n_cached)), full((N_HEADS * n_new, n_new)),
                  full((MLP_GROUPS, n_new, n_new)), full((n_sample, MLP_WIDTH)),
                  full((ATTN_WIDTH, D_MODEL)), full((MLP_WIDTH, D_MODEL)), full((D_MODEL, D_MODEL)),
                  full((1, D_MODEL)), full((D_MODEL, N_EXPERTS)), full((1, N_EXPERTS)), full((tb, tb)),
                  any_spec, any_spec, any_spec, any_spec],
        out_specs=[full((n_sample, D_MODEL)), *route_specs(lambda i: n_prompt_blocks),
                   full((n_streams, n_cached, KV_WIDTH)), full((n_streams, n_cached, KV_WIDTH)),
                   full((n_sample, MLP_GROUPS, MLP_GROUP_DIM))],
        out_shape=[jax.ShapeDtypeStruct((n_sample, D_MODEL), F32), *route_shapes,
                   jax.ShapeDtypeStruct((n_streams, n_cached, KV_WIDTH), F32),
                   jax.ShapeDtypeStruct((n_streams, n_cached, KV_WIDTH), F32),
                   jax.ShapeDtypeStruct((n_sample, MLP_GROUPS, MLP_GROUP_DIM), F32)],
        input_output_aliases={23: 1, 24: 2, 25: 3, 26: 4},
        compiler_params=pltpu.CompilerParams(dimension_semantics=("arbitrary",), vmem_limit_bytes=VMEM_LIMIT),
        name="mixer_sample",
    )(sinks, x_sample.reshape(n_sample, D_MODEL), ck, cv, wq_b, w_in_b, gmix, qg, kg, sgug, seg64, seg128, bias_c,
      bias_n, wsp_s, bsp_s, wba_b, wbs_b, wout_b, gffn, wr_b, br, tri, xc, gw_p, pos_p, cnt_p)

    tm = EXPERT_TILE
    counts = cnt_a[:, 0, :].astype(jnp.int32)
    cnt_all = jnp.sum(counts, axis=0)
    padded = ((cnt_all + tm - 1) // tm) * tm
    pad_end = jnp.cumsum(padded)
    pad_off = pad_end - padded
    n_pairs = TOP_K * n_tok
    n_tiles = n_pairs // tm + N_EXPERTS
    n_steps = n_tiles + 1
    in_block = jnp.cumsum(counts, axis=1) - counts
    in_expert = jnp.cumsum(counts, axis=0) - counts
    run_src = (jnp.arange(n_tok_blocks, dtype=jnp.int32)[:, None] * pb + in_block).T.reshape(-1)
    run_start = (pad_off[None, :] + in_expert).T.reshape(-1)
    run_len = counts.T.reshape(-1)
    head = jnp.minimum(run_len, tm - run_start % tm)
    piece_start = jnp.stack([run_start, run_start + head], axis=1).reshape(-1)
    piece_src = jnp.stack([run_src, run_src + head], axis=1).reshape(-1)
    piece_len = jnp.stack([head, run_len - head], axis=1).reshape(-1)
    piece_tile = piece_start // tm
    tile_ids = jnp.arange(n_steps + 1, dtype=jnp.int32)
    piece_lo = jnp.sum((piece_tile[None, :] < tile_ids[:, None]).astype(jnp.int32), axis=1)
    n_used = (pad_end[-1] // tm).astype(jnp.int32)
    tile_start = jnp.minimum(tile_ids[:n_steps], n_used - 1) * tm
    tile_expert = jnp.sum((pad_end[None, :] <= tile_start[:, None]).astype(jnp.int32), axis=1)
    tile_expert = jnp.minimum(tile_expert, N_EXPERTS - 1)
    of_tile = tile_expert[:, None] == jnp.arange(N_EXPERTS, dtype=jnp.int32)[None, :]
    rows_end = jnp.sum(jnp.where(of_tile, (pad_off + cnt_all)[None, :], 0), axis=1)
    run_end = jnp.sum(jnp.where(of_tile, pad_end[None, :], 0), axis=1)
    tile_valid = jnp.clip(rows_end - tile_start, 0, tm)
    next_expert = jnp.sum((pad_end[None, :] <= run_end[:, None]).astype(jnp.int32), axis=1)
    next_expert = jnp.where(run_end < pad_end[-1], jnp.minimum(next_expert, N_EXPERTS - 1), -1)

    prefetch = (tile_expert, next_expert, n_used.reshape(1), tile_valid * ROW_TILE, piece_lo,
                piece_src * ROW_TILE, (piece_start % tm) * ROW_TILE, piece_len * ROW_TILE)
    yc = pl.pallas_call(
        _expert_body,
        grid_spec=pltpu.PrefetchScalarGridSpec(
            num_scalar_prefetch=len(prefetch),
            grid=(n_steps,),
            in_specs=[any_spec, any_spec,
                      pl.BlockSpec((1, 1, 2 * D_EXPERT), lambda i, te, *_: (te[i], 0, 0)),
                      any_spec,
                      pl.BlockSpec((1, 1, D_MODEL), lambda i, te, *_: (te[i], 0, 0))],
            out_specs=any_spec,
            scratch_shapes=[pltpu.VMEM((tm * ROW_TILE, LANES), F32), pltpu.VMEM((tm * ROW_TILE, LANES), F32),
                            pltpu.VMEM((tm * ROW_TILE, LANES), F32), pltpu.VMEM((tm * ROW_TILE, LANES), F32),
                            pltpu.VMEM((D_MODEL, 2 * D_EXPERT), F32), pltpu.VMEM((D_EXPERT, D_MODEL), F32),
                            pltpu.VMEM((D_MODEL, 2 * D_EXPERT), BF16), pltpu.VMEM((D_EXPERT, D_MODEL), BF16),
                            pltpu.SemaphoreType.DMA((2,)), pltpu.SemaphoreType.DMA(()),
                            pltpu.SemaphoreType.DMA((2,))]),
        out_shape=jax.ShapeDtypeStruct((n_pairs * ROW_TILE, LANES), F32),
        compiler_params=pltpu.CompilerParams(dimension_semantics=("arbitrary",), vmem_limit_bytes=VMEM_LIMIT),
        name="moe_experts",
    )(*[p.astype(jnp.int32) for p in prefetch], xc, w_gate_up[0],
      b_gate_up[0].reshape(N_EXPERTS, 1, 2 * D_EXPERT), w_down[0], b_down[0].reshape(N_EXPERTS, 1, D_MODEL))

    pos_tiles = (pos_a * ROW_TILE).reshape(n_tok_blocks, pb // LANES, LANES)
    gw_tiles = gw_a.reshape(n_tok_blocks, pb // LANES, LANES)

    def combine(x1, first_block, n_blocks):
        scalars = pl.BlockSpec((1, pb // LANES, LANES), lambda i: (first_block + i, 0, 0), memory_space=pltpu.SMEM)
        return pl.pallas_call(
            _combine_body,
            grid=(n_blocks,),
            in_specs=[scalars, scalars,
                      pl.BlockSpec((tb, D_MODEL), lambda i: (i, 0)),
                      pl.BlockSpec((pb * ROW_TILE, LANES), lambda i: (first_block + i, 0))],
            out_specs=pl.BlockSpec((tb, D_MODEL), lambda i: (i, 0)),
            out_shape=jax.ShapeDtypeStruct(x1.shape, F32),
            scratch_shapes=[pltpu.VMEM((tb * ROW_TILE, LANES), F32)],
            compiler_params=pltpu.CompilerParams(dimension_semantics=("arbitrary",), vmem_limit_bytes=VMEM_LIMIT),
            name="moe_combine",
        )(pos_tiles, gw_tiles, x1, yc)

    y_prompt = combine(x1p.reshape(n_prompt, D_MODEL), 0, n_prompt_blocks).reshape(n_b, seq, D_MODEL)
    y_sample = combine(x1s, n_prompt_blocks, 1).reshape(n_streams, n_new, D_MODEL)

    kv_shape = (N_KV_HEADS, HEAD_DIM)
    sg_shape = (MLP_GROUPS, MLP_GROUP_DIM)
    return (y_prompt, y_sample,
            kwin_p.reshape(1, n_b, WINDOW, *kv_shape), vwin_p.reshape(1, n_b, WINDOW, *kv_shape),
            kwin_s.reshape(1, n_streams, n_cached, *kv_shape), vwin_s.reshape(1, n_streams, n_cached, *kv_shape),
            sguv_p.reshape(1, n_b, MLP_CHUNK, *sg_shape), sguv_s.reshape(1, n_streams, n_new, *sg_shape))
```

```python
import functools

import jax
import jax.numpy as jnp
import numpy as np
from jax import lax
from jax.experimental import pallas as pl
from jax.experimental.pallas import tpu as pltpu

D_MODEL = 1024
CHUNK = 64
WINDOW = 128
HEAD_DIM = 64
N_HEADS = 8
N_KV_HEADS = 2
Q_PER_KV = N_HEADS // N_KV_HEADS
ATTN_WIDTH = N_HEADS * HEAD_DIM
KV_WIDTH = N_KV_HEADS * HEAD_DIM
ATTN_SCALE = HEAD_DIM ** -0.5
MLP_CHUNK = 128
MLP_GROUPS = 8
MLP_WIDTH = D_MODEL
MLP_GROUP_DIM = MLP_WIDTH // MLP_GROUPS
N_EXPERTS = 32
TOP_K = 4
D_EXPERT = D_MODEL
SWIGLU_ALPHA = 1.702
SWIGLU_LIMIT = 7.0
EPS = 1e-6
NEG_INF = -1e30
Q_END = ATTN_WIDTH
K_END = Q_END + KV_WIDTH
V_END = K_END + KV_WIDTH
U_END = V_END + MLP_WIDTH
VM_END = U_END + MLP_WIDTH
GA_END = VM_END + D_MODEL
IN_COLS = GA_END + D_MODEL

LANES = 128
ROW_TILE = D_MODEL // LANES
TOKEN_BLOCK = 256
PAIR = 2 * CHUNK
BAND = PAIR + WINDOW
EXPERT_TILE = 1024
TILE_QUARTER = EXPERT_TILE // 8
VMEM_LIMIT = 56 * 1024 * 1024

F32 = jnp.float32
BF16 = jnp.bfloat16


def _dot(a, b):
    return jnp.dot(a, b, preferred_element_type=F32)


def _dot_nt(a, b):
    return lax.dot_general(a, b, (((1,), (1,)), ((), ())), preferred_element_type=F32)


def _store_token_major(ref, val):
    rows = val.shape[0]
    for s in range(ROW_TILE):
        ref[pl.ds(s, rows, stride=ROW_TILE), :] = val[:, s * LANES:(s + 1) * LANES]


def _load_token_major(ref, rows):
    return jnp.concatenate([ref[pl.ds(s, rows, stride=ROW_TILE), :] for s in range(ROW_TILE)], axis=-1)


def _segment_mean(sq, seg):
    hi = sq.astype(BF16)
    lo = (sq - hi.astype(F32)).astype(BF16)
    return _dot(jnp.concatenate([hi, lo], axis=-1), seg)


def _rms_rows(x, gain):
    ms = jnp.mean(x * x, axis=-1, keepdims=True)
    return (x * lax.rsqrt(ms + EPS)) * gain


def _project(h, wq_ref, w_in_ref, qg, kg, sgug, seg64, seg128, between=None):
    qkv = jnp.concatenate([_dot(h, wq_ref[...]), _dot(h, w_in_ref[:, Q_END:V_END])], axis=-1)
    qk_cols = []
    for c in range(K_END // LANES):
        blk = qkv[:, c * LANES:(c + 1) * LANES]
        ms = _segment_mean(blk * blk, seg64)
        g = qg if c < Q_END // LANES else kg
        qk_cols.append((blk * lax.rsqrt(ms + EPS)) * g)
    qn = jnp.concatenate(qk_cols[:Q_END // LANES], axis=-1)
    kn = qk_cols[Q_END // LANES]
    v = qkv[:, K_END:V_END]
    u = jax.nn.gelu(_dot(h, w_in_ref[:, V_END:U_END]))
    vg = jax.nn.gelu(_dot(h, w_in_ref[:, U_END:VM_END]))
    vm_cols = []
    for g in range(MLP_GROUPS):
        blk = vg[:, g * LANES:(g + 1) * LANES]
        ms = _segment_mean(blk * blk, seg128)
        vm_cols.append((blk * lax.rsqrt(ms + EPS)) * sgug[:, g * LANES:(g + 1) * LANES])
    vm = jnp.concatenate(vm_cols, axis=-1)
    if between is not None:
        between()
    ga = jax.nn.sigmoid(_dot(h, w_in_ref[:, VM_END:GA_END]))
    gb = jax.nn.sigmoid(_dot(h, w_in_ref[:, GA_END:IN_COLS]))
    return qn, kn, v, u, vm, ga, gb


def _stack_heads(q_rows):
    lane = lax.broadcasted_iota(jnp.int32, (q_rows.shape[0], LANES), 1)
    blocks = []
    for head in range(N_HEADS):
        j, half = head % Q_PER_KV, head // Q_PER_KV
        col = q_rows[:, j * LANES:(j + 1) * LANES]
        keep = (lane < HEAD_DIM) if half == 0 else (lane >= HEAD_DIM)
        blocks.append(jnp.where(keep, col, 0.0))
    return jnp.concatenate(blocks, axis=0).astype(BF16)


def _unstack_heads(o, rows):
    lane = lax.broadcasted_iota(jnp.int32, (rows, LANES), 1)
    cols = []
    for j in range(Q_PER_KV):
        lo = o[j * rows:(j + 1) * rows]
        hi = o[(j + Q_PER_KV) * rows:(j + Q_PER_KV + 1) * rows]
        cols.append(jnp.where(lane < HEAD_DIM, lo, hi))
    return jnp.concatenate(cols, axis=-1)


def _merge(att, sgu, ga, gb, wba_ref, wbs_ref, wout_ref):
    m = ga * _dot(att.astype(BF16), wba_ref[...]) + gb * _dot(sgu.astype(BF16), wbs_ref[...])
    return _dot(m.astype(BF16), wout_ref[...])


def _route(h2, wr_ref, br_ref, tri_ref):
    rows = h2.shape[0]
    logits = _dot(h2, wr_ref[...]) + br_ref[...]
    eidx = lax.broadcasted_iota(jnp.int32, (rows, N_EXPERTS), 1).astype(F32)
    work = logits
    vals, picks, onehots = [], [], []
    for _ in range(TOP_K):
        m = jnp.max(work, axis=-1, keepdims=True)
        sel = jnp.min(jnp.where(work == m, eidx, float(N_EXPERTS)), axis=-1, keepdims=True)
        oh = eidx == sel
        vals.append(m)
        picks.append(sel)
        onehots.append(oh)
        work = jnp.where(oh, -jnp.inf, work)
    exps = [jnp.exp(v - vals[0]) for v in vals]
    den = exps[0] + exps[1] + exps[2] + exps[3]
    mask = jnp.zeros((rows, N_EXPERTS), F32)
    for oh in onehots:
        mask = mask + jnp.where(oh, 1.0, 0.0)
    before = _dot(tri_ref[...], mask.astype(BF16))
    counts = jnp.sum(mask, axis=0, keepdims=True)
    k4 = lax.broadcasted_iota(jnp.int32, (rows, TOP_K), 1)
    w4 = jnp.zeros((rows, TOP_K), F32)
    pos4 = jnp.zeros((rows, TOP_K), jnp.int32)
    for k in range(TOP_K):
        lower_experts = jnp.sum(jnp.where(eidx < picks[k], counts, 0.0), axis=-1, keepdims=True)
        rank = jnp.sum(jnp.where(onehots[k], before, 0.0), axis=-1, keepdims=True)
        w4 = jnp.where(k4 == k, exps[k] / den, w4)
        pos4 = jnp.where(k4 == k, (lower_experts + rank).astype(jnp.int32), pos4)
    return w4, pos4, counts


def _pair_selector(pos4, values=None):
    rows = pos4.shape[0]
    col = lax.broadcasted_iota(jnp.int32, (rows, TOP_K * rows), 1)
    sel = jnp.zeros((rows, TOP_K * rows), F32)
    for k in range(TOP_K):
        sel = sel + jnp.where(col == pos4[:, k:k + 1], 1.0 if values is None else values[:, k:k + 1], 0.0)
    return sel


def _route_block(h2, wr_ref, br_ref, tri_ref, gw_ref, pos_ref, cnt_ref):
    w4, pos4, counts = _route(h2, wr_ref, br_ref, tri_ref)
    gw_ref[...] = w4
    pos_ref[...] = pos4
    cnt_ref[0] = counts
    return pos4


def _dispatch_block(pos4, h2, xc_ref):
    select = _pair_selector(pos4).astype(BF16)
    pairs = lax.dot_general(select, h2, (((0,), (0,)), ((), ())), preferred_element_type=F32)
    _store_token_major(xc_ref, pairs)


def _route_and_dispatch(h2, wr_ref, br_ref, tri_ref, gw_ref, pos_ref, cnt_ref, xc_ref):
    _dispatch_block(_route_block(h2, wr_ref, br_ref, tri_ref, gw_ref, pos_ref, cnt_ref), h2, xc_ref)


def _softmax_rows(parts, sink):
    m = sink
    for l in parts:
        m = jnp.maximum(m, jnp.max(l, axis=-1, keepdims=True))
    es = [jnp.exp(l - m) for l in parts]
    den = jnp.exp(sink - m)
    for e in es:
        den = den + jnp.sum(e, axis=-1, keepdims=True)
    return [e / den for e in es]


def _mixer_prompt_body(sinks_ref, x_ref, wq_ref, w_in_ref, gmix_ref, qg_ref, kg_ref, sgug_ref, seg64_ref, seg128_ref,
                       bias_ref, wsp_ref, bsp_ref, wba_ref, wbs_ref, wout_ref, gffn_ref, wr_ref, br_ref, tri_ref,
                       x1_ref, xc_ref, gw_ref, pos_ref, cnt_ref, kwin_ref, vwin_ref, sguv_ref,
                       kcarry, vcarry, h2_prev_ref, *, blocks_per_seq, n_blocks):
    i = pl.program_id(0)

    @pl.when(i == 0)
    def _():
        h2_prev_ref[...] = jnp.zeros_like(h2_prev_ref)

    def route_previous():
        return _route_block(h2_prev_ref[...], wr_ref, br_ref, tri_ref, gw_ref, pos_ref, cnt_ref)

    def dispatch_previous(pos4):
        _dispatch_block(pos4, h2_prev_ref[...], xc_ref)

    @pl.when(i == n_blocks)
    def _():
        dispatch_previous(route_previous())

    @pl.when(i == n_blocks + 1)
    def _():
        xc_ref[...] = jnp.zeros_like(xc_ref)
        gw_ref[...] = jnp.zeros_like(gw_ref)
        pos_ref[...] = jnp.zeros_like(pos_ref)
        cnt_ref[...] = jnp.zeros_like(cnt_ref)

    @pl.when(jnp.logical_and(i < n_blocks, i % blocks_per_seq == 0))
    def _():
        kcarry[...] = jnp.zeros_like(kcarry)
        vcarry[...] = jnp.zeros_like(vcarry)

    @pl.when(i < n_blocks)
    def _():
        _mixer_prompt_block(i % blocks_per_seq, sinks_ref, x_ref, wq_ref, w_in_ref, gmix_ref, qg_ref, kg_ref, sgug_ref,
                            seg64_ref, seg128_ref, bias_ref, wsp_ref, bsp_ref, wba_ref, wbs_ref, wout_ref, gffn_ref,
                            x1_ref, kwin_ref, vwin_ref, sguv_ref, kcarry, vcarry, h2_prev_ref,
                            route_previous, dispatch_previous)


def _mixer_prompt_block(j, sinks_ref, x_ref, wq_ref, w_in_ref, gmix_ref, qg_ref, kg_ref, sgug_ref, seg64_ref, seg128_ref,
                        bias_ref, wsp_ref, bsp_ref, wba_ref, wbs_ref, wout_ref, gffn_ref,
                        x1_ref, kwin_ref, vwin_ref, sguv_ref, kcarry, vcarry, h2_prev_ref,
                        route_previous, dispatch_previous):
    routed = []
    x = x_ref[0]
    h = _rms_rows(x, gmix_ref[...]).astype(BF16)
    qn, kn, v, u, vm, ga, gb = _project(h, wq_ref, w_in_ref, qg_ref[...], kg_ref[...], sgug_ref[...],
                                        seg64_ref[...], seg128_ref[...],
                                        between=lambda: routed.append(route_previous()))
    k_ext = jnp.concatenate([kcarry[...], kn.astype(BF16)], axis=0)
    v_ext = jnp.concatenate([vcarry[...], v.astype(BF16)], axis=0)
    kcarry[...] = k_ext[TOKEN_BLOCK:]
    vcarry[...] = v_ext[TOKEN_BLOCK:]

    tri_mask = (lax.broadcasted_iota(jnp.int32, (MLP_CHUNK, MLP_CHUNK), 0)
                >= lax.broadcasted_iota(jnp.int32, (MLP_CHUNK, MLP_CHUNK), 1))
    att_rows, sgu_rows = [], []
    for pm in range(TOKEN_BLOCK // PAIR):
        r0 = pm * PAIR
        q_stack = _stack_heads(qn[r0:r0 + PAIR])
        k_band = k_ext[r0:r0 + BAND]
        v_band = v_ext[r0:r0 + BAND]
        s = _dot_nt(q_stack, k_band)
        first = jnp.where(j == 0, 0, 1) if pm == 0 else 1
        probs = []
        for head in range(N_HEADS):
            logit = s[head * PAIR:(head + 1) * PAIR] * ATTN_SCALE + bias_ref[first, head]
            probs.append(_softmax_rows([logit], sinks_ref[head])[0].astype(BF16))
        o = _dot(jnp.concatenate(probs, axis=0), v_band)
        att_rows.append(_unstack_heads(o, PAIR))
        cols = []
        for g in range(MLP_GROUPS):
            wm = jnp.where(tri_mask, wsp_ref[g], 0.0).astype(BF16)
            cols.append(_dot(wm, vm[r0:r0 + PAIR, g * LANES:(g + 1) * LANES].astype(BF16)))
        mixed = jnp.concatenate(cols, axis=-1) + bsp_ref[...]
        sgu_rows.append(u[r0:r0 + PAIR] * mixed)
        if pm == 0:
            dispatch_previous(routed[0])
    att = jnp.concatenate(att_rows, axis=0)
    sgu = jnp.concatenate(sgu_rows, axis=0)

    x1 = x + _merge(att, sgu, ga, gb, wba_ref, wbs_ref, wout_ref)
    h2 = _rms_rows(x1, gffn_ref[...])
    h2_prev_ref[...] = h2.astype(BF16)

    x1_ref[0] = x1
    kwin_ref[0] = kn[TOKEN_BLOCK - WINDOW:]
    vwin_ref[0] = v[TOKEN_BLOCK - WINDOW:]
    for g in range(MLP_GROUPS):
        sguv_ref[0, :, g, :] = vm[TOKEN_BLOCK - MLP_CHUNK:, g * LANES:(g + 1) * LANES]


def _mixer_sample_body(sinks_ref, x_ref, ck_ref, cv_ref, wq_ref, w_in_ref, gmix_ref, qg_ref, kg_ref, sgug_ref, seg64_ref,
                       seg128_ref, biasc_ref, biasn_ref, wsp_ref, bsp_ref, wba_ref, wbs_ref, wout_ref, gffn_ref,
                       wr_ref, br_ref, tri_ref, xc_in_ref, gw_in_ref, pos_in_ref, cnt_in_ref,
                       x1_ref, xc_ref, gw_ref, pos_ref, cnt_ref, kwin_ref, vwin_ref, sguv_ref, *, n_streams, n_new):
    del xc_in_ref, gw_in_ref, pos_in_ref, cnt_in_ref
    x = x_ref[...]
    h = _rms_rows(x, gmix_ref[...]).astype(BF16)
    qn, kn, v, u, vm, ga, gb = _project(h, wq_ref, w_in_ref, qg_ref[...], kg_ref[...], sgug_ref[...],
                                        seg64_ref[...], seg128_ref[...])
    n_cached = ck_ref.shape[1]
    att_rows = []
    for s_i in range(n_streams):
        r0 = s_i * n_new
        q_stack = _stack_heads(qn[r0:r0 + n_new])
        k_new = kn[r0:r0 + n_new]
        v_new = v[r0:r0 + n_new]
        s_c = _dot_nt(q_stack, ck_ref[s_i].astype(BF16))
        s_n = _dot_nt(q_stack, k_new.astype(BF16))
        pc, pn = [], []
        for head in range(N_HEADS):
            rows = slice(head * n_new, (head + 1) * n_new)
            lc = s_c[rows] * ATTN_SCALE + biasc_ref[rows]
            ln = s_n[rows] * ATTN_SCALE + biasn_ref[rows]
            p_c, p_n = _softmax_rows([lc, ln], sinks_ref[head])
            pc.append(p_c.astype(BF16))
            pn.append(p_n.astype(BF16))
        o = (_dot(jnp.concatenate(pc, axis=0), cv_ref[s_i].astype(BF16))
             + _dot(jnp.concatenate(pn, axis=0), v_new.astype(BF16)))
        att_rows.append(_unstack_heads(o, n_new))
        kwin_ref[s_i, 0:n_cached - n_new] = ck_ref[s_i, n_new:n_cached]
        kwin_ref[s_i, n_cached - n_new:n_cached] = k_new
        vwin_ref[s_i, 0:n_cached - n_new] = cv_ref[s_i, n_new:n_cached]
        vwin_ref[s_i, n_cached - n_new:n_cached] = v_new
    att = jnp.concatenate(att_rows, axis=0)

    rows = n_streams * n_new
    ri = lax.broadcasted_iota(jnp.int32, (rows, rows), 0)
    ci = lax.broadcasted_iota(jnp.int32, (rows, rows), 1)
    keep = jnp.logical_and(ri // n_new == ci // n_new, ri % n_new >= ci % n_new)
    expand = (lax.broadcasted_iota(jnp.int32, (rows, n_new), 0) % n_new
              == lax.broadcasted_iota(jnp.int32, (rows, n_new), 1)).astype(BF16)
    cols = []
    for g in range(MLP_GROUPS):
        tiled = _dot_nt(_dot(expand, wsp_ref[g].astype(BF16)).astype(BF16), expand)
        wm = jnp.where(keep, tiled, 0.0).astype(BF16)
        cols.append(_dot(wm, vm[:, g * LANES:(g + 1) * LANES].astype(BF16)))
    sgu = u * (jnp.concatenate(cols, axis=-1) + bsp_ref[...])

    x1 = x + _merge(att, sgu, ga, gb, wba_ref, wbs_ref, wout_ref)
    h2 = _rms_rows(x1, gffn_ref[...])
    _route_and_dispatch(h2.astype(BF16), wr_ref, br_ref, tri_ref, gw_ref, pos_ref, cnt_ref, xc_ref)
    x1_ref[...] = x1
    for g in range(MLP_GROUPS):
        sguv_ref[:, g, :] = vm[:, g * LANES:(g + 1) * LANES]


def _piece_copies(i_tile, lo_ref, src_ref, off_ref, len_ref, hbm, buf, sem, to_hbm):
    def one(p, carry):
        n = pl.multiple_of(len_ref[p], ROW_TILE)

        @pl.when(n > 0)
        def _():
            far = hbm.at[pl.ds(pl.multiple_of(src_ref[p], ROW_TILE), n), :]
            near = buf.at[pl.ds(pl.multiple_of(off_ref[p], ROW_TILE), n), :]
            if to_hbm:
                pltpu.make_async_copy(near, far, sem).start()
            else:
                pltpu.make_async_copy(far, near, sem).start()
        return carry
    lax.fori_loop(lo_ref[i_tile], lo_ref[i_tile + 1], one, 0)


def _wait_rows(hbm, buf, sem, n_rows):
    n = pl.multiple_of(n_rows, ROW_TILE)
    pltpu.make_async_copy(hbm.at[pl.ds(0, n), :], buf.at[pl.ds(0, n), :], sem).wait()


def _expert_mlp(x_cur, y_cur, bgu_ref, bd_ref, wgu_bf, wd_bf, n_rows):
    rows = pl.ds(0, n_rows * ROW_TILE)
    xb = _load_token_major(x_cur.at[rows, :], n_rows).astype(BF16)
    hgu = _dot(xb, wgu_bf[...]) + bgu_ref[0]
    glu = jnp.minimum(hgu[:, :D_EXPERT], SWIGLU_LIMIT)
    lin = jnp.clip(hgu[:, D_EXPERT:], -SWIGLU_LIMIT, SWIGLU_LIMIT)
    act = glu * jax.nn.sigmoid(glu * SWIGLU_ALPHA) * (lin + 1.0)
    _store_token_major(y_cur.at[rows, :], _dot(act.astype(BF16), wd_bf[...]) + bd_ref[0])


def _weight_copies(wgu_hbm, wd_hbm, wgu_stage, wd_stage, sem_w, expert):
    return (pltpu.make_async_copy(wgu_hbm.at[expert], wgu_stage, sem_w.at[0]),
            pltpu.make_async_copy(wd_hbm.at[expert], wd_stage, sem_w.at[1]))


def _expert_body(te_ref, nxt_ref, nused_ref, valid_ref, lo_ref, src_ref, off_ref, len_ref, xc_hbm, wgu_hbm, bgu_ref,
                 wd_hbm, bd_ref, yc_hbm, xbuf0, xbuf1, ybuf0, ybuf1, wgu_stage, wd_stage, wgu_bf, wd_bf,
                 sem_in, sem_out, sem_w):
    i = pl.program_id(0)
    n_used = nused_ref[0]
    xbufs, ybufs = (xbuf0, xbuf1), (ybuf0, ybuf1)
    pieces = (lo_ref, src_ref, off_ref, len_ref)

    @pl.when(i == 0)
    def _():
        xbuf0[...] = jnp.zeros_like(xbuf0)
        xbuf1[...] = jnp.zeros_like(xbuf1)
        for cp in _weight_copies(wgu_hbm, wd_hbm, wgu_stage, wd_stage, sem_w, te_ref[0]):
            cp.start(priority=1)
        _piece_copies(0, *pieces, xc_hbm, xbuf0, sem_in.at[0], to_hbm=False)

    @pl.when(jnp.logical_and(i < n_used, jnp.logical_or(i == 0, te_ref[i] != te_ref[jnp.maximum(i - 1, 0)])))
    def _():
        for cp in _weight_copies(wgu_hbm, wd_hbm, wgu_stage, wd_stage, sem_w, te_ref[i]):
            cp.wait()
        wgu_bf[...] = wgu_stage[...].astype(BF16)
        wd_bf[...] = wd_stage[...].astype(BF16)

        @pl.when(nxt_ref[i] >= 0)
        def _():
            for cp in _weight_copies(wgu_hbm, wd_hbm, wgu_stage, wd_stage, sem_w, nxt_ref[i]):
                cp.start(priority=1)

    for par in range(2):
        x_cur, x_next, y_cur, y_prev = xbufs[par], xbufs[1 - par], ybufs[par], ybufs[1 - par]

        @pl.when(jnp.logical_and(i < n_used, i % 2 == par))
        def _(x_cur=x_cur, x_next=x_next, y_cur=y_cur, y_prev=y_prev, par=par):
            _wait_rows(xc_hbm, x_cur, sem_in.at[par], valid_ref[i])

            @pl.when(i + 1 < n_used)
            def _():
                _piece_copies(i + 1, *pieces, xc_hbm, x_next, sem_in.at[1 - par], to_hbm=False)

            @pl.when(i > 0)
            def _():
                _piece_copies(i - 1, *pieces, yc_hbm, y_prev, sem_out, to_hbm=True)

            quarters = (valid_ref[i] + (TILE_QUARTER * ROW_TILE - 1)) // (TILE_QUARTER * ROW_TILE)
            for q in range(1, EXPERT_TILE // TILE_QUARTER + 1):
                @pl.when(quarters == q)
                def _(q=q):
                    _expert_mlp(x_cur, y_cur, bgu_ref, bd_ref, wgu_bf, wd_bf, q * TILE_QUARTER)

            @pl.when(i > 0)
            def _():
                _wait_rows(yc_hbm, y_prev, sem_out, valid_ref[i - 1])

        @pl.when(jnp.logical_and(i == n_used, i % 2 == par))
        def _(y_prev=y_prev):
            _piece_copies(i - 1, *pieces, yc_hbm, y_prev, sem_out, to_hbm=True)
            _wait_rows(yc_hbm, y_prev, sem_out, valid_ref[i - 1])


def _combine_body(pos_ref, gw_ref, x1_ref, yc_ref, out_ref, acc_ref):
    rows = x1_ref.shape[0]
    _store_token_major(acc_ref, x1_ref[...])
    for t in range(rows):
        row = acc_ref[pl.ds(t * ROW_TILE, ROW_TILE), :]
        for k in range(TOP_K):
            flat = t * TOP_K + k
            at = pl.multiple_of(pos_ref[0, flat // LANES, flat % LANES], ROW_TILE)
            row = row + gw_ref[0, flat // LANES, flat % LANES] * yc_ref[pl.ds(at, ROW_TILE), :]
        acc_ref[pl.ds(t * ROW_TILE, ROW_TILE), :] = row
    out_ref[...] = _load_token_major(acc_ref, rows)


def _const_spec(shape):
    nd = len(shape)
    return pl.BlockSpec(shape, lambda *_: (0,) * nd, pipeline_mode=pl.Buffered(1))


def _q_perm():
    cols = np.arange(ATTN_WIDTH)
    j, half, d = cols // LANES, (cols % LANES) // HEAD_DIM, cols % HEAD_DIM
    return (j + Q_PER_KV * half) * HEAD_DIM + d


def _alibi_slopes():
    return 2.0 ** (-8.0 * np.arange(1, N_HEADS + 1) / N_HEADS)


def _prompt_bias():
    qi = np.arange(PAIR)[:, None]
    kj = np.arange(BAND)[None, :]
    dist = np.abs(qi + WINDOW - kj).astype(np.float64)
    cq, ck = qi // CHUNK, kj // CHUNK
    in_band = (ck >= cq) & (ck <= cq + WINDOW // CHUNK)
    base = -_alibi_slopes()[:, None, None] * dist[None]
    later = np.where(in_band[None], base, NEG_INF)
    first = np.where((kj >= WINDOW)[None], later, NEG_INF)
    return np.stack([first, later]).astype(np.float32)


def _sample_bias(n_new, n_cached):
    qi = np.arange(n_new)[:, None]
    dc = np.abs(qi + n_cached - np.arange(n_cached)[None, :]).astype(np.float64)
    dn = np.abs(qi - np.arange(n_new)[None, :]).astype(np.float64)
    sl = _alibi_slopes()[:, None, None]
    bc = (-sl * dc[None]).reshape(N_HEADS * n_new, n_cached)
    bn = (-sl * dn[None]).reshape(N_HEADS * n_new, n_new)
    return bc.astype(np.float32), bn.astype(np.float32)


def kernel(x_prompt, x_sample, cache_k_win, cache_v_win, g_mix, w_in, q_norm_g, k_norm_g, attn_sinks, sgu_norm_g,
           w_spatial, b_spatial, w_branch_attn, w_branch_sgu, w_out, g_ffn, w_router, b_router, w_gate_up,
           b_gate_up, w_down, b_down):
    n_b, seq, _ = x_prompt.shape
    n_streams, n_new, _ = x_sample.shape
    n_cached = cache_k_win.shape[2]
    n_prompt = n_b * seq
    n_sample = n_streams * n_new
    n_tok = n_prompt + n_sample
    assert seq % TOKEN_BLOCK == 0 and n_sample == TOKEN_BLOCK and n_cached == WINDOW
    blocks_per_seq = seq // TOKEN_BLOCK
    n_prompt_blocks = n_prompt // TOKEN_BLOCK

    perm = _q_perm()
    w_in_l = w_in[0]
    w_in_b = w_in_l.astype(BF16)
    halves = N_HEADS // Q_PER_KV
    wq_b = (w_in_l[:, :Q_END].reshape(D_MODEL, halves, Q_PER_KV, HEAD_DIM).transpose(0, 2, 1, 3)
            .reshape(D_MODEL, Q_END).astype(BF16))
    wba_b = w_branch_attn[0][perm, :].astype(BF16)
    wbs_b = w_branch_sgu[0].astype(BF16)
    wout_b = w_out[0].astype(BF16)
    wr_b = w_router[0].astype(BF16)
    br = b_router[0].reshape(1, N_EXPERTS).astype(F32)
    gmix = g_mix[0].reshape(1, D_MODEL)
    gffn = g_ffn[0].reshape(1, D_MODEL)
    qg = jnp.tile(q_norm_g[0], LANES // HEAD_DIM).reshape(1, LANES)
    kg = jnp.tile(k_norm_g[0], LANES // HEAD_DIM).reshape(1, LANES)
    sgug = sgu_norm_g[0].reshape(1, MLP_WIDTH)
    sinks = attn_sinks[0].astype(F32)
    lane_seg = np.arange(LANES) // HEAD_DIM
    seg64 = jnp.asarray(np.tile((lane_seg[:, None] == lane_seg[None, :]) / HEAD_DIM, (2, 1)), BF16)
    seg128 = jnp.full((2 * LANES, LANES), 1.0 / MLP_GROUP_DIM, BF16)
    tri = jnp.asarray(np.tril(np.ones((TOKEN_BLOCK, TOKEN_BLOCK)), -1), BF16)
    wsp = w_spatial[0]
    bsp_p = jnp.repeat(b_spatial[0].T, MLP_GROUP_DIM, axis=1)
    wsp_s = wsp[:, :n_new, :n_new]
    bsp_s = jnp.tile(jnp.repeat(b_spatial[0][:, :n_new].T, MLP_GROUP_DIM, axis=1), (n_streams, 1))
    bias_p = jnp.asarray(_prompt_bias())
    bias_c, bias_n = (jnp.asarray(a) for a in _sample_bias(n_new, n_cached))

    smem = pl.BlockSpec(memory_space=pltpu.SMEM)
    tb = TOKEN_BLOCK
    pb = TOP_K * tb
    n_tok_blocks = n_tok // tb
    last_blk = n_prompt_blocks - 1
    seq_of = lambda i: jnp.minimum(i, last_blk) // blocks_per_seq
    x_spec = pl.BlockSpec((1, tb, D_MODEL), lambda i: (seq_of(i), jnp.minimum(i, last_blk) % blocks_per_seq, 0))
    routed_blk = lambda i: jnp.where(i > n_prompt_blocks, n_prompt_blocks, jnp.clip(i - 1, 0, last_blk))
    win_spec = lambda width: pl.BlockSpec((1, WINDOW, width), lambda i: (seq_of(i), 0, 0))
    route_specs = lambda blk: [pl.BlockSpec((pb * ROW_TILE, LANES), lambda i: (blk(i), 0)),
                               pl.BlockSpec((tb, TOP_K), lambda i: (blk(i), 0)),
                               pl.BlockSpec((tb, TOP_K), lambda i: (blk(i), 0)),
                               pl.BlockSpec((1, 1, N_EXPERTS), lambda i: (blk(i), 0, 0))]
    route_shapes = [jax.ShapeDtypeStruct((n_tok * TOP_K * ROW_TILE, LANES), F32),
                    jax.ShapeDtypeStruct((n_tok, TOP_K), F32),
                    jax.ShapeDtypeStruct((n_tok, TOP_K), jnp.int32),
                    jax.ShapeDtypeStruct((n_tok_blocks, 1, N_EXPERTS), F32)]

    x1p, xc, gw_p, pos_p, cnt_p, kwin_p, vwin_p, sguv_p = pl.pallas_call(
        functools.partial(_mixer_prompt_body, blocks_per_seq=blocks_per_seq, n_blocks=n_prompt_blocks),
        grid=(n_prompt_blocks + 2,),
        in_specs=[smem, x_spec, _const_spec((D_MODEL, Q_END)),
                  _const_spec((D_MODEL, IN_COLS)), _const_spec((1, D_MODEL)), _const_spec((1, LANES)),
                  _const_spec((1, LANES)), _const_spec((1, MLP_WIDTH)), _const_spec((2 * LANES, LANES)),
                  _const_spec((2 * LANES, LANES)), _const_spec((2, N_HEADS, PAIR, BAND)),
                  _const_spec((MLP_GROUPS, MLP_CHUNK, MLP_CHUNK)), _const_spec((MLP_CHUNK, MLP_WIDTH)),
                  _const_spec((ATTN_WIDTH, D_MODEL)), _const_spec((MLP_WIDTH, D_MODEL)),
                  _const_spec((D_MODEL, D_MODEL)), _const_spec((1, D_MODEL)), _const_spec((D_MODEL, N_EXPERTS)),
                  _const_spec((1, N_EXPERTS)), _const_spec((tb, tb))],
        out_specs=[x_spec, *route_specs(routed_blk),
                   win_spec(KV_WIDTH), win_spec(KV_WIDTH),
                   pl.BlockSpec((1, MLP_CHUNK, MLP_GROUPS, MLP_GROUP_DIM), lambda i: (seq_of(i), 0, 0, 0))],
        out_shape=[jax.ShapeDtypeStruct((n_b, seq, D_MODEL), F32), *route_shapes,
                   jax.ShapeDtypeStruct((n_b, WINDOW, KV_WIDTH), F32),
                   jax.ShapeDtypeStruct((n_b, WINDOW, KV_WIDTH), F32),
                   jax.ShapeDtypeStruct((n_b, MLP_CHUNK, MLP_GROUPS, MLP_GROUP_DIM), F32)],
        scratch_shapes=[pltpu.VMEM((WINDOW, KV_WIDTH), BF16), pltpu.VMEM((WINDOW, KV_WIDTH), BF16),
                        pltpu.VMEM((tb, D_MODEL), BF16)],
        compiler_params=pltpu.CompilerParams(dimension_semantics=("arbitrary",), vmem_limit_bytes=VMEM_LIMIT),
        name="mixer_prompt",
    )(sinks, x_prompt, wq_b, w_in_b, gmix, qg, kg, sgug, seg64, seg128, bias_p, wsp, bsp_p, wba_b, wbs_b, wout_b,
      gffn, wr_b, br, tri)

    full = lambda shape: pl.BlockSpec(shape, lambda i: (0,) * len(shape))
    any_spec = pl.BlockSpec(memory_space=pl.ANY)
    ck = cache_k_win[0].reshape(n_streams, n_cached, KV_WIDTH)
    cv = cache_v_win[0].reshape(n_streams, n_cached, KV_WIDTH)
    x1s, xc, gw_a, pos_a, cnt_a, kwin_s, vwin_s, sguv_s = pl.pallas_call(
        functools.partial(_mixer_sample_body, n_streams=n_streams, n_new=n_new),
        grid=(1,),
        in_specs=[smem, full((n_sample, D_MODEL)), full((n_streams, n_cached, KV_WIDTH)),
                  full((n_streams, n_cached, KV_WIDTH)),
                  full((D_MODEL, Q_END)), full((D_MODEL, IN_COLS)), full((1, D_MODEL)), full((1, LANES)),
                  full((1, LANES)),
                  full((1, MLP_WIDTH)), full((2 * LANES, LANES)), full((2 * LANES, LANES)),
                  full((N_HEADS * n_new, n_cached)), full((N_HEADS * n_new, n_new)),
                  full((MLP_GROUPS, n_new, n_new)), full((n_sample, MLP_WIDTH)),
                  full((ATTN_WIDTH, D_MODEL)), full((MLP_WIDTH, D_MODEL)), full((D_MODEL, D_MODEL)),
                  full((1, D_MODEL)), full((D_MODEL, N_EXPERTS)), full((1, N_EXPERTS)), full((tb, tb)),
                  any_spec, any_spec, any_spec, any_spec],
        out_specs=[full((n_sample, D_MODEL)), *route_specs(lambda i: n_prompt_blocks),
                   full((n_streams, n_cached, KV_WIDTH)), full((n_streams, n_cached, KV_WIDTH)),
                   full((n_sample, MLP_GROUPS, MLP_GROUP_DIM))],
        out_shape=[jax.ShapeDtypeStruct((n_sample, D_MODEL), F32), *route_shapes,
                   jax.ShapeDtypeStruct((n_streams, n_cached, KV_WIDTH), F32),
                   jax.ShapeDtypeStruct((n_streams, n_cached, KV_WIDTH), F32),
                   jax.ShapeDtypeStruct((n_sample, MLP_GROUPS, MLP_GROUP_DIM), F32)],
        input_output_aliases={23: 1, 24: 2, 25: 3, 26: 4},
        compiler_params=pltpu.CompilerParams(dimension_semantics=("arbitrary",), vmem_limit_bytes=VMEM_LIMIT),
        name="mixer_sample",
    )(sinks, x_sample.reshape(n_sample, D_MODEL), ck, cv, wq_b, w_in_b, gmix, qg, kg, sgug, seg64, seg128, bias_c,
      bias_n, wsp_s, bsp_s, wba_b, wbs_b, wout_b, gffn, wr_b, br, tri, xc, gw_p, pos_p, cnt_p)

    tm = EXPERT_TILE
    counts = cnt_a[:, 0, :].astype(jnp.int32)
    cnt_all = jnp.sum(counts, axis=0)
    padded = ((cnt_all + tm - 1) // tm) * tm
    pad_end = jnp.cumsum(padded)
    pad_off = pad_end - padded
    n_pairs = TOP_K * n_tok
    n_tiles = n_pairs // tm + N_EXPERTS
    n_steps = n_tiles + 1
    in_block = jnp.cumsum(counts, axis=1) - counts
    in_expert = jnp.cumsum(counts, axis=0) - counts
    run_src = (jnp.arange(n_tok_blocks, dtype=jnp.int32)[:, None] * pb + in_block).T.reshape(-1)
    run_start = (pad_off[None, :] + in_expert).T.reshape(-1)
    run_len = counts.T.reshape(-1)
    head = jnp.minimum(run_len, tm - run_start % tm)
    piece_start = jnp.stack([run_start, run_start + head], axis=1).reshape(-1)
    piece_src = jnp.stack([run_src, run_src + head], axis=1).reshape(-1)
    piece_len = jnp.stack([head, run_len - head], axis=1).reshape(-1)
    piece_tile = piece_start // tm
    tile_ids = jnp.arange(n_steps + 1, dtype=jnp.int32)
    piece_lo = jnp.sum((piece_tile[None, :] < tile_ids[:, None]).astype(jnp.int32), axis=1)
    n_used = (pad_end[-1] // tm).astype(jnp.int32)
    tile_start = jnp.minimum(tile_ids[:n_steps], n_used - 1) * tm
    tile_expert = jnp.sum((pad_end[None, :] <= tile_start[:, None]).astype(jnp.int32), axis=1)
    tile_expert = jnp.minimum(tile_expert, N_EXPERTS - 1)
    of_tile = tile_expert[:, None] == jnp.arange(N_EXPERTS, dtype=jnp.int32)[None, :]
    rows_end = jnp.sum(jnp.where(of_tile, (pad_off + cnt_all)[None, :], 0), axis=1)
    run_end = jnp.sum(jnp.where(of_tile, pad_end[None, :], 0), axis=1)
    tile_valid = jnp.clip(rows_end - tile_start, 0, tm)
    next_expert = jnp.sum((pad_end[None, :] <= run_end[:, None]).astype(jnp.int32), axis=1)
    next_expert = jnp.where(run_end < pad_end[-1], jnp.minimum(next_expert, N_EXPERTS - 1), -1)

    prefetch = (tile_expert, next_expert, n_used.reshape(1), tile_valid * ROW_TILE, piece_lo,
                piece_src * ROW_TILE, (piece_start % tm) * ROW_TILE, piece_len * ROW_TILE)
    yc = pl.pallas_call(
        _expert_body,
        grid_spec=pltpu.PrefetchScalarGridSpec(
            num_scalar_prefetch=len(prefetch),
            grid=(n_steps,),
            in_specs=[any_spec, any_spec,
                      pl.BlockSpec((1, 1, 2 * D_EXPERT), lambda i, te, *_: (te[i], 0, 0)),
                      any_spec,
                      pl.BlockSpec((1, 1, D_MODEL), lambda i, te, *_: (te[i], 0, 0))],
            out_specs=any_spec,
            scratch_shapes=[pltpu.VMEM((tm * ROW_TILE, LANES), F32), pltpu.VMEM((tm * ROW_TILE, LANES), F32),
                            pltpu.VMEM((tm * ROW_TILE, LANES), F32), pltpu.VMEM((tm * ROW_TILE, LANES), F32),
                            pltpu.VMEM((D_MODEL, 2 * D_EXPERT), F32), pltpu.VMEM((D_EXPERT, D_MODEL), F32),
                            pltpu.VMEM((D_MODEL, 2 * D_EXPERT), BF16), pltpu.VMEM((D_EXPERT, D_MODEL), BF16),
                            pltpu.SemaphoreType.DMA((2,)), pltpu.SemaphoreType.DMA(()),
                            pltpu.SemaphoreType.DMA((2,))]),
        out_shape=jax.ShapeDtypeStruct((n_pairs * ROW_TILE, LANES), F32),
        compiler_params=pltpu.CompilerParams(dimension_semantics=("arbitrary",), vmem_limit_bytes=VMEM_LIMIT),
        name="moe_experts",
    )(*[p.astype(jnp.int32) for p in prefetch], xc, w_gate_up[0],
      b_gate_up[0].reshape(N_EXPERTS, 1, 2 * D_EXPERT), w_down[0], b_down[0].reshape(N_EXPERTS, 1, D_MODEL))

    pos_tiles = (pos_a * ROW_TILE).reshape(n_tok_blocks, pb // LANES, LANES)
    gw_tiles = gw_a.reshape(n_tok_blocks, pb // LANES, LANES)

    def combine(x1, first_block, n_blocks):
        scalars = pl.BlockSpec((1, pb // LANES, LANES), lambda i: (first_block + i, 0, 0), memory_space=pltpu.SMEM)
        return pl.pallas_call(
            _combine_body,
            grid=(n_blocks,),
            in_specs=[scalars, scalars,
                      pl.BlockSpec((tb, D_MODEL), lambda i: (i, 0)),
                      pl.BlockSpec((pb * ROW_TILE, LANES), lambda i: (first_block + i, 0))],
            out_specs=pl.BlockSpec((tb, D_MODEL), lambda i: (i, 0)),
            out_shape=jax.ShapeDtypeStruct(x1.shape, F32),
            scratch_shapes=[pltpu.VMEM((tb * ROW_TILE, LANES), F32)],
            compiler_params=pltpu.CompilerParams(dimension_semantics=("arbitrary",), vmem_limit_bytes=VMEM_LIMIT),
            name="moe_combine",
        )(pos_tiles, gw_tiles, x1, yc)

    y_prompt = combine(x1p.reshape(n_prompt, D_MODEL), 0, n_prompt_blocks).reshape(n_b, seq, D_MODEL)
    y_sample = combine(x1s, n_prompt_blocks, 1).reshape(n_streams, n_new, D_MODEL)

    kv_shape = (N_KV_HEADS, HEAD_DIM)
    sg_shape = (MLP_GROUPS, MLP_GROUP_DIM)
    return (y_prompt, y_sample,
            kwin_p.reshape(1, n_b, WINDOW, *kv_shape), vwin_p.reshape(1, n_b, WINDOW, *kv_shape),
            kwin_s.reshape(1, n_streams, n_cached, *kv_shape), vwin_s.reshape(1, n_streams, n_cached, *kv_shape),
            sguv_p.reshape(1, n_b, MLP_CHUNK, *sg_shape), sguv_s.reshape(1, n_streams, n_new, *sg_shape))
```

```python
import functools

import jax
import jax.numpy as jnp
import numpy as np
from jax import lax
from jax.experimental import pallas as pl
from jax.experimental.pallas import tpu as pltpu

D_MODEL = 1024
CHUNK = 64
WINDOW = 128
HEAD_DIM = 64
N_HEADS = 8
N_KV_HEADS = 2
Q_PER_KV = N_HEADS // N_KV_HEADS
ATTN_WIDTH = N_HEADS * HEAD_DIM
KV_WIDTH = N_KV_HEADS * HEAD_DIM
ATTN_SCALE = HEAD_DIM ** -0.5
MLP_CHUNK = 128
MLP_GROUPS = 8
MLP_WIDTH = D_MODEL
MLP_GROUP_DIM = MLP_WIDTH // MLP_GROUPS
N_EXPERTS = 32
TOP_K = 4
D_EXPERT = D_MODEL
SWIGLU_ALPHA = 1.702
SWIGLU_LIMIT = 7.0
EPS = 1e-6
NEG_INF = -1e30
Q_END = ATTN_WIDTH
K_END = Q_END + KV_WIDTH
V_END = K_END + KV_WIDTH
U_END = V_END + MLP_WIDTH
VM_END = U_END + MLP_WIDTH
GA_END = VM_END + D_MODEL
IN_COLS = GA_END + D_MODEL

LANES = 128
ROW_TILE = D_MODEL // LANES
TOKEN_BLOCK = 256
PAIR = 2 * CHUNK
BAND = PAIR + WINDOW
EXPERT_TILE = 512
TILE_QUARTER = EXPERT_TILE // 4
VMEM_LIMIT = 56 * 1024 * 1024

F32 = jnp.float32
BF16 = jnp.bfloat16


def _dot(a, b):
    return jnp.dot(a, b, preferred_element_type=F32)


def _dot_nt(a, b):
    return lax.dot_general(a, b, (((1,), (1,)), ((), ())), preferred_element_type=F32)


def _store_token_major(ref, val):
    rows = val.shape[0]
    for s in range(ROW_TILE):
        ref[pl.ds(s, rows, stride=ROW_TILE), :] = val[:, s * LANES:(s + 1) * LANES]


def _load_token_major(ref, rows):
    return jnp.concatenate([ref[pl.ds(s, rows, stride=ROW_TILE), :] for s in range(ROW_TILE)], axis=-1)


def _segment_mean(sq, seg):
    hi = sq.astype(BF16)
    lo = (sq - hi.astype(F32)).astype(BF16)
    return _dot(jnp.concatenate([hi, lo], axis=-1), seg)


def _rms_rows(x, gain):
    ms = jnp.mean(x * x, axis=-1, keepdims=True)
    return (x * lax.rsqrt(ms + EPS)) * gain


def _project(h, wq_ref, w_in_ref, qg, kg, sgug, seg64, seg128, between=None):
    qkv = jnp.concatenate([_dot(h, wq_ref[...]), _dot(h, w_in_ref[:, Q_END:V_END])], axis=-1)
    qk_cols = []
    for c in range(K_END // LANES):
        blk = qkv[:, c * LANES:(c + 1) * LANES]
        ms = _segment_mean(blk * blk, seg64)
        g = qg if c < Q_END // LANES else kg
        qk_cols.append((blk * lax.rsqrt(ms + EPS)) * g)
    qn = jnp.concatenate(qk_cols[:Q_END // LANES], axis=-1)
    kn = qk_cols[Q_END // LANES]
    v = qkv[:, K_END:V_END]
    u = jax.nn.gelu(_dot(h, w_in_ref[:, V_END:U_END]))
    vg = jax.nn.gelu(_dot(h, w_in_ref[:, U_END:VM_END]))
    vm_cols = []
    for g in range(MLP_GROUPS):
        blk = vg[:, g * LANES:(g + 1) * LANES]
        ms = _segment_mean(blk * blk, seg128)
        vm_cols.append((blk * lax.rsqrt(ms + EPS)) * sgug[:, g * LANES:(g + 1) * LANES])
    vm = jnp.concatenate(vm_cols, axis=-1)
    if between is not None:
        between()
    ga = jax.nn.sigmoid(_dot(h, w_in_ref[:, VM_END:GA_END]))
    gb = jax.nn.sigmoid(_dot(h, w_in_ref[:, GA_END:IN_COLS]))
    return qn, kn, v, u, vm, ga, gb


def _stack_heads(q_rows):
    lane = lax.broadcasted_iota(jnp.int32, (q_rows.shape[0], LANES), 1)
    blocks = []
    for head in range(N_HEADS):
        j, half = head % Q_PER_KV, head // Q_PER_KV
        col = q_rows[:, j * LANES:(j + 1) * LANES]
        keep = (lane < HEAD_DIM) if half == 0 else (lane >= HEAD_DIM)
        blocks.append(jnp.where(keep, col, 0.0))
    return jnp.concatenate(blocks, axis=0).astype(BF16)


def _unstack_heads(o, rows):
    lane = lax.broadcasted_iota(jnp.int32, (rows, LANES), 1)
    cols = []
    for j in range(Q_PER_KV):
        lo = o[j * rows:(j + 1) * rows]
        hi = o[(j + Q_PER_KV) * rows:(j + Q_PER_KV + 1) * rows]
        cols.append(jnp.where(lane < HEAD_DIM, lo, hi))
    return jnp.concatenate(cols, axis=-1)


def _merge(att, sgu, ga, gb, wba_ref, wbs_ref, wout_ref):
    m = ga * _dot(att.astype(BF16), wba_ref[...]) + gb * _dot(sgu.astype(BF16), wbs_ref[...])
    return _dot(m.astype(BF16), wout_ref[...])


def _route(h2, wr_ref, br_ref, tri_ref):
    rows = h2.shape[0]
    logits = _dot(h2, wr_ref[...]) + br_ref[...]
    eidx = lax.broadcasted_iota(jnp.int32, (rows, N_EXPERTS), 1).astype(F32)
    work = logits
    vals, picks, onehots = [], [], []
    for _ in range(TOP_K):
        m = jnp.max(work, axis=-1, keepdims=True)
        sel = jnp.min(jnp.where(work == m, eidx, float(N_EXPERTS)), axis=-1, keepdims=True)
        oh = eidx == sel
        vals.append(m)
        picks.append(sel)
        onehots.append(oh)
        work = jnp.where(oh, -jnp.inf, work)
    exps = [jnp.exp(v - vals[0]) for v in vals]
    den = exps[0] + exps[1] + exps[2] + exps[3]
    mask = jnp.zeros((rows, N_EXPERTS), F32)
    for oh in onehots:
        mask = mask + jnp.where(oh, 1.0, 0.0)
    before = _dot(tri_ref[...], mask.astype(BF16))
    counts = jnp.sum(mask, axis=0, keepdims=True)
    k4 = lax.broadcasted_iota(jnp.int32, (rows, TOP_K), 1)
    w4 = jnp.zeros((rows, TOP_K), F32)
    pos4 = jnp.zeros((rows, TOP_K), jnp.int32)
    for k in range(TOP_K):
        lower_experts = jnp.sum(jnp.where(eidx < picks[k], counts, 0.0), axis=-1, keepdims=True)
        rank = jnp.sum(jnp.where(onehots[k], before, 0.0), axis=-1, keepdims=True)
        w4 = jnp.where(k4 == k, exps[k] / den, w4)
        pos4 = jnp.where(k4 == k, (lower_experts + rank).astype(jnp.int32), pos4)
    return w4, pos4, counts


def _pair_selector(pos4, values=None):
    rows = pos4.shape[0]
    col = lax.broadcasted_iota(jnp.int32, (rows, TOP_K * rows), 1)
    sel = jnp.zeros((rows, TOP_K * rows), F32)
    for k in range(TOP_K):
        sel = sel + jnp.where(col == pos4[:, k:k + 1], 1.0 if values is None else values[:, k:k + 1], 0.0)
    return sel


def _route_block(h2, wr_ref, br_ref, tri_ref, gw_ref, pos_ref, cnt_ref):
    w4, pos4, counts = _route(h2, wr_ref, br_ref, tri_ref)
    gw_ref[...] = w4
    pos_ref[...] = pos4
    cnt_ref[0] = counts
    return pos4


def _dispatch_block(pos4, h2, xc_ref):
    select = _pair_selector(pos4).astype(BF16)
    pairs = lax.dot_general(select, h2, (((0,), (0,)), ((), ())), preferred_element_type=F32)
    _store_token_major(xc_ref, pairs)


def _route_and_dispatch(h2, wr_ref, br_ref, tri_ref, gw_ref, pos_ref, cnt_ref, xc_ref):
    _dispatch_block(_route_block(h2, wr_ref, br_ref, tri_ref, gw_ref, pos_ref, cnt_ref), h2, xc_ref)


def _softmax_rows(parts, sink):
    m = sink
    for l in parts:
        m = jnp.maximum(m, jnp.max(l, axis=-1, keepdims=True))
    es = [jnp.exp(l - m) for l in parts]
    den = jnp.exp(sink - m)
    for e in es:
        den = den + jnp.sum(e, axis=-1, keepdims=True)
    return [e / den for e in es]


def _mixer_prompt_body(sinks_ref, x_ref, wq_ref, w_in_ref, gmix_ref, qg_ref, kg_ref, sgug_ref, seg64_ref, seg128_ref,
                       bias_ref, wsp_ref, bsp_ref, wba_ref, wbs_ref, wout_ref, gffn_ref, wr_ref, br_ref, tri_ref,
                       x1_ref, xc_ref, gw_ref, pos_ref, cnt_ref, kwin_ref, vwin_ref, sguv_ref,
                       kcarry, vcarry, h2_prev_ref, *, blocks_per_seq, n_blocks):
    i = pl.program_id(0)

    @pl.when(i == 0)
    def _():
        h2_prev_ref[...] = jnp.zeros_like(h2_prev_ref)

    def route_previous():
        return _route_block(h2_prev_ref[...], wr_ref, br_ref, tri_ref, gw_ref, pos_ref, cnt_ref)

    def dispatch_previous(pos4):
        _dispatch_block(pos4, h2_prev_ref[...], xc_ref)

    @pl.when(i == n_blocks)
    def _():
        dispatch_previous(route_previous())

    @pl.when(i == n_blocks + 1)
    def _():
        xc_ref[...] = jnp.zeros_like(xc_ref)
        gw_ref[...] = jnp.zeros_like(gw_ref)
        pos_ref[...] = jnp.zeros_like(pos_ref)
        cnt_ref[...] = jnp.zeros_like(cnt_ref)

    @pl.when(jnp.logical_and(i < n_blocks, i % blocks_per_seq == 0))
    def _():
        kcarry[...] = jnp.zeros_like(kcarry)
        vcarry[...] = jnp.zeros_like(vcarry)

    @pl.when(i < n_blocks)
    def _():
        _mixer_prompt_block(i % blocks_per_seq, sinks_ref, x_ref, wq_ref, w_in_ref, gmix_ref, qg_ref, kg_ref, sgug_ref,
                            seg64_ref, seg128_ref, bias_ref, wsp_ref, bsp_ref, wba_ref, wbs_ref, wout_ref, gffn_ref,
                            x1_ref, kwin_ref, vwin_ref, sguv_ref, kcarry, vcarry, h2_prev_ref,
                            route_previous, dispatch_previous)


def _mixer_prompt_block(j, sinks_ref, x_ref, wq_ref, w_in_ref, gmix_ref, qg_ref, kg_ref, sgug_ref, seg64_ref, seg128_ref,
                        bias_ref, wsp_ref, bsp_ref, wba_ref, wbs_ref, wout_ref, gffn_ref,
                        x1_ref, kwin_ref, vwin_ref, sguv_ref, kcarry, vcarry, h2_prev_ref,
                        route_previous, dispatch_previous):
    routed = []
    x = x_ref[0]
    h = _rms_rows(x, gmix_ref[...]).astype(BF16)
    qn, kn, v, u, vm, ga, gb = _project(h, wq_ref, w_in_ref, qg_ref[...], kg_ref[...], sgug_ref[...],
                                        seg64_ref[...], seg128_ref[...],
                                        between=lambda: routed.append(route_previous()))
    k_ext = jnp.concatenate([kcarry[...], kn.astype(BF16)], axis=0)
    v_ext = jnp.concatenate([vcarry[...], v.astype(BF16)], axis=0)
    kcarry[...] = k_ext[TOKEN_BLOCK:]
    vcarry[...] = v_ext[TOKEN_BLOCK:]

    tri_mask = (lax.broadcasted_iota(jnp.int32, (MLP_CHUNK, MLP_CHUNK), 0)
                >= lax.broadcasted_iota(jnp.int32, (MLP_CHUNK, MLP_CHUNK), 1))
    att_rows, sgu_rows = [], []
    for pm in range(TOKEN_BLOCK // PAIR):
        r0 = pm * PAIR
        q_stack = _stack_heads(qn[r0:r0 + PAIR])
        k_band = k_ext[r0:r0 + BAND]
        v_band = v_ext[r0:r0 + BAND]
        s = _dot_nt(q_stack, k_band)
        first = jnp.where(j == 0, 0, 1) if pm == 0 else 1
        probs = []
        for head in range(N_HEADS):
            logit = s[head * PAIR:(head + 1) * PAIR] * ATTN_SCALE + bias_ref[first, head]
            probs.append(_softmax_rows([logit], sinks_ref[head])[0].astype(BF16))
        o = _dot(jnp.concatenate(probs, axis=0), v_band)
        att_rows.append(_unstack_heads(o, PAIR))
        cols = []
        for g in range(MLP_GROUPS):
            wm = jnp.where(tri_mask, wsp_ref[g], 0.0).astype(BF16)
            cols.append(_dot(wm, vm[r0:r0 + PAIR, g * LANES:(g + 1) * LANES].astype(BF16)))
        mixed = jnp.concatenate(cols, axis=-1) + bsp_ref[...]
        sgu_rows.append(u[r0:r0 + PAIR] * mixed)
        if pm == 0:
            dispatch_previous(routed[0])
    att = jnp.concatenate(att_rows, axis=0)
    sgu = jnp.concatenate(sgu_rows, axis=0)

    x1 = x + _merge(att, sgu, ga, gb, wba_ref, wbs_ref, wout_ref)
    h2 = _rms_rows(x1, gffn_ref[...])
    h2_prev_ref[...] = h2.astype(BF16)

    x1_ref[0] = x1
    kwin_ref[0] = kn[TOKEN_BLOCK - WINDOW:]
    vwin_ref[0] = v[TOKEN_BLOCK - WINDOW:]
    for g in range(MLP_GROUPS):
        sguv_ref[0, :, g, :] = vm[TOKEN_BLOCK - MLP_CHUNK:, g * LANES:(g + 1) * LANES]


def _mixer_sample_body(sinks_ref, x_ref, ck_ref, cv_ref, wq_ref, w_in_ref, gmix_ref, qg_ref, kg_ref, sgug_ref, seg64_ref,
                       seg128_ref, biasc_ref, biasn_ref, wsp_ref, bsp_ref, wba_ref, wbs_ref, wout_ref, gffn_ref,
                       wr_ref, br_ref, tri_ref, xc_in_ref, gw_in_ref, pos_in_ref, cnt_in_ref,
                       x1_ref, xc_ref, gw_ref, pos_ref, cnt_ref, kwin_ref, vwin_ref, sguv_ref, *, n_streams, n_new):
    del xc_in_ref, gw_in_ref, pos_in_ref, cnt_in_ref
    x = x_ref[...]
    h = _rms_rows(x, gmix_ref[...]).astype(BF16)
    qn, kn, v, u, vm, ga, gb = _project(h, wq_ref, w_in_ref, qg_ref[...], kg_ref[...], sgug_ref[...],
                                        seg64_ref[...], seg128_ref[...])
    n_cached = ck_ref.shape[1]
    att_rows = []
    for s_i in range(n_streams):
        r0 = s_i * n_new
        q_stack = _stack_heads(qn[r0:r0 + n_new])
        k_new = kn[r0:r0 + n_new]
        v_new = v[r0:r0 + n_new]
        s_c = _dot_nt(q_stack, ck_ref[s_i].astype(BF16))
        s_n = _dot_nt(q_stack, k_new.astype(BF16))
        pc, pn = [], []
        for head in range(N_HEADS):
            rows = slice(head * n_new, (head + 1) * n_new)
            lc = s_c[rows] * ATTN_SCALE + biasc_ref[rows]
            ln = s_n[rows] * ATTN_SCALE + biasn_ref[rows]
            p_c, p_n = _softmax_rows([lc, ln], sinks_ref[head])
            pc.append(p_c.astype(BF16))
            pn.append(p_n.astype(BF16))
        o = (_dot(jnp.concatenate(pc, axis=0), cv_ref[s_i].astype(BF16))
             + _dot(jnp.concatenate(pn, axis=0), v_new.astype(BF16)))
        att_rows.append(_unstack_heads(o, n_new))
        kwin_ref[s_i, 0:n_cached - n_new] = ck_ref[s_i, n_new:n_cached]
        kwin_ref[s_i, n_cached - n_new:n_cached] = k_new
        vwin_ref[s_i, 0:n_cached - n_new] = cv_ref[s_i, n_new:n_cached]
        vwin_ref[s_i, n_cached - n_new:n_cached] = v_new
    att = jnp.concatenate(att_rows, axis=0)

    rows = n_streams * n_new
    ri = lax.broadcasted_iota(jnp.int32, (rows, rows), 0)
    ci = lax.broadcasted_iota(jnp.int32, (rows, rows), 1)
    keep = jnp.logical_and(ri // n_new == ci // n_new, ri % n_new >= ci % n_new)
    expand = (lax.broadcasted_iota(jnp.int32, (rows, n_new), 0) % n_new
              == lax.broadcasted_iota(jnp.int32, (rows, n_new), 1)).astype(BF16)
    cols = []
    for g in range(MLP_GROUPS):
        tiled = _dot_nt(_dot(expand, wsp_ref[g].astype(BF16)).astype(BF16), expand)
        wm = jnp.where(keep, tiled, 0.0).astype(BF16)
        cols.append(_dot(wm, vm[:, g * LANES:(g + 1) * LANES].astype(BF16)))
    sgu = u * (jnp.concatenate(cols, axis=-1) + bsp_ref[...])

    x1 = x + _merge(att, sgu, ga, gb, wba_ref, wbs_ref, wout_ref)
    h2 = _rms_rows(x1, gffn_ref[...])
    _route_and_dispatch(h2.astype(BF16), wr_ref, br_ref, tri_ref, gw_ref, pos_ref, cnt_ref, xc_ref)
    x1_ref[...] = x1
    for g in range(MLP_GROUPS):
        sguv_ref[:, g, :] = vm[:, g * LANES:(g + 1) * LANES]


def _piece_copies(i_tile, lo_ref, src_ref, off_ref, len_ref, hbm, buf, sem, to_hbm):
    def one(p, carry):
        n = pl.multiple_of(len_ref[p], ROW_TILE)

        @pl.when(n > 0)
        def _():
            far = hbm.at[pl.ds(pl.multiple_of(src_ref[p], ROW_TILE), n), :]
            near = buf.at[pl.ds(pl.multiple_of(off_ref[p], ROW_TILE), n), :]
            if to_hbm:
                pltpu.make_async_copy(near, far, sem).start()
            else:
                pltpu.make_async_copy(far, near, sem).start()
        return carry
    lax.fori_loop(lo_ref[i_tile], lo_ref[i_tile + 1], one, 0)


def _wait_rows(hbm, buf, sem, n_rows):
    n = pl.multiple_of(n_rows, ROW_TILE)
    pltpu.make_async_copy(hbm.at[pl.ds(0, n), :], buf.at[pl.ds(0, n), :], sem).wait()


def _expert_mlp(x_cur, y_cur, bgu_ref, bd_ref, wgu_bf, wd_bf, n_rows):
    rows = pl.ds(0, n_rows * ROW_TILE)
    xb = _load_token_major(x_cur.at[rows, :], n_rows).astype(BF16)
    hgu = _dot(xb, wgu_bf[...]) + bgu_ref[0]
    glu = jnp.minimum(hgu[:, :D_EXPERT], SWIGLU_LIMIT)
    lin = jnp.clip(hgu[:, D_EXPERT:], -SWIGLU_LIMIT, SWIGLU_LIMIT)
    act = glu * jax.nn.sigmoid(glu * SWIGLU_ALPHA) * (lin + 1.0)
    _store_token_major(y_cur.at[rows, :], _dot(act.astype(BF16), wd_bf[...]) + bd_ref[0])


def _weight_copies(wgu_hbm, wd_hbm, wgu_stage, wd_stage, sem_w, expert):
    return (pltpu.make_async_copy(wgu_hbm.at[expert], wgu_stage, sem_w.at[0]),
            pltpu.make_async_copy(wd_hbm.at[expert], wd_stage, sem_w.at[1]))


def _expert_body(te_ref, nxt_ref, nused_ref, valid_ref, lo_ref, src_ref, off_ref, len_ref, xc_hbm, wgu_hbm, bgu_ref,
                 wd_hbm, bd_ref, yc_hbm, xbuf0, xbuf1, ybuf0, ybuf1, wgu_stage, wd_stage, wgu_bf, wd_bf,
                 sem_in, sem_out, sem_w):
    i = pl.program_id(0)
    n_used = nused_ref[0]
    xbufs, ybufs = (xbuf0, xbuf1), (ybuf0, ybuf1)
    pieces = (lo_ref, src_ref, off_ref, len_ref)

    @pl.when(i == 0)
    def _():
        xbuf0[...] = jnp.zeros_like(xbuf0)
        xbuf1[...] = jnp.zeros_like(xbuf1)
        for cp in _weight_copies(wgu_hbm, wd_hbm, wgu_stage, wd_stage, sem_w, te_ref[0]):
            cp.start(priority=1)
        _piece_copies(0, *pieces, xc_hbm, xbuf0, sem_in.at[0], to_hbm=False)

    @pl.when(jnp.logical_and(i < n_used, jnp.logical_or(i == 0, te_ref[i] != te_ref[jnp.maximum(i - 1, 0)])))
    def _():
        for cp in _weight_copies(wgu_hbm, wd_hbm, wgu_stage, wd_stage, sem_w, te_ref[i]):
            cp.wait()
        wgu_bf[...] = wgu_stage[...].astype(BF16)
        wd_bf[...] = wd_stage[...].astype(BF16)

        @pl.when(nxt_ref[i] >= 0)
        def _():
            for cp in _weight_copies(wgu_hbm, wd_hbm, wgu_stage, wd_stage, sem_w, nxt_ref[i]):
                cp.start(priority=1)

    for par in range(2):
        x_cur, x_next, y_cur, y_prev = xbufs[par], xbufs[1 - par], ybufs[par], ybufs[1 - par]

        @pl.when(jnp.logical_and(i < n_used, i % 2 == par))
        def _(x_cur=x_cur, x_next=x_next, y_cur=y_cur, y_prev=y_prev, par=par):
            _wait_rows(xc_hbm, x_cur, sem_in.at[par], valid_ref[i])

            @pl.when(i + 1 < n_used)
            def _():
                _piece_copies(i + 1, *pieces, xc_hbm, x_next, sem_in.at[1 - par], to_hbm=False)

            @pl.when(i > 0)
            def _():
                _piece_copies(i - 1, *pieces, yc_hbm, y_prev, sem_out, to_hbm=True)

            quarters = (valid_ref[i] + (TILE_QUARTER * ROW_TILE - 1)) // (TILE_QUARTER * ROW_TILE)
            for q in range(1, EXPERT_TILE // TILE_QUARTER + 1):
                @pl.when(quarters == q)
                def _(q=q):
                    _expert_mlp(x_cur, y_cur, bgu_ref, bd_ref, wgu_bf, wd_bf, q * TILE_QUARTER)

            @pl.when(i > 0)
            def _():
                _wait_rows(yc_hbm, y_prev, sem_out, valid_ref[i - 1])

        @pl.when(jnp.logical_and(i == n_used, i % 2 == par))
        def _(y_prev=y_prev):
            _piece_copies(i - 1, *pieces, yc_hbm, y_prev, sem_out, to_hbm=True)
            _wait_rows(yc_hbm, y_prev, sem_out, valid_ref[i - 1])


COMBINE_SLOTS = 3


def _combine_body(pos_ref, gw_ref, x1_ref, yc_hbm, out_ref, acc_ref, ring, sem, *, first_block, n_blocks):
    i = pl.program_id(0)
    rows = x1_ref.shape[0]
    block_rows = ring.shape[1]

    def fetch(step):
        src = yc_hbm.at[pl.ds(pl.multiple_of((first_block + step) * block_rows, ROW_TILE), block_rows), :]
        return pltpu.make_async_copy(src, ring.at[step % COMBINE_SLOTS], sem.at[step % COMBINE_SLOTS])

    @pl.when(i == 0)
    def _():
        for step in range(min(COMBINE_SLOTS - 1, n_blocks)):
            fetch(step).start()

    @pl.when(i + COMBINE_SLOTS - 1 < n_blocks)
    def _():
        fetch(i + COMBINE_SLOTS - 1).start()

    fetch(i).wait()
    yc_ref = ring.at[i % COMBINE_SLOTS]
    _store_token_major(acc_ref, x1_ref[...])
    for t in range(rows):
        row = acc_ref[pl.ds(t * ROW_TILE, ROW_TILE), :]
        for k in range(TOP_K):
            flat = t * TOP_K + k
            at = pl.multiple_of(pos_ref[0, flat // LANES, flat % LANES], ROW_TILE)
            row = row + gw_ref[0, flat // LANES, flat % LANES] * yc_ref[pl.ds(at, ROW_TILE), :]
        acc_ref[pl.ds(t * ROW_TILE, ROW_TILE), :] = row
    out_ref[...] = _load_token_major(acc_ref, rows)


def _const_spec(shape):
    nd = len(shape)
    return pl.BlockSpec(shape, lambda *_: (0,) * nd, pipeline_mode=pl.Buffered(1))


def _q_perm():
    cols = np.arange(ATTN_WIDTH)
    j, half, d = cols // LANES, (cols % LANES) // HEAD_DIM, cols % HEAD_DIM
    return (j + Q_PER_KV * half) * HEAD_DIM + d


def _alibi_slopes():
    return 2.0 ** (-8.0 * np.arange(1, N_HEADS + 1) / N_HEADS)


def _prompt_bias():
    qi = np.arange(PAIR)[:, None]
    kj = np.arange(BAND)[None, :]
    dist = np.abs(qi + WINDOW - kj).astype(np.float64)
    cq, ck = qi // CHUNK, kj // CHUNK
    in_band = (ck >= cq) & (ck <= cq + WINDOW // CHUNK)
    base = -_alibi_slopes()[:, None, None] * dist[None]
    later = np.where(in_band[None], base, NEG_INF)
    first = np.where((kj >= WINDOW)[None], later, NEG_INF)
    return np.stack([first, later]).astype(np.float32)


def _sample_bias(n_new, n_cached):
    qi = np.arange(n_new)[:, None]
    dc = np.abs(qi + n_cached - np.arange(n_cached)[None, :]).astype(np.float64)
    dn = np.abs(qi - np.arange(n_new)[None, :]).astype(np.float64)
    sl = _alibi_slopes()[:, None, None]
    bc = (-sl * dc[None]).reshape(N_HEADS * n_new, n_cached)
    bn = (-sl * dn[None]).reshape(N_HEADS * n_new, n_new)
    return bc.astype(np.float32), bn.astype(np.float32)


def kernel(x_prompt, x_sample, cache_k_win, cache_v_win, g_mix, w_in, q_norm_g, k_norm_g, attn_sinks, sgu_norm_g,
           w_spatial, b_spatial, w_branch_attn, w_branch_sgu, w_out, g_ffn, w_router, b_router, w_gate_up,
           b_gate_up, w_down, b_down):
    n_b, seq, _ = x_prompt.shape
    n_streams, n_new, _ = x_sample.shape
    n_cached = cache_k_win.shape[2]
    n_prompt = n_b * seq
    n_sample = n_streams * n_new
    n_tok = n_prompt + n_sample
    assert seq % TOKEN_BLOCK == 0 and n_sample == TOKEN_BLOCK and n_cached == WINDOW
    blocks_per_seq = seq // TOKEN_BLOCK
    n_prompt_blocks = n_prompt // TOKEN_BLOCK

    perm = _q_perm()
    w_in_l = w_in[0]
    w_in_b = w_in_l.astype(BF16)
    halves = N_HEADS // Q_PER_KV
    wq_b = (w_in_l[:, :Q_END].reshape(D_MODEL, halves, Q_PER_KV, HEAD_DIM).transpose(0, 2, 1, 3)
            .reshape(D_MODEL, Q_END).astype(BF16))
    wba_b = w_branch_attn[0][perm, :].astype(BF16)
    wbs_b = w_branch_sgu[0].astype(BF16)
    wout_b = w_out[0].astype(BF16)
    wr_b = w_router[0].astype(BF16)
    br = b_router[0].reshape(1, N_EXPERTS).astype(F32)
    gmix = g_mix[0].reshape(1, D_MODEL)
    gffn = g_ffn[0].reshape(1, D_MODEL)
    qg = jnp.tile(q_norm_g[0], LANES // HEAD_DIM).reshape(1, LANES)
    kg = jnp.tile(k_norm_g[0], LANES // HEAD_DIM).reshape(1, LANES)
    sgug = sgu_norm_g[0].reshape(1, MLP_WIDTH)
    sinks = attn_sinks[0].astype(F32)
    lane_seg = np.arange(LANES) // HEAD_DIM
    seg64 = jnp.asarray(np.tile((lane_seg[:, None] == lane_seg[None, :]) / HEAD_DIM, (2, 1)), BF16)
    seg128 = jnp.full((2 * LANES, LANES), 1.0 / MLP_GROUP_DIM, BF16)
    tri = jnp.asarray(np.tril(np.ones((TOKEN_BLOCK, TOKEN_BLOCK)), -1), BF16)
    wsp = w_spatial[0]
    bsp_p = jnp.repeat(b_spatial[0].T, MLP_GROUP_DIM, axis=1)
    wsp_s = wsp[:, :n_new, :n_new]
    bsp_s = jnp.tile(jnp.repeat(b_spatial[0][:, :n_new].T, MLP_GROUP_DIM, axis=1), (n_streams, 1))
    bias_p = jnp.asarray(_prompt_bias())
    bias_c, bias_n = (jnp.asarray(a) for a in _sample_bias(n_new, n_cached))

    smem = pl.BlockSpec(memory_space=pltpu.SMEM)
    tb = TOKEN_BLOCK
    pb = TOP_K * tb
    n_tok_blocks = n_tok // tb
    last_blk = n_prompt_blocks - 1
    seq_of = lambda i: jnp.minimum(i, last_blk) // blocks_per_seq
    x_spec = pl.BlockSpec((1, tb, D_MODEL), lambda i: (seq_of(i), jnp.minimum(i, last_blk) % blocks_per_seq, 0))
    routed_blk = lambda i: jnp.where(i > n_prompt_blocks, n_prompt_blocks, jnp.clip(i - 1, 0, last_blk))
    win_spec = lambda width: pl.BlockSpec((1, WINDOW, width), lambda i: (seq_of(i), 0, 0))
    route_specs = lambda blk: [pl.BlockSpec((pb * ROW_TILE, LANES), lambda i: (blk(i), 0)),
                               pl.BlockSpec((tb, TOP_K), lambda i: (blk(i), 0)),
                               pl.BlockSpec((tb, TOP_K), lambda i: (blk(i), 0)),
                               pl.BlockSpec((1, 1, N_EXPERTS), lambda i: (blk(i), 0, 0))]
    route_shapes = [jax.ShapeDtypeStruct((n_tok * TOP_K * ROW_TILE, LANES), F32),
                    jax.ShapeDtypeStruct((n_tok, TOP_K), F32),
                    jax.ShapeDtypeStruct((n_tok, TOP_K), jnp.int32),
                    jax.ShapeDtypeStruct((n_tok_blocks, 1, N_EXPERTS), F32)]

    x1p, xc, gw_p, pos_p, cnt_p, kwin_p, vwin_p, sguv_p = pl.pallas_call(
        functools.partial(_mixer_prompt_body, blocks_per_seq=blocks_per_seq, n_blocks=n_prompt_blocks),
        grid=(n_prompt_blocks + 2,),
        in_specs=[smem, x_spec, _const_spec((D_MODEL, Q_END)),
                  _const_spec((D_MODEL, IN_COLS)), _const_spec((1, D_MODEL)), _const_spec((1, LANES)),
                  _const_spec((1, LANES)), _const_spec((1, MLP_WIDTH)), _const_spec((2 * LANES, LANES)),
                  _const_spec((2 * LANES, LANES)), _const_spec((2, N_HEADS, PAIR, BAND)),
                  _const_spec((MLP_GROUPS, MLP_CHUNK, MLP_CHUNK)), _const_spec((MLP_CHUNK, MLP_WIDTH)),
                  _const_spec((ATTN_WIDTH, D_MODEL)), _const_spec((MLP_WIDTH, D_MODEL)),
                  _const_spec((D_MODEL, D_MODEL)), _const_spec((1, D_MODEL)), _const_spec((D_MODEL, N_EXPERTS)),
                  _const_spec((1, N_EXPERTS)), _const_spec((tb, tb))],
        out_specs=[x_spec, *route_specs(routed_blk),
                   win_spec(KV_WIDTH), win_spec(KV_WIDTH),
                   pl.BlockSpec((1, MLP_CHUNK, MLP_GROUPS, MLP_GROUP_DIM), lambda i: (seq_of(i), 0, 0, 0))],
        out_shape=[jax.ShapeDtypeStruct((n_b, seq, D_MODEL), F32), *route_shapes,
                   jax.ShapeDtypeStruct((n_b, WINDOW, KV_WIDTH), F32),
                   jax.ShapeDtypeStruct((n_b, WINDOW, KV_WIDTH), F32),
                   jax.ShapeDtypeStruct((n_b, MLP_CHUNK, MLP_GROUPS, MLP_GROUP_DIM), F32)],
        scratch_shapes=[pltpu.VMEM((WINDOW, KV_WIDTH), BF16), pltpu.VMEM((WINDOW, KV_WIDTH), BF16),
                        pltpu.VMEM((tb, D_MODEL), BF16)],
        compiler_params=pltpu.CompilerParams(dimension_semantics=("arbitrary",), vmem_limit_bytes=VMEM_LIMIT),
        name="mixer_prompt",
    )(sinks, x_prompt, wq_b, w_in_b, gmix, qg, kg, sgug, seg64, seg128, bias_p, wsp, bsp_p, wba_b, wbs_b, wout_b,
      gffn, wr_b, br, tri)

    full = lambda shape: pl.BlockSpec(shape, lambda i: (0,) * len(shape))
    any_spec = pl.BlockSpec(memory_space=pl.ANY)
    ck = cache_k_win[0].reshape(n_streams, n_cached, KV_WIDTH)
    cv = cache_v_win[0].reshape(n_streams, n_cached, KV_WIDTH)
    x1s, xc, gw_a, pos_a, cnt_a, kwin_s, vwin_s, sguv_s = pl.pallas_call(
        functools.partial(_mixer_sample_body, n_streams=n_streams, n_new=n_new),
        grid=(1,),
        in_specs=[smem, full((n_sample, D_MODEL)), full((n_streams, n_cached, KV_WIDTH)),
                  full((n_streams, n_cached, KV_WIDTH)),
                  full((D_MODEL, Q_END)), full((D_MODEL, IN_COLS)), full((1, D_MODEL)), full((1, LANES)),
                  full((1, LANES)),
                  full((1, MLP_WIDTH)), full((2 * LANES, LANES)), full((2 * LANES, LANES)),
                  full((N_HEADS * n_new, n_cached)), full((N_HEADS * n_new, n_new)),
                  full((MLP_GROUPS, n_new, n_new)), full((n_sample, MLP_WIDTH)),
                  full((ATTN_WIDTH, D_MODEL)), full((MLP_WIDTH, D_MODEL)), full((D_MODEL, D_MODEL)),
                  full((1, D_MODEL)), full((D_MODEL, N_EXPERTS)), full((1, N_EXPERTS)), full((tb, tb)),
                  any_spec, any_spec, any_spec, any_spec],
        out_specs=[full((n_sample, D_MODEL)), *route_specs(lambda i: n_prompt_blocks),
                   full((n_streams, n_cached, KV_WIDTH)), full((n_streams, n_cached, KV_WIDTH)),
                   full((n_sample, MLP_GROUPS, MLP_GROUP_DIM))],
        out_shape=[jax.ShapeDtypeStruct((n_sample, D_MODEL), F32), *route_shapes,
                   jax.ShapeDtypeStruct((n_streams, n_cached, KV_WIDTH), F32),
                   jax.ShapeDtypeStruct((n_streams, n_cached, KV_WIDTH), F32),
                   jax.ShapeDtypeStruct((n_sample, MLP_GROUPS, MLP_GROUP_DIM), F32)],
        input_output_aliases={23: 1, 24: 2, 25: 3, 26: 4},
        compiler_params=pltpu.CompilerParams(dimension_semantics=("arbitrary",), vmem_limit_bytes=VMEM_LIMIT),
        name="mixer_sample",
    )(sinks, x_sample.reshape(n_sample, D_MODEL), ck, cv, wq_b, w_in_b, gmix, qg, kg, sgug, seg64, seg128, bias_c,
      bias_n, wsp_s, bsp_s, wba_b, wbs_b, wout_b, gffn, wr_b, br, tri, xc, gw_p, pos_p, cnt_p)

    tm = EXPERT_TILE
    counts = cnt_a[:, 0, :].astype(jnp.int32)
    cnt_all = jnp.sum(counts, axis=0)
    padded = ((cnt_all + tm - 1) // tm) * tm
    pad_end = jnp.cumsum(padded)
    pad_off = pad_end - padded
    n_pairs = TOP_K * n_tok
    n_tiles = n_pairs // tm + N_EXPERTS
    n_steps = n_tiles + 1
    in_block = jnp.cumsum(counts, axis=1) - counts
    in_expert = jnp.cumsum(counts, axis=0) - counts
    run_src = (jnp.arange(n_tok_blocks, dtype=jnp.int32)[:, None] * pb + in_block).T.reshape(-1)
    run_start = (pad_off[None, :] + in_expert).T.reshape(-1)
    run_len = counts.T.reshape(-1)
    head = jnp.minimum(run_len, tm - run_start % tm)
    piece_start = jnp.stack([run_start, run_start + head], axis=1).reshape(-1)
    piece_src = jnp.stack([run_src, run_src + head], axis=1).reshape(-1)
    piece_len = jnp.stack([head, run_len - head], axis=1).reshape(-1)
    piece_tile = piece_start // tm
    tile_ids = jnp.arange(n_steps + 1, dtype=jnp.int32)
    piece_lo = jnp.sum((piece_tile[None, :] < tile_ids[:, None]).astype(jnp.int32), axis=1)
    n_used = (pad_end[-1] // tm).astype(jnp.int32)
    tile_start = jnp.minimum(tile_ids[:n_steps], n_used - 1) * tm
    tile_expert = jnp.sum((pad_end[None, :] <= tile_start[:, None]).astype(jnp.int32), axis=1)
    tile_expert = jnp.minimum(tile_expert, N_EXPERTS - 1)
    of_tile = tile_expert[:, None] == jnp.arange(N_EXPERTS, dtype=jnp.int32)[None, :]
    rows_end = jnp.sum(jnp.where(of_tile, (pad_off + cnt_all)[None, :], 0), axis=1)
    run_end = jnp.sum(jnp.where(of_tile, pad_end[None, :], 0), axis=1)
    tile_valid = jnp.clip(rows_end - tile_start, 0, tm)
    next_expert = jnp.sum((pad_end[None, :] <= run_end[:, None]).astype(jnp.int32), axis=1)
    next_expert = jnp.where(run_end < pad_end[-1], jnp.minimum(next_expert, N_EXPERTS - 1), -1)

    prefetch = (tile_expert, next_expert, n_used.reshape(1), tile_valid * ROW_TILE, piece_lo,
                piece_src * ROW_TILE, (piece_start % tm) * ROW_TILE, piece_len * ROW_TILE)
    yc = pl.pallas_call(
        _expert_body,
        grid_spec=pltpu.PrefetchScalarGridSpec(
            num_scalar_prefetch=len(prefetch),
            grid=(n_steps,),
            in_specs=[any_spec, any_spec,
                      pl.BlockSpec((1, 1, 2 * D_EXPERT), lambda i, te, *_: (te[i], 0, 0)),
                      any_spec,
                      pl.BlockSpec((1, 1, D_MODEL), lambda i, te, *_: (te[i], 0, 0))],
            out_specs=any_spec,
            scratch_shapes=[pltpu.VMEM((tm * ROW_TILE, LANES), F32), pltpu.VMEM((tm * ROW_TILE, LANES), F32),
                            pltpu.VMEM((tm * ROW_TILE, LANES), F32), pltpu.VMEM((tm * ROW_TILE, LANES), F32),
                            pltpu.VMEM((D_MODEL, 2 * D_EXPERT), F32), pltpu.VMEM((D_EXPERT, D_MODEL), F32),
                            pltpu.VMEM((D_MODEL, 2 * D_EXPERT), BF16), pltpu.VMEM((D_EXPERT, D_MODEL), BF16),
                            pltpu.SemaphoreType.DMA((2,)), pltpu.SemaphoreType.DMA(()),
                            pltpu.SemaphoreType.DMA((2,))]),
        out_shape=jax.ShapeDtypeStruct((n_pairs * ROW_TILE, LANES), F32),
        compiler_params=pltpu.CompilerParams(dimension_semantics=("arbitrary",), vmem_limit_bytes=VMEM_LIMIT),
        name="moe_experts",
    )(*[p.astype(jnp.int32) for p in prefetch], xc, w_gate_up[0],
      b_gate_up[0].reshape(N_EXPERTS, 1, 2 * D_EXPERT), w_down[0], b_down[0].reshape(N_EXPERTS, 1, D_MODEL))

    pos_tiles = (pos_a * ROW_TILE).reshape(n_tok_blocks, pb // LANES, LANES)
    gw_tiles = gw_a.reshape(n_tok_blocks, pb // LANES, LANES)

    def combine(x1, first_block, n_blocks):
        scalars = pl.BlockSpec((1, pb // LANES, LANES), lambda i: (first_block + i, 0, 0), memory_space=pltpu.SMEM)
        return pl.pallas_call(
            functools.partial(_combine_body, first_block=first_block, n_blocks=n_blocks),
            grid=(n_blocks,),
            in_specs=[scalars, scalars,
                      pl.BlockSpec((tb, D_MODEL), lambda i: (i, 0)),
                      pl.BlockSpec(memory_space=pl.ANY)],
            out_specs=pl.BlockSpec((tb, D_MODEL), lambda i: (i, 0)),
            out_shape=jax.ShapeDtypeStruct(x1.shape, F32),
            scratch_shapes=[pltpu.VMEM((tb * ROW_TILE, LANES), F32),
                            pltpu.VMEM((COMBINE_SLOTS, pb * ROW_TILE, LANES), F32),
                            pltpu.SemaphoreType.DMA((COMBINE_SLOTS,))],
            compiler_params=pltpu.CompilerParams(dimension_semantics=("arbitrary",), vmem_limit_bytes=VMEM_LIMIT),
            name="moe_combine",
        )(pos_tiles, gw_tiles, x1, yc)

    y_prompt = combine(x1p.reshape(n_prompt, D_MODEL), 0, n_prompt_blocks).reshape(n_b, seq, D_MODEL)
    y_sample = combine(x1s, n_prompt_blocks, 1).reshape(n_streams, n_new, D_MODEL)

    kv_shape = (N_KV_HEADS, HEAD_DIM)
    sg_shape = (MLP_GROUPS, MLP_GROUP_DIM)
    return (y_prompt, y_sample,
            kwin_p.reshape(1, n_b, WINDOW, *kv_shape), vwin_p.reshape(1, n_b, WINDOW, *kv_shape),
            kwin_s.reshape(1, n_streams, n_cached, *kv_shape), vwin_s.reshape(1, n_streams, n_cached, *kv_shape),
            sguv_p.reshape(1, n_b, MLP_CHUNK, *sg_shape), sguv_s.reshape(1, n_streams, n_new, *sg_shape))
```

```python
import functools

import jax
import jax.numpy as jnp
import numpy as np
from jax import lax
from jax.experimental import pallas as pl
from jax.experimental.pallas import tpu as pltpu

D_MODEL = 1024
CHUNK = 64
WINDOW = 128
HEAD_DIM = 64
N_HEADS = 8
N_KV_HEADS = 2
Q_PER_KV = N_HEADS // N_KV_HEADS
ATTN_WIDTH = N_HEADS * HEAD_DIM
KV_WIDTH = N_KV_HEADS * HEAD_DIM
ATTN_SCALE = HEAD_DIM ** -0.5
MLP_CHUNK = 128
MLP_GROUPS = 8
MLP_WIDTH = D_MODEL
MLP_GROUP_DIM = MLP_WIDTH // MLP_GROUPS
N_EXPERTS = 32
TOP_K = 4
D_EXPERT = D_MODEL
SWIGLU_ALPHA = 1.702
SWIGLU_LIMIT = 7.0
EPS = 1e-6
NEG_INF = -1e30
Q_END = ATTN_WIDTH
K_END = Q_END + KV_WIDTH
V_END = K_END + KV_WIDTH
U_END = V_END + MLP_WIDTH
VM_END = U_END + MLP_WIDTH
GA_END = VM_END + D_MODEL
IN_COLS = GA_END + D_MODEL

LANES = 128
ROW_TILE = D_MODEL // LANES
TOKEN_BLOCK = 256
PAIR = 2 * CHUNK
BAND = PAIR + WINDOW
EXPERT_TILE = 512
TILE_QUARTER = EXPERT_TILE // 4
VMEM_LIMIT = 56 * 1024 * 1024

F32 = jnp.float32
BF16 = jnp.bfloat16


def _dot(a, b):
    return jnp.dot(a, b, preferred_element_type=F32)


def _dot_nt(a, b):
    return lax.dot_general(a, b, (((1,), (1,)), ((), ())), preferred_element_type=F32)


def _store_token_major(ref, val):
    rows = val.shape[0]
    for s in range(ROW_TILE):
        ref[pl.ds(s, rows, stride=ROW_TILE), :] = val[:, s * LANES:(s + 1) * LANES]


def _load_token_major(ref, rows):
    return jnp.concatenate([ref[pl.ds(s, rows, stride=ROW_TILE), :] for s in range(ROW_TILE)], axis=-1)


def _segment_mean(sq, seg):
    hi = sq.astype(BF16)
    lo = (sq - hi.astype(F32)).astype(BF16)
    return _dot(jnp.concatenate([hi, lo], axis=-1), seg)


def _rms_rows(x, gain):
    ms = jnp.mean(x * x, axis=-1, keepdims=True)
    return (x * lax.rsqrt(ms + EPS)) * gain


def _project(h, wq_ref, w_in_ref, qg, kg, sgug, seg64, seg128, between=None):
    qkv = jnp.concatenate([_dot(h, wq_ref[...]), _dot(h, w_in_ref[:, Q_END:V_END])], axis=-1)
    qk_cols = []
    for c in range(K_END // LANES):
        blk = qkv[:, c * LANES:(c + 1) * LANES]
        ms = _segment_mean(blk * blk, seg64)
        g = qg if c < Q_END // LANES else kg
        qk_cols.append((blk * lax.rsqrt(ms + EPS)) * g)
    qn = jnp.concatenate(qk_cols[:Q_END // LANES], axis=-1)
    kn = qk_cols[Q_END // LANES]
    v = qkv[:, K_END:V_END]
    u = jax.nn.gelu(_dot(h, w_in_ref[:, V_END:U_END]))
    vg = jax.nn.gelu(_dot(h, w_in_ref[:, U_END:VM_END]))
    vm_cols = []
    for g in range(MLP_GROUPS):
        blk = vg[:, g * LANES:(g + 1) * LANES]
        ms = _segment_mean(blk * blk, seg128)
        vm_cols.append((blk * lax.rsqrt(ms + EPS)) * sgug[:, g * LANES:(g + 1) * LANES])
    vm = jnp.concatenate(vm_cols, axis=-1)
    if between is not None:
        between()
    ga = jax.nn.sigmoid(_dot(h, w_in_ref[:, VM_END:GA_END]))
    gb = jax.nn.sigmoid(_dot(h, w_in_ref[:, GA_END:IN_COLS]))
    return qn, kn, v, u, vm, ga, gb


def _stack_heads(q_rows):
    lane = lax.broadcasted_iota(jnp.int32, (q_rows.shape[0], LANES), 1)
    blocks = []
    for head in range(N_HEADS):
        j, half = head % Q_PER_KV, head // Q_PER_KV
        col = q_rows[:, j * LANES:(j + 1) * LANES]
        keep = (lane < HEAD_DIM) if half == 0 else (lane >= HEAD_DIM)
        blocks.append(jnp.where(keep, col, 0.0))
    return jnp.concatenate(blocks, axis=0).astype(BF16)


def _unstack_heads(o, rows):
    lane = lax.broadcasted_iota(jnp.int32, (rows, LANES), 1)
    cols = []
    for j in range(Q_PER_KV):
        lo = o[j * rows:(j + 1) * rows]
        hi = o[(j + Q_PER_KV) * rows:(j + Q_PER_KV + 1) * rows]
        cols.append(jnp.where(lane < HEAD_DIM, lo, hi))
    return jnp.concatenate(cols, axis=-1)


def _merge(att, sgu, ga, gb, wba_ref, wbs_ref, wout_ref):
    m = ga * _dot(att.astype(BF16), wba_ref[...]) + gb * _dot(sgu.astype(BF16), wbs_ref[...])
    return _dot(m.astype(BF16), wout_ref[...])


def _route(h2, wr_ref, br_ref, tri_ref):
    rows = h2.shape[0]
    logits = _dot(h2, wr_ref[...]) + br_ref[...]
    eidx = lax.broadcasted_iota(jnp.int32, (rows, N_EXPERTS), 1).astype(F32)
    work = logits
    vals, picks, onehots = [], [], []
    for _ in range(TOP_K):
        m = jnp.max(work, axis=-1, keepdims=True)
        sel = jnp.min(jnp.where(work == m, eidx, float(N_EXPERTS)), axis=-1, keepdims=True)
        oh = eidx == sel
        vals.append(m)
        picks.append(sel)
        onehots.append(oh)
        work = jnp.where(oh, -jnp.inf, work)
    exps = [jnp.exp(v - vals[0]) for v in vals]
    den = exps[0] + exps[1] + exps[2] + exps[3]
    mask = jnp.zeros((rows, N_EXPERTS), F32)
    for oh in onehots:
        mask = mask + jnp.where(oh, 1.0, 0.0)
    before = _dot(tri_ref[...], mask.astype(BF16))
    counts = jnp.sum(mask, axis=0, keepdims=True)
    k4 = lax.broadcasted_iota(jnp.int32, (rows, TOP_K), 1)
    w4 = jnp.zeros((rows, TOP_K), F32)
    pos4 = jnp.zeros((rows, TOP_K), jnp.int32)
    for k in range(TOP_K):
        lower_experts = jnp.sum(jnp.where(eidx < picks[k], counts, 0.0), axis=-1, keepdims=True)
        rank = jnp.sum(jnp.where(onehots[k], before, 0.0), axis=-1, keepdims=True)
        w4 = jnp.where(k4 == k, exps[k] / den, w4)
        pos4 = jnp.where(k4 == k, (lower_experts + rank).astype(jnp.int32), pos4)
    return w4, pos4, counts


def _pair_selector(pos4, values=None):
    rows = pos4.shape[0]
    col = lax.broadcasted_iota(jnp.int32, (rows, TOP_K * rows), 1)
    sel = jnp.zeros((rows, TOP_K * rows), F32)
    for k in range(TOP_K):
        sel = sel + jnp.where(col == pos4[:, k:k + 1], 1.0 if values is None else values[:, k:k + 1], 0.0)
    return sel


def _route_block(h2, wr_ref, br_ref, tri_ref, gw_ref, pos_ref, cnt_ref):
    w4, pos4, counts = _route(h2, wr_ref, br_ref, tri_ref)
    gw_ref[...] = w4
    pos_ref[...] = pos4
    cnt_ref[0] = counts
    return pos4


def _dispatch_block(pos4, h2, xc_ref):
    select = _pair_selector(pos4).astype(BF16)
    pairs = lax.dot_general(select, h2, (((0,), (0,)), ((), ())), preferred_element_type=F32)
    _store_token_major(xc_ref, pairs)


def _route_and_dispatch(h2, wr_ref, br_ref, tri_ref, gw_ref, pos_ref, cnt_ref, xc_ref):
    _dispatch_block(_route_block(h2, wr_ref, br_ref, tri_ref, gw_ref, pos_ref, cnt_ref), h2, xc_ref)


def _softmax_rows(parts, sink):
    m = sink
    for l in parts:
        m = jnp.maximum(m, jnp.max(l, axis=-1, keepdims=True))
    es = [jnp.exp(l - m) for l in parts]
    den = jnp.exp(sink - m)
    for e in es:
        den = den + jnp.sum(e, axis=-1, keepdims=True)
    return [e / den for e in es]


def _mixer_prompt_body(sinks_ref, x_ref, wq_ref, w_in_ref, gmix_ref, qg_ref, kg_ref, sgug_ref, seg64_ref, seg128_ref,
                       bias_ref, wsp_ref, bsp_ref, wba_ref, wbs_ref, wout_ref, gffn_ref, wr_ref, br_ref, tri_ref,
                       x1_ref, xc_ref, gw_ref, pos_ref, cnt_ref, kwin_ref, vwin_ref, sguv_ref,
                       kcarry, vcarry, h2_prev_ref, *, blocks_per_seq, n_blocks):
    i = pl.program_id(0)

    @pl.when(i == 0)
    def _():
        h2_prev_ref[...] = jnp.zeros_like(h2_prev_ref)

    def route_previous():
        return _route_block(h2_prev_ref[...], wr_ref, br_ref, tri_ref, gw_ref, pos_ref, cnt_ref)

    def dispatch_previous(pos4):
        _dispatch_block(pos4, h2_prev_ref[...], xc_ref)

    @pl.when(i == n_blocks)
    def _():
        dispatch_previous(route_previous())

    @pl.when(i == n_blocks + 1)
    def _():
        xc_ref[...] = jnp.zeros_like(xc_ref)
        gw_ref[...] = jnp.zeros_like(gw_ref)
        pos_ref[...] = jnp.zeros_like(pos_ref)
        cnt_ref[...] = jnp.zeros_like(cnt_ref)

    @pl.when(jnp.logical_and(i < n_blocks, i % blocks_per_seq == 0))
    def _():
        kcarry[...] = jnp.zeros_like(kcarry)
        vcarry[...] = jnp.zeros_like(vcarry)

    @pl.when(i < n_blocks)
    def _():
        _mixer_prompt_block(i % blocks_per_seq, sinks_ref, x_ref, wq_ref, w_in_ref, gmix_ref, qg_ref, kg_ref, sgug_ref,
                            seg64_ref, seg128_ref, bias_ref, wsp_ref, bsp_ref, wba_ref, wbs_ref, wout_ref, gffn_ref,
                            x1_ref, kwin_ref, vwin_ref, sguv_ref, kcarry, vcarry, h2_prev_ref,
                            route_previous, dispatch_previous)


def _mixer_prompt_block(j, sinks_ref, x_ref, wq_ref, w_in_ref, gmix_ref, qg_ref, kg_ref, sgug_ref, seg64_ref, seg128_ref,
                        bias_ref, wsp_ref, bsp_ref, wba_ref, wbs_ref, wout_ref, gffn_ref,
                        x1_ref, kwin_ref, vwin_ref, sguv_ref, kcarry, vcarry, h2_prev_ref,
                        route_previous, dispatch_previous):
    routed = []
    x = x_ref[0]
    h = _rms_rows(x, gmix_ref[...]).astype(BF16)
    qn, kn, v, u, vm, ga, gb = _project(h, wq_ref, w_in_ref, qg_ref[...], kg_ref[...], sgug_ref[...],
                                        seg64_ref[...], seg128_ref[...],
                                        between=lambda: routed.append(route_previous()))
    k_ext = jnp.concatenate([kcarry[...], kn.astype(BF16)], axis=0)
    v_ext = jnp.concatenate([vcarry[...], v.astype(BF16)], axis=0)
    kcarry[...] = k_ext[TOKEN_BLOCK:]
    vcarry[...] = v_ext[TOKEN_BLOCK:]

    tri_mask = (lax.broadcasted_iota(jnp.int32, (MLP_CHUNK, MLP_CHUNK), 0)
                >= lax.broadcasted_iota(jnp.int32, (MLP_CHUNK, MLP_CHUNK), 1))
    att_rows, sgu_rows = [], []
    for pm in range(TOKEN_BLOCK // PAIR):
        r0 = pm * PAIR
        q_stack = _stack_heads(qn[r0:r0 + PAIR])
        k_band = k_ext[r0:r0 + BAND]
        v_band = v_ext[r0:r0 + BAND]
        s = _dot_nt(q_stack, k_band)
        first = jnp.where(j == 0, 0, 1) if pm == 0 else 1
        probs = []
        for head in range(N_HEADS):
            logit = s[head * PAIR:(head + 1) * PAIR] * ATTN_SCALE + bias_ref[first, head]
            probs.append(_softmax_rows([logit], sinks_ref[head])[0].astype(BF16))
        o = _dot(jnp.concatenate(probs, axis=0), v_band)
        att_rows.append(_unstack_heads(o, PAIR))
        cols = []
        for g in range(MLP_GROUPS):
            wm = jnp.where(tri_mask, wsp_ref[g], 0.0).astype(BF16)
            cols.append(_dot(wm, vm[r0:r0 + PAIR, g * LANES:(g + 1) * LANES].astype(BF16)))
        mixed = jnp.concatenate(cols, axis=-1) + bsp_ref[...]
        sgu_rows.append(u[r0:r0 + PAIR] * mixed)
        if pm == 0:
            dispatch_previous(routed[0])
    att = jnp.concatenate(att_rows, axis=0)
    sgu = jnp.concatenate(sgu_rows, axis=0)

    x1 = x + _merge(att, sgu, ga, gb, wba_ref, wbs_ref, wout_ref)
    h2 = _rms_rows(x1, gffn_ref[...])
    h2_prev_ref[...] = h2.astype(BF16)

    x1_ref[0] = x1
    kwin_ref[0] = kn[TOKEN_BLOCK - WINDOW:]
    vwin_ref[0] = v[TOKEN_BLOCK - WINDOW:]
    for g in range(MLP_GROUPS):
        sguv_ref[0, :, g, :] = vm[TOKEN_BLOCK - MLP_CHUNK:, g * LANES:(g + 1) * LANES]


def _mixer_sample_body(sinks_ref, x_ref, ck_ref, cv_ref, wq_ref, w_in_ref, gmix_ref, qg_ref, kg_ref, sgug_ref, seg64_ref,
                       seg128_ref, biasc_ref, biasn_ref, wsp_ref, bsp_ref, wba_ref, wbs_ref, wout_ref, gffn_ref,
                       wr_ref, br_ref, tri_ref, xc_in_ref, gw_in_ref, pos_in_ref, cnt_in_ref,
                       x1_ref, xc_ref, gw_ref, pos_ref, cnt_ref, kwin_ref, vwin_ref, sguv_ref, *, n_streams, n_new):
    del xc_in_ref, gw_in_ref, pos_in_ref, cnt_in_ref
    x = x_ref[...]
    h = _rms_rows(x, gmix_ref[...]).astype(BF16)
    qn, kn, v, u, vm, ga, gb = _project(h, wq_ref, w_in_ref, qg_ref[...], kg_ref[...], sgug_ref[...],
                                        seg64_ref[...], seg128_ref[...])
    n_cached = ck_ref.shape[1]
    att_rows = []
    for s_i in range(n_streams):
        r0 = s_i * n_new
        q_stack = _stack_heads(qn[r0:r0 + n_new])
        k_new = kn[r0:r0 + n_new]
        v_new = v[r0:r0 + n_new]
        s_c = _dot_nt(q_stack, ck_ref[s_i].astype(BF16))
        s_n = _dot_nt(q_stack, k_new.astype(BF16))
        pc, pn = [], []
        for head in range(N_HEADS):
            rows = slice(head * n_new, (head + 1) * n_new)
            lc = s_c[rows] * ATTN_SCALE + biasc_ref[rows]
            ln = s_n[rows] * ATTN_SCALE + biasn_ref[rows]
            p_c, p_n = _softmax_rows([lc, ln], sinks_ref[head])
            pc.append(p_c.astype(BF16))
            pn.append(p_n.astype(BF16))
        o = (_dot(jnp.concatenate(pc, axis=0), cv_ref[s_i].astype(BF16))
             + _dot(jnp.concatenate(pn, axis=0), v_new.astype(BF16)))
        att_rows.append(_unstack_heads(o, n_new))
        kwin_ref[s_i, 0:n_cached - n_new] = ck_ref[s_i, n_new:n_cached]
        kwin_ref[s_i, n_cached - n_new:n_cached] = k_new
        vwin_ref[s_i, 0:n_cached - n_new] = cv_ref[s_i, n_new:n_cached]
        vwin_ref[s_i, n_cached - n_new:n_cached] = v_new
    att = jnp.concatenate(att_rows, axis=0)

    rows = n_streams * n_new
    ri = lax.broadcasted_iota(jnp.int32, (rows, rows), 0)
    ci = lax.broadcasted_iota(jnp.int32, (rows, rows), 1)
    keep = jnp.logical_and(ri // n_new == ci // n_new, ri % n_new >= ci % n_new)
    expand = (lax.broadcasted_iota(jnp.int32, (rows, n_new), 0) % n_new
              == lax.broadcasted_iota(jnp.int32, (rows, n_new), 1)).astype(BF16)
    cols = []
    for g in range(MLP_GROUPS):
        tiled = _dot_nt(_dot(expand, wsp_ref[g].astype(BF16)).astype(BF16), expand)
        wm = jnp.where(keep, tiled, 0.0).astype(BF16)
        cols.append(_dot(wm, vm[:, g * LANES:(g + 1) * LANES].astype(BF16)))
    sgu = u * (jnp.concatenate(cols, axis=-1) + bsp_ref[...])

    x1 = x + _merge(att, sgu, ga, gb, wba_ref, wbs_ref, wout_ref)
    h2 = _rms_rows(x1, gffn_ref[...])
    _route_and_dispatch(h2.astype(BF16), wr_ref, br_ref, tri_ref, gw_ref, pos_ref, cnt_ref, xc_ref)
    x1_ref[...] = x1
    for g in range(MLP_GROUPS):
        sguv_ref[:, g, :] = vm[:, g * LANES:(g + 1) * LANES]


def _piece_copies(i_tile, lo_ref, src_ref, off_ref, len_ref, hbm, buf, sem, to_hbm):
    def one(p, carry):
        n = pl.multiple_of(len_ref[p], ROW_TILE)

        @pl.when(n > 0)
        def _():
            far = hbm.at[pl.ds(pl.multiple_of(src_ref[p], ROW_TILE), n), :]
            near = buf.at[pl.ds(pl.multiple_of(off_ref[p], ROW_TILE), n), :]
            if to_hbm:
                pltpu.make_async_copy(near, far, sem).start()
            else:
                pltpu.make_async_copy(far, near, sem).start()
        return carry
    lax.fori_loop(lo_ref[i_tile], lo_ref[i_tile + 1], one, 0)


def _wait_rows(hbm, buf, sem, n_rows):
    n = pl.multiple_of(n_rows, ROW_TILE)
    pltpu.make_async_copy(hbm.at[pl.ds(0, n), :], buf.at[pl.ds(0, n), :], sem).wait()


def _expert_mlp(x_cur, y_cur, bgu_ref, bd_ref, wgu_bf, wd_bf, n_rows):
    rows = pl.ds(0, n_rows * ROW_TILE)
    xb = _load_token_major(x_cur.at[rows, :], n_rows).astype(BF16)
    hgu = _dot(xb, wgu_bf[...]) + bgu_ref[0]
    glu = jnp.minimum(hgu[:, :D_EXPERT], SWIGLU_LIMIT)
    lin = jnp.clip(hgu[:, D_EXPERT:], -SWIGLU_LIMIT, SWIGLU_LIMIT)
    act = glu * jax.nn.sigmoid(glu * SWIGLU_ALPHA) * (lin + 1.0)
    _store_token_major(y_cur.at[rows, :], _dot(act.astype(BF16), wd_bf[...]) + bd_ref[0])


def _weight_copies(wgu_hbm, wd_hbm, wgu_stage, wd_stage, sem_w, expert):
    return (pltpu.make_async_copy(wgu_hbm.at[expert], wgu_stage, sem_w.at[0]),
            pltpu.make_async_copy(wd_hbm.at[expert], wd_stage, sem_w.at[1]))


def _expert_body(te_ref, nxt_ref, nused_ref, valid_ref, lo_ref, src_ref, off_ref, len_ref, xc_hbm, wgu_hbm, bgu_ref,
                 wd_hbm, bd_ref, yc_hbm, xbuf0, xbuf1, ybuf0, ybuf1, wgu_stage, wd_stage, wgu_bf, wd_bf,
                 sem_in, sem_out, sem_w):
    i = pl.program_id(0)
    n_used = nused_ref[0]
    xbufs, ybufs = (xbuf0, xbuf1), (ybuf0, ybuf1)
    pieces = (lo_ref, src_ref, off_ref, len_ref)

    @pl.when(i == 0)
    def _():
        xbuf0[...] = jnp.zeros_like(xbuf0)
        xbuf1[...] = jnp.zeros_like(xbuf1)
        for cp in _weight_copies(wgu_hbm, wd_hbm, wgu_stage, wd_stage, sem_w, te_ref[0]):
            cp.start(priority=1)
        _piece_copies(0, *pieces, xc_hbm, xbuf0, sem_in.at[0], to_hbm=False)

    @pl.when(jnp.logical_and(i < n_used, jnp.logical_or(i == 0, te_ref[i] != te_ref[jnp.maximum(i - 1, 0)])))
    def _():
        for cp in _weight_copies(wgu_hbm, wd_hbm, wgu_stage, wd_stage, sem_w, te_ref[i]):
            cp.wait()
        wgu_bf[...] = wgu_stage[...].astype(BF16)
        wd_bf[...] = wd_stage[...].astype(BF16)

        @pl.when(nxt_ref[i] >= 0)
        def _():
            for cp in _weight_copies(wgu_hbm, wd_hbm, wgu_stage, wd_stage, sem_w, nxt_ref[i]):
                cp.start(priority=1)

    for par in range(2):
        x_cur, x_next, y_cur, y_prev = xbufs[par], xbufs[1 - par], ybufs[par], ybufs[1 - par]

        @pl.when(jnp.logical_and(i < n_used, i % 2 == par))
        def _(x_cur=x_cur, x_next=x_next, y_cur=y_cur, y_prev=y_prev, par=par):
            _wait_rows(xc_hbm, x_cur, sem_in.at[par], valid_ref[i])

            @pl.when(i + 1 < n_used)
            def _():
                _piece_copies(i + 1, *pieces, xc_hbm, x_next, sem_in.at[1 - par], to_hbm=False)

            @pl.when(i > 0)
            def _():
                _piece_copies(i - 1, *pieces, yc_hbm, y_prev, sem_out, to_hbm=True)

            quarters = (valid_ref[i] + (TILE_QUARTER * ROW_TILE - 1)) // (TILE_QUARTER * ROW_TILE)
            for q in range(1, EXPERT_TILE // TILE_QUARTER + 1):
                @pl.when(quarters == q)
                def _(q=q):
                    _expert_mlp(x_cur, y_cur, bgu_ref, bd_ref, wgu_bf, wd_bf, q * TILE_QUARTER)

            @pl.when(i > 0)
            def _():
                _wait_rows(yc_hbm, y_prev, sem_out, valid_ref[i - 1])

        @pl.when(jnp.logical_and(i == n_used, i % 2 == par))
        def _(y_prev=y_prev):
            _piece_copies(i - 1, *pieces, yc_hbm, y_prev, sem_out, to_hbm=True)
            _wait_rows(yc_hbm, y_prev, sem_out, valid_ref[i - 1])


COMBINE_SLOTS = 3


def _combine_body(pos_ref, gw_ref, x1_hbm, yc_hbm, out_ref, acc_ref, ring, x1_ring, sem, *, first_block, n_blocks):
    i = pl.program_id(0)
    rows = x1_ring.shape[1]
    block_rows = ring.shape[1]

    def fetch(step):
        slot = step % COMBINE_SLOTS
        src = yc_hbm.at[pl.ds(pl.multiple_of((first_block + step) * block_rows, ROW_TILE), block_rows), :]
        x1_src = x1_hbm.at[pl.ds(pl.multiple_of(step * rows, ROW_TILE), rows), :]
        return (pltpu.make_async_copy(src, ring.at[slot], sem.at[0, slot]),
                pltpu.make_async_copy(x1_src, x1_ring.at[slot], sem.at[1, slot]))

    @pl.when(i == 0)
    def _():
        for step in range(min(COMBINE_SLOTS - 1, n_blocks)):
            for cp in fetch(step):
                cp.start()

    @pl.when(i + COMBINE_SLOTS - 1 < n_blocks)
    def _():
        for cp in fetch(i + COMBINE_SLOTS - 1):
            cp.start()

    for cp in fetch(i):
        cp.wait()
    yc_ref = ring.at[i % COMBINE_SLOTS]
    _store_token_major(acc_ref, x1_ring[i % COMBINE_SLOTS])
    for t in range(rows):
        row = acc_ref[pl.ds(t * ROW_TILE, ROW_TILE), :]
        for k in range(TOP_K):
            flat = t * TOP_K + k
            at = pl.multiple_of(pos_ref[0, flat // LANES, flat % LANES], ROW_TILE)
            row = row + gw_ref[0, flat // LANES, flat % LANES] * yc_ref[pl.ds(at, ROW_TILE), :]
        acc_ref[pl.ds(t * ROW_TILE, ROW_TILE), :] = row
    out_ref[...] = _load_token_major(acc_ref, rows)


def _const_spec(shape):
    nd = len(shape)
    return pl.BlockSpec(shape, lambda *_: (0,) * nd, pipeline_mode=pl.Buffered(1))


def _q_perm():
    cols = np.arange(ATTN_WIDTH)
    j, half, d = cols // LANES, (cols % LANES) // HEAD_DIM, cols % HEAD_DIM
    return (j + Q_PER_KV * half) * HEAD_DIM + d


def _alibi_slopes():
    return 2.0 ** (-8.0 * np.arange(1, N_HEADS + 1) / N_HEADS)


def _prompt_bias():
    qi = np.arange(PAIR)[:, None]
    kj = np.arange(BAND)[None, :]
    dist = np.abs(qi + WINDOW - kj).astype(np.float64)
    cq, ck = qi // CHUNK, kj // CHUNK
    in_band = (ck >= cq) & (ck <= cq + WINDOW // CHUNK)
    base = -_alibi_slopes()[:, None, None] * dist[None]
    later = np.where(in_band[None], base, NEG_INF)
    first = np.where((kj >= WINDOW)[None], later, NEG_INF)
    return np.stack([first, later]).astype(np.float32)


def _sample_bias(n_new, n_cached):
    qi = np.arange(n_new)[:, None]
    dc = np.abs(qi + n_cached - np.arange(n_cached)[None, :]).astype(np.float64)
    dn = np.abs(qi - np.arange(n_new)[None, :]).astype(np.float64)
    sl = _alibi_slopes()[:, None, None]
    bc = (-sl * dc[None]).reshape(N_HEADS * n_new, n_cached)
    bn = (-sl * dn[None]).reshape(N_HEADS * n_new, n_new)
    return bc.astype(np.float32), bn.astype(np.float32)


def kernel(x_prompt, x_sample, cache_k_win, cache_v_win, g_mix, w_in, q_norm_g, k_norm_g, attn_sinks, sgu_norm_g,
           w_spatial, b_spatial, w_branch_attn, w_branch_sgu, w_out, g_ffn, w_router, b_router, w_gate_up,
           b_gate_up, w_down, b_down):
    n_b, seq, _ = x_prompt.shape
    n_streams, n_new, _ = x_sample.shape
    n_cached = cache_k_win.shape[2]
    n_prompt = n_b * seq
    n_sample = n_streams * n_new
    n_tok = n_prompt + n_sample
    assert seq % TOKEN_BLOCK == 0 and n_sample == TOKEN_BLOCK and n_cached == WINDOW
    blocks_per_seq = seq // TOKEN_BLOCK
    n_prompt_blocks = n_prompt // TOKEN_BLOCK

    perm = _q_perm()
    w_in_l = w_in[0]
    w_in_b = w_in_l.astype(BF16)
    halves = N_HEADS // Q_PER_KV
    wq_b = (w_in_l[:, :Q_END].reshape(D_MODEL, halves, Q_PER_KV, HEAD_DIM).transpose(0, 2, 1, 3)
            .reshape(D_MODEL, Q_END).astype(BF16))
    wba_b = w_branch_attn[0][perm, :].astype(BF16)
    wbs_b = w_branch_sgu[0].astype(BF16)
    wout_b = w_out[0].astype(BF16)
    wr_b = w_router[0].astype(BF16)
    br = b_router[0].reshape(1, N_EXPERTS).astype(F32)
    gmix = g_mix[0].reshape(1, D_MODEL)
    gffn = g_ffn[0].reshape(1, D_MODEL)
    qg = jnp.tile(q_norm_g[0], LANES // HEAD_DIM).reshape(1, LANES)
    kg = jnp.tile(k_norm_g[0], LANES // HEAD_DIM).reshape(1, LANES)
    sgug = sgu_norm_g[0].reshape(1, MLP_WIDTH)
    sinks = attn_sinks[0].astype(F32)
    lane_seg = np.arange(LANES) // HEAD_DIM
    seg64 = jnp.asarray(np.tile((lane_seg[:, None] == lane_seg[None, :]) / HEAD_DIM, (2, 1)), BF16)
    seg128 = jnp.full((2 * LANES, LANES), 1.0 / MLP_GROUP_DIM, BF16)
    tri = jnp.asarray(np.tril(np.ones((TOKEN_BLOCK, TOKEN_BLOCK)), -1), BF16)
    wsp = w_spatial[0]
    bsp_p = jnp.repeat(b_spatial[0].T, MLP_GROUP_DIM, axis=1)
    wsp_s = wsp[:, :n_new, :n_new]
    bsp_s = jnp.tile(jnp.repeat(b_spatial[0][:, :n_new].T, MLP_GROUP_DIM, axis=1), (n_streams, 1))
    bias_p = jnp.asarray(_prompt_bias())
    bias_c, bias_n = (jnp.asarray(a) for a in _sample_bias(n_new, n_cached))

    smem = pl.BlockSpec(memory_space=pltpu.SMEM)
    tb = TOKEN_BLOCK
    pb = TOP_K * tb
    n_tok_blocks = n_tok // tb
    last_blk = n_prompt_blocks - 1
    seq_of = lambda i: jnp.minimum(i, last_blk) // blocks_per_seq
    x_spec = pl.BlockSpec((1, tb, D_MODEL), lambda i: (seq_of(i), jnp.minimum(i, last_blk) % blocks_per_seq, 0))
    routed_blk = lambda i: jnp.where(i > n_prompt_blocks, n_prompt_blocks, jnp.clip(i - 1, 0, last_blk))
    win_spec = lambda width: pl.BlockSpec((1, WINDOW, width), lambda i: (seq_of(i), 0, 0))
    route_specs = lambda blk: [pl.BlockSpec((pb * ROW_TILE, LANES), lambda i: (blk(i), 0)),
                               pl.BlockSpec((tb, TOP_K), lambda i: (blk(i), 0)),
                               pl.BlockSpec((tb, TOP_K), lambda i: (blk(i), 0)),
                               pl.BlockSpec((1, 1, N_EXPERTS), lambda i: (blk(i), 0, 0))]
    route_shapes = [jax.ShapeDtypeStruct((n_tok * TOP_K * ROW_TILE, LANES), F32),
                    jax.ShapeDtypeStruct((n_tok, TOP_K), F32),
                    jax.ShapeDtypeStruct((n_tok, TOP_K), jnp.int32),
                    jax.ShapeDtypeStruct((n_tok_blocks, 1, N_EXPERTS), F32)]

    x1p, xc, gw_p, pos_p, cnt_p, kwin_p, vwin_p, sguv_p = pl.pallas_call(
        functools.partial(_mixer_prompt_body, blocks_per_seq=blocks_per_seq, n_blocks=n_prompt_blocks),
        grid=(n_prompt_blocks + 2,),
        in_specs=[smem, x_spec, _const_spec((D_MODEL, Q_END)),
                  _const_spec((D_MODEL, IN_COLS)), _const_spec((1, D_MODEL)), _const_spec((1, LANES)),
                  _const_spec((1, LANES)), _const_spec((1, MLP_WIDTH)), _const_spec((2 * LANES, LANES)),
                  _const_spec((2 * LANES, LANES)), _const_spec((2, N_HEADS, PAIR, BAND)),
                  _const_spec((MLP_GROUPS, MLP_CHUNK, MLP_CHUNK)), _const_spec((MLP_CHUNK, MLP_WIDTH)),
                  _const_spec((ATTN_WIDTH, D_MODEL)), _const_spec((MLP_WIDTH, D_MODEL)),
                  _const_spec((D_MODEL, D_MODEL)), _const_spec((1, D_MODEL)), _const_spec((D_MODEL, N_EXPERTS)),
                  _const_spec((1, N_EXPERTS)), _const_spec((tb, tb))],
        out_specs=[x_spec, *route_specs(routed_blk),
                   win_spec(KV_WIDTH), win_spec(KV_WIDTH),
                   pl.BlockSpec((1, MLP_CHUNK, MLP_GROUPS, MLP_GROUP_DIM), lambda i: (seq_of(i), 0, 0, 0))],
        out_shape=[jax.ShapeDtypeStruct((n_b, seq, D_MODEL), F32), *route_shapes,
                   jax.ShapeDtypeStruct((n_b, WINDOW, KV_WIDTH), F32),
                   jax.ShapeDtypeStruct((n_b, WINDOW, KV_WIDTH), F32),
                   jax.ShapeDtypeStruct((n_b, MLP_CHUNK, MLP_GROUPS, MLP_GROUP_DIM), F32)],
        scratch_shapes=[pltpu.VMEM((WINDOW, KV_WIDTH), BF16), pltpu.VMEM((WINDOW, KV_WIDTH), BF16),
                        pltpu.VMEM((tb, D_MODEL), BF16)],
        compiler_params=pltpu.CompilerParams(dimension_semantics=("arbitrary",), vmem_limit_bytes=VMEM_LIMIT),
        name="mixer_prompt",
    )(sinks, x_prompt, wq_b, w_in_b, gmix, qg, kg, sgug, seg64, seg128, bias_p, wsp, bsp_p, wba_b, wbs_b, wout_b,
      gffn, wr_b, br, tri)

    full = lambda shape: pl.BlockSpec(shape, lambda i: (0,) * len(shape))
    any_spec = pl.BlockSpec(memory_space=pl.ANY)
    ck = cache_k_win[0].reshape(n_streams, n_cached, KV_WIDTH)
    cv = cache_v_win[0].reshape(n_streams, n_cached, KV_WIDTH)
    x1s, xc, gw_a, pos_a, cnt_a, kwin_s, vwin_s, sguv_s = pl.pallas_call(
        functools.partial(_mixer_sample_body, n_streams=n_streams, n_new=n_new),
        grid=(1,),
        in_specs=[smem, full((n_sample, D_MODEL)), full((n_streams, n_cached, KV_WIDTH)),
                  full((n_streams, n_cached, KV_WIDTH)),
                  full((D_MODEL, Q_END)), full((D_MODEL, IN_COLS)), full((1, D_MODEL)), full((1, LANES)),
                  full((1, LANES)),
                  full((1, MLP_WIDTH)), full((2 * LANES, LANES)), full((2 * LANES, LANES)),
                  full((N_HEADS * n_new, n_cached)), full((N_HEADS * n_new, n_new)),
                  full((MLP_GROUPS, n_new, n_new)), full((n_sample, MLP_WIDTH)),
                  full((ATTN_WIDTH, D_MODEL)), full((MLP_WIDTH, D_MODEL)), full((D_MODEL, D_MODEL)),
                  full((1, D_MODEL)), full((D_MODEL, N_EXPERTS)), full((1, N_EXPERTS)), full((tb, tb)),
                  any_spec, any_spec, any_spec, any_spec],
        out_specs=[full((n_sample, D_MODEL)), *route_specs(lambda i: n_prompt_blocks),
                   full((n_streams, n_cached, KV_WIDTH)), full((n_streams, n_cached, KV_WIDTH)),
                   full((n_sample, MLP_GROUPS, MLP_GROUP_DIM))],
        out_shape=[jax.ShapeDtypeStruct((n_sample, D_MODEL), F32), *route_shapes,
                   jax.ShapeDtypeStruct((n_streams, n_cached, KV_WIDTH), F32),
                   jax.ShapeDtypeStruct((n_streams, n_cached, KV_WIDTH), F32),
                   jax.ShapeDtypeStruct((n_sample, MLP_GROUPS, MLP_GROUP_DIM), F32)],
        input_output_aliases={23: 1, 24: 2, 25: 3, 26: 4},
        compiler_params=pltpu.CompilerParams(dimension_semantics=("arbitrary",), vmem_limit_bytes=VMEM_LIMIT),
        name="mixer_sample",
    )(sinks, x_sample.reshape(n_sample, D_MODEL), ck, cv, wq_b, w_in_b, gmix, qg, kg, sgug, seg64, seg128, bias_c,
      bias_n, wsp_s, bsp_s, wba_b, wbs_b, wout_b, gffn, wr_b, br, tri, xc, gw_p, pos_p, cnt_p)

    tm = EXPERT_TILE
    counts = cnt_a[:, 0, :].astype(jnp.int32)
    cnt_all = jnp.sum(counts, axis=0)
    padded = ((cnt_all + tm - 1) // tm) * tm
    pad_end = jnp.cumsum(padded)
    pad_off = pad_end - padded
    n_pairs = TOP_K * n_tok
    n_tiles = n_pairs // tm + N_EXPERTS
    n_steps = n_tiles + 1
    in_block = jnp.cumsum(counts, axis=1) - counts
    in_expert = jnp.cumsum(counts, axis=0) - counts
    run_src = (jnp.arange(n_tok_blocks, dtype=jnp.int32)[:, None] * pb + in_block).T.reshape(-1)
    run_start = (pad_off[None, :] + in_expert).T.reshape(-1)
    run_len = counts.T.reshape(-1)
    head = jnp.minimum(run_len, tm - run_start % tm)
    piece_start = jnp.stack([run_start, run_start + head], axis=1).reshape(-1)
    piece_src = jnp.stack([run_src, run_src + head], axis=1).reshape(-1)
    piece_len = jnp.stack([head, run_len - head], axis=1).reshape(-1)
    piece_tile = piece_start // tm
    tile_ids = jnp.arange(n_steps + 1, dtype=jnp.int32)
    piece_lo = jnp.sum((piece_tile[None, :] < tile_ids[:, None]).astype(jnp.int32), axis=1)
    n_used = (pad_end[-1] // tm).astype(jnp.int32)
    tile_start = jnp.minimum(tile_ids[:n_steps], n_used - 1) * tm
    tile_expert = jnp.sum((pad_end[None, :] <= tile_start[:, None]).astype(jnp.int32), axis=1)
    tile_expert = jnp.minimum(tile_expert, N_EXPERTS - 1)
    of_tile = tile_expert[:, None] == jnp.arange(N_EXPERTS, dtype=jnp.int32)[None, :]
    rows_end = jnp.sum(jnp.where(of_tile, (pad_off + cnt_all)[None, :], 0), axis=1)
    run_end = jnp.sum(jnp.where(of_tile, pad_end[None, :], 0), axis=1)
    tile_valid = jnp.clip(rows_end - tile_start, 0, tm)
    next_expert = jnp.sum((pad_end[None, :] <= run_end[:, None]).astype(jnp.int32), axis=1)
    next_expert = jnp.where(run_end < pad_end[-1], jnp.minimum(next_expert, N_EXPERTS - 1), -1)

    prefetch = (tile_expert, next_expert, n_used.reshape(1), tile_valid * ROW_TILE, piece_lo,
                piece_src * ROW_TILE, (piece_start % tm) * ROW_TILE, piece_len * ROW_TILE)
    yc = pl.pallas_call(
        _expert_body,
        grid_spec=pltpu.PrefetchScalarGridSpec(
            num_scalar_prefetch=len(prefetch),
            grid=(n_steps,),
            in_specs=[any_spec, any_spec,
                      pl.BlockSpec((1, 1, 2 * D_EXPERT), lambda i, te, *_: (te[i], 0, 0)),
                      any_spec,
                      pl.BlockSpec((1, 1, D_MODEL), lambda i, te, *_: (te[i], 0, 0))],
            out_specs=any_spec,
            scratch_shapes=[pltpu.VMEM((tm * ROW_TILE, LANES), F32), pltpu.VMEM((tm * ROW_TILE, LANES), F32),
                            pltpu.VMEM((tm * ROW_TILE, LANES), F32), pltpu.VMEM((tm * ROW_TILE, LANES), F32),
                            pltpu.VMEM((D_MODEL, 2 * D_EXPERT), F32), pltpu.VMEM((D_EXPERT, D_MODEL), F32),
                            pltpu.VMEM((D_MODEL, 2 * D_EXPERT), BF16), pltpu.VMEM((D_EXPERT, D_MODEL), BF16),
                            pltpu.SemaphoreType.DMA((2,)), pltpu.SemaphoreType.DMA(()),
                            pltpu.SemaphoreType.DMA((2,))]),
        out_shape=jax.ShapeDtypeStruct((n_pairs * ROW_TILE, LANES), F32),
        compiler_params=pltpu.CompilerParams(dimension_semantics=("arbitrary",), vmem_limit_bytes=VMEM_LIMIT),
        name="moe_experts",
    )(*[p.astype(jnp.int32) for p in prefetch], xc, w_gate_up[0],
      b_gate_up[0].reshape(N_EXPERTS, 1, 2 * D_EXPERT), w_down[0], b_down[0].reshape(N_EXPERTS, 1, D_MODEL))

    pos_tiles = (pos_a * ROW_TILE).reshape(n_tok_blocks, pb // LANES, LANES)
    gw_tiles = gw_a.reshape(n_tok_blocks, pb // LANES, LANES)

    def combine(x1, first_block, n_blocks):
        scalars = pl.BlockSpec((1, pb // LANES, LANES), lambda i: (first_block + i, 0, 0), memory_space=pltpu.SMEM)
        return pl.pallas_call(
            functools.partial(_combine_body, first_block=first_block, n_blocks=n_blocks),
            grid=(n_blocks,),
            in_specs=[scalars, scalars, pl.BlockSpec(memory_space=pl.ANY), pl.BlockSpec(memory_space=pl.ANY)],
            out_specs=pl.BlockSpec((tb, D_MODEL), lambda i: (i, 0)),
            out_shape=jax.ShapeDtypeStruct(x1.shape, F32),
            scratch_shapes=[pltpu.VMEM((tb * ROW_TILE, LANES), F32),
                            pltpu.VMEM((COMBINE_SLOTS, pb * ROW_TILE, LANES), F32),
                            pltpu.VMEM((COMBINE_SLOTS, tb, D_MODEL), F32),
                            pltpu.SemaphoreType.DMA((2, COMBINE_SLOTS))],
            compiler_params=pltpu.CompilerParams(dimension_semantics=("arbitrary",), vmem_limit_bytes=VMEM_LIMIT),
            name="moe_combine",
        )(pos_tiles, gw_tiles, x1, yc)

    y_prompt = combine(x1p.reshape(n_prompt, D_MODEL), 0, n_prompt_blocks).reshape(n_b, seq, D_MODEL)
    y_sample = combine(x1s, n_prompt_blocks, 1).reshape(n_streams, n_new, D_MODEL)

    kv_shape = (N_KV_HEADS, HEAD_DIM)
    sg_shape = (MLP_GROUPS, MLP_GROUP_DIM)
    return (y_prompt, y_sample,
            kwin_p.reshape(1, n_b, WINDOW, *kv_shape), vwin_p.reshape(1, n_b, WINDOW, *kv_shape),
            kwin_s.reshape(1, n_streams, n_cached, *kv_shape), vwin_s.reshape(1, n_streams, n_cached, *kv_shape),
            sguv_p.reshape(1, n_b, MLP_CHUNK, *sg_shape), sguv_s.reshape(1, n_streams, n_new, *sg_shape))
```

```python
import functools

import jax
import jax.numpy as jnp
import numpy as np
from jax import lax
from jax.experimental import pallas as pl
from jax.experimental.pallas import tpu as pltpu

D_MODEL = 1024
CHUNK = 64
WINDOW = 128
HEAD_DIM = 64
N_HEADS = 8
N_KV_HEADS = 2
Q_PER_KV = N_HEADS // N_KV_HEADS
ATTN_WIDTH = N_HEADS * HEAD_DIM
KV_WIDTH = N_KV_HEADS * HEAD_DIM
ATTN_SCALE = HEAD_DIM ** -0.5
MLP_CHUNK = 128
MLP_GROUPS = 8
MLP_WIDTH = D_MODEL
MLP_GROUP_DIM = MLP_WIDTH // MLP_GROUPS
N_EXPERTS = 32
TOP_K = 4
D_EXPERT = D_MODEL
SWIGLU_ALPHA = 1.702
SWIGLU_LIMIT = 7.0
EPS = 1e-6
NEG_INF = -1e30
Q_END = ATTN_WIDTH
K_END = Q_END + KV_WIDTH
V_END = K_END + KV_WIDTH
U_END = V_END + MLP_WIDTH
VM_END = U_END + MLP_WIDTH
GA_END = VM_END + D_MODEL
IN_COLS = GA_END + D_MODEL

LANES = 128
ROW_TILE = D_MODEL // LANES
TOKEN_BLOCK = 256
PAIR = 2 * CHUNK
BAND = PAIR + WINDOW
EXPERT_TILE = 512
TILE_QUARTER = EXPERT_TILE // 4
VMEM_LIMIT = 56 * 1024 * 1024

F32 = jnp.float32
BF16 = jnp.bfloat16


def _dot(a, b):
    return jnp.dot(a, b, preferred_element_type=F32)


def _dot_nt(a, b):
    return lax.dot_general(a, b, (((1,), (1,)), ((), ())), preferred_element_type=F32)


def _store_token_major(ref, val):
    rows = val.shape[0]
    for s in range(ROW_TILE):
        ref[pl.ds(s, rows, stride=ROW_TILE), :] = val[:, s * LANES:(s + 1) * LANES]


def _load_token_major(ref, rows):
    return jnp.concatenate([ref[pl.ds(s, rows, stride=ROW_TILE), :] for s in range(ROW_TILE)], axis=-1)


def _segment_mean(sq, seg):
    hi = sq.astype(BF16)
    lo = (sq - hi.astype(F32)).astype(BF16)
    return _dot(jnp.concatenate([hi, lo], axis=-1), seg)


def _rms_rows(x, gain):
    ms = jnp.mean(x * x, axis=-1, keepdims=True)
    return (x * lax.rsqrt(ms + EPS)) * gain


def _project(h, wq_ref, w_in_ref, qg, kg, sgug, seg64, seg128, between=None):
    qkv = jnp.concatenate([_dot(h, wq_ref[...]), _dot(h, w_in_ref[:, Q_END:V_END])], axis=-1)
    qk_cols = []
    for c in range(K_END // LANES):
        blk = qkv[:, c * LANES:(c + 1) * LANES]
        ms = _segment_mean(blk * blk, seg64)
        g = qg if c < Q_END // LANES else kg
        qk_cols.append((blk * lax.rsqrt(ms + EPS)) * g)
    qn = jnp.concatenate(qk_cols[:Q_END // LANES], axis=-1)
    kn = qk_cols[Q_END // LANES]
    v = qkv[:, K_END:V_END]
    u = jax.nn.gelu(_dot(h, w_in_ref[:, V_END:U_END]))
    vg = jax.nn.gelu(_dot(h, w_in_ref[:, U_END:VM_END]))
    vm_cols = []
    for g in range(MLP_GROUPS):
        blk = vg[:, g * LANES:(g + 1) * LANES]
        ms = _segment_mean(blk * blk, seg128)
        vm_cols.append((blk * lax.rsqrt(ms + EPS)) * sgug[:, g * LANES:(g + 1) * LANES])
    vm = jnp.concatenate(vm_cols, axis=-1)
    if between is not None:
        between()
    ga = jax.nn.sigmoid(_dot(h, w_in_ref[:, VM_END:GA_END]))
    gb = jax.nn.sigmoid(_dot(h, w_in_ref[:, GA_END:IN_COLS]))
    return qn, kn, v, u, vm, ga, gb


def _stack_heads(q_rows):
    lane = lax.broadcasted_iota(jnp.int32, (q_rows.shape[0], LANES), 1)
    blocks = []
    for head in range(N_HEADS):
        j, half = head % Q_PER_KV, head // Q_PER_KV
        col = q_rows[:, j * LANES:(j + 1) * LANES]
        keep = (lane < HEAD_DIM) if half == 0 else (lane >= HEAD_DIM)
        blocks.append(jnp.where(keep, col, 0.0))
    return jnp.concatenate(blocks, axis=0).astype(BF16)


def _unstack_heads(o, rows):
    lane = lax.broadcasted_iota(jnp.int32, (rows, LANES), 1)
    cols = []
    for j in range(Q_PER_KV):
        lo = o[j * rows:(j + 1) * rows]
        hi = o[(j + Q_PER_KV) * rows:(j + Q_PER_KV + 1) * rows]
        cols.append(jnp.where(lane < HEAD_DIM, lo, hi))
    return jnp.concatenate(cols, axis=-1)


def _merge(att, sgu, ga, gb, wba_ref, wbs_ref, wout_ref):
    m = ga * _dot(att.astype(BF16), wba_ref[...]) + gb * _dot(sgu.astype(BF16), wbs_ref[...])
    return _dot(m.astype(BF16), wout_ref[...])


def _route(h2, wr_ref, br_ref, tri_ref):
    rows = h2.shape[0]
    logits = _dot(h2, wr_ref[...]) + br_ref[...]
    eidx = lax.broadcasted_iota(jnp.int32, (rows, N_EXPERTS), 1).astype(F32)
    work = logits
    vals, picks, onehots = [], [], []
    for _ in range(TOP_K):
        m = jnp.max(work, axis=-1, keepdims=True)
        sel = jnp.min(jnp.where(work == m, eidx, float(N_EXPERTS)), axis=-1, keepdims=True)
        oh = eidx == sel
        vals.append(m)
        picks.append(sel)
        onehots.append(oh)
        work = jnp.where(oh, -jnp.inf, work)
    exps = [jnp.exp(v - vals[0]) for v in vals]
    den = exps[0] + exps[1] + exps[2] + exps[3]
    mask = jnp.zeros((rows, N_EXPERTS), F32)
    for oh in onehots:
        mask = mask + jnp.where(oh, 1.0, 0.0)
    before = _dot(tri_ref[...], mask.astype(BF16))
    counts = jnp.sum(mask, axis=0, keepdims=True)
    k4 = lax.broadcasted_iota(jnp.int32, (rows, TOP_K), 1)
    w4 = jnp.zeros((rows, TOP_K), F32)
    pos4 = jnp.zeros((rows, TOP_K), jnp.int32)
    for k in range(TOP_K):
        lower_experts = jnp.sum(jnp.where(eidx < picks[k], counts, 0.0), axis=-1, keepdims=True)
        rank = jnp.sum(jnp.where(onehots[k], before, 0.0), axis=-1, keepdims=True)
        w4 = jnp.where(k4 == k, exps[k] / den, w4)
        pos4 = jnp.where(k4 == k, (lower_experts + rank).astype(jnp.int32), pos4)
    return w4, pos4, counts


def _pair_selector(pos4, values=None):
    rows = pos4.shape[0]
    col = lax.broadcasted_iota(jnp.int32, (rows, TOP_K * rows), 1)
    sel = jnp.zeros((rows, TOP_K * rows), F32)
    for k in range(TOP_K):
        sel = sel + jnp.where(col == pos4[:, k:k + 1], 1.0 if values is None else values[:, k:k + 1], 0.0)
    return sel


def _route_block(h2, wr_ref, br_ref, tri_ref, gw_ref, pos_ref, cnt_ref):
    w4, pos4, counts = _route(h2, wr_ref, br_ref, tri_ref)
    gw_ref[...] = w4
    pos_ref[...] = pos4
    cnt_ref[0] = counts
    return pos4


def _dispatch_block(pos4, h2, xc_ref):
    select = _pair_selector(pos4).astype(BF16)
    pairs = lax.dot_general(select, h2, (((0,), (0,)), ((), ())), preferred_element_type=F32)
    _store_token_major(xc_ref, pairs)


def _route_and_dispatch(h2, wr_ref, br_ref, tri_ref, gw_ref, pos_ref, cnt_ref, xc_ref):
    _dispatch_block(_route_block(h2, wr_ref, br_ref, tri_ref, gw_ref, pos_ref, cnt_ref), h2, xc_ref)


def _softmax_rows(parts, sink):
    m = sink
    for l in parts:
        m = jnp.maximum(m, jnp.max(l, axis=-1, keepdims=True))
    es = [jnp.exp(l - m) for l in parts]
    den = jnp.exp(sink - m)
    for e in es:
        den = den + jnp.sum(e, axis=-1, keepdims=True)
    return [e / den for e in es]


def _mixer_prompt_body(sinks_ref, x_ref, wq_ref, w_in_ref, gmix_ref, qg_ref, kg_ref, sgug_ref, seg64_ref, seg128_ref,
                       bias_ref, wsp_ref, bsp_ref, wba_ref, wbs_ref, wout_ref, gffn_ref, wr_ref, br_ref, tri_ref,
                       x1_ref, xc_ref, gw_ref, pos_ref, cnt_ref, kwin_ref, vwin_ref, sguv_ref,
                       kcarry, vcarry, h2_prev_ref, *, blocks_per_seq, n_blocks):
    i = pl.program_id(0)

    @pl.when(i == 0)
    def _():
        h2_prev_ref[...] = jnp.zeros_like(h2_prev_ref)

    def route_previous():
        return _route_block(h2_prev_ref[...], wr_ref, br_ref, tri_ref, gw_ref, pos_ref, cnt_ref)

    def dispatch_previous(pos4):
        _dispatch_block(pos4, h2_prev_ref[...], xc_ref)

    @pl.when(i == n_blocks)
    def _():
        dispatch_previous(route_previous())

    @pl.when(i == n_blocks + 1)
    def _():
        xc_ref[...] = jnp.zeros_like(xc_ref)
        gw_ref[...] = jnp.zeros_like(gw_ref)
        pos_ref[...] = jnp.zeros_like(pos_ref)
        cnt_ref[...] = jnp.zeros_like(cnt_ref)

    @pl.when(jnp.logical_and(i < n_blocks, i % blocks_per_seq == 0))
    def _():
        kcarry[...] = jnp.zeros_like(kcarry)
        vcarry[...] = jnp.zeros_like(vcarry)

    @pl.when(i < n_blocks)
    def _():
        _mixer_prompt_block(i % blocks_per_seq, sinks_ref, x_ref, wq_ref, w_in_ref, gmix_ref, qg_ref, kg_ref, sgug_ref,
                            seg64_ref, seg128_ref, bias_ref, wsp_ref, bsp_ref, wba_ref, wbs_ref, wout_ref, gffn_ref,
                            x1_ref, kwin_ref, vwin_ref, sguv_ref, kcarry, vcarry, h2_prev_ref,
                            route_previous, dispatch_previous)


def _mixer_prompt_block(j, sinks_ref, x_ref, wq_ref, w_in_ref, gmix_ref, qg_ref, kg_ref, sgug_ref, seg64_ref, seg128_ref,
                        bias_ref, wsp_ref, bsp_ref, wba_ref, wbs_ref, wout_ref, gffn_ref,
                        x1_ref, kwin_ref, vwin_ref, sguv_ref, kcarry, vcarry, h2_prev_ref,
                        route_previous, dispatch_previous):
    routed = []
    x = x_ref[0]
    h = _rms_rows(x, gmix_ref[...]).astype(BF16)
    qn, kn, v, u, vm, ga, gb = _project(h, wq_ref, w_in_ref, qg_ref[...], kg_ref[...], sgug_ref[...],
                                        seg64_ref[...], seg128_ref[...],
                                        between=lambda: routed.append(route_previous()))
    k_ext = jnp.concatenate([kcarry[...], kn.astype(BF16)], axis=0)
    v_ext = jnp.concatenate([vcarry[...], v.astype(BF16)], axis=0)
    kcarry[...] = k_ext[TOKEN_BLOCK:]
    vcarry[...] = v_ext[TOKEN_BLOCK:]

    tri_mask = (lax.broadcasted_iota(jnp.int32, (MLP_CHUNK, MLP_CHUNK), 0)
                >= lax.broadcasted_iota(jnp.int32, (MLP_CHUNK, MLP_CHUNK), 1))
    att_rows, sgu_rows = [], []
    for pm in range(TOKEN_BLOCK // PAIR):
        r0 = pm * PAIR
        q_stack = _stack_heads(qn[r0:r0 + PAIR])
        k_band = k_ext[r0:r0 + BAND]
        v_band = v_ext[r0:r0 + BAND]
        s = _dot_nt(q_stack, k_band)
        first = jnp.where(j == 0, 0, 1) if pm == 0 else 1
        probs = []
        for head in range(N_HEADS):
            logit = s[head * PAIR:(head + 1) * PAIR] * ATTN_SCALE + bias_ref[first, head]
            probs.append(_softmax_rows([logit], sinks_ref[head])[0].astype(BF16))
        o = _dot(jnp.concatenate(probs, axis=0), v_band)
        att_rows.append(_unstack_heads(o, PAIR))
        cols = []
        for g in range(MLP_GROUPS):
            wm = jnp.where(tri_mask, wsp_ref[g], 0.0).astype(BF16)
            cols.append(_dot(wm, vm[r0:r0 + PAIR, g * LANES:(g + 1) * LANES].astype(BF16)))
        mixed = jnp.concatenate(cols, axis=-1) + bsp_ref[...]
        sgu_rows.append(u[r0:r0 + PAIR] * mixed)
        if pm == 0:
            dispatch_previous(routed[0])
    att = jnp.concatenate(att_rows, axis=0)
    sgu = jnp.concatenate(sgu_rows, axis=0)

    x1 = x + _merge(att, sgu, ga, gb, wba_ref, wbs_ref, wout_ref)
    h2 = _rms_rows(x1, gffn_ref[...])
    h2_prev_ref[...] = h2.astype(BF16)

    x1_ref[0] = x1
    kwin_ref[0] = kn[TOKEN_BLOCK - WINDOW:]
    vwin_ref[0] = v[TOKEN_BLOCK - WINDOW:]
    for g in range(MLP_GROUPS):
        sguv_ref[0, :, g, :] = vm[TOKEN_BLOCK - MLP_CHUNK:, g * LANES:(g + 1) * LANES]


def _mixer_sample_body(sinks_ref, x_ref, ck_ref, cv_ref, wq_ref, w_in_ref, gmix_ref, qg_ref, kg_ref, sgug_ref, seg64_ref,
                       seg128_ref, biasc_ref, biasn_ref, wsp_ref, bsp_ref, wba_ref, wbs_ref, wout_ref, gffn_ref,
                       wr_ref, br_ref, tri_ref, xc_in_ref, gw_in_ref, pos_in_ref, cnt_in_ref,
                       x1_ref, xc_ref, gw_ref, pos_ref, cnt_ref, kwin_ref, vwin_ref, sguv_ref, *, n_streams, n_new):
    del xc_in_ref, gw_in_ref, pos_in_ref, cnt_in_ref
    x = x_ref[...]
    h = _rms_rows(x, gmix_ref[...]).astype(BF16)
    qn, kn, v, u, vm, ga, gb = _project(h, wq_ref, w_in_ref, qg_ref[...], kg_ref[...], sgug_ref[...],
                                        seg64_ref[...], seg128_ref[...])
    n_cached = ck_ref.shape[1]
    att_rows = []
    for s_i in range(n_streams):
        r0 = s_i * n_new
        q_stack = _stack_heads(qn[r0:r0 + n_new])
        k_new = kn[r0:r0 + n_new]
        v_new = v[r0:r0 + n_new]
        s_c = _dot_nt(q_stack, ck_ref[s_i].astype(BF16))
        s_n = _dot_nt(q_stack, k_new.astype(BF16))
        pc, pn = [], []
        for head in range(N_HEADS):
            rows = slice(head * n_new, (head + 1) * n_new)
            lc = s_c[rows] * ATTN_SCALE + biasc_ref[rows]
            ln = s_n[rows] * ATTN_SCALE + biasn_ref[rows]
            p_c, p_n = _softmax_rows([lc, ln], sinks_ref[head])
            pc.append(p_c.astype(BF16))
            pn.append(p_n.astype(BF16))
        o = (_dot(jnp.concatenate(pc, axis=0), cv_ref[s_i].astype(BF16))
             + _dot(jnp.concatenate(pn, axis=0), v_new.astype(BF16)))
        att_rows.append(_unstack_heads(o, n_new))
        kwin_ref[s_i, 0:n_cached - n_new] = ck_ref[s_i, n_new:n_cached]
        kwin_ref[s_i, n_cached - n_new:n_cached] = k_new
        vwin_ref[s_i, 0:n_cached - n_new] = cv_ref[s_i, n_new:n_cached]
        vwin_ref[s_i, n_cached - n_new:n_cached] = v_new
    att = jnp.concatenate(att_rows, axis=0)

    rows = n_streams * n_new
    ri = lax.broadcasted_iota(jnp.int32, (rows, rows), 0)
    ci = lax.broadcasted_iota(jnp.int32, (rows, rows), 1)
    keep = jnp.logical_and(ri // n_new == ci // n_new, ri % n_new >= ci % n_new)
    expand = (lax.broadcasted_iota(jnp.int32, (rows, n_new), 0) % n_new
              == lax.broadcasted_iota(jnp.int32, (rows, n_new), 1)).astype(BF16)
    cols = []
    for g in range(MLP_GROUPS):
        tiled = _dot_nt(_dot(expand, wsp_ref[g].astype(BF16)).astype(BF16), expand)
        wm = jnp.where(keep, tiled, 0.0).astype(BF16)
        cols.append(_dot(wm, vm[:, g * LANES:(g + 1) * LANES].astype(BF16)))
    sgu = u * (jnp.concatenate(cols, axis=-1) + bsp_ref[...])

    x1 = x + _merge(att, sgu, ga, gb, wba_ref, wbs_ref, wout_ref)
    h2 = _rms_rows(x1, gffn_ref[...])
    _route_and_dispatch(h2.astype(BF16), wr_ref, br_ref, tri_ref, gw_ref, pos_ref, cnt_ref, xc_ref)
    x1_ref[...] = x1
    for g in range(MLP_GROUPS):
        sguv_ref[:, g, :] = vm[:, g * LANES:(g + 1) * LANES]


def _piece_copies(i_tile, lo_ref, src_ref, off_ref, len_ref, hbm, buf, sem, to_hbm):
    def one(p, carry):
        n = pl.multiple_of(len_ref[p], ROW_TILE)

        @pl.when(n > 0)
        def _():
            far = hbm.at[pl.ds(pl.multiple_of(src_ref[p], ROW_TILE), n), :]
            near = buf.at[pl.ds(pl.multiple_of(off_ref[p], ROW_TILE), n), :]
            if to_hbm:
                pltpu.make_async_copy(near, far, sem).start()
            else:
                pltpu.make_async_copy(far, near, sem).start()
        return carry
    lax.fori_loop(lo_ref[i_tile], lo_ref[i_tile + 1], one, 0)


def _wait_rows(hbm, buf, sem, n_rows):
    n = pl.multiple_of(n_rows, ROW_TILE)
    pltpu.make_async_copy(hbm.at[pl.ds(0, n), :], buf.at[pl.ds(0, n), :], sem).wait()


def _expert_mlp(x_cur, y_cur, bgu_ref, bd_ref, wgu_bf, wd_bf, n_rows):
    rows = pl.ds(0, n_rows * ROW_TILE)
    xb = _load_token_major(x_cur.at[rows, :], n_rows).astype(BF16)
    hgu = _dot(xb, wgu_bf[...]) + bgu_ref[0]
    glu = jnp.minimum(hgu[:, :D_EXPERT], SWIGLU_LIMIT)
    lin = jnp.clip(hgu[:, D_EXPERT:], -SWIGLU_LIMIT, SWIGLU_LIMIT)
    act = glu * jax.nn.sigmoid(glu * SWIGLU_ALPHA) * (lin + 1.0)
    _store_token_major(y_cur.at[rows, :], _dot(act.astype(BF16), wd_bf[...]) + bd_ref[0])


def _weight_copies(wgu_hbm, wd_hbm, wgu_stage, wd_stage, sem_w, expert):
    return (pltpu.make_async_copy(wgu_hbm.at[expert], wgu_stage, sem_w.at[0]),
            pltpu.make_async_copy(wd_hbm.at[expert], wd_stage, sem_w.at[1]))


def _expert_body(te_ref, nxt_ref, nused_ref, valid_ref, lo_ref, src_ref, off_ref, len_ref, xc_hbm, wgu_hbm, bgu_ref,
                 wd_hbm, bd_ref, yc_hbm, xbuf0, xbuf1, ybuf0, ybuf1, wgu_stage, wd_stage, wgu_bf, wd_bf,
                 sem_in, sem_out, sem_w):
    i = pl.program_id(0)
    n_used = nused_ref[0]
    xbufs, ybufs = (xbuf0, xbuf1), (ybuf0, ybuf1)
    pieces = (lo_ref, src_ref, off_ref, len_ref)

    @pl.when(i == 0)
    def _():
        xbuf0[...] = jnp.zeros_like(xbuf0)
        xbuf1[...] = jnp.zeros_like(xbuf1)
        for cp in _weight_copies(wgu_hbm, wd_hbm, wgu_stage, wd_stage, sem_w, te_ref[0]):
            cp.start(priority=1)
        _piece_copies(0, *pieces, xc_hbm, xbuf0, sem_in.at[0], to_hbm=False)

    @pl.when(jnp.logical_and(i < n_used, jnp.logical_or(i == 0, te_ref[i] != te_ref[jnp.maximum(i - 1, 0)])))
    def _():
        for cp in _weight_copies(wgu_hbm, wd_hbm, wgu_stage, wd_stage, sem_w, te_ref[i]):
            cp.wait()
        wgu_bf[...] = wgu_stage[...].astype(BF16)
        wd_bf[...] = wd_stage[...].astype(BF16)

        @pl.when(nxt_ref[i] >= 0)
        def _():
            for cp in _weight_copies(wgu_hbm, wd_hbm, wgu_stage, wd_stage, sem_w, nxt_ref[i]):
                cp.start(priority=1)

    for par in range(2):
        x_cur, x_next, y_cur, y_prev = xbufs[par], xbufs[1 - par], ybufs[par], ybufs[1 - par]

        @pl.when(jnp.logical_and(i < n_used, i % 2 == par))
        def _(x_cur=x_cur, x_next=x_next, y_cur=y_cur, y_prev=y_prev, par=par):
            _wait_rows(xc_hbm, x_cur, sem_in.at[par], valid_ref[i])

            @pl.when(i + 1 < n_used)
            def _():
                _piece_copies(i + 1, *pieces, xc_hbm, x_next, sem_in.at[1 - par], to_hbm=False)

            @pl.when(i > 0)
            def _():
                _piece_copies(i - 1, *pieces, yc_hbm, y_prev, sem_out, to_hbm=True)

            quarters = (valid_ref[i] + (TILE_QUARTER * ROW_TILE - 1)) // (TILE_QUARTER * ROW_TILE)
            for q in range(1, EXPERT_TILE // TILE_QUARTER + 1):
                @pl.when(quarters == q)
                def _(q=q):
                    _expert_mlp(x_cur, y_cur, bgu_ref, bd_ref, wgu_bf, wd_bf, q * TILE_QUARTER)

            @pl.when(i > 0)
            def _():
                _wait_rows(yc_hbm, y_prev, sem_out, valid_ref[i - 1])

        @pl.when(jnp.logical_and(i == n_used, i % 2 == par))
        def _(y_prev=y_prev):
            _piece_copies(i - 1, *pieces, yc_hbm, y_prev, sem_out, to_hbm=True)
            _wait_rows(yc_hbm, y_prev, sem_out, valid_ref[i - 1])


COMBINE_SLOTS = 3


def _combine_body(pos_ref, gw_ref, x1_ref, yc_hbm, out_ref, acc_ref, ring, sem, *, first_block, n_blocks):
    i = pl.program_id(0)
    rows = x1_ref.shape[0]
    block_rows = ring.shape[1]

    def fetch(step):
        src = yc_hbm.at[pl.ds(pl.multiple_of((first_block + step) * block_rows, ROW_TILE), block_rows), :]
        return pltpu.make_async_copy(src, ring.at[step % COMBINE_SLOTS], sem.at[step % COMBINE_SLOTS])

    @pl.when(i == 0)
    def _():
        for step in range(min(COMBINE_SLOTS - 1, n_blocks)):
            fetch(step).start()

    @pl.when(i + COMBINE_SLOTS - 1 < n_blocks)
    def _():
        fetch(i + COMBINE_SLOTS - 1).start()

    _store_token_major(acc_ref, x1_ref[...])
    fetch(i).wait()
    yc_ref = ring.at[i % COMBINE_SLOTS]
    for t in range(rows):
        row = acc_ref[pl.ds(t * ROW_TILE, ROW_TILE), :]
        for k in range(TOP_K):
            flat = t * TOP_K + k
            at = pl.multiple_of(pos_ref[0, flat // LANES, flat % LANES], ROW_TILE)
            row = row + gw_ref[0, flat // LANES, flat % LANES] * yc_ref[pl.ds(at, ROW_TILE), :]
        acc_ref[pl.ds(t * ROW_TILE, ROW_TILE), :] = row
    out_ref[...] = _load_token_major(acc_ref, rows)


def _const_spec(shape):
    nd = len(shape)
    return pl.BlockSpec(shape, lambda *_: (0,) * nd, pipeline_mode=pl.Buffered(1))


def _q_perm():
    cols = np.arange(ATTN_WIDTH)
    j, half, d = cols // LANES, (cols % LANES) // HEAD_DIM, cols % HEAD_DIM
    return (j + Q_PER_KV * half) * HEAD_DIM + d


def _alibi_slopes():
    return 2.0 ** (-8.0 * np.arange(1, N_HEADS + 1) / N_HEADS)


def _prompt_bias():
    qi = np.arange(PAIR)[:, None]
    kj = np.arange(BAND)[None, :]
    dist = np.abs(qi + WINDOW - kj).astype(np.float64)
    cq, ck = qi // CHUNK, kj // CHUNK
    in_band = (ck >= cq) & (ck <= cq + WINDOW // CHUNK)
    base = -_alibi_slopes()[:, None, None] * dist[None]
    later = np.where(in_band[None], base, NEG_INF)
    first = np.where((kj >= WINDOW)[None], later, NEG_INF)
    return np.stack([first, later]).astype(np.float32)


def _sample_bias(n_new, n_cached):
    qi = np.arange(n_new)[:, None]
    dc = np.abs(qi + n_cached - np.arange(n_cached)[None, :]).astype(np.float64)
    dn = np.abs(qi - np.arange(n_new)[None, :]).astype(np.float64)
    sl = _alibi_slopes()[:, None, None]
    bc = (-sl * dc[None]).reshape(N_HEADS * n_new, n_cached)
    bn = (-sl * dn[None]).reshape(N_HEADS * n_new, n_new)
    return bc.astype(np.float32), bn.astype(np.float32)


def kernel(x_prompt, x_sample, cache_k_win, cache_v_win, g_mix, w_in, q_norm_g, k_norm_g, attn_sinks, sgu_norm_g,
           w_spatial, b_spatial, w_branch_attn, w_branch_sgu, w_out, g_ffn, w_router, b_router, w_gate_up,
           b_gate_up, w_down, b_down):
    n_b, seq, _ = x_prompt.shape
    n_streams, n_new, _ = x_sample.shape
    n_cached = cache_k_win.shape[2]
    n_prompt = n_b * seq
    n_sample = n_streams * n_new
    n_tok = n_prompt + n_sample
    assert seq % TOKEN_BLOCK == 0 and n_sample == TOKEN_BLOCK and n_cached == WINDOW
    blocks_per_seq = seq // TOKEN_BLOCK
    n_prompt_blocks = n_prompt // TOKEN_BLOCK

    perm = _q_perm()
    w_in_l = w_in[0]
    w_in_b = w_in_l.astype(BF16)
    halves = N_HEADS // Q_PER_KV
    wq_b = (w_in_l[:, :Q_END].reshape(D_MODEL, halves, Q_PER_KV, HEAD_DIM).transpose(0, 2, 1, 3)
            .reshape(D_MODEL, Q_END).astype(BF16))
    wba_b = w_branch_attn[0][perm, :].astype(BF16)
    wbs_b = w_branch_sgu[0].astype(BF16)
    wout_b = w_out[0].astype(BF16)
    wr_b = w_router[0].astype(BF16)
    br = b_router[0].reshape(1, N_EXPERTS).astype(F32)
    gmix = g_mix[0].reshape(1, D_MODEL)
    gffn = g_ffn[0].reshape(1, D_MODEL)
    qg = jnp.tile(q_norm_g[0], LANES // HEAD_DIM).reshape(1, LANES)
    kg = jnp.tile(k_norm_g[0], LANES // HEAD_DIM).reshape(1, LANES)
    sgug = sgu_norm_g[0].reshape(1, MLP_WIDTH)
    sinks = attn_sinks[0].astype(F32)
    lane_seg = np.arange(LANES) // HEAD_DIM
    seg64 = jnp.asarray(np.tile((lane_seg[:, None] == lane_seg[None, :]) / HEAD_DIM, (2, 1)), BF16)
    seg128 = jnp.full((2 * LANES, LANES), 1.0 / MLP_GROUP_DIM, BF16)
    tri = jnp.asarray(np.tril(np.ones((TOKEN_BLOCK, TOKEN_BLOCK)), -1), BF16)
    wsp = w_spatial[0]
    bsp_p = jnp.repeat(b_spatial[0].T, MLP_GROUP_DIM, axis=1)
    wsp_s = wsp[:, :n_new, :n_new]
    bsp_s = jnp.tile(jnp.repeat(b_spatial[0][:, :n_new].T, MLP_GROUP_DIM, axis=1), (n_streams, 1))
    bias_p = jnp.asarray(_prompt_bias())
    bias_c, bias_n = (jnp.asarray(a) for a in _sample_bias(n_new, n_cached))

    smem = pl.BlockSpec(memory_space=pltpu.SMEM)
    tb = TOKEN_BLOCK
    pb = TOP_K * tb
    n_tok_blocks = n_tok // tb
    last_blk = n_prompt_blocks - 1
    seq_of = lambda i: jnp.minimum(i, last_blk) // blocks_per_seq
    x_spec = pl.BlockSpec((1, tb, D_MODEL), lambda i: (seq_of(i), jnp.minimum(i, last_blk) % blocks_per_seq, 0))
    routed_blk = lambda i: jnp.where(i > n_prompt_blocks, n_prompt_blocks, jnp.clip(i - 1, 0, last_blk))
    win_spec = lambda width: pl.BlockSpec((1, WINDOW, width), lambda i: (seq_of(i), 0, 0))
    route_specs = lambda blk: [pl.BlockSpec((pb * ROW_TILE, LANES), lambda i: (blk(i), 0)),
                               pl.BlockSpec((tb, TOP_K), lambda i: (blk(i), 0)),
                               pl.BlockSpec((tb, TOP_K), lambda i: (blk(i), 0)),
                               pl.BlockSpec((1, 1, N_EXPERTS), lambda i: (blk(i), 0, 0))]
    route_shapes = [jax.ShapeDtypeStruct((n_tok * TOP_K * ROW_TILE, LANES), F32),
                    jax.ShapeDtypeStruct((n_tok, TOP_K), F32),
                    jax.ShapeDtypeStruct((n_tok, TOP_K), jnp.int32),
                    jax.ShapeDtypeStruct((n_tok_blocks, 1, N_EXPERTS), F32)]

    x1p, xc, gw_p, pos_p, cnt_p, kwin_p, vwin_p, sguv_p = pl.pallas_call(
        functools.partial(_mixer_prompt_body, blocks_per_seq=blocks_per_seq, n_blocks=n_prompt_blocks),
        grid=(n_prompt_blocks + 2,),
        in_specs=[smem, x_spec, _const_spec((D_MODEL, Q_END)),
                  _const_spec((D_MODEL, IN_COLS)), _const_spec((1, D_MODEL)), _const_spec((1, LANES)),
                  _const_spec((1, LANES)), _const_spec((1, MLP_WIDTH)), _const_spec((2 * LANES, LANES)),
                  _const_spec((2 * LANES, LANES)), _const_spec((2, N_HEADS, PAIR, BAND)),
                  _const_spec((MLP_GROUPS, MLP_CHUNK, MLP_CHUNK)), _const_spec((MLP_CHUNK, MLP_WIDTH)),
                  _const_spec((ATTN_WIDTH, D_MODEL)), _const_spec((MLP_WIDTH, D_MODEL)),
                  _const_spec((D_MODEL, D_MODEL)), _const_spec((1, D_MODEL)), _const_spec((D_MODEL, N_EXPERTS)),
                  _const_spec((1, N_EXPERTS)), _const_spec((tb, tb))],
        out_specs=[x_spec, *route_specs(routed_blk),
                   win_spec(KV_WIDTH), win_spec(KV_WIDTH),
                   pl.BlockSpec((1, MLP_CHUNK, MLP_GROUPS, MLP_GROUP_DIM), lambda i: (seq_of(i), 0, 0, 0))],
        out_shape=[jax.ShapeDtypeStruct((n_b, seq, D_MODEL), F32), *route_shapes,
                   jax.ShapeDtypeStruct((n_b, WINDOW, KV_WIDTH), F32),
                   jax.ShapeDtypeStruct((n_b, WINDOW, KV_WIDTH), F32),
                   jax.ShapeDtypeStruct((n_b, MLP_CHUNK, MLP_GROUPS, MLP_GROUP_DIM), F32)],
        scratch_shapes=[pltpu.VMEM((WINDOW, KV_WIDTH), BF16), pltpu.VMEM((WINDOW, KV_WIDTH), BF16),
                        pltpu.VMEM((tb, D_MODEL), BF16)],
        compiler_params=pltpu.CompilerParams(dimension_semantics=("arbitrary",), vmem_limit_bytes=VMEM_LIMIT),
        name="mixer_prompt",
    )(sinks, x_prompt, wq_b, w_in_b, gmix, qg, kg, sgug, seg64, seg128, bias_p, wsp, bsp_p, wba_b, wbs_b, wout_b,
      gffn, wr_b, br, tri)

    full = lambda shape: pl.BlockSpec(shape, lambda i: (0,) * len(shape))
    any_spec = pl.BlockSpec(memory_space=pl.ANY)
    ck = cache_k_win[0].reshape(n_streams, n_cached, KV_WIDTH)
    cv = cache_v_win[0].reshape(n_streams, n_cached, KV_WIDTH)
    x1s, xc, gw_a, pos_a, cnt_a, kwin_s, vwin_s, sguv_s = pl.pallas_call(
        functools.partial(_mixer_sample_body, n_streams=n_streams, n_new=n_new),
        grid=(1,),
        in_specs=[smem, full((n_sample, D_MODEL)), full((n_streams, n_cached, KV_WIDTH)),
                  full((n_streams, n_cached, KV_WIDTH)),
                  full((D_MODEL, Q_END)), full((D_MODEL, IN_COLS)), full((1, D_MODEL)), full((1, LANES)),
                  full((1, LANES)),
                  full((1, MLP_WIDTH)), full((2 * LANES, LANES)), full((2 * LANES, LANES)),
                  full((N_HEADS * n_new, n_cached)), full((N_HEADS * n_new, n_new)),
                  full((MLP_GROUPS, n_new, n_new)), full((n_sample, MLP_WIDTH)),
                  full((ATTN_WIDTH, D_MODEL)), full((MLP_WIDTH, D_MODEL)), full((D_MODEL, D_MODEL)),
                  full((1, D_MODEL)), full((D_MODEL, N_EXPERTS)), full((1, N_EXPERTS)), full((tb, tb)),
                  any_spec, any_spec, any_spec, any_spec],
        out_specs=[full((n_sample, D_MODEL)), *route_specs(lambda i: n_prompt_blocks),
                   full((n_streams, n_cached, KV_WIDTH)), full((n_streams, n_cached, KV_WIDTH)),
                   full((n_sample, MLP_GROUPS, MLP_GROUP_DIM))],
        out_shape=[jax.ShapeDtypeStruct((n_sample, D_MODEL), F32), *route_shapes,
                   jax.ShapeDtypeStruct((n_streams, n_cached, KV_WIDTH), F32),
                   jax.ShapeDtypeStruct((n_streams, n_cached, KV_WIDTH), F32),
                   jax.ShapeDtypeStruct((n_sample, MLP_GROUPS, MLP_GROUP_DIM), F32)],
        input_output_aliases={23: 1, 24: 2, 25: 3, 26: 4},
        compiler_params=pltpu.CompilerParams(dimension_semantics=("arbitrary",), vmem_limit_bytes=VMEM_LIMIT),
        name="mixer_sample",
    )(sinks, x_sample.reshape(n_sample, D_MODEL), ck, cv, wq_b, w_in_b, gmix, qg, kg, sgug, seg64, seg128, bias_c,
      bias_n, wsp_s, bsp_s, wba_b, wbs_b, wout_b, gffn, wr_b, br, tri, xc, gw_p, pos_p, cnt_p)

    tm = EXPERT_TILE
    counts = cnt_a[:, 0, :].astype(jnp.int32)
    cnt_all = jnp.sum(counts, axis=0)
    padded = ((cnt_all + tm - 1) // tm) * tm
    pad_end = jnp.cumsum(padded)
    pad_off = pad_end - padded
    n_pairs = TOP_K * n_tok
    n_tiles = n_pairs // tm + N_EXPERTS
    n_steps = n_tiles + 1
    in_block = jnp.cumsum(counts, axis=1) - counts
    in_expert = jnp.cumsum(counts, axis=0) - counts
    run_src = (jnp.arange(n_tok_blocks, dtype=jnp.int32)[:, None] * pb + in_block).T.reshape(-1)
    run_start = (pad_off[None, :] + in_expert).T.reshape(-1)
    run_len = counts.T.reshape(-1)
    head = jnp.minimum(run_len, tm - run_start % tm)
    piece_start = jnp.stack([run_start, run_start + head], axis=1).reshape(-1)
    piece_src = jnp.stack([run_src, run_src + head], axis=1).reshape(-1)
    piece_len = jnp.stack([head, run_len - head], axis=1).reshape(-1)
    piece_tile = piece_start // tm
    tile_ids = jnp.arange(n_steps + 1, dtype=jnp.int32)
    piece_lo = jnp.sum((piece_tile[None, :] < tile_ids[:, None]).astype(jnp.int32), axis=1)
    n_used = (pad_end[-1] // tm).astype(jnp.int32)
    tile_start = jnp.minimum(tile_ids[:n_steps], n_used - 1) * tm
    tile_expert = jnp.sum((pad_end[None, :] <= tile_start[:, None]).astype(jnp.int32), axis=1)
    tile_expert = jnp.minimum(tile_expert, N_EXPERTS - 1)
    of_tile = tile_expert[:, None] == jnp.arange(N_EXPERTS, dtype=jnp.int32)[None, :]
    rows_end = jnp.sum(jnp.where(of_tile, (pad_off + cnt_all)[None, :], 0), axis=1)
    run_end = jnp.sum(jnp.where(of_tile, pad_end[None, :], 0), axis=1)
    tile_valid = jnp.clip(rows_end - tile_start, 0, tm)
    next_expert = jnp.sum((pad_end[None, :] <= run_end[:, None]).astype(jnp.int32), axis=1)
    next_expert = jnp.where(run_end < pad_end[-1], jnp.minimum(next_expert, N_EXPERTS - 1), -1)

    prefetch = (tile_expert, next_expert, n_used.reshape(1), tile_valid * ROW_TILE, piece_lo,
                piece_src * ROW_TILE, (piece_start % tm) * ROW_TILE, piece_len * ROW_TILE)
    yc = pl.pallas_call(
        _expert_body,
        grid_spec=pltpu.PrefetchScalarGridSpec(
            num_scalar_prefetch=len(prefetch),
            grid=(n_steps,),
            in_specs=[any_spec, any_spec,
                      pl.BlockSpec((1, 1, 2 * D_EXPERT), lambda i, te, *_: (te[i], 0, 0)),
                      any_spec,
                      pl.BlockSpec((1, 1, D_MODEL), lambda i, te, *_: (te[i], 0, 0))],
            out_specs=any_spec,
            scratch_shapes=[pltpu.VMEM((tm * ROW_TILE, LANES), F32), pltpu.VMEM((tm * ROW_TILE, LANES), F32),
                            pltpu.VMEM((tm * ROW_TILE, LANES), F32), pltpu.VMEM((tm * ROW_TILE, LANES), F32),
                            pltpu.VMEM((D_MODEL, 2 * D_EXPERT), F32), pltpu.VMEM((D_EXPERT, D_MODEL), F32),
                            pltpu.VMEM((D_MODEL, 2 * D_EXPERT), BF16), pltpu.VMEM((D_EXPERT, D_MODEL), BF16),
                            pltpu.SemaphoreType.DMA((2,)), pltpu.SemaphoreType.DMA(()),
                            pltpu.SemaphoreType.DMA((2,))]),
        out_shape=jax.ShapeDtypeStruct((n_pairs * ROW_TILE, LANES), F32),
        compiler_params=pltpu.CompilerParams(dimension_semantics=("arbitrary",), vmem_limit_bytes=VMEM_LIMIT),
        name="moe_experts",
    )(*[p.astype(jnp.int32) for p in prefetch], xc, w_gate_up[0],
      b_gate_up[0].reshape(N_EXPERTS, 1, 2 * D_EXPERT), w_down[0], b_down[0].reshape(N_EXPERTS, 1, D_MODEL))

    pos_tiles = (pos_a * ROW_TILE).reshape(n_tok_blocks, pb // LANES, LANES)
    gw_tiles = gw_a.reshape(n_tok_blocks, pb // LANES, LANES)

    def combine(x1, first_block, n_blocks):
        scalars = pl.BlockSpec((1, pb // LANES, LANES), lambda i: (first_block + i, 0, 0), memory_space=pltpu.SMEM)
        return pl.pallas_call(
            functools.partial(_combine_body, first_block=first_block, n_blocks=n_blocks),
            grid=(n_blocks,),
            in_specs=[scalars, scalars,
                      pl.BlockSpec((tb, D_MODEL), lambda i: (i, 0)),
                      pl.BlockSpec(memory_space=pl.ANY)],
            out_specs=pl.BlockSpec((tb, D_MODEL), lambda i: (i, 0)),
            out_shape=jax.ShapeDtypeStruct(x1.shape, F32),
            scratch_shapes=[pltpu.VMEM((tb * ROW_TILE, LANES), F32),
                            pltpu.VMEM((COMBINE_SLOTS, pb * ROW_TILE, LANES), F32),
                            pltpu.SemaphoreType.DMA((COMBINE_SLOTS,))],
            compiler_params=pltpu.CompilerParams(dimension_semantics=("arbitrary",), vmem_limit_bytes=VMEM_LIMIT),
            name="moe_combine",
        )(pos_tiles, gw_tiles, x1, yc)

    y_prompt = combine(x1p.reshape(n_prompt, D_MODEL), 0, n_prompt_blocks).reshape(n_b, seq, D_MODEL)
    y_sample = combine(x1s, n_prompt_blocks, 1).reshape(n_streams, n_new, D_MODEL)

    kv_shape = (N_KV_HEADS, HEAD_DIM)
    sg_shape = (MLP_GROUPS, MLP_GROUP_DIM)
    return (y_prompt, y_sample,
            kwin_p.reshape(1, n_b, WINDOW, *kv_shape), vwin_p.reshape(1, n_b, WINDOW, *kv_shape),
            kwin_s.reshape(1, n_streams, n_cached, *kv_shape), vwin_s.reshape(1, n_streams, n_cached, *kv_shape),
            sguv_p.reshape(1, n_b, MLP_CHUNK, *sg_shape), sguv_s.reshape(1, n_streams, n_new, *sg_shape))
```
